```python
import jax
import jax.numpy as jnp
from jax import lax
import numpy as np

D_MODEL = 2048
BATCH = 8
SEQ = 4096
DEPTH = 4

GDN_HEAD_DIM = 128
GDN_HEADS = D_MODEL // GDN_HEAD_DIM
GDN_WIDTH = GDN_HEADS * GDN_HEAD_DIM
CONV_K = 4
CHUNK = 64
SWA_HEAD_DIM = 64
SWA_Q_HEADS = D_MODEL // SWA_HEAD_DIM
SWA_KV_HEADS = SWA_Q_HEADS // 8
SWA_Q_WIDTH = SWA_Q_HEADS * SWA_HEAD_DIM
SWA_KV_WIDTH = SWA_KV_HEADS * SWA_HEAD_DIM
WINDOW = 128
N_BRANCH = 2
D_FF = 4 * D_MODEL
PROJ_SIZES = (3 * GDN_WIDTH, GDN_WIDTH, GDN_HEADS, GDN_HEADS, SWA_Q_WIDTH, SWA_KV_WIDTH, SWA_KV_WIDTH, N_BRANCH * D_MODEL)
D_IN = 3 * GDN_WIDTH + GDN_WIDTH + 2 * GDN_HEADS + SWA_Q_WIDTH + 2 * SWA_KV_WIDTH + N_BRANCH * D_MODEL
NORM_EPS = 1e-6

kernel_name = "hybrid_gdn_swa_gated_parallel_trunk"


def rms_norm(x, gain):
    xf = x.astype(jnp.float32)
    y = xf * lax.rsqrt(jnp.mean(xf * xf, axis=-1, keepdims=True) + NORM_EPS)
    return (y * gain.astype(jnp.float32)).astype(x.dtype)


def l2_normalize(x):
    xf = x.astype(jnp.float32)
    return xf * lax.rsqrt(jnp.sum(xf * xf, axis=-1, keepdims=True) + NORM_EPS)


def causal_depthwise_conv(x, w):
    return lax.conv_general_dilated(
        x, w[:, None, :].astype(x.dtype), window_strides=(1,), padding=[(w.shape[0] - 1, 0)],
        dimension_numbers=("NWC", "WIO", "NWC"), feature_group_count=x.shape[-1])


def alibi_slopes(n_heads):
    return 2.0 ** (-8.0 * jnp.arange(1, n_heads + 1, dtype=jnp.float32) / n_heads)


def gated_delta_rule(q, k, v, g, beta):
    B, T, H, Dk = q.shape
    Dv = v.shape[-1]
    N = T // CHUNK
    f32 = jnp.float32

    def to_chunks(t):
        return t.astype(f32).reshape(B, N, CHUNK, H, -1).transpose(0, 3, 1, 2, 4)

    q = to_chunks(q) * (Dk ** -0.5)
    k = to_chunks(k)
    v = to_chunks(v)
    g = to_chunks(g[..., None])[..., 0]
    beta = to_chunks(beta[..., None])
    decay = jnp.cumsum(g, axis=-1)
    idx = jnp.arange(CHUNK)
    causal = idx[:, None] >= idx[None, :]
    strict = idx[:, None] > idx[None, :]
    gamma = jnp.exp(jnp.where(causal, decay[..., :, None] - decay[..., None, :], -jnp.inf))
    k_beta = k * beta
    a_low = jnp.where(strict, jnp.einsum("bhnid,bhnjd->bhnij", k_beta, k) * gamma, 0.0)
    rhs = jnp.concatenate([v * beta, k_beta * jnp.exp(decay)[..., None]], axis=-1)
    sol = lax.linalg.triangular_solve(a_low + jnp.eye(CHUNK, dtype=f32), rhs,
                                      left_side=True, lower=True, unit_diagonal=True)
    u, w = sol[..., :Dv], sol[..., Dv:]
    qk = jnp.einsum("bhnid,bhnjd->bhnij", q, k) * gamma
    q_dec = q * jnp.exp(decay)[..., None]
    k_dec = k * jnp.exp(decay[..., -1:] - decay)[..., None]
    chunk_decay = jnp.exp(decay[..., -1])

    def step(state, inp):
        qk_n, q_n, w_n, u_n, k_n, d_n = inp
        v_new = u_n - jnp.einsum("bhcd,bhde->bhce", w_n, state)
        o_n = jnp.einsum("bhcd,bhde->bhce", q_n, state) + jnp.einsum("bhij,bhje->bhie", qk_n, v_new)
        state = state * d_n[..., None, None] + jnp.einsum("bhcd,bhce->bhde", k_n, v_new)
        return state, o_n

    xs = tuple(jnp.moveaxis(t, 2, 0) for t in (qk, q_dec, w, u, k_dec, chunk_decay))
    _, o = lax.scan(step, jnp.zeros((B, H, Dk, Dv), f32), xs)
    return o.transpose(1, 0, 3, 2, 4).reshape(B, T, H, Dv)


def gdn_branch(qkv, z, b_logit, a_logit, conv_w, a_log, dt_bias, norm_g):
    B, T, _ = qkv.shape
    f32 = jnp.float32
    qkv_c = jax.nn.silu(causal_depthwise_conv(qkv, conv_w))
    q, k, v = jnp.split(qkv_c, 3, axis=-1)

    def heads(t):
        return t.reshape(B, T, GDN_HEADS, GDN_HEAD_DIM)

    q = l2_normalize(heads(q))
    k = l2_normalize(heads(k))
    beta = jax.nn.sigmoid(b_logit.astype(f32))
    g = -jnp.exp(a_log.astype(f32)) * jax.nn.softplus(a_logit.astype(f32) + dt_bias.astype(f32))
    o = gated_delta_rule(q, k, heads(v), g, beta)
    o = rms_norm(o, norm_g) * jax.nn.silu(heads(z).astype(f32))
    return o.reshape(B, T, GDN_WIDTH).astype(qkv.dtype)


def sliding_window_gqa(q, k, v, sinks):
    B, T, _ = q.shape
    NB = T // WINDOW
    G = SWA_Q_HEADS // SWA_KV_HEADS
    f32 = jnp.float32
    qb = q.reshape(B, NB, WINDOW, SWA_KV_HEADS, G, SWA_HEAD_DIM)

    def band(t):
        tb = t.reshape(B, NB, WINDOW, SWA_KV_HEADS, SWA_HEAD_DIM)
        prev = jnp.pad(tb, ((0, 0), (1, 0), (0, 0), (0, 0), (0, 0)))[:, :-1]
        return jnp.concatenate([prev, tb], axis=2)

    kb, vb = band(k), band(v)
    scores = jnp.einsum("bnqkgd,bnskd->bnkgqs", qb, kb).astype(f32) * (SWA_HEAD_DIM ** -0.5)
    qi = jnp.arange(WINDOW)[:, None]
    sj = jnp.arange(2 * WINDOW)[None, :]
    dist = qi + WINDOW - sj
    in_window = (dist >= 0) & (dist < WINDOW)
    after_start = (jnp.arange(NB)[:, None, None] > 0) | (sj >= WINDOW)[None]
    valid = in_window[None] & after_start
    slopes = alibi_slopes(SWA_Q_HEADS).reshape(SWA_KV_HEADS, G)
    bias = -slopes[:, :, None, None] * dist.astype(f32)
    scores = jnp.where(valid[None, :, None, None], scores + bias, -jnp.inf)
    sink = sinks.astype(f32).reshape(SWA_KV_HEADS, G)[:, :, None, None]
    m = jnp.maximum(scores.max(axis=-1, keepdims=True), sink)
    p = jnp.exp(scores - m)
    probs = (p / (p.sum(axis=-1, keepdims=True) + jnp.exp(sink - m))).astype(v.dtype)
    o = jnp.einsum("bnkgqs,bnskd->bnqkgd", probs, vb)
    return o.reshape(B, T, SWA_Q_WIDTH)


def _fwd_setup_inputs(seed: int = 0) -> dict:
    key = jax.random.key(seed)
    ks = jax.random.split(key, 16)
    f32 = jnp.float32
    nrm = lambda k, shape, scale: jax.random.normal(k, shape, f32) * scale
    dt = jnp.exp(jax.random.uniform(ks[5], (DEPTH, GDN_HEADS), f32, np.log(1e-3), np.log(1e-1)))
    return {
        "x": jax.random.normal(ks[0], (BATCH, SEQ, D_MODEL), f32),
        "norm1_g": 1.0 + nrm(ks[1], (DEPTH, D_MODEL), 0.02),
        "w_in": nrm(ks[2], (DEPTH, D_MODEL, D_IN), D_MODEL ** -0.5),
        "conv_w": nrm(ks[3], (DEPTH, CONV_K, 3 * GDN_WIDTH), CONV_K ** -0.5),
        "a_log": jnp.log(jax.random.uniform(ks[4], (DEPTH, GDN_HEADS), f32, 1.0, 16.0)),
        "dt_bias": dt + jnp.log(-jnp.expm1(-dt)),
        "gdn_norm_g": 1.0 + nrm(ks[6], (DEPTH, GDN_HEAD_DIM), 0.02),
        "attn_sinks": nrm(ks[7], (DEPTH, SWA_Q_HEADS), 0.5),
        "w_branch_gdn": nrm(ks[8], (DEPTH, GDN_WIDTH, D_MODEL), GDN_WIDTH ** -0.5),
        "w_branch_swa": nrm(ks[9], (DEPTH, SWA_Q_WIDTH, D_MODEL), SWA_Q_WIDTH ** -0.5),
        "w_out": nrm(ks[10], (DEPTH, D_MODEL, D_MODEL), D_MODEL ** -0.5),
        "norm2_g": 1.0 + nrm(ks[11], (DEPTH, D_MODEL), 0.02),
        "w_ff_up": nrm(ks[12], (DEPTH, D_MODEL, D_FF), D_MODEL ** -0.5),
        "w_ff_down": nrm(ks[13], (DEPTH, D_FF, D_MODEL), D_FF ** -0.5),
        "final_norm_g": 1.0 + nrm(ks[14], (D_MODEL,), 0.02),
    }


def _fwd_reference(x, norm1_g, w_in, conv_w, a_log, dt_bias, gdn_norm_g, attn_sinks, w_branch_gdn,
              w_branch_swa, w_out, norm2_g, w_ff_up, w_ff_down, final_norm_g):
    split_points = [int(s) for s in np.cumsum(PROJ_SIZES)[:-1]]
    for l in range(DEPTH):
        h = rms_norm(x, norm1_g[l])
        proj = h @ w_in[l]
        qkv_g, z, b_logit, a_logit, q_s, k_s, v_s, gate_logits = jnp.split(proj, split_points, axis=-1)
        y_gdn = gdn_branch(qkv_g, z, b_logit, a_logit, conv_w[l], a_log[l], dt_bias[l],
                           gdn_norm_g[l]) @ w_branch_gdn[l]
        y_swa = sliding_window_gqa(q_s, k_s, v_s, attn_sinks[l]) @ w_branch_swa[l]
        gates = jax.nn.sigmoid(gate_logits.astype(jnp.float32)).astype(x.dtype)
        g_gdn, g_swa = jnp.split(gates, N_BRANCH, axis=-1)
        x = x + (g_gdn * y_gdn + g_swa * y_swa) @ w_out[l]
        h = rms_norm(x, norm2_g[l])
        x = x + jnp.square(jax.nn.relu(h @ w_ff_up[l])) @ w_ff_down[l]
    return rms_norm(x, final_norm_g)


import jax as _jax
import jax.numpy as _jnp

TWIN_FORMAT = 'train_step'
FWD_PARAMS = ['x', 'norm1_g', 'w_in', 'conv_w', 'a_log', 'dt_bias', 'gdn_norm_g', 'attn_sinks', 'w_branch_gdn', 'w_branch_swa', 'w_out', 'norm2_g', 'w_ff_up', 'w_ff_down', 'final_norm_g']
TWIN_WEIGHTS = ['norm1_g', 'w_in', 'conv_w', 'a_log', 'dt_bias', 'gdn_norm_g', 'attn_sinks', 'w_branch_gdn', 'w_branch_swa', 'w_out', 'norm2_g', 'w_ff_up', 'w_ff_down', 'final_norm_g']
TWIN_DIFF_INPUT = 'x'
TWIN_INPUTS = ['x', 'norm1_g', 'w_in', 'conv_w', 'a_log', 'dt_bias', 'gdn_norm_g', 'attn_sinks', 'w_branch_gdn', 'w_branch_swa', 'w_out', 'norm2_g', 'w_ff_up', 'w_ff_down', 'final_norm_g', 'loss_target', 'm_norm1_g', 'm_w_in', 'm_conv_w', 'm_a_log', 'm_dt_bias', 'm_gdn_norm_g', 'm_attn_sinks', 'm_w_branch_gdn', 'm_w_branch_swa', 'm_w_out', 'm_norm2_g', 'm_w_ff_up', 'm_w_ff_down', 'm_final_norm_g', 'v_norm1_g', 'v_w_in', 'v_conv_w', 'v_a_log', 'v_dt_bias', 'v_gdn_norm_g', 'v_attn_sinks', 'v_w_branch_gdn', 'v_w_branch_swa', 'v_w_out', 'v_norm2_g', 'v_w_ff_up', 'v_w_ff_down', 'v_final_norm_g']
TWIN_OUTPUTS = ['loss', 'grad_x', 'grad_norm1_g', 'grad_w_in', 'grad_conv_w', 'grad_a_log', 'grad_dt_bias', 'grad_gdn_norm_g', 'grad_attn_sinks', 'grad_w_branch_gdn', 'grad_w_branch_swa', 'grad_w_out', 'grad_norm2_g', 'grad_w_ff_up', 'grad_w_ff_down', 'grad_final_norm_g', 'delta_norm1_g', 'delta_w_in', 'delta_conv_w', 'delta_a_log', 'delta_dt_bias', 'delta_gdn_norm_g', 'delta_attn_sinks', 'delta_w_branch_gdn', 'delta_w_branch_swa', 'delta_w_out', 'delta_norm2_g', 'delta_w_ff_up', 'delta_w_ff_down', 'delta_final_norm_g', 'new_m_norm1_g', 'new_m_w_in', 'new_m_conv_w', 'new_m_a_log', 'new_m_dt_bias', 'new_m_gdn_norm_g', 'new_m_attn_sinks', 'new_m_w_branch_gdn', 'new_m_w_branch_swa', 'new_m_w_out', 'new_m_norm2_g', 'new_m_w_ff_up', 'new_m_w_ff_down', 'new_m_final_norm_g', 'new_v_norm1_g', 'new_v_w_in', 'new_v_conv_w', 'new_v_a_log', 'new_v_dt_bias', 'new_v_gdn_norm_g', 'new_v_attn_sinks', 'new_v_w_branch_gdn', 'new_v_w_branch_swa', 'new_v_w_out', 'new_v_norm2_g', 'new_v_w_ff_up', 'new_v_w_ff_down', 'new_v_final_norm_g']
TWIN_LEAF_KINDS = {'loss': 'loss', 'grad_x': 'grad_x', 'grad_norm1_g': 'grad_w', 'grad_w_in': 'grad_w', 'grad_conv_w': 'grad_w', 'grad_a_log': 'grad_w', 'grad_dt_bias': 'grad_w', 'grad_gdn_norm_g': 'grad_w', 'grad_attn_sinks': 'grad_w', 'grad_w_branch_gdn': 'grad_w', 'grad_w_branch_swa': 'grad_w', 'grad_w_out': 'grad_w', 'grad_norm2_g': 'grad_w', 'grad_w_ff_up': 'grad_w', 'grad_w_ff_down': 'grad_w', 'grad_final_norm_g': 'grad_w', 'delta_norm1_g': 'delta_w', 'delta_w_in': 'delta_w', 'delta_conv_w': 'delta_w', 'delta_a_log': 'delta_w', 'delta_dt_bias': 'delta_w', 'delta_gdn_norm_g': 'delta_w', 'delta_attn_sinks': 'delta_w', 'delta_w_branch_gdn': 'delta_w', 'delta_w_branch_swa': 'delta_w', 'delta_w_out': 'delta_w', 'delta_norm2_g': 'delta_w', 'delta_w_ff_up': 'delta_w', 'delta_w_ff_down': 'delta_w', 'delta_final_norm_g': 'delta_w', 'new_m_norm1_g': 'new_m', 'new_m_w_in': 'new_m', 'new_m_conv_w': 'new_m', 'new_m_a_log': 'new_m', 'new_m_dt_bias': 'new_m', 'new_m_gdn_norm_g': 'new_m', 'new_m_attn_sinks': 'new_m', 'new_m_w_branch_gdn': 'new_m', 'new_m_w_branch_swa': 'new_m', 'new_m_w_out': 'new_m', 'new_m_norm2_g': 'new_m', 'new_m_w_ff_up': 'new_m', 'new_m_w_ff_down': 'new_m', 'new_m_final_norm_g': 'new_m', 'new_v_norm1_g': 'new_v', 'new_v_w_in': 'new_v', 'new_v_conv_w': 'new_v', 'new_v_a_log': 'new_v', 'new_v_dt_bias': 'new_v', 'new_v_gdn_norm_g': 'new_v', 'new_v_attn_sinks': 'new_v', 'new_v_w_branch_gdn': 'new_v', 'new_v_w_branch_swa': 'new_v', 'new_v_w_out': 'new_v', 'new_v_norm2_g': 'new_v', 'new_v_w_ff_up': 'new_v', 'new_v_w_ff_down': 'new_v', 'new_v_final_norm_g': 'new_v'}


def _forward(args):
    return _fwd_reference(*[args[k] for k in FWD_PARAMS])


def _output_shape():
    def fwd():
        inp = _fwd_setup_inputs(0)
        return _fwd_reference(*[inp[k] for k in FWD_PARAMS])
    out = _jax.eval_shape(fwd)
    return out.shape, out.dtype

N_MICROBATCH = 1
ADAM_LR = 0.001
ADAM_B1 = 0.9
ADAM_B2 = 0.999
ADAM_EPS = 1e-08
ADAM_WD = 0.01
ADAM_STEP = 10
PER_EXAMPLE_BATCH_AXIS = {'x': 0, 'loss_target': 0}
SHARED_INPUTS = []
_WEIGHT_DTYPES = {'norm1_g': _jnp.float32, 'w_in': _jnp.float32, 'conv_w': _jnp.float32, 'a_log': _jnp.float32, 'dt_bias': _jnp.float32, 'gdn_norm_g': _jnp.float32, 'attn_sinks': _jnp.float32, 'w_branch_gdn': _jnp.float32, 'w_branch_swa': _jnp.float32, 'w_out': _jnp.float32, 'norm2_g': _jnp.float32, 'w_ff_up': _jnp.float32, 'w_ff_down': _jnp.float32, 'final_norm_g': _jnp.float32}
MOMENT_SCALE = {'norm1_g': 4.596689e-02, 'w_in': 1.701085e-02, 'conv_w': 1.833047e-02, 'a_log': 7.700164e-02, 'dt_bias': 7.393637e-02, 'gdn_norm_g': 8.978130e-02, 'attn_sinks': 2.080229e-02, 'w_branch_gdn': 2.356255e-02, 'w_branch_swa': 1.855312e-02, 'w_out': 2.949122e-02, 'norm2_g': 7.209993e-02, 'w_ff_up': 3.589309e-02, 'w_ff_down': 7.716027e-02, 'final_norm_g': 1.655210e+01}


def _to_microbatches(a, axis):
    t = _jnp.moveaxis(a, axis, 0)
    t = t.reshape((N_MICROBATCH, t.shape[0] // N_MICROBATCH) + t.shape[1:])
    return _jnp.moveaxis(t, 1, axis + 1)


def setup_inputs(seed: int = 0) -> dict:
    inp = _fwd_setup_inputs(seed)
    key = _jax.random.fold_in(_jax.random.key(seed), 7919)
    shape, _ = _output_shape()
    out = dict(inp)
    out["loss_target"] = _jax.random.normal(_jax.random.fold_in(key, 0), shape, _jnp.float32)
    for i, name in enumerate(TWIN_WEIGHTS):
        w = inp[name].astype(_jnp.float32)
        if MOMENT_SCALE is None:
            s = _jnp.sqrt(_jnp.mean(_jnp.square(w)) + 1e-30)
        else:
            s = MOMENT_SCALE[name]
        km, kv = _jax.random.split(_jax.random.fold_in(key, i + 1))
        out[name] = w
        out["m_" + name] = s * _jax.random.normal(km, w.shape, _jnp.float32)
        out["v_" + name] = (s * s) * _jax.random.uniform(kv, w.shape, _jnp.float32, 0.5, 1.5)
    if N_MICROBATCH > 1:
        for name, axis in PER_EXAMPLE_BATCH_AXIS.items():
            out[name] = _to_microbatches(out[name], axis)
    return {'x': out['x'], 'norm1_g': out['norm1_g'], 'w_in': out['w_in'], 'conv_w': out['conv_w'], 'a_log': out['a_log'], 'dt_bias': out['dt_bias'], 'gdn_norm_g': out['gdn_norm_g'], 'attn_sinks': out['attn_sinks'], 'w_branch_gdn': out['w_branch_gdn'], 'w_branch_swa': out['w_branch_swa'], 'w_out': out['w_out'], 'norm2_g': out['norm2_g'], 'w_ff_up': out['w_ff_up'], 'w_ff_down': out['w_ff_down'], 'final_norm_g': out['final_norm_g'], 'loss_target': out['loss_target'], 'm_norm1_g': out['m_norm1_g'], 'm_w_in': out['m_w_in'], 'm_conv_w': out['m_conv_w'], 'm_a_log': out['m_a_log'], 'm_dt_bias': out['m_dt_bias'], 'm_gdn_norm_g': out['m_gdn_norm_g'], 'm_attn_sinks': out['m_attn_sinks'], 'm_w_branch_gdn': out['m_w_branch_gdn'], 'm_w_branch_swa': out['m_w_branch_swa'], 'm_w_out': out['m_w_out'], 'm_norm2_g': out['m_norm2_g'], 'm_w_ff_up': out['m_w_ff_up'], 'm_w_ff_down': out['m_w_ff_down'], 'm_final_norm_g': out['m_final_norm_g'], 'v_norm1_g': out['v_norm1_g'], 'v_w_in': out['v_w_in'], 'v_conv_w': out['v_conv_w'], 'v_a_log': out['v_a_log'], 'v_dt_bias': out['v_dt_bias'], 'v_gdn_norm_g': out['v_gdn_norm_g'], 'v_attn_sinks': out['v_attn_sinks'], 'v_w_branch_gdn': out['v_w_branch_gdn'], 'v_w_branch_swa': out['v_w_branch_swa'], 'v_w_out': out['v_w_out'], 'v_norm2_g': out['v_norm2_g'], 'v_w_ff_up': out['v_w_ff_up'], 'v_w_ff_down': out['v_w_ff_down'], 'v_final_norm_g': out['v_final_norm_g']}


def _loss(weights, diff, rest, loss_target):
    with _jax.named_scope("forward"):
        args = {**rest, TWIN_DIFF_INPUT: diff, **{k: w.astype(_WEIGHT_DTYPES[k]) for k, w in weights.items()}}
        y = _forward(args)
    with _jax.named_scope("loss_head"):
        err = _jnp.square(y.astype(_jnp.float32) - loss_target)
        return 0.5 * _jnp.sum(_jnp.mean(err, axis=-1)) if err.ndim else 0.5 * err


def _adamw(w, g, m, v):
    m = ADAM_B1 * m + (1.0 - ADAM_B1) * g
    v = ADAM_B2 * v + (1.0 - ADAM_B2) * _jnp.square(g)
    m_hat = m / (1.0 - ADAM_B1 ** ADAM_STEP)
    v_hat = v / (1.0 - ADAM_B2 ** ADAM_STEP)
    delta = -ADAM_LR * (m_hat / (_jnp.sqrt(v_hat) + ADAM_EPS) + ADAM_WD * w)
    return delta, m, v


def reference(x, norm1_g, w_in, conv_w, a_log, dt_bias, gdn_norm_g, attn_sinks, w_branch_gdn, w_branch_swa, w_out, norm2_g, w_ff_up, w_ff_down, final_norm_g, loss_target, m_norm1_g, m_w_in, m_conv_w, m_a_log, m_dt_bias, m_gdn_norm_g, m_attn_sinks, m_w_branch_gdn, m_w_branch_swa, m_w_out, m_norm2_g, m_w_ff_up, m_w_ff_down, m_final_norm_g, v_norm1_g, v_w_in, v_conv_w, v_a_log, v_dt_bias, v_gdn_norm_g, v_attn_sinks, v_w_branch_gdn, v_w_branch_swa, v_w_out, v_norm2_g, v_w_ff_up, v_w_ff_down, v_final_norm_g):
    given = dict(x=x, norm1_g=norm1_g, w_in=w_in, conv_w=conv_w, a_log=a_log, dt_bias=dt_bias, gdn_norm_g=gdn_norm_g, attn_sinks=attn_sinks, w_branch_gdn=w_branch_gdn, w_branch_swa=w_branch_swa, w_out=w_out, norm2_g=norm2_g, w_ff_up=w_ff_up, w_ff_down=w_ff_down, final_norm_g=final_norm_g, loss_target=loss_target, m_norm1_g=m_norm1_g, m_w_in=m_w_in, m_conv_w=m_conv_w, m_a_log=m_a_log, m_dt_bias=m_dt_bias, m_gdn_norm_g=m_gdn_norm_g, m_attn_sinks=m_attn_sinks, m_w_branch_gdn=m_w_branch_gdn, m_w_branch_swa=m_w_branch_swa, m_w_out=m_w_out, m_norm2_g=m_norm2_g, m_w_ff_up=m_w_ff_up, m_w_ff_down=m_w_ff_down, m_final_norm_g=m_final_norm_g, v_norm1_g=v_norm1_g, v_w_in=v_w_in, v_conv_w=v_conv_w, v_a_log=v_a_log, v_dt_bias=v_dt_bias, v_gdn_norm_g=v_gdn_norm_g, v_attn_sinks=v_attn_sinks, v_w_branch_gdn=v_w_branch_gdn, v_w_branch_swa=v_w_branch_swa, v_w_out=v_w_out, v_norm2_g=v_norm2_g, v_w_ff_up=v_w_ff_up, v_w_ff_down=v_w_ff_down, v_final_norm_g=v_final_norm_g)
    weights = {n: given[n] for n in TWIN_WEIGHTS}
    shared = {n: given[n] for n in SHARED_INPUTS}
    per_example = {n: given[n] for n in ['x']}
    grad_fn = _jax.value_and_grad(_loss, argnums=(0, 1))

    def one_microbatch(ex, loss_target):
        ex = dict(ex)
        diff = ex.pop(TWIN_DIFF_INPUT)
        return grad_fn(weights, diff, {**shared, **ex}, loss_target)

    if N_MICROBATCH == 1:
        loss, (grad_w, grad_x) = one_microbatch(per_example, given["loss_target"])
    else:
        def body(carry, xs):
            loss_sum, grad_sum = carry
            l_k, (gw_k, gx_k) = one_microbatch(xs[0], xs[1])
            with _jax.named_scope("update"):
                return (loss_sum + l_k, _jax.tree.map(_jnp.add, grad_sum, gw_k)), gx_k

        init = (_jnp.zeros((), _jnp.float32), _jax.tree.map(_jnp.zeros_like, weights))
        (loss, grad_w), grad_x = _jax.lax.scan(body, init, (per_example, given["loss_target"]))
    with _jax.named_scope("update"):
        delta_w, new_m, new_v = {}, {}, {}
        for n in TWIN_WEIGHTS:
            delta_w[n], new_m[n], new_v[n] = _adamw(weights[n], grad_w[n], given["m_" + n], given["v_" + n])
    return (loss, grad_x, *[grad_w[n] for n in TWIN_WEIGHTS], *[delta_w[n] for n in TWIN_WEIGHTS],
            *[new_m[n] for n in TWIN_WEIGHTS], *[new_v[n] for n in TWIN_WEIGHTS])
```

```python
import functools

import jax
import jax.numpy as jnp
import numpy as np
from jax import lax
from jax.experimental import pallas as pl
from jax.experimental.pallas import tpu as pltpu

F32 = jnp.float32
BF16 = jnp.bfloat16

GDN_HEAD_DIM = 128
CHUNK = 64
SWA_HEAD_DIM = 64
WINDOW = 128
CONV_K = 4
GQA_GROUP = 8
NORM_EPS = 1e-6
N_CHIPS = 4
LANES = 128
CONV_HALO = 8
VMEM_LIMIT = 56 * 1024 * 1024

ADAM_LR = 0.001
ADAM_B1 = 0.9
ADAM_B2 = 0.999
ADAM_EPS = 1e-08
ADAM_WD = 0.01
ADAM_STEP = 10

NN = (((1,), (0,)), ((), ()))
NT = (((1,), (1,)), ((), ()))
TN = (((0,), (0,)), ((), ()))


def _pick(dim, cap, mult=LANES):
    if dim <= cap:
        return dim
    t = (cap // mult) * mult
    while t >= mult:
        if dim % t == 0:
            return t
        t -= mult
    return dim


def _params(sem):
    return pltpu.CompilerParams(dimension_semantics=sem, vmem_limit_bytes=VMEM_LIMIT)


def _bdot(a, b, dn):
    return lax.dot_general(a.astype(BF16), b.astype(BF16), dn, preferred_element_type=F32)


@jax.custom_vjp
def mm_nn(a, b):
    return _bdot(a, b, NN)


@jax.custom_vjp
def mm_nt(a, b):
    return _bdot(a, b, NT)


@jax.custom_vjp
def mm_tn(a, b):
    return _bdot(a, b, TN)


mm_nn.defvjp(lambda a, b: (_bdot(a, b, NN), (a, b)), lambda r, g: (_bdot(g, r[1], NT), _bdot(r[0], g, TN)))
mm_nt.defvjp(lambda a, b: (_bdot(a, b, NT), (a, b)), lambda r, g: (_bdot(g, r[1], NN), _bdot(g, r[0], TN)))
mm_tn.defvjp(lambda a, b: (_bdot(a, b, TN), (a, b)), lambda r, g: (_bdot(r[1], g, NT), _bdot(r[0], g, NN)))


def _hdot(a, b, dn=NN):
    return lax.dot_general(a, b, dn, precision=lax.Precision.HIGHEST, preferred_element_type=F32)


def _sigmoid(x):
    return 1.0 / (1.0 + jnp.exp(-x))


def _silu(x):
    return x * _sigmoid(x)


def _softplus(x):
    return jnp.maximum(x, 0.0) + jnp.log(1.0 + jnp.exp(-jnp.abs(x)))


def _lane_pick(row, lane, idx):
    return jnp.sum(jnp.where(lane == idx, row, 0.0), axis=1, keepdims=True)


def _matmul(a, b, *, ta=False, tb=False, out_dtype=F32, add=None, name, tm_cap=1024, tn_cap=1024, tk_cap=2048, b_split=1,
            out_split=1):
    m, k = (a.shape[1], a.shape[0]) if ta else a.shape
    b_rows, b_cols = (b.shape[-2], b.shape[-1] * b_split)
    n = b_rows if tb else b_cols
    assert k == (b_cols if tb else b_rows), (a.shape, b.shape, ta, tb)
    tm = _pick(m, tm_cap)
    tn = _pick(n // max(1 if tb else b_split, out_split), tn_cap)
    tk = _pick(k // (b_split if tb else 1), tk_cap)
    nk = k // tk
    dn = (((0 if ta else 1,), (1 if tb else 0,)), ((), ()))

    def body(*refs):
        if add is None:
            a_ref, b_ref, o_ref, acc_ref = refs
            add_ref = None
        else:
            a_ref, b_ref, add_ref, o_ref, acc_ref = refs
        kk = pl.program_id(2)
        p = lax.dot_general(a_ref[...].astype(BF16), b_ref[...].astype(BF16), dn, preferred_element_type=F32)

        @pl.when(kk == 0)
        def _():
            acc_ref[...] = p

        @pl.when(kk > 0)
        def _():
            acc_ref[...] += p

        @pl.when(kk == nk - 1)
        def _():
            r = acc_ref[...]
            if add_ref is not None:
                r = r + add_ref[...].astype(F32)
            o_ref[...] = r.astype(o_ref.dtype)

    a_spec = pl.BlockSpec((tk, tm), lambda i, j, q: (q, i)) if ta else pl.BlockSpec((tm, tk), lambda i, j, q: (i, q))
    if b_split == 1:
        b_spec = pl.BlockSpec((tn, tk), lambda i, j, q: (j, q)) if tb else pl.BlockSpec((tk, tn), lambda i, j, q: (q, j))
    elif tb:
        per_b = k // b_split // tk
        b_spec = pl.BlockSpec((None, tn, tk), lambda i, j, q: (q // per_b, j, q % per_b))
    else:
        per_b = n // b_split // tn
        b_spec = pl.BlockSpec((None, tk, tn), lambda i, j, q: (j // per_b, q, j % per_b))
    add_spec = pl.BlockSpec((tm, tn), lambda i, j, q: (i, j))
    if out_split == 1:
        o_spec, o_shape = add_spec, (m, n)
    else:
        per_o = n // out_split // tn
        o_spec, o_shape = pl.BlockSpec((None, tm, tn), lambda i, j, q: (j // per_o, i, j % per_o)), (out_split, m, n // out_split)
    in_specs = [a_spec, b_spec] + ([add_spec] if add is not None else [])
    args = (a, b) + ((add,) if add is not None else ())
    return pl.pallas_call(
        body, name=name, grid=(m // tm, n // tn, nk), in_specs=in_specs, out_specs=o_spec,
        out_shape=jax.ShapeDtypeStruct(o_shape, out_dtype), scratch_shapes=[pltpu.VMEM((tm, tn), F32)],
        compiler_params=_params(("parallel", "parallel", "arbitrary")),
    )(*args)


def _rows(fn, row_args, full_args, row_outs, acc_outs, *, t, tm, name):
    n_row, n_full, n_ro = len(row_args), len(full_args), len(row_outs)

    def body(*refs):
        ins = [r[...] for r in refs[:n_row + n_full]]
        outs = fn(*ins)
        o_refs = refs[n_row + n_full:]
        for r, v in zip(o_refs[:n_ro], outs[:n_ro]):
            r[...] = v.astype(r.dtype)
        i = pl.program_id(0)
        for r, v in zip(o_refs[n_ro:], outs[n_ro:]):
            @pl.when(i == 0)
            def _(r=r, v=v):
                r[...] = v

            @pl.when(i > 0)
            def _(r=r, v=v):
                r[...] += v

    in_specs = [pl.BlockSpec((tm, w), functools.partial(lambda i, cb: (i, cb), cb=cb)) for (_, w, cb) in row_args]
    in_specs += [pl.BlockSpec(f.shape, lambda i: (0, 0)) for f in full_args]
    out_specs = [pl.BlockSpec((tm, w), lambda i: (i, 0)) for (w, _) in row_outs]
    out_specs += [pl.BlockSpec(s, lambda i: (0, 0)) for s in acc_outs]
    out_shape = [jax.ShapeDtypeStruct((t, w), d) for (w, d) in row_outs]
    out_shape += [jax.ShapeDtypeStruct(s, F32) for s in acc_outs]
    return pl.pallas_call(
        body, name=name, grid=(t // tm,), in_specs=in_specs, out_specs=out_specs, out_shape=out_shape,
        compiler_params=_params(("arbitrary",)),
    )(*[a for (a, _, _) in row_args], *full_args)


def _rms(x, g):
    return x * lax.rsqrt(jnp.mean(x * x, axis=-1, keepdims=True) + NORM_EPS) * g


def _rmsnorm_fwd(x, g, name):
    t, d = x.shape
    (h,) = _rows(lambda xb, gb: (_rms(xb, gb),), [(x, d, 0)], [g], [(d, BF16)], [], t=t, tm=_pick(t, 512, 8), name=name)
    return h


def _rmsnorm_bwd(x, g, dh, dx_in, name):
    t, d = x.shape

    def fn(xb, dhb, dxb, gb):
        _, vjp = jax.vjp(_rms, xb, gb)
        dx, dg = vjp(dhb)
        return dxb + dx, dg

    return _rows(fn, [(x, d, 0), (dh, d, 0), (dx_in, d, 0)], [g], [(d, F32)], [(1, d)], t=t, tm=_pick(t, 256, 8), name=name)


def _merge(yg, ys, lg, ls):
    return _sigmoid(lg) * yg + _sigmoid(ls) * ys


def _merge_fwd(y_gdn, y_swa, proj, gate_off, name):
    t, d = y_gdn.shape
    cb = gate_off // d
    (mix,) = _rows(lambda a, b, c, e: (_merge(a, b, c, e),), [(y_gdn, d, 0), (y_swa, d, 0), (proj, d, cb), (proj, d, cb + 1)], [],
                   [(d, BF16)], [], t=t, tm=_pick(t, 256, 8), name=name)
    return mix


def _merge_bwd(y_gdn, y_swa, proj, gate_off, dmix, name):
    t, d = y_gdn.shape
    cb = gate_off // d

    def fn(a, b, c, e, g):
        _, vjp = jax.vjp(_merge, a, b, c, e)
        da, db, dc, de = vjp(g)
        return da, db, jnp.concatenate([dc, de], axis=1)

    return _rows(fn, [(y_gdn, d, 0), (y_swa, d, 0), (proj, d, cb), (proj, d, cb + 1), (dmix, d, 0)], [],
                 [(d, BF16), (d, BF16), (2 * d, BF16)], [], t=t, tm=_pick(t, 128, 8), name=name)


def _relu2_fwd(up, name):
    t, f = up.shape
    (act,) = _rows(lambda u: (jnp.square(jnp.maximum(u, 0.0)),), [(up, f, 0)], [], [(f, BF16)], [], t=t, tm=_pick(t, 256, 8), name=name)
    return act


def _relu2_bwd(up, dact, name):
    t, f = up.shape
    (dup,) = _rows(lambda u, g: (g * 2.0 * jnp.maximum(u, 0.0),), [(up, f, 0), (dact, f, 0)], [], [(f, BF16)], [], t=t,
                   tm=_pick(t, 128, 8), name=name)
    return dup


def _loss_head(x, g, target, name):
    t, d = x.shape

    def loss_fn(xb, gb, tb):
        err = _rms(xb, gb) - tb
        return 0.5 * jnp.sum(jnp.mean(err * err, axis=-1, keepdims=True), axis=0, keepdims=True)

    def fn(xb, tb, gb):
        lv, vjp = jax.vjp(lambda a, b: loss_fn(a, b, tb), xb, gb)
        dx, dg = vjp(jnp.ones((1, 1), F32))
        return dx, dg, jnp.broadcast_to(lv, (1, LANES))

    return _rows(fn, [(x, d, 0), (target, d, 0)], [g], [(d, F32)], [(1, d), (1, LANES)], t=t, tm=_pick(t, 256, 8), name=name)


def _conv_silu(prev, cur, w, keep_prev):
    tm = cur.shape[0]
    xp = jnp.concatenate([prev * keep_prev, cur], axis=0)
    y = w[0:1, :] * xp[CONV_HALO - 3:CONV_HALO - 3 + tm]
    for j in range(1, CONV_K):
        y = y + w[j:j + 1, :] * xp[CONV_HALO - 3 + j:CONV_HALO - 3 + j + tm]
    return _silu(y)


def _conv_tiles(t, width):
    tm = _pick(t, 512, CONV_HALO)
    tc = _pick(width, 512)
    return tm, tc, t // tm, width // tc


def _conv_fwd(proj, conv_w, width, name):
    t = proj.shape[0]
    tm, tc, nt, ncw = _conv_tiles(t, width)
    hb = tm // CONV_HALO

    def body(prev_ref, cur_ref, w_ref, o_ref):
        keep = (pl.program_id(1) > 0).astype(F32)
        o_ref[0] = _conv_silu(prev_ref[...], cur_ref[...], w_ref[...], keep)

    return pl.pallas_call(
        body, name=name, grid=(3 * ncw, nt),
        in_specs=[pl.BlockSpec((CONV_HALO, tc), lambda j, i: (jnp.maximum(i * hb - 1, 0), j)),
                  pl.BlockSpec((tm, tc), lambda j, i: (i, j)),
                  pl.BlockSpec((CONV_K, tc), lambda j, i: (0, j))],
        out_specs=pl.BlockSpec((1, tm, tc), lambda j, i: (j // ncw, i, j % ncw)),
        out_shape=jax.ShapeDtypeStruct((3, t, width), F32),
        compiler_params=_params(("parallel", "arbitrary")),
    )(proj, proj, conv_w)


def _conv_bwd(proj, conv_w, dout, width, name):
    t = proj.shape[0]
    tm, tc, nt, ncw = _conv_tiles(t, width)
    hb = tm // CONV_HALO

    def body(prev_ref, cur_ref, w_ref, g_ref, dx_ref, dw_ref, carry_ref):
        s = pl.program_id(1)
        keep = (s < nt - 1).astype(F32)
        _, vjp = jax.vjp(lambda p, c, w: _conv_silu(p, c, w, keep), prev_ref[...], cur_ref[...], w_ref[...])
        dprev, dcur, dw = vjp(g_ref[0])

        @pl.when(s == 0)
        def _():
            carry_ref[...] = jnp.zeros_like(carry_ref)
            dw_ref[...] = dw

        @pl.when(s > 0)
        def _():
            dw_ref[...] += dw

        tail = jnp.concatenate([jnp.zeros((tm - CONV_HALO, tc), F32), carry_ref[...]], axis=0)
        dx_ref[...] = (dcur + tail).astype(dx_ref.dtype)
        carry_ref[...] = dprev

    def row(s):
        return nt - 1 - s

    return pl.pallas_call(
        body, name=name, grid=(3 * ncw, nt),
        in_specs=[pl.BlockSpec((CONV_HALO, tc), lambda j, s: (jnp.maximum(row(s) * hb - 1, 0), j)),
                  pl.BlockSpec((tm, tc), lambda j, s: (row(s), j)),
                  pl.BlockSpec((CONV_K, tc), lambda j, s: (0, j)),
                  pl.BlockSpec((1, tm, tc), lambda j, s: (j // ncw, row(s), j % ncw))],
        out_specs=[pl.BlockSpec((tm, tc), lambda j, s: (row(s), j)),
                   pl.BlockSpec((CONV_K, tc), lambda j, s: (0, j))],
        out_shape=[jax.ShapeDtypeStruct((t, 3 * width), BF16), jax.ShapeDtypeStruct((CONV_K, 3 * width), F32)],
        scratch_shapes=[pltpu.VMEM((CONV_HALO, tc), F32)],
        compiler_params=_params(("parallel", "arbitrary")),
    )(proj, proj, conv_w, dout)


def _inv_unit_lower_raw(a):
    n = a.shape[0]
    r = lax.broadcasted_iota(jnp.int32, (n, n), 0)
    c = lax.broadcasted_iota(jnp.int32, (n, n), 1)
    eye = (r == c).astype(F32)
    same = jnp.right_shift(r, 4) == jnp.right_shift(c, 4)
    dg = jnp.where(same, a, 0.0)
    lo = a - dg
    p = eye - dg
    q = dg
    for _ in range(3):
        q = _hdot(q, q)
        p = _hdot(p, eye + q)
    nm = _hdot(p, lo)
    n2 = _hdot(nm, nm)
    return _hdot(_hdot(eye - nm, eye + n2), p)


@jax.custom_vjp
def _inv_unit_lower(a):
    return _inv_unit_lower_raw(a)


def _inv_fwd(a):
    t = _inv_unit_lower_raw(a)
    return t, t


def _inv_bwd(t, g):
    return (-_hdot(_hdot(t, g, TN), t, NT),)


_inv_unit_lower.defvjp(_inv_fwd, _inv_bwd)


def _l2n(x):
    return x * lax.rsqrt(jnp.sum(x * x, axis=-1, keepdims=True) + NORM_EPS)


def _gdn_chunk(qc, kc, vc, z, bg, alog_row, dtb_row, gnorm, state, head, n_heads):
    cs = qc.shape[0]
    lane = lax.broadcasted_iota(jnp.int32, (1, LANES), 1)
    q = _l2n(qc) * (GDN_HEAD_DIM ** -0.5)
    k = _l2n(kc)
    beta = _sigmoid(_lane_pick(bg, lane, head))
    g = -jnp.exp(_lane_pick(alog_row, lane, head)) * _softplus(_lane_pick(bg, lane, n_heads + head) + _lane_pick(dtb_row, lane, head))
    r = lax.broadcasted_iota(jnp.int32, (cs, cs), 0)
    c = lax.broadcasted_iota(jnp.int32, (cs, cs), 1)
    g_row = jnp.sum(jnp.where(r == c, g, 0.0), axis=0, keepdims=True)
    dec_col = jnp.sum(jnp.where(r >= c, g_row, 0.0), axis=1, keepdims=True)
    dec_row = jnp.sum(jnp.where(r <= c, g, 0.0), axis=0, keepdims=True)
    gamma = jnp.exp(jnp.where(r >= c, dec_col - dec_row, -1e30))
    kb = k * beta
    a = jnp.where(r > c, mm_nt(kb, k) * gamma, 0.0)
    tinv = _inv_unit_lower(a)
    e_col = jnp.exp(dec_col)
    u = mm_nn(tinv, vc * beta)
    w = mm_nn(tinv, kb * e_col)
    qk = mm_nt(q, k) * gamma
    total = jnp.sum(g, axis=0, keepdims=True)
    v_new = u - mm_nn(w, state)
    o = mm_nn(q * e_col, state) + mm_nn(qk, v_new)
    new_state = state * jnp.exp(total) + mm_tn(k * jnp.exp(total - dec_col), v_new)
    y = _rms(o, gnorm) * _silu(z)
    return y, new_state


def _gdn_heads_per_step(n_heads):
    return 2 if n_heads % 2 == 0 else 1


def _gdn_fwd(qkvc, proj, alog_row, dtb_row, gnorm, *, d, z_off, bg_off, name):
    t = qkvc.shape[1]
    nh = d // GDN_HEAD_DIM
    hb = _gdn_heads_per_step(nh)
    wb = hb * GDN_HEAD_DIM
    nc = t // CHUNK

    def body(qkv_ref, z_ref, bg_ref, al_ref, dt_ref, gn_ref, y_ref, sin_ref, s_scr):
        n, hg = pl.program_id(0), pl.program_id(1)
        for i in range(hb):
            h = hg * hb + i
            sl = slice(i * GDN_HEAD_DIM, (i + 1) * GDN_HEAD_DIM)

            @pl.when(n == 0)
            def _(h=h):
                s_scr[h] = jnp.zeros((GDN_HEAD_DIM, GDN_HEAD_DIM), F32)

            state = s_scr[h]
            sin_ref[0, i] = state
            y, ns = _gdn_chunk(qkv_ref[0, :, sl], qkv_ref[1, :, sl], qkv_ref[2, :, sl], z_ref[:, sl], bg_ref[...], al_ref[...],
                               dt_ref[...], gn_ref[...], state, h, nh)
            y_ref[:, sl] = y.astype(y_ref.dtype)
            s_scr[h] = ns

    row = lambda n, hg: (0, 0)
    return pl.pallas_call(
        body, name=name, grid=(nc, nh // hb),
        in_specs=[pl.BlockSpec((3, CHUNK, wb), lambda n, hg: (0, n, hg)),
                  pl.BlockSpec((CHUNK, wb), lambda n, hg: (n, z_off // wb + hg)),
                  pl.BlockSpec((CHUNK, LANES), lambda n, hg: (n, bg_off // LANES)),
                  pl.BlockSpec((1, LANES), row), pl.BlockSpec((1, LANES), row), pl.BlockSpec((1, LANES), row)],
        out_specs=[pl.BlockSpec((CHUNK, wb), lambda n, hg: (n, hg)),
                   pl.BlockSpec((1, hb, GDN_HEAD_DIM, GDN_HEAD_DIM), lambda n, hg: (n, hg, 0, 0))],
        out_shape=[jax.ShapeDtypeStruct((t, d), BF16), jax.ShapeDtypeStruct((nc, nh, GDN_HEAD_DIM, GDN_HEAD_DIM), F32)],
        scratch_shapes=[pltpu.VMEM((nh, GDN_HEAD_DIM, GDN_HEAD_DIM), F32)],
        compiler_params=_params(("arbitrary", "arbitrary")),
    )(qkvc, proj, proj, alog_row, dtb_row, gnorm)


def _gdn_bwd(qkvc, proj, alog_row, dtb_row, gnorm, states, dy, *, d, z_off, bg_off, name):
    t = qkvc.shape[1]
    nh = d // GDN_HEAD_DIM
    hb = _gdn_heads_per_step(nh)
    wb = hb * GDN_HEAD_DIM
    nc = t // CHUNK
    ng = nh // hb

    def body(qkv_ref, z_ref, bg_ref, al_ref, dt_ref, gn_ref, sin_ref, dy_ref, dqkv_ref, dz_ref, dbg_ref, dal_ref, ddt_ref, dgn_ref,
             ds_scr):
        s, hg = pl.program_id(0), pl.program_id(1)

        @pl.when(hg == 0)
        def _():
            dbg_ref[...] = jnp.zeros_like(dbg_ref)

        @pl.when((s == 0) & (hg == 0))
        def _():
            dal_ref[...] = jnp.zeros_like(dal_ref)
            ddt_ref[...] = jnp.zeros_like(ddt_ref)
            dgn_ref[...] = jnp.zeros_like(dgn_ref)

        for i in range(hb):
            h = hg * hb + i
            sl = slice(i * GDN_HEAD_DIM, (i + 1) * GDN_HEAD_DIM)

            @pl.when(s == 0)
            def _(h=h):
                ds_scr[h] = jnp.zeros((GDN_HEAD_DIM, GDN_HEAD_DIM), F32)

            fn = functools.partial(_gdn_chunk, head=h, n_heads=nh)
            _, vjp = jax.vjp(fn, qkv_ref[0, :, sl], qkv_ref[1, :, sl], qkv_ref[2, :, sl], z_ref[:, sl], bg_ref[...], al_ref[...],
                             dt_ref[...], gn_ref[...], sin_ref[0, i])
            dq, dk, dv, dz, dbg, dal, ddt, dgn, dstate = vjp((dy_ref[:, sl], ds_scr[h]))
            dqkv_ref[0, :, sl] = dq
            dqkv_ref[1, :, sl] = dk
            dqkv_ref[2, :, sl] = dv
            dz_ref[:, sl] = dz.astype(dz_ref.dtype)
            dbg_ref[...] += dbg
            dal_ref[...] += dal
            ddt_ref[...] += ddt
            dgn_ref[...] += dgn
            ds_scr[h] = dstate

    def ch(s):
        return nc - 1 - s

    row = lambda s, hg: (0, 0)
    return pl.pallas_call(
        body, name=name, grid=(nc, ng),
        in_specs=[pl.BlockSpec((3, CHUNK, wb), lambda s, hg: (0, ch(s), hg)),
                  pl.BlockSpec((CHUNK, wb), lambda s, hg: (ch(s), z_off // wb + hg)),
                  pl.BlockSpec((CHUNK, LANES), lambda s, hg: (ch(s), bg_off // LANES)),
                  pl.BlockSpec((1, LANES), row), pl.BlockSpec((1, LANES), row), pl.BlockSpec((1, LANES), row),
                  pl.BlockSpec((1, hb, GDN_HEAD_DIM, GDN_HEAD_DIM), lambda s, hg: (ch(s), hg, 0, 0)),
                  pl.BlockSpec((CHUNK, wb), lambda s, hg: (ch(s), hg))],
        out_specs=[pl.BlockSpec((3, CHUNK, wb), lambda s, hg: (0, ch(s), hg)),
                   pl.BlockSpec((CHUNK, wb), lambda s, hg: (ch(s), hg)),
                   pl.BlockSpec((CHUNK, LANES), lambda s, hg: (ch(s), 0)),
                   pl.BlockSpec((1, LANES), row), pl.BlockSpec((1, LANES), row), pl.BlockSpec((1, LANES), row)],
        out_shape=[jax.ShapeDtypeStruct((3, t, d), F32), jax.ShapeDtypeStruct((t, d), BF16), jax.ShapeDtypeStruct((t, LANES), F32),
                   jax.ShapeDtypeStruct((1, LANES), F32), jax.ShapeDtypeStruct((1, LANES), F32), jax.ShapeDtypeStruct((1, LANES), F32)],
        scratch_shapes=[pltpu.VMEM((nh, GDN_HEAD_DIM, GDN_HEAD_DIM), F32)],
        compiler_params=_params(("arbitrary", "arbitrary")),
    )(qkvc, proj, proj, alog_row, dtb_row, gnorm, states, dy)


@jax.custom_vjp
def _swap_halves(x):
    return pltpu.roll(x, SWA_HEAD_DIM, 1)


_swap_halves.defvjp(lambda x: (pltpu.roll(x, SWA_HEAD_DIM, 1), None), lambda _, g: (pltpu.roll(g, SWA_HEAD_DIM, 1),))

SWA_PAIR_Q = 2 * GQA_GROUP * SWA_HEAD_DIM


def _swa_block(q, kp, kc, vp, vc, sink_row, slope_row, keep_prev, pair):
    kb = jnp.concatenate([kp, kc], axis=0)
    vb = jnp.concatenate([vp, vc], axis=0)
    lane = lax.broadcasted_iota(jnp.int32, (1, LANES), 1)
    low = lane < SWA_HEAD_DIM
    qi = lax.broadcasted_iota(jnp.int32, (WINDOW, 2 * WINDOW), 0)
    sj = lax.broadcasted_iota(jnp.int32, (WINDOW, 2 * WINDOW), 1)
    dist = qi + WINDOW - sj
    valid = (dist >= 0) & (dist < WINDOW) & ((sj >= WINDOW) | (keep_prev > 0.5))
    distf = dist.astype(F32)
    halves = []
    for kv in range(2):
        mine = low if kv == 0 else jnp.logical_not(low)
        kk = jnp.where(mine, kb, 0.0)
        kk = kk + _swap_halves(kk)
        vv = jnp.where(mine, vb, 0.0)
        vv = vv + _swap_halves(vv)
        for gi in range(GQA_GROUP):
            hl = kv * GQA_GROUP + gi
            qp = q[:, (hl // 2) * LANES:(hl // 2 + 1) * LANES]
            half = low if hl % 2 == 0 else jnp.logical_not(low)
            hq = pair * (2 * GQA_GROUP) + hl
            slope = _lane_pick(slope_row, lane, hq)
            sink = _lane_pick(sink_row, lane, hq)
            sc = mm_nt(jnp.where(half, qp, 0.0), kk) * (SWA_HEAD_DIM ** -0.5)
            sc = jnp.where(valid, sc - slope * distf, -1e30)
            m = lax.stop_gradient(jnp.maximum(jnp.max(sc, axis=-1, keepdims=True), sink))
            p = jnp.exp(sc - m)
            probs = p / (jnp.sum(p, axis=-1, keepdims=True) + jnp.exp(sink - m))
            halves.append(jnp.where(half, mm_nn(probs, vv), 0.0))
    return jnp.concatenate([halves[2 * i] + halves[2 * i + 1] for i in range(GQA_GROUP)], axis=1)


def _swa_specs(t, q_off, k_off, v_off, order):
    nb = t // WINDOW

    def blk(s):
        return order(s, nb)

    return nb, [pl.BlockSpec((WINDOW, SWA_PAIR_Q), lambda p, s: (blk(s), q_off // SWA_PAIR_Q + p)),
                pl.BlockSpec((WINDOW, LANES), lambda p, s: (jnp.maximum(blk(s) - 1, 0), k_off // LANES + p)),
                pl.BlockSpec((WINDOW, LANES), lambda p, s: (blk(s), k_off // LANES + p)),
                pl.BlockSpec((WINDOW, LANES), lambda p, s: (jnp.maximum(blk(s) - 1, 0), v_off // LANES + p)),
                pl.BlockSpec((WINDOW, LANES), lambda p, s: (blk(s), v_off // LANES + p)),
                pl.BlockSpec((1, LANES), lambda p, s: (0, 0)), pl.BlockSpec((1, LANES), lambda p, s: (0, 0))]


def _swa_fwd(proj, sink_row, slope_row, *, d, q_off, k_off, v_off, name):
    t = proj.shape[0]
    n_pairs = d // SWA_PAIR_Q
    nb, in_specs = _swa_specs(t, q_off, k_off, v_off, lambda s, nb: s)

    def body(q_ref, kp_ref, kc_ref, vp_ref, vc_ref, sink_ref, slope_ref, o_ref):
        keep = (pl.program_id(1) > 0).astype(F32)
        o = _swa_block(q_ref[...], kp_ref[...], kc_ref[...], vp_ref[...], vc_ref[...], sink_ref[...], slope_ref[...], keep,
                       pl.program_id(0))
        o_ref[...] = o.astype(o_ref.dtype)

    return pl.pallas_call(
        body, name=name, grid=(n_pairs, nb), in_specs=in_specs,
        out_specs=pl.BlockSpec((WINDOW, SWA_PAIR_Q), lambda p, s: (s, p)),
        out_shape=jax.ShapeDtypeStruct((t, d), BF16),
        compiler_params=_params(("parallel", "arbitrary")),
    )(proj, proj, proj, proj, proj, sink_row, slope_row)


def _swa_bwd(proj, sink_row, slope_row, do, *, d, q_off, k_off, v_off, name):
    t = proj.shape[0]
    n_pairs = d // SWA_PAIR_Q
    nb, in_specs = _swa_specs(t, q_off, k_off, v_off, lambda s, nb: nb - 1 - s)

    def body(q_ref, kp_ref, kc_ref, vp_ref, vc_ref, sink_ref, slope_ref, do_ref, dq_ref, dk_ref, dv_ref, dsink_ref, ck_ref, cv_ref):
        p, s = pl.program_id(0), pl.program_id(1)
        keep = (s < nb - 1).astype(F32)
        fn = functools.partial(_swa_block, slope_row=slope_ref[...], keep_prev=keep, pair=p)
        _, vjp = jax.vjp(fn, q_ref[...], kp_ref[...], kc_ref[...], vp_ref[...], vc_ref[...], sink_ref[...])
        dq, dkp, dkc, dvp, dvc, dsink = vjp(do_ref[...])

        @pl.when(s == 0)
        def _():
            ck_ref[...] = jnp.zeros_like(ck_ref)
            cv_ref[...] = jnp.zeros_like(cv_ref)

        @pl.when((s == 0) & (p == 0))
        def _():
            dsink_ref[...] = jnp.zeros_like(dsink_ref)

        dq_ref[...] = dq.astype(dq_ref.dtype)
        dk_ref[...] = (dkc + ck_ref[...]).astype(dk_ref.dtype)
        dv_ref[...] = (dvc + cv_ref[...]).astype(dv_ref.dtype)
        ck_ref[...] = dkp
        cv_ref[...] = dvp
        dsink_ref[...] += dsink

    in_specs = in_specs + [pl.BlockSpec((WINDOW, SWA_PAIR_Q), lambda p, s: (nb - 1 - s, p))]
    kv_w = d // GQA_GROUP
    return pl.pallas_call(
        body, name=name, grid=(n_pairs, nb), in_specs=in_specs,
        out_specs=[pl.BlockSpec((WINDOW, SWA_PAIR_Q), lambda p, s: (nb - 1 - s, p)),
                   pl.BlockSpec((WINDOW, LANES), lambda p, s: (nb - 1 - s, p)),
                   pl.BlockSpec((WINDOW, LANES), lambda p, s: (nb - 1 - s, p)),
                   pl.BlockSpec((1, LANES), lambda p, s: (0, 0))],
        out_shape=[jax.ShapeDtypeStruct((t, d), BF16), jax.ShapeDtypeStruct((t, kv_w), BF16), jax.ShapeDtypeStruct((t, kv_w), BF16),
                   jax.ShapeDtypeStruct((1, LANES), F32)],
        scratch_shapes=[pltpu.VMEM((WINDOW, LANES), F32), pltpu.VMEM((WINDOW, LANES), F32)],
        compiler_params=_params(("arbitrary", "arbitrary")),
    )(proj, proj, proj, proj, proj, sink_row, slope_row, do)


def _layout(d):
    kv = d // GQA_GROUP
    return dict(z=3 * d, q=4 * d, gate=5 * d, k=7 * d, v=7 * d + kv, bg=7 * d + 2 * kv, width=7 * d + 2 * kv + LANES)


def _pack_w_in(w, d):
    nh = d // GDN_HEAD_DIM
    kv = d // GQA_GROUP
    o = 4 * d + 2 * nh
    parts = [w[..., :4 * d], w[..., o:o + d], w[..., o + d + 2 * kv:o + 3 * d + 2 * kv], w[..., o + d:o + d + 2 * kv],
             w[..., 4 * d:o], jnp.zeros(w.shape[:-1] + (LANES - 2 * nh,), w.dtype)]
    return jnp.concatenate(parts, axis=-1)


def _unpack_w_in(wp, d):
    nh = d // GDN_HEAD_DIM
    kv = d // GQA_GROUP
    lay = _layout(d)
    parts = [wp[..., :4 * d], wp[..., lay["bg"]:lay["bg"] + 2 * nh], wp[..., lay["q"]:lay["q"] + d],
             wp[..., lay["k"]:lay["k"] + 2 * kv], wp[..., lay["gate"]:lay["gate"] + 2 * d]]
    return jnp.concatenate(parts, axis=-1)


def _pad_row(v):
    return jnp.pad(v.astype(F32), (0, LANES - v.shape[0]))[None, :]


def _alibi_row(d):
    nq = d // SWA_HEAD_DIM
    return _pad_row(2.0 ** (-8.0 * jnp.arange(1, nq + 1, dtype=F32) / nq))


def _layer_fwd(x, p, tag):
    t, d = x.shape
    lay = _layout(d)
    tn = 1152 if lay["width"] % 1152 == 0 else 1024
    h1 = _rmsnorm_fwd(x, p["norm1_g"], tag + "rms1")
    proj = _matmul(h1, p["w_in"], name=tag + "mm_in", tn_cap=tn)
    qkvc = _conv_fwd(proj, p["conv_w"], d, tag + "conv")
    gdn_o, states = _gdn_fwd(qkvc, proj, p["a_log"], p["dt_bias"], p["gdn_norm_g"], d=d, z_off=lay["z"], bg_off=lay["bg"],
                             name=tag + "gdn")
    swa_o = _swa_fwd(proj, p["attn_sinks"], p["alibi"], d=d, q_off=lay["q"], k_off=lay["k"], v_off=lay["v"], name=tag + "swa")
    y_gdn = _matmul(gdn_o, p["w_branch_gdn"], name=tag + "mm_bg")
    y_swa = _matmul(swa_o, p["w_branch_swa"], name=tag + "mm_bs")
    mix = _merge_fwd(y_gdn, y_swa, proj, lay["gate"], tag + "merge")
    x1 = _matmul(mix, p["w_out"], add=x, name=tag + "mm_out")
    h2 = _rmsnorm_fwd(x1, p["norm2_g"], tag + "rms2")
    up = _matmul(h2, p["w_ff_up"], name=tag + "mm_up", b_split=N_CHIPS)
    act = _relu2_fwd(up, tag + "relu2")
    x2 = _matmul(act, p["w_ff_down"], add=x1, name=tag + "mm_down")
    return x2, dict(x=x, h1=h1, proj=proj, qkvc=qkvc, states=states, gdn_o=gdn_o, swa_o=swa_o, y_gdn=y_gdn, y_swa=y_swa, mix=mix,
                    x1=x1, h2=h2, up=up, act=act)


def _layer_bwd(dx2, p, s, tag):
    t, d = dx2.shape
    lay = _layout(d)
    tn = 1152 if lay["width"] % 1152 == 0 else 1024
    nh = d // GDN_HEAD_DIM
    g = {}
    dact = _matmul(dx2, p["w_ff_down"], tb=True, name=tag + "mm_dact", tm_cap=512)
    g["w_ff_down"] = _matmul(s["act"], dx2, ta=True, out_dtype=BF16, name=tag + "mm_dwdown", tk_cap=1024)
    dup = _relu2_bwd(s["up"], dact, tag + "relu2b")
    g["w_ff_up"] = _matmul(s["h2"], dup, ta=True, out_dtype=BF16, name=tag + "mm_dwup", out_split=N_CHIPS)
    dh2 = _matmul(dup, p["w_ff_up"], tb=True, name=tag + "mm_dh2", b_split=N_CHIPS)
    dx1, g["norm2_g"] = _rmsnorm_bwd(s["x1"], p["norm2_g"], dh2, dx2, tag + "rms2b")
    dmix = _matmul(dx1, p["w_out"], tb=True, name=tag + "mm_dmix", tm_cap=512)
    g["w_out"] = _matmul(s["mix"], dx1, ta=True, out_dtype=BF16, name=tag + "mm_dwout", tk_cap=1024)
    dyg, dys, dgl = _merge_bwd(s["y_gdn"], s["y_swa"], s["proj"], lay["gate"], dmix, tag + "mergeb")
    g["w_branch_gdn"] = _matmul(s["gdn_o"], dyg, ta=True, out_dtype=BF16, name=tag + "mm_dwbg")
    g["w_branch_swa"] = _matmul(s["swa_o"], dys, ta=True, out_dtype=BF16, name=tag + "mm_dwbs")
    dgdn_o = _matmul(dyg, p["w_branch_gdn"], tb=True, name=tag + "mm_dgdn")
    dswa_o = _matmul(dys, p["w_branch_swa"], tb=True, name=tag + "mm_dswa")
    dq_s, dk_s, dv_s, dsink = _swa_bwd(s["proj"], p["attn_sinks"], p["alibi"], dswa_o, d=d, q_off=lay["q"], k_off=lay["k"],
                                       v_off=lay["v"], name=tag + "swab")
    dqkvc, dz, dbg, dal, ddt, dgn = _gdn_bwd(s["qkvc"], s["proj"], p["a_log"], p["dt_bias"], p["gdn_norm_g"], s["states"], dgdn_o,
                                             d=d, z_off=lay["z"], bg_off=lay["bg"], name=tag + "gdnb")
    dqkv, g["conv_w"] = _conv_bwd(s["proj"], p["conv_w"], dqkvc, d, tag + "convb")
    dproj = jnp.concatenate([dqkv, dz, dq_s, dgl, dk_s, dv_s, lax.reduce_precision(dbg, 8, 7).astype(BF16)], axis=1)
    g["w_in"] = _matmul(s["h1"], dproj, ta=True, out_dtype=BF16, name=tag + "mm_dwin", tn_cap=tn)
    dh1 = _matmul(dproj, p["w_in"], tb=True, name=tag + "mm_dh1", tk_cap=tn)
    dx, g["norm1_g"] = _rmsnorm_bwd(s["x"], p["norm1_g"], dh1, dx1, tag + "rms1b")
    g["a_log"], g["dt_bias"], g["gdn_norm_g"], g["attn_sinks"] = dal[0, :nh], ddt[0, :nh], dgn[0], dsink[0, :d // SWA_HEAD_DIM]
    return dx, g


MESH = pl.DeviceIdType.MESH
HBM_SPEC = pl.BlockSpec(memory_space=pl.ANY)


def _place():
    x, y, c = lax.axis_index("x"), lax.axis_index("y"), lax.axis_index("c")
    return x, y, c, 2 * x + y


def _flip(x, y, k):
    px, py = x ^ (k >> 1), y ^ (k & 1)
    return px, py, 2 * px + py


def _gather_weights(shards, name):
    n = len(shards)

    def body(*refs):
        ins, outs = refs[:n], refs[n:2 * n]
        send_sems, recv_sems, fsend_sems, frecv_sems, local_sems = refs[2 * n:]
        x, y, c, ci = _place()
        waits = []
        for a in range(n):
            hr = shards[a].shape[0] // 2
            mine = pl.ds(c * hr, hr)
            lc = pltpu.make_async_copy(ins[a], outs[a].at[ci], local_sems.at[a])
            lc.start()
            waits.append(lc.wait)
            for k in (1, 2, 3):
                px, py, _ = _flip(x, y, k)
                cp = pltpu.make_async_remote_copy(src_ref=ins[a].at[mine], dst_ref=outs[a].at[ci, mine], send_sem=send_sems.at[a, k - 1],
                                                  recv_sem=recv_sems.at[a, k - 1], device_id=(px, py, c), device_id_type=MESH)
                cp.start()
                waits.append(cp.wait_send)
        for a in range(n):
            hr = shards[a].shape[0] // 2
            mine = pl.ds(c * hr, hr)
            for k in (1, 2, 3):
                px, py, pj = _flip(x, y, k)
                landed = outs[a].at[pj, mine]
                pltpu.make_async_remote_copy(src_ref=landed, dst_ref=landed, send_sem=send_sems.at[a, k - 1], recv_sem=recv_sems.at[a, k - 1],
                                             device_id=(px, py, c), device_id_type=MESH).wait_recv()
                fw = pltpu.make_async_remote_copy(src_ref=landed, dst_ref=landed, send_sem=fsend_sems.at[a, k - 1],
                                                  recv_sem=frecv_sems.at[a, k - 1], device_id=(x, y, 1 - c), device_id_type=MESH)
                fw.start()
                waits.append(fw.wait_send)
        for a in range(n):
            hr = shards[a].shape[0] // 2
            theirs = pl.ds((1 - c) * hr, hr)
            for k in (1, 2, 3):
                _, _, pj = _flip(x, y, k)
                passed = outs[a].at[pj, theirs]
                pltpu.make_async_remote_copy(src_ref=passed, dst_ref=passed, send_sem=fsend_sems.at[a, k - 1], recv_sem=frecv_sems.at[a, k - 1],
                                             device_id=(x, y, 1 - c), device_id_type=MESH).wait_recv()
        for w in waits:
            w()

    return pl.pallas_call(
        body, name=name, in_specs=[HBM_SPEC] * n, out_specs=[HBM_SPEC] * n,
        out_shape=[jax.ShapeDtypeStruct((N_CHIPS,) + s.shape, s.dtype) for s in shards],
        scratch_shapes=[pltpu.SemaphoreType.DMA((n, 3))] * 4 + [pltpu.SemaphoreType.DMA((n,))],
    )(*shards)


def _swap_with_sibling(gs, name):
    n = len(gs)

    def body(*refs):
        ins, outs = refs[:n], refs[n:2 * n]
        send_sems, recv_sems = refs[2 * n:]
        x, y, c, _ = _place()
        cps = []
        for a in range(n):
            hr = gs[a].shape[1] // 2
            cp = pltpu.make_async_remote_copy(src_ref=ins[a].at[:, pl.ds((1 - c) * hr, hr)], dst_ref=outs[a], send_sem=send_sems.at[a],
                                              recv_sem=recv_sems.at[a], device_id=(x, y, 1 - c), device_id_type=MESH)
            cp.start()
            cps.append(cp)
        for cp in cps:
            cp.wait()

    return pl.pallas_call(
        body, name=name, in_specs=[HBM_SPEC] * n, out_specs=[HBM_SPEC] * n,
        out_shape=[jax.ShapeDtypeStruct((g.shape[0], g.shape[1] // 2, g.shape[2]), g.dtype) for g in gs],
        scratch_shapes=[pltpu.SemaphoreType.DMA((n,))] * 2,
    )(*gs)


def _scatter_to_chips(hs, name):
    n = len(hs)

    def body(*refs):
        ins, outs = refs[:n], refs[n:2 * n]
        send_sems, recv_sems, local_sems = refs[2 * n:]
        x, y, c, ci = _place()
        waits = []
        for a in range(n):
            lc = pltpu.make_async_copy(ins[a].at[ci], outs[a].at[ci], local_sems.at[a])
            lc.start()
            waits.append(lc.wait)
            for k in (1, 2, 3):
                px, py, pj = _flip(x, y, k)
                cp = pltpu.make_async_remote_copy(src_ref=ins[a].at[pj], dst_ref=outs[a].at[ci], send_sem=send_sems.at[a, k - 1],
                                                  recv_sem=recv_sems.at[a, k - 1], device_id=(px, py, c), device_id_type=MESH)
                cp.start()
                waits.append(cp.wait_send)
                got = outs[a].at[pj]
                waits.append(pltpu.make_async_remote_copy(src_ref=got, dst_ref=got, send_sem=send_sems.at[a, k - 1],
                                                          recv_sem=recv_sems.at[a, k - 1], device_id=(px, py, c),
                                                          device_id_type=MESH).wait_recv)
        for w in waits:
            w()

    return pl.pallas_call(
        body, name=name, in_specs=[HBM_SPEC] * n, out_specs=[HBM_SPEC] * n,
        out_shape=[jax.ShapeDtypeStruct(h.shape, h.dtype) for h in hs],
        scratch_shapes=[pltpu.SemaphoreType.DMA((n, 3))] * 2 + [pltpu.SemaphoreType.DMA((n,))],
    )(*hs)


def _share_with_sibling(rs, name):
    n = len(rs)

    def body(*refs):
        ins, outs = refs[:n], refs[n:2 * n]
        send_sems, recv_sems, local_sems = refs[2 * n:]
        x, y, c, _ = _place()
        waits = []
        for a in range(n):
            lc = pltpu.make_async_copy(ins[a], outs[a].at[c], local_sems.at[a])
            lc.start()
            waits.append(lc.wait)
            cp = pltpu.make_async_remote_copy(src_ref=ins[a], dst_ref=outs[a].at[c], send_sem=send_sems.at[a], recv_sem=recv_sems.at[a],
                                              device_id=(x, y, 1 - c), device_id_type=MESH)
            cp.start()
            waits.append(cp.wait_send)
            got = outs[a].at[1 - c]
            waits.append(pltpu.make_async_remote_copy(src_ref=got, dst_ref=got, send_sem=send_sems.at[a], recv_sem=recv_sems.at[a],
                                                      device_id=(x, y, 1 - c), device_id_type=MESH).wait_recv)
        for w in waits:
            w()

    return pl.pallas_call(
        body, name=name, in_specs=[HBM_SPEC] * n, out_specs=[HBM_SPEC] * n,
        out_shape=[jax.ShapeDtypeStruct((2,) + r.shape, r.dtype) for r in rs],
        scratch_shapes=[pltpu.SemaphoreType.DMA((n,))] * 3,
    )(*rs)


def _add_sibling_half(g, got, core, name):
    nc, r, cols = g.shape
    hr = r // 2
    tm = _pick(hr, 256, 16)

    def body(core_ref, g_ref, o_ref, s_ref):
        s_ref[...] = (g_ref[...].astype(F32) + o_ref[...].astype(F32)).astype(s_ref.dtype)

    return pl.pallas_call(
        body, name=name,
        grid_spec=pltpu.PrefetchScalarGridSpec(
            num_scalar_prefetch=1, grid=(nc, hr // tm),
            in_specs=[pl.BlockSpec((None, None, tm, cols), lambda j, i, cr: (j, cr[0], i, 0)),
                      pl.BlockSpec((None, tm, cols), lambda j, i, cr: (j, i, 0))],
            out_specs=pl.BlockSpec((None, tm, cols), lambda j, i, cr: (j, i, 0))),
        out_shape=jax.ShapeDtypeStruct((nc, hr, cols), g.dtype),
        compiler_params=_params(("parallel", "parallel")),
    )(core, g.reshape(nc, 2, hr, cols), got)


def _sum_chips(parts, name):
    nc, r, cols = parts.shape
    tm = _pick(r, 256, 16)

    def body(p_ref, o_ref):
        acc = p_ref[0].astype(F32)
        for j in range(1, nc):
            acc = acc + p_ref[j].astype(F32)
        o_ref[...] = acc

    return pl.pallas_call(
        body, name=name, grid=(r // tm,), in_specs=[pl.BlockSpec((nc, tm, cols), lambda i: (0, i, 0))],
        out_specs=pl.BlockSpec((tm, cols), lambda i: (i, 0)), out_shape=jax.ShapeDtypeStruct((r, cols), F32),
        compiler_params=_params(("parallel",)),
    )(parts)


def _reduce_scatter(gs, core, tag):
    got = _swap_with_sibling(gs, tag + "rs_swap")
    hs = [_add_sibling_half(g, o, core, tag + "rs_add%d" % i) for i, (g, o) in enumerate(zip(gs, got))]
    parts = _scatter_to_chips(hs, tag + "rs_scatter")
    rs = [_sum_chips(p, tag + "rs_sum%d" % i) for i, p in enumerate(parts)]
    both = _share_with_sibling(rs, tag + "rs_share")
    return [b.reshape(2 * b.shape[1], b.shape[2]) for b in both]


def _allreduce_small(v, name):
    rows = v.shape[0]

    def body(v_ref, o_ref, buf, send_sems, recv_sems, local_sem):
        x, y, c, _ = _place()
        me, sibling = (x, y, c), (x, y, 1 - c)
        chips = [_flip(x, y, k)[:2] for k in (1, 2, 3)]

        def slot(px, py, pc):
            return buf.at[4 * px + 2 * py + pc]

        def copy(k, block, to, src=None):
            return pltpu.make_async_remote_copy(src_ref=slot(*block) if src is None else src, dst_ref=slot(*block), send_sem=send_sems.at[k],
                                                recv_sem=recv_sems.at[k], device_id=to, device_id_type=MESH)

        mine = pltpu.make_async_copy(v_ref, slot(*me), local_sem)
        mine.start()
        first = [copy(0, me, sibling, src=v_ref)] + [copy(1 + j, me, (*chip, c), src=v_ref) for j, chip in enumerate(chips)]
        for cp in first:
            cp.start()
        passed = [copy(4 + j, (*chip, c), sibling) for j, chip in enumerate(chips)]
        for j, chip in enumerate(chips):
            copy(1 + j, (*chip, c), me).wait_recv()
            passed[j].start()
        copy(0, sibling, me).wait_recv()
        for j, chip in enumerate(chips):
            copy(4 + j, (*chip, 1 - c), me).wait_recv()
        for cp in first + passed:
            cp.wait_send()
        mine.wait()
        acc = buf[0]
        for i in range(1, 2 * N_CHIPS):
            acc = acc + buf[i]
        o_ref[...] = acc

    vm = pl.BlockSpec(memory_space=pltpu.VMEM)
    return pl.pallas_call(
        body, name=name, in_specs=[vm], out_specs=vm, out_shape=jax.ShapeDtypeStruct(v.shape, F32),
        scratch_shapes=[pltpu.VMEM((2 * N_CHIPS, rows, LANES), F32), pltpu.SemaphoreType.DMA((7,)), pltpu.SemaphoreType.DMA((7,)),
                        pltpu.SemaphoreType.DMA],
        compiler_params=pltpu.CompilerParams(vmem_limit_bytes=VMEM_LIMIT),
    )(v)


def _adamw(w, g, m, v, name):
    r, cols = w.shape
    tm = _pick(r, max(8, (1 << 18) // max(cols, 1) // 8 * 8), 8)

    def body(w_ref, g_ref, m_ref, v_ref, d_ref, nm_ref, nv_ref):
        gg = g_ref[...]
        nm = ADAM_B1 * m_ref[...] + (1.0 - ADAM_B1) * gg
        nv = ADAM_B2 * v_ref[...] + (1.0 - ADAM_B2) * jnp.square(gg)
        m_hat = nm / (1.0 - ADAM_B1 ** ADAM_STEP)
        v_hat = nv / (1.0 - ADAM_B2 ** ADAM_STEP)
        d_ref[...] = -ADAM_LR * (m_hat / (jnp.sqrt(v_hat) + ADAM_EPS) + ADAM_WD * w_ref[...])
        nm_ref[...] = nm
        nv_ref[...] = nv

    spec = pl.BlockSpec((tm, cols), lambda i: (i, 0))
    return pl.pallas_call(
        body, name=name, grid=(r // tm,), in_specs=[spec] * 4, out_specs=[spec] * 3,
        out_shape=[jax.ShapeDtypeStruct((r, cols), F32)] * 3, compiler_params=_params(("parallel",)),
    )(w, g, m, v)


def _adamw_nd(w, g, m, v, name):
    shape = w.shape
    two = (1, shape[0]) if len(shape) == 1 else (int(np.prod(shape[:-1])), shape[-1])
    outs = _adamw(w.reshape(two), g.reshape(two), m.reshape(two), v.reshape(two), name)
    return [o.reshape(shape) for o in outs]


WEIGHTS = ("norm1_g", "w_in", "conv_w", "a_log", "dt_bias", "gdn_norm_g", "attn_sinks", "w_branch_gdn", "w_branch_swa", "w_out",
           "norm2_g", "w_ff_up", "w_ff_down", "final_norm_g")
MATRICES = ("w_in", "w_branch_gdn", "w_branch_swa", "w_out", "w_ff_up", "w_ff_down")


def _to_rows(vec):
    n = vec.shape[0]
    rows = -(-n // (8 * LANES)) * 8
    return jnp.pad(vec, (0, rows * LANES - n)).reshape(rows, LANES)


def kernel(x, norm1_g, w_in, conv_w, a_log, dt_bias, gdn_norm_g, attn_sinks, w_branch_gdn, w_branch_swa, w_out, norm2_g, w_ff_up, w_ff_down, final_norm_g, loss_target, m_norm1_g, m_w_in, m_conv_w, m_a_log, m_dt_bias, m_gdn_norm_g, m_attn_sinks, m_w_branch_gdn, m_w_branch_swa, m_w_out, m_norm2_g, m_w_ff_up, m_w_ff_down, m_final_norm_g, v_norm1_g, v_w_in, v_conv_w, v_a_log, v_dt_bias, v_gdn_norm_g, v_attn_sinks, v_w_branch_gdn, v_w_branch_swa, v_w_out, v_norm2_g, v_w_ff_up, v_w_ff_down, v_final_norm_g):
    w = dict(norm1_g=norm1_g, w_in=w_in, conv_w=conv_w, a_log=a_log, dt_bias=dt_bias, gdn_norm_g=gdn_norm_g, attn_sinks=attn_sinks,
             w_branch_gdn=w_branch_gdn, w_branch_swa=w_branch_swa, w_out=w_out, norm2_g=norm2_g, w_ff_up=w_ff_up, w_ff_down=w_ff_down,
             final_norm_g=final_norm_g)
    mom = dict(norm1_g=m_norm1_g, w_in=m_w_in, conv_w=m_conv_w, a_log=m_a_log, dt_bias=m_dt_bias, gdn_norm_g=m_gdn_norm_g,
               attn_sinks=m_attn_sinks, w_branch_gdn=m_w_branch_gdn, w_branch_swa=m_w_branch_swa, w_out=m_w_out, norm2_g=m_norm2_g,
               w_ff_up=m_w_ff_up, w_ff_down=m_w_ff_down, final_norm_g=m_final_norm_g)
    var = dict(norm1_g=v_norm1_g, w_in=v_w_in, conv_w=v_conv_w, a_log=v_a_log, dt_bias=v_dt_bias, gdn_norm_g=v_gdn_norm_g,
               attn_sinks=v_attn_sinks, w_branch_gdn=v_w_branch_gdn, w_branch_swa=v_w_branch_swa, w_out=v_w_out, norm2_g=v_norm2_g,
               w_ff_up=v_w_ff_up, w_ff_down=v_w_ff_down, final_norm_g=v_final_norm_g)
    depth, d = norm1_g.shape
    xs, target = x[0], loss_target[0]
    core = lax.axis_index("c")
    chip = 2 * lax.axis_index("x") + lax.axis_index("y")
    core_arr = jnp.reshape(core, (1,)).astype(jnp.int32)

    cw = conv_w.shape[-1]
    placed = lax.dynamic_update_slice(jnp.zeros((depth, CONV_K, N_CHIPS * cw), F32), conv_w, (0, 0, chip * cw))
    placed = placed * (core == 0).astype(F32)
    conv_full = _allreduce_small(_to_rows(placed.reshape(-1)), "gather_conv_w")
    conv_full = conv_full.reshape(-1)[:depth * CONV_K * N_CHIPS * cw].reshape(depth, CONV_K, N_CHIPS * cw)

    alibi = _alibi_row(d)
    layers = []
    for l in range(depth):
        shards = [w[n][l].astype(BF16) for n in MATRICES]
        full = dict(zip(MATRICES, _gather_weights(shards, "l%d_gather" % l)))
        w_in_full = jnp.transpose(full["w_in"], (1, 0, 2)).reshape(d, -1)
        layers.append(dict(
            norm1_g=norm1_g[l][None], norm2_g=norm2_g[l][None], conv_w=conv_full[l], a_log=_pad_row(a_log[l]), dt_bias=_pad_row(dt_bias[l]),
            gdn_norm_g=gdn_norm_g[l][None], attn_sinks=_pad_row(attn_sinks[l]), alibi=alibi, w_in=_pack_w_in(w_in_full, d),
            w_branch_gdn=full["w_branch_gdn"].reshape(-1, d), w_branch_swa=full["w_branch_swa"].reshape(-1, d),
            w_out=full["w_out"].reshape(-1, d), w_ff_up=full["w_ff_up"], w_ff_down=full["w_ff_down"].reshape(-1, d)))

    h = xs
    saved = []
    for l in range(depth):
        h, s = _layer_fwd(h, layers[l], "l%d_" % l)
        saved.append(s)
    dh, d_final, loss_row = _loss_head(h, final_norm_g[None], target, "loss_head")

    grads = {n: [None] * depth for n in WEIGHTS if n != "final_norm_g"}
    for l in reversed(range(depth)):
        dh, g = _layer_bwd(dh, layers[l], saved[l], "l%d_" % l)
        g_in = _unpack_w_in(g["w_in"], d)
        g_in = jnp.transpose(g_in.reshape(d, N_CHIPS, -1), (1, 0, 2))
        mats = [g_in] + [g[n] if n == "w_ff_up" else g[n].reshape(N_CHIPS, -1, g[n].shape[-1]) for n in MATRICES[1:]]
        for n, r in zip(MATRICES, _reduce_scatter(mats, core_arr, "l%d_" % l)):
            grads[n][l] = r
        for n in ("norm1_g", "norm2_g", "a_log", "dt_bias", "gdn_norm_g", "attn_sinks", "conv_w"):
            grads[n][l] = g[n].reshape(-1)

    small = ("norm1_g", "norm2_g", "a_log", "dt_bias", "gdn_norm_g", "attn_sinks", "conv_w")
    pieces = [jnp.stack(grads[n]).reshape(-1) for n in small] + [d_final.reshape(-1), loss_row[0, :1]]
    sizes = [p.shape[0] for p in pieces]
    packed = _allreduce_small(_to_rows(jnp.concatenate(pieces)), "reduce_small").reshape(-1)
    offs = np.concatenate([[0], np.cumsum(sizes)])
    red = {n: packed[offs[i]:offs[i + 1]] for i, n in enumerate(small + ("final_norm_g", "loss"))}
    loss = red["loss"][0]

    grad_out = {}
    for n in MATRICES:
        grad_out[n] = jnp.stack(grads[n]).reshape(w[n].shape)
    for n in ("norm1_g", "norm2_g", "a_log", "dt_bias", "gdn_norm_g", "attn_sinks"):
        grad_out[n] = red[n].reshape(w[n].shape)
    grad_out["final_norm_g"] = red["final_norm_g"]
    conv_g = red["conv_w"].reshape(depth, CONV_K, N_CHIPS * cw)
    grad_out["conv_w"] = lax.dynamic_slice(conv_g, (0, 0, chip * cw), (depth, CONV_K, cw))

    delta, new_m, new_v = {}, {}, {}
    for n in WEIGHTS:
        delta[n], new_m[n], new_v[n] = _adamw_nd(w[n], grad_out[n], mom[n], var[n], "adamw_" + n)
    return (loss, dh[None], *[grad_out[n] for n in WEIGHTS], *[delta[n] for n in WEIGHTS], *[new_m[n] for n in WEIGHTS],
            *[new_v[n] for n in WEIGHTS])
```

```python
import functools

import jax
import jax.numpy as jnp
import numpy as np
from jax import lax
from jax.experimental import pallas as pl
from jax.experimental.pallas import tpu as pltpu

F32 = jnp.float32
BF16 = jnp.bfloat16

GDN_HEAD_DIM = 128
CHUNK = 64
SWA_HEAD_DIM = 64
WINDOW = 128
CONV_K = 4
GQA_GROUP = 8
NORM_EPS = 1e-6
N_CHIPS = 4
LANES = 128
CONV_HALO = 8
VMEM_LIMIT = 56 * 1024 * 1024

ADAM_LR = 0.001
ADAM_B1 = 0.9
ADAM_B2 = 0.999
ADAM_EPS = 1e-08
ADAM_WD = 0.01
ADAM_STEP = 10

NN = (((1,), (0,)), ((), ()))
NT = (((1,), (1,)), ((), ()))
TN = (((0,), (0,)), ((), ()))


def _pick(dim, cap, mult=LANES):
    if dim <= cap:
        return dim
    t = (cap // mult) * mult
    while t >= mult:
        if dim % t == 0:
            return t
        t -= mult
    return dim


def _params(sem):
    return pltpu.CompilerParams(dimension_semantics=sem, vmem_limit_bytes=VMEM_LIMIT)


def _bdot(a, b, dn):
    return lax.dot_general(a.astype(BF16), b.astype(BF16), dn, preferred_element_type=F32)


@jax.custom_vjp
def mm_nn(a, b):
    return _bdot(a, b, NN)


@jax.custom_vjp
def mm_nt(a, b):
    return _bdot(a, b, NT)


@jax.custom_vjp
def mm_tn(a, b):
    return _bdot(a, b, TN)


mm_nn.defvjp(lambda a, b: (_bdot(a, b, NN), (a, b)), lambda r, g: (_bdot(g, r[1], NT), _bdot(r[0], g, TN)))
mm_nt.defvjp(lambda a, b: (_bdot(a, b, NT), (a, b)), lambda r, g: (_bdot(g, r[1], NN), _bdot(g, r[0], TN)))
mm_tn.defvjp(lambda a, b: (_bdot(a, b, TN), (a, b)), lambda r, g: (_bdot(r[1], g, NT), _bdot(r[0], g, NN)))


def _hdot(a, b, dn=NN):
    return lax.dot_general(a, b, dn, precision=lax.Precision.HIGHEST, preferred_element_type=F32)


def _sigmoid(x):
    return 1.0 / (1.0 + jnp.exp(-x))


def _silu(x):
    return x * _sigmoid(x)


def _softplus(x):
    return jnp.maximum(x, 0.0) + jnp.log(1.0 + jnp.exp(-jnp.abs(x)))


def _lane_pick(row, lane, idx):
    return jnp.sum(jnp.where(lane == idx, row, 0.0), axis=1, keepdims=True)


def _matmul(a, b, *, ta=False, tb=False, out_dtype=F32, add=None, name, tm_cap=1024, tn_cap=1024, tk_cap=2048, b_split=1,
            out_split=1):
    m, k = (a.shape[1], a.shape[0]) if ta else a.shape
    b_rows, b_cols = (b.shape[-2], b.shape[-1] * b_split)
    n = b_rows if tb else b_cols
    assert k == (b_cols if tb else b_rows), (a.shape, b.shape, ta, tb)
    tm = _pick(m, tm_cap)
    tn = _pick(n // max(1 if tb else b_split, out_split), tn_cap)
    tk = _pick(k // (b_split if tb else 1), tk_cap)
    nk = k // tk
    dn = (((0 if ta else 1,), (1 if tb else 0,)), ((), ()))

    def body(*refs):
        if add is None:
            a_ref, b_ref, o_ref, acc_ref = refs
            add_ref = None
        else:
            a_ref, b_ref, add_ref, o_ref, acc_ref = refs
        kk = pl.program_id(2)
        p = lax.dot_general(a_ref[...].astype(BF16), b_ref[...].astype(BF16), dn, preferred_element_type=F32)

        @pl.when(kk == 0)
        def _():
            acc_ref[...] = p

        @pl.when(kk > 0)
        def _():
            acc_ref[...] += p

        @pl.when(kk == nk - 1)
        def _():
            r = acc_ref[...]
            if add_ref is not None:
                r = r + add_ref[...].astype(F32)
            o_ref[...] = r.astype(o_ref.dtype)

    a_spec = pl.BlockSpec((tk, tm), lambda i, j, q: (q, i)) if ta else pl.BlockSpec((tm, tk), lambda i, j, q: (i, q))
    if b_split == 1:
        b_spec = pl.BlockSpec((tn, tk), lambda i, j, q: (j, q)) if tb else pl.BlockSpec((tk, tn), lambda i, j, q: (q, j))
    elif tb:
        per_b = k // b_split // tk
        b_spec = pl.BlockSpec((None, tn, tk), lambda i, j, q: (q // per_b, j, q % per_b))
    else:
        per_b = n // b_split // tn
        b_spec = pl.BlockSpec((None, tk, tn), lambda i, j, q: (j // per_b, q, j % per_b))
    add_spec = pl.BlockSpec((tm, tn), lambda i, j, q: (i, j))
    if out_split == 1:
        o_spec, o_shape = add_spec, (m, n)
    else:
        per_o = n // out_split // tn
        o_spec, o_shape = pl.BlockSpec((None, tm, tn), lambda i, j, q: (j // per_o, i, j % per_o)), (out_split, m, n // out_split)
    in_specs = [a_spec, b_spec] + ([add_spec] if add is not None else [])
    args = (a, b) + ((add,) if add is not None else ())
    return pl.pallas_call(
        body, name=name, grid=(m // tm, n // tn, nk), in_specs=in_specs, out_specs=o_spec,
        out_shape=jax.ShapeDtypeStruct(o_shape, out_dtype), scratch_shapes=[pltpu.VMEM((tm, tn), F32)],
        compiler_params=_params(("parallel", "parallel", "arbitrary")),
    )(*args)


def _rows(fn, row_args, full_args, row_outs, acc_outs, *, t, tm, name):
    n_row, n_full, n_ro = len(row_args), len(full_args), len(row_outs)

    def body(*refs):
        ins = [r[...] for r in refs[:n_row + n_full]]
        outs = fn(*ins)
        o_refs = refs[n_row + n_full:]
        for r, v in zip(o_refs[:n_ro], outs[:n_ro]):
            r[...] = v.astype(r.dtype)
        i = pl.program_id(0)
        for r, v in zip(o_refs[n_ro:], outs[n_ro:]):
            @pl.when(i == 0)
            def _(r=r, v=v):
                r[...] = v

            @pl.when(i > 0)
            def _(r=r, v=v):
                r[...] += v

    in_specs = [pl.BlockSpec((tm, w), functools.partial(lambda i, cb: (i, cb), cb=cb)) for (_, w, cb) in row_args]
    in_specs += [pl.BlockSpec(f.shape, lambda i: (0, 0)) for f in full_args]
    out_specs = [pl.BlockSpec((tm, w), lambda i: (i, 0)) for (w, _) in row_outs]
    out_specs += [pl.BlockSpec(s, lambda i: (0, 0)) for s in acc_outs]
    out_shape = [jax.ShapeDtypeStruct((t, w), d) for (w, d) in row_outs]
    out_shape += [jax.ShapeDtypeStruct(s, F32) for s in acc_outs]
    return pl.pallas_call(
        body, name=name, grid=(t // tm,), in_specs=in_specs, out_specs=out_specs, out_shape=out_shape,
        compiler_params=_params(("arbitrary",)),
    )(*[a for (a, _, _) in row_args], *full_args)


def _rms(x, g):
    return x * lax.rsqrt(jnp.mean(x * x, axis=-1, keepdims=True) + NORM_EPS) * g


def _rmsnorm_fwd(x, g, name):
    t, d = x.shape
    (h,) = _rows(lambda xb, gb: (_rms(xb, gb),), [(x, d, 0)], [g], [(d, BF16)], [], t=t, tm=_pick(t, 512, 8), name=name)
    return h


def _rmsnorm_bwd(x, g, dh, dx_in, name):
    t, d = x.shape

    def fn(xb, dhb, dxb, gb):
        _, vjp = jax.vjp(_rms, xb, gb)
        dx, dg = vjp(dhb)
        return dxb + dx, dg

    return _rows(fn, [(x, d, 0), (dh, d, 0), (dx_in, d, 0)], [g], [(d, F32)], [(1, d)], t=t, tm=_pick(t, 256, 8), name=name)


def _merge(yg, ys, lg, ls):
    return _sigmoid(lg) * yg + _sigmoid(ls) * ys


def _merge_fwd(y_gdn, y_swa, proj, gate_off, name):
    t, d = y_gdn.shape
    cb = gate_off // d
    (mix,) = _rows(lambda a, b, c, e: (_merge(a, b, c, e),), [(y_gdn, d, 0), (y_swa, d, 0), (proj, d, cb), (proj, d, cb + 1)], [],
                   [(d, BF16)], [], t=t, tm=_pick(t, 256, 8), name=name)
    return mix


def _merge_bwd(y_gdn, y_swa, proj, gate_off, dmix, name):
    t, d = y_gdn.shape
    cb = gate_off // d

    def fn(a, b, c, e, g):
        _, vjp = jax.vjp(_merge, a, b, c, e)
        da, db, dc, de = vjp(g)
        return da, db, jnp.concatenate([dc, de], axis=1)

    return _rows(fn, [(y_gdn, d, 0), (y_swa, d, 0), (proj, d, cb), (proj, d, cb + 1), (dmix, d, 0)], [],
                 [(d, BF16), (d, BF16), (2 * d, BF16)], [], t=t, tm=_pick(t, 128, 8), name=name)


def _relu2_fwd(up, name):
    t, f = up.shape
    (act,) = _rows(lambda u: (jnp.square(jnp.maximum(u, 0.0)),), [(up, f, 0)], [], [(f, BF16)], [], t=t, tm=_pick(t, 256, 8), name=name)
    return act


def _relu2_bwd(up, dact, name):
    t, f = up.shape
    (dup,) = _rows(lambda u, g: (g * 2.0 * jnp.maximum(u, 0.0),), [(up, f, 0), (dact, f, 0)], [], [(f, BF16)], [], t=t,
                   tm=_pick(t, 128, 8), name=name)
    return dup


def _loss_head(x, g, target, name):
    t, d = x.shape

    def loss_fn(xb, gb, tb):
        err = _rms(xb, gb) - tb
        return 0.5 * jnp.sum(jnp.mean(err * err, axis=-1, keepdims=True), axis=0, keepdims=True)

    def fn(xb, tb, gb):
        lv, vjp = jax.vjp(lambda a, b: loss_fn(a, b, tb), xb, gb)
        dx, dg = vjp(jnp.ones((1, 1), F32))
        return dx, dg, jnp.broadcast_to(lv, (1, LANES))

    return _rows(fn, [(x, d, 0), (target, d, 0)], [g], [(d, F32)], [(1, d), (1, LANES)], t=t, tm=_pick(t, 256, 8), name=name)


def _conv_silu(prev, cur, w, keep_prev):
    tm = cur.shape[0]
    xp = jnp.concatenate([prev * keep_prev, cur], axis=0)
    y = w[0:1, :] * xp[CONV_HALO - 3:CONV_HALO - 3 + tm]
    for j in range(1, CONV_K):
        y = y + w[j:j + 1, :] * xp[CONV_HALO - 3 + j:CONV_HALO - 3 + j + tm]
    return _silu(y)


def _conv_tiles(t, width):
    tm = _pick(t, 512, CONV_HALO)
    tc = _pick(width, 512)
    return tm, tc, t // tm, width // tc


def _conv_fwd(proj, conv_w, width, name):
    t = proj.shape[0]
    tm, tc, nt, ncw = _conv_tiles(t, width)
    hb = tm // CONV_HALO

    def body(prev_ref, cur_ref, w_ref, o_ref):
        keep = (pl.program_id(1) > 0).astype(F32)
        o_ref[0] = _conv_silu(prev_ref[...], cur_ref[...], w_ref[...], keep)

    return pl.pallas_call(
        body, name=name, grid=(3 * ncw, nt),
        in_specs=[pl.BlockSpec((CONV_HALO, tc), lambda j, i: (jnp.maximum(i * hb - 1, 0), j)),
                  pl.BlockSpec((tm, tc), lambda j, i: (i, j)),
                  pl.BlockSpec((CONV_K, tc), lambda j, i: (0, j))],
        out_specs=pl.BlockSpec((1, tm, tc), lambda j, i: (j // ncw, i, j % ncw)),
        out_shape=jax.ShapeDtypeStruct((3, t, width), F32),
        compiler_params=_params(("parallel", "arbitrary")),
    )(proj, proj, conv_w)


def _conv_bwd(proj, conv_w, dout, width, name):
    t = proj.shape[0]
    tm, tc, nt, ncw = _conv_tiles(t, width)
    hb = tm // CONV_HALO

    def body(prev_ref, cur_ref, w_ref, g_ref, dx_ref, dw_ref, carry_ref):
        s = pl.program_id(1)
        keep = (s < nt - 1).astype(F32)
        _, vjp = jax.vjp(lambda p, c, w: _conv_silu(p, c, w, keep), prev_ref[...], cur_ref[...], w_ref[...])
        dprev, dcur, dw = vjp(g_ref[0])

        @pl.when(s == 0)
        def _():
            carry_ref[...] = jnp.zeros_like(carry_ref)
            dw_ref[...] = dw

        @pl.when(s > 0)
        def _():
            dw_ref[...] += dw

        tail = jnp.concatenate([jnp.zeros((tm - CONV_HALO, tc), F32), carry_ref[...]], axis=0)
        dx_ref[...] = (dcur + tail).astype(dx_ref.dtype)
        carry_ref[...] = dprev

    def row(s):
        return nt - 1 - s

    return pl.pallas_call(
        body, name=name, grid=(3 * ncw, nt),
        in_specs=[pl.BlockSpec((CONV_HALO, tc), lambda j, s: (jnp.maximum(row(s) * hb - 1, 0), j)),
                  pl.BlockSpec((tm, tc), lambda j, s: (row(s), j)),
                  pl.BlockSpec((CONV_K, tc), lambda j, s: (0, j)),
                  pl.BlockSpec((1, tm, tc), lambda j, s: (j // ncw, row(s), j % ncw))],
        out_specs=[pl.BlockSpec((tm, tc), lambda j, s: (row(s), j)),
                   pl.BlockSpec((CONV_K, tc), lambda j, s: (0, j))],
        out_shape=[jax.ShapeDtypeStruct((t, 3 * width), BF16), jax.ShapeDtypeStruct((CONV_K, 3 * width), F32)],
        scratch_shapes=[pltpu.VMEM((CONV_HALO, tc), F32)],
        compiler_params=_params(("parallel", "arbitrary")),
    )(proj, proj, conv_w, dout)


def _inv_unit_lower_raw(mats):
    n = mats[0].shape[0]
    r = lax.broadcasted_iota(jnp.int32, (n, n), 0)
    c = lax.broadcasted_iota(jnp.int32, (n, n), 1)
    eye = (r == c).astype(F32)
    same = jnp.right_shift(r, 4) == jnp.right_shift(c, 4)
    dg = [jnp.where(same, a, 0.0) for a in mats]
    lo = [a - d for a, d in zip(mats, dg)]
    p = [eye - d for d in dg]
    q = dg
    for _ in range(3):
        q = [_hdot(x, x) for x in q]
        p = [_hdot(x, eye + y) for x, y in zip(p, q)]
    nm = [_hdot(x, y) for x, y in zip(p, lo)]
    n2 = [_hdot(x, x) for x in nm]
    left = [_hdot(eye - x, eye + y) for x, y in zip(nm, n2)]
    return [_hdot(x, y) for x, y in zip(left, p)]


@jax.custom_vjp
def _inv_unit_lower(mats):
    return _inv_unit_lower_raw(mats)


def _inv_fwd(mats):
    t = _inv_unit_lower_raw(mats)
    return t, t


def _inv_bwd(ts, gs):
    x = [_hdot(t, g, TN) for t, g in zip(ts, gs)]
    return ([-_hdot(a, t, NT) for a, t in zip(x, ts)],)


_inv_unit_lower.defvjp(_inv_fwd, _inv_bwd)


def _l2n(x):
    return x * lax.rsqrt(jnp.sum(x * x, axis=-1, keepdims=True) + NORM_EPS)


def _gdn_chunk(qcs, kcs, vcs, zs, bg, alog_row, dtb_row, gnorm, states, first_head, n_heads):
    nb = len(qcs)
    hs = range(nb)
    cs = qcs[0].shape[0]
    lane = lax.broadcasted_iota(jnp.int32, (1, LANES), 1)
    r = lax.broadcasted_iota(jnp.int32, (cs, cs), 0)
    c = lax.broadcasted_iota(jnp.int32, (cs, cs), 1)
    q = [_l2n(x) * (GDN_HEAD_DIM ** -0.5) for x in qcs]
    k = [_l2n(x) for x in kcs]
    beta = [_sigmoid(_lane_pick(bg, lane, first_head + i)) for i in hs]
    g = [-jnp.exp(_lane_pick(alog_row, lane, first_head + i)) *
         _softplus(_lane_pick(bg, lane, n_heads + first_head + i) + _lane_pick(dtb_row, lane, first_head + i)) for i in hs]
    g_row = [jnp.sum(jnp.where(r == c, x, 0.0), axis=0, keepdims=True) for x in g]
    dec_col = [jnp.sum(jnp.where(r >= c, x, 0.0), axis=1, keepdims=True) for x in g_row]
    dec_row = [jnp.sum(jnp.where(r <= c, x, 0.0), axis=0, keepdims=True) for x in g]
    gamma = [jnp.exp(jnp.where(r >= c, dc - dr, -1e30)) for dc, dr in zip(dec_col, dec_row)]
    kb = [x * b for x, b in zip(k, beta)]
    a = [jnp.where(r > c, mm_nt(x, y) * gm, 0.0) for x, y, gm in zip(kb, k, gamma)]
    tinv = _inv_unit_lower(a)
    e_col = [jnp.exp(x) for x in dec_col]
    u = [mm_nn(t, v * b) for t, v, b in zip(tinv, vcs, beta)]
    w = [mm_nn(t, x * e) for t, x, e in zip(tinv, kb, e_col)]
    qk = [mm_nt(x, y) * gm for x, y, gm in zip(q, k, gamma)]
    total = [jnp.sum(x, axis=0, keepdims=True) for x in g]
    v_new = [x - mm_nn(y, s) for x, y, s in zip(u, w, states)]
    o = [mm_nn(x * e, s) + mm_nn(y, v) for x, e, s, y, v in zip(q, e_col, states, qk, v_new)]
    new_states = [s * jnp.exp(tt) + mm_tn(x * jnp.exp(tt - dc), v) for s, tt, x, dc, v in zip(states, total, k, dec_col, v_new)]
    ys = [_rms(x, gnorm) * _silu(z) for x, z in zip(o, zs)]
    return ys, new_states


GDN_HEADS_FWD = 16
GDN_HEADS_BWD = 16


def _gdn_fwd(qkvc, proj, alog_row, dtb_row, gnorm, *, d, z_off, bg_off, name, hb=GDN_HEADS_FWD):
    t = qkvc.shape[1]
    nh = d // GDN_HEAD_DIM
    hb = min(hb, nh)
    wb = hb * GDN_HEAD_DIM
    nc = t // CHUNK
    ng = nh // hb

    def body(qkv_ref, z_ref, bg_ref, al_ref, dt_ref, gn_ref, y_ref, sin_ref, s_scr):
        n, hg = pl.program_id(0), pl.program_id(1)

        @pl.when(n == 0)
        def _():
            s_scr[hg] = jnp.zeros((hb, GDN_HEAD_DIM, GDN_HEAD_DIM), F32)

        states = [s_scr[hg, i] for i in range(hb)]
        bg, al, dt, gn = bg_ref[...], al_ref[...], dt_ref[...], gn_ref[...]
        sls = [slice(i * GDN_HEAD_DIM, (i + 1) * GDN_HEAD_DIM) for i in range(hb)]
        ys, new_states = _gdn_chunk([qkv_ref[0, :, sl] for sl in sls], [qkv_ref[1, :, sl] for sl in sls], [qkv_ref[2, :, sl] for sl in sls],
                                    [z_ref[:, sl] for sl in sls], bg, al, dt, gn, states, hg * hb, nh)
        for i in range(hb):
            sin_ref[0, i] = states[i]
            y_ref[:, sls[i]] = ys[i].astype(y_ref.dtype)
            s_scr[hg, i] = new_states[i]

    row = lambda n, hg: (0, 0)
    return pl.pallas_call(
        body, name=name, grid=(nc, ng),
        in_specs=[pl.BlockSpec((3, CHUNK, wb), lambda n, hg: (0, n, hg)),
                  pl.BlockSpec((CHUNK, wb), lambda n, hg: (n, z_off // wb + hg)),
                  pl.BlockSpec((CHUNK, LANES), lambda n, hg: (n, bg_off // LANES)),
                  pl.BlockSpec((1, LANES), row), pl.BlockSpec((1, LANES), row), pl.BlockSpec((1, LANES), row)],
        out_specs=[pl.BlockSpec((CHUNK, wb), lambda n, hg: (n, hg)),
                   pl.BlockSpec((1, hb, GDN_HEAD_DIM, GDN_HEAD_DIM), lambda n, hg: (n, hg, 0, 0))],
        out_shape=[jax.ShapeDtypeStruct((t, d), BF16), jax.ShapeDtypeStruct((nc, nh, GDN_HEAD_DIM, GDN_HEAD_DIM), F32)],
        scratch_shapes=[pltpu.VMEM((ng, hb, GDN_HEAD_DIM, GDN_HEAD_DIM), F32)],
        compiler_params=_params(("arbitrary", "arbitrary")),
    )(qkvc, proj, proj, alog_row, dtb_row, gnorm)


def _gdn_bwd(qkvc, proj, alog_row, dtb_row, gnorm, states, dy, *, d, z_off, bg_off, name, hb=GDN_HEADS_BWD):
    t = qkvc.shape[1]
    nh = d // GDN_HEAD_DIM
    hb = min(hb, nh)
    wb = hb * GDN_HEAD_DIM
    nc = t // CHUNK
    ng = nh // hb

    def body(qkv_ref, z_ref, bg_ref, al_ref, dt_ref, gn_ref, sin_ref, dy_ref, dqkv_ref, dz_ref, dbg_ref, dal_ref, ddt_ref, dgn_ref,
             ds_scr):
        s, hg = pl.program_id(0), pl.program_id(1)

        @pl.when(s == 0)
        def _():
            ds_scr[hg] = jnp.zeros((hb, GDN_HEAD_DIM, GDN_HEAD_DIM), F32)

        @pl.when(hg == 0)
        def _():
            dbg_ref[...] = jnp.zeros_like(dbg_ref)

        @pl.when((s == 0) & (hg == 0))
        def _():
            dal_ref[...] = jnp.zeros_like(dal_ref)
            ddt_ref[...] = jnp.zeros_like(ddt_ref)
            dgn_ref[...] = jnp.zeros_like(dgn_ref)

        dstates = [ds_scr[hg, i] for i in range(hb)]
        bg, al, dt, gn = bg_ref[...], al_ref[...], dt_ref[...], gn_ref[...]
        sls = [slice(i * GDN_HEAD_DIM, (i + 1) * GDN_HEAD_DIM) for i in range(hb)]
        fn = functools.partial(_gdn_chunk, first_head=hg * hb, n_heads=nh)
        _, vjp = jax.vjp(fn, [qkv_ref[0, :, sl] for sl in sls], [qkv_ref[1, :, sl] for sl in sls], [qkv_ref[2, :, sl] for sl in sls],
                         [z_ref[:, sl] for sl in sls], bg, al, dt, gn, [sin_ref[0, i] for i in range(hb)])
        dq, dk, dv, dz, dbg, dal, ddt, dgn, dst = vjp(([dy_ref[:, sl] for sl in sls], dstates))
        for i in range(hb):
            dqkv_ref[0, :, sls[i]] = dq[i]
            dqkv_ref[1, :, sls[i]] = dk[i]
            dqkv_ref[2, :, sls[i]] = dv[i]
            dz_ref[:, sls[i]] = dz[i].astype(dz_ref.dtype)
            ds_scr[hg, i] = dst[i]
        dbg_ref[...] += dbg
        dal_ref[...] += dal
        ddt_ref[...] += ddt
        dgn_ref[...] += dgn

    def ch(s):
        return nc - 1 - s

    row = lambda s, hg: (0, 0)
    return pl.pallas_call(
        body, name=name, grid=(nc, ng),
        in_specs=[pl.BlockSpec((3, CHUNK, wb), lambda s, hg: (0, ch(s), hg)),
                  pl.BlockSpec((CHUNK, wb), lambda s, hg: (ch(s), z_off // wb + hg)),
                  pl.BlockSpec((CHUNK, LANES), lambda s, hg: (ch(s), bg_off // LANES)),
                  pl.BlockSpec((1, LANES), row), pl.BlockSpec((1, LANES), row), pl.BlockSpec((1, LANES), row),
                  pl.BlockSpec((1, hb, GDN_HEAD_DIM, GDN_HEAD_DIM), lambda s, hg: (ch(s), hg, 0, 0)),
                  pl.BlockSpec((CHUNK, wb), lambda s, hg: (ch(s), hg))],
        out_specs=[pl.BlockSpec((3, CHUNK, wb), lambda s, hg: (0, ch(s), hg)),
                   pl.BlockSpec((CHUNK, wb), lambda s, hg: (ch(s), hg)),
                   pl.BlockSpec((CHUNK, LANES), lambda s, hg: (ch(s), 0)),
                   pl.BlockSpec((1, LANES), row), pl.BlockSpec((1, LANES), row), pl.BlockSpec((1, LANES), row)],
        out_shape=[jax.ShapeDtypeStruct((3, t, d), F32), jax.ShapeDtypeStruct((t, d), BF16), jax.ShapeDtypeStruct((t, LANES), F32),
                   jax.ShapeDtypeStruct((1, LANES), F32), jax.ShapeDtypeStruct((1, LANES), F32), jax.ShapeDtypeStruct((1, LANES), F32)],
        scratch_shapes=[pltpu.VMEM((ng, hb, GDN_HEAD_DIM, GDN_HEAD_DIM), F32)],
        compiler_params=_params(("arbitrary", "arbitrary")),
    )(qkvc, proj, proj, alog_row, dtb_row, gnorm, states, dy)


@jax.custom_vjp
def _swap_halves(x):
    return pltpu.roll(x, SWA_HEAD_DIM, 1)


_swap_halves.defvjp(lambda x: (pltpu.roll(x, SWA_HEAD_DIM, 1), None), lambda _, g: (pltpu.roll(g, SWA_HEAD_DIM, 1),))

SWA_PAIR_Q = 2 * GQA_GROUP * SWA_HEAD_DIM


def _swa_block(q, kp, kc, vp, vc, sink_row, slope_row, keep_prev, pair):
    kb = jnp.concatenate([kp, kc], axis=0)
    vb = jnp.concatenate([vp, vc], axis=0)
    lane = lax.broadcasted_iota(jnp.int32, (1, LANES), 1)
    low = lane < SWA_HEAD_DIM
    high = jnp.logical_not(low)
    qi = lax.broadcasted_iota(jnp.int32, (WINDOW, 2 * WINDOW), 0)
    sj = lax.broadcasted_iota(jnp.int32, (WINDOW, 2 * WINDOW), 1)
    dist = qi + WINDOW - sj
    valid = (dist >= 0) & (dist < WINDOW) & ((sj >= WINDOW) | (keep_prev > 0.5))
    distf = dist.astype(F32)
    kk, vv = [], []
    for mine in (low, high):
        x = jnp.where(mine, kb, 0.0)
        kk.append(x + _swap_halves(x))
        y = jnp.where(mine, vb, 0.0)
        vv.append(y + _swap_halves(y))
    hl = range(2 * GQA_GROUP)
    half = [low if h % 2 == 0 else high for h in hl]
    slope = [_lane_pick(slope_row, lane, pair * (2 * GQA_GROUP) + h) for h in hl]
    sink = [_lane_pick(sink_row, lane, pair * (2 * GQA_GROUP) + h) for h in hl]
    qm = [jnp.where(half[h], q[:, (h // 2) * LANES:(h // 2 + 1) * LANES], 0.0) for h in hl]
    sc = [mm_nt(qm[h], kk[h // GQA_GROUP]) * (SWA_HEAD_DIM ** -0.5) for h in hl]
    sc = [jnp.where(valid, sc[h] - slope[h] * distf, -1e30) for h in hl]
    m = [lax.stop_gradient(jnp.maximum(jnp.max(sc[h], axis=-1, keepdims=True), sink[h])) for h in hl]
    p = [jnp.exp(sc[h] - m[h]) for h in hl]
    probs = [p[h] / (jnp.sum(p[h], axis=-1, keepdims=True) + jnp.exp(sink[h] - m[h])) for h in hl]
    od = [jnp.where(half[h], mm_nn(probs[h], vv[h // GQA_GROUP]), 0.0) for h in hl]
    return jnp.concatenate([od[2 * i] + od[2 * i + 1] for i in range(GQA_GROUP)], axis=1)


def _swa_specs(t, q_off, k_off, v_off, order):
    nb = t // WINDOW

    def blk(s):
        return order(s, nb)

    return nb, [pl.BlockSpec((WINDOW, SWA_PAIR_Q), lambda p, s: (blk(s), q_off // SWA_PAIR_Q + p)),
                pl.BlockSpec((WINDOW, LANES), lambda p, s: (jnp.maximum(blk(s) - 1, 0), k_off // LANES + p)),
                pl.BlockSpec((WINDOW, LANES), lambda p, s: (blk(s), k_off // LANES + p)),
                pl.BlockSpec((WINDOW, LANES), lambda p, s: (jnp.maximum(blk(s) - 1, 0), v_off // LANES + p)),
                pl.BlockSpec((WINDOW, LANES), lambda p, s: (blk(s), v_off // LANES + p)),
                pl.BlockSpec((1, LANES), lambda p, s: (0, 0)), pl.BlockSpec((1, LANES), lambda p, s: (0, 0))]


def _swa_fwd(proj, sink_row, slope_row, *, d, q_off, k_off, v_off, name):
    t = proj.shape[0]
    n_pairs = d // SWA_PAIR_Q
    nb, in_specs = _swa_specs(t, q_off, k_off, v_off, lambda s, nb: s)

    def body(q_ref, kp_ref, kc_ref, vp_ref, vc_ref, sink_ref, slope_ref, o_ref):
        keep = (pl.program_id(1) > 0).astype(F32)
        o = _swa_block(q_ref[...], kp_ref[...], kc_ref[...], vp_ref[...], vc_ref[...], sink_ref[...], slope_ref[...], keep,
                       pl.program_id(0))
        o_ref[...] = o.astype(o_ref.dtype)

    return pl.pallas_call(
        body, name=name, grid=(n_pairs, nb), in_specs=in_specs,
        out_specs=pl.BlockSpec((WINDOW, SWA_PAIR_Q), lambda p, s: (s, p)),
        out_shape=jax.ShapeDtypeStruct((t, d), BF16),
        compiler_params=_params(("parallel", "arbitrary")),
    )(proj, proj, proj, proj, proj, sink_row, slope_row)


def _swa_bwd(proj, sink_row, slope_row, do, *, d, q_off, k_off, v_off, name):
    t = proj.shape[0]
    n_pairs = d // SWA_PAIR_Q
    nb, in_specs = _swa_specs(t, q_off, k_off, v_off, lambda s, nb: nb - 1 - s)

    def body(q_ref, kp_ref, kc_ref, vp_ref, vc_ref, sink_ref, slope_ref, do_ref, dq_ref, dk_ref, dv_ref, dsink_ref, ck_ref, cv_ref):
        p, s = pl.program_id(0), pl.program_id(1)
        keep = (s < nb - 1).astype(F32)
        fn = functools.partial(_swa_block, slope_row=slope_ref[...], keep_prev=keep, pair=p)
        _, vjp = jax.vjp(fn, q_ref[...], kp_ref[...], kc_ref[...], vp_ref[...], vc_ref[...], sink_ref[...])
        dq, dkp, dkc, dvp, dvc, dsink = vjp(do_ref[...])

        @pl.when(s == 0)
        def _():
            ck_ref[...] = jnp.zeros_like(ck_ref)
            cv_ref[...] = jnp.zeros_like(cv_ref)

        @pl.when((s == 0) & (p == 0))
        def _():
            dsink_ref[...] = jnp.zeros_like(dsink_ref)

        dq_ref[...] = dq.astype(dq_ref.dtype)
        dk_ref[...] = (dkc + ck_ref[...]).astype(dk_ref.dtype)
        dv_ref[...] = (dvc + cv_ref[...]).astype(dv_ref.dtype)
        ck_ref[...] = dkp
        cv_ref[...] = dvp
        dsink_ref[...] += dsink

    in_specs = in_specs + [pl.BlockSpec((WINDOW, SWA_PAIR_Q), lambda p, s: (nb - 1 - s, p))]
    kv_w = d // GQA_GROUP
    return pl.pallas_call(
        body, name=name, grid=(n_pairs, nb), in_specs=in_specs,
        out_specs=[pl.BlockSpec((WINDOW, SWA_PAIR_Q), lambda p, s: (nb - 1 - s, p)),
                   pl.BlockSpec((WINDOW, LANES), lambda p, s: (nb - 1 - s, p)),
                   pl.BlockSpec((WINDOW, LANES), lambda p, s: (nb - 1 - s, p)),
                   pl.BlockSpec((1, LANES), lambda p, s: (0, 0))],
        out_shape=[jax.ShapeDtypeStruct((t, d), BF16), jax.ShapeDtypeStruct((t, kv_w), BF16), jax.ShapeDtypeStruct((t, kv_w), BF16),
                   jax.ShapeDtypeStruct((1, LANES), F32)],
        scratch_shapes=[pltpu.VMEM((WINDOW, LANES), F32), pltpu.VMEM((WINDOW, LANES), F32)],
        compiler_params=_params(("arbitrary", "arbitrary")),
    )(proj, proj, proj, proj, proj, sink_row, slope_row, do)


def _layout(d):
    kv = d // GQA_GROUP
    return dict(z=3 * d, q=4 * d, gate=5 * d, k=7 * d, v=7 * d + kv, bg=7 * d + 2 * kv, width=7 * d + 2 * kv + LANES)


def _pack_w_in(w, d):
    nh = d // GDN_HEAD_DIM
    kv = d // GQA_GROUP
    o = 4 * d + 2 * nh
    parts = [w[..., :4 * d], w[..., o:o + d], w[..., o + d + 2 * kv:o + 3 * d + 2 * kv], w[..., o + d:o + d + 2 * kv],
             w[..., 4 * d:o], jnp.zeros(w.shape[:-1] + (LANES - 2 * nh,), w.dtype)]
    return jnp.concatenate(parts, axis=-1)


def _unpack_w_in(wp, d):
    nh = d // GDN_HEAD_DIM
    kv = d // GQA_GROUP
    lay = _layout(d)
    parts = [wp[..., :4 * d], wp[..., lay["bg"]:lay["bg"] + 2 * nh], wp[..., lay["q"]:lay["q"] + d],
             wp[..., lay["k"]:lay["k"] + 2 * kv], wp[..., lay["gate"]:lay["gate"] + 2 * d]]
    return jnp.concatenate(parts, axis=-1)


def _pad_row(v):
    return jnp.pad(v.astype(F32), (0, LANES - v.shape[0]))[None, :]


def _alibi_row(d):
    nq = d // SWA_HEAD_DIM
    return _pad_row(2.0 ** (-8.0 * jnp.arange(1, nq + 1, dtype=F32) / nq))


def _layer_fwd(x, p, tag):
    t, d = x.shape
    lay = _layout(d)
    tn = 1152 if lay["width"] % 1152 == 0 else 1024
    h1 = _rmsnorm_fwd(x, p["norm1_g"], tag + "rms1")
    proj = _matmul(h1, p["w_in"], name=tag + "mm_in", tn_cap=tn)
    qkvc = _conv_fwd(proj, p["conv_w"], d, tag + "conv")
    gdn_o, states = _gdn_fwd(qkvc, proj, p["a_log"], p["dt_bias"], p["gdn_norm_g"], d=d, z_off=lay["z"], bg_off=lay["bg"],
                             name=tag + "gdn")
    swa_o = _swa_fwd(proj, p["attn_sinks"], p["alibi"], d=d, q_off=lay["q"], k_off=lay["k"], v_off=lay["v"], name=tag + "swa")
    y_gdn = _matmul(gdn_o, p["w_branch_gdn"], name=tag + "mm_bg")
    y_swa = _matmul(swa_o, p["w_branch_swa"], name=tag + "mm_bs")
    mix = _merge_fwd(y_gdn, y_swa, proj, lay["gate"], tag + "merge")
    x1 = _matmul(mix, p["w_out"], add=x, name=tag + "mm_out")
    h2 = _rmsnorm_fwd(x1, p["norm2_g"], tag + "rms2")
    up = _matmul(h2, p["w_ff_up"], name=tag + "mm_up", b_split=N_CHIPS)
    act = _relu2_fwd(up, tag + "relu2")
    x2 = _matmul(act, p["w_ff_down"], add=x1, name=tag + "mm_down")
    return x2, dict(x=x, h1=h1, proj=proj, qkvc=qkvc, states=states, gdn_o=gdn_o, swa_o=swa_o, y_gdn=y_gdn, y_swa=y_swa, mix=mix,
                    x1=x1, h2=h2, up=up, act=act)


def _layer_bwd(dx2, p, s, tag):
    t, d = dx2.shape
    lay = _layout(d)
    tn = 1152 if lay["width"] % 1152 == 0 else 1024
    nh = d // GDN_HEAD_DIM
    g = {}
    dact = _matmul(dx2, p["w_ff_down"], tb=True, name=tag + "mm_dact", tm_cap=512)
    g["w_ff_down"] = _matmul(s["act"], dx2, ta=True, out_dtype=BF16, name=tag + "mm_dwdown", tk_cap=1024)
    dup = _relu2_bwd(s["up"], dact, tag + "relu2b")
    g["w_ff_up"] = _matmul(s["h2"], dup, ta=True, out_dtype=BF16, name=tag + "mm_dwup", out_split=N_CHIPS)
    dh2 = _matmul(dup, p["w_ff_up"], tb=True, name=tag + "mm_dh2", b_split=N_CHIPS)
    dx1, g["norm2_g"] = _rmsnorm_bwd(s["x1"], p["norm2_g"], dh2, dx2, tag + "rms2b")
    dmix = _matmul(dx1, p["w_out"], tb=True, name=tag + "mm_dmix", tm_cap=512)
    g["w_out"] = _matmul(s["mix"], dx1, ta=True, out_dtype=BF16, name=tag + "mm_dwout", tk_cap=1024)
    dyg, dys, dgl = _merge_bwd(s["y_gdn"], s["y_swa"], s["proj"], lay["gate"], dmix, tag + "mergeb")
    g["w_branch_gdn"] = _matmul(s["gdn_o"], dyg, ta=True, out_dtype=BF16, name=tag + "mm_dwbg")
    g["w_branch_swa"] = _matmul(s["swa_o"], dys, ta=True, out_dtype=BF16, name=tag + "mm_dwbs")
    dgdn_o = _matmul(dyg, p["w_branch_gdn"], tb=True, name=tag + "mm_dgdn")
    dswa_o = _matmul(dys, p["w_branch_swa"], tb=True, name=tag + "mm_dswa")
    dq_s, dk_s, dv_s, dsink = _swa_bwd(s["proj"], p["attn_sinks"], p["alibi"], dswa_o, d=d, q_off=lay["q"], k_off=lay["k"],
                                       v_off=lay["v"], name=tag + "swab")
    dqkvc, dz, dbg, dal, ddt, dgn = _gdn_bwd(s["qkvc"], s["proj"], p["a_log"], p["dt_bias"], p["gdn_norm_g"], s["states"], dgdn_o,
                                             d=d, z_off=lay["z"], bg_off=lay["bg"], name=tag + "gdnb")
    dqkv, g["conv_w"] = _conv_bwd(s["proj"], p["conv_w"], dqkvc, d, tag + "convb")
    dproj = jnp.concatenate([dqkv, dz, dq_s, dgl, dk_s, dv_s, lax.reduce_precision(dbg, 8, 7).astype(BF16)], axis=1)
    g["w_in"] = _matmul(s["h1"], dproj, ta=True, out_dtype=BF16, name=tag + "mm_dwin", tn_cap=tn)
    dh1 = _matmul(dproj, p["w_in"], tb=True, name=tag + "mm_dh1", tk_cap=tn)
    dx, g["norm1_g"] = _rmsnorm_bwd(s["x"], p["norm1_g"], dh1, dx1, tag + "rms1b")
    g["a_log"], g["dt_bias"], g["gdn_norm_g"], g["attn_sinks"] = dal[0, :nh], ddt[0, :nh], dgn[0], dsink[0, :d // SWA_HEAD_DIM]
    return dx, g


MESH = pl.DeviceIdType.MESH
HBM_SPEC = pl.BlockSpec(memory_space=pl.ANY)


def _place():
    x, y, c = lax.axis_index("x"), lax.axis_index("y"), lax.axis_index("c")
    return x, y, c, 2 * x + y


def _flip(x, y, k):
    px, py = x ^ (k >> 1), y ^ (k & 1)
    return px, py, 2 * px + py


def _cast_into_slot(w, layer, pos, name):
    _, r, cols = w.shape
    tm = _pick(r, max(16, (1 << 19) // cols // 16 * 16), 16)

    def body(x_ref, y_ref, w_ref, o_ref):
        o_ref[...] = w_ref[...].astype(o_ref.dtype)

    return pl.pallas_call(
        body, name=name,
        grid_spec=pltpu.PrefetchScalarGridSpec(
            num_scalar_prefetch=2, grid=(r // tm,),
            in_specs=[pl.BlockSpec((None, tm, cols), lambda i, xr, yr: (layer, i, 0))],
            out_specs=pl.BlockSpec((None, tm, cols), lambda i, xr, yr: (2 * xr[0] + yr[0], i, 0))),
        out_shape=jax.ShapeDtypeStruct((N_CHIPS, r, cols), BF16),
        compiler_params=_params(("parallel",)),
    )(pos[0], pos[1], w)


def _gather_weights(bufs, name):
    n = len(bufs)

    def body(*refs):
        outs = refs[n:2 * n]
        send_sems, recv_sems, fsend_sems, frecv_sems = refs[2 * n:]
        x, y, c, ci = _place()
        waits = []
        for a in range(n):
            hr = bufs[a].shape[1] // 2
            mine = outs[a].at[ci, pl.ds(c * hr, hr)]
            for k in (1, 2, 3):
                px, py, _ = _flip(x, y, k)
                cp = pltpu.make_async_remote_copy(src_ref=mine, dst_ref=mine, send_sem=send_sems.at[a, k - 1], recv_sem=recv_sems.at[a, k - 1],
                                                  device_id=(px, py, c), device_id_type=MESH)
                cp.start()
                waits.append(cp.wait_send)
        for a in range(n):
            hr = bufs[a].shape[1] // 2
            for k in (1, 2, 3):
                px, py, pj = _flip(x, y, k)
                landed = outs[a].at[pj, pl.ds(c * hr, hr)]
                pltpu.make_async_remote_copy(src_ref=landed, dst_ref=landed, send_sem=send_sems.at[a, k - 1], recv_sem=recv_sems.at[a, k - 1],
                                             device_id=(px, py, c), device_id_type=MESH).wait_recv()
                fw = pltpu.make_async_remote_copy(src_ref=landed, dst_ref=landed, send_sem=fsend_sems.at[a, k - 1],
                                                  recv_sem=frecv_sems.at[a, k - 1], device_id=(x, y, 1 - c), device_id_type=MESH)
                fw.start()
                waits.append(fw.wait_send)
        for a in range(n):
            hr = bufs[a].shape[1] // 2
            for k in (1, 2, 3):
                _, _, pj = _flip(x, y, k)
                passed = outs[a].at[pj, pl.ds((1 - c) * hr, hr)]
                pltpu.make_async_remote_copy(src_ref=passed, dst_ref=passed, send_sem=fsend_sems.at[a, k - 1], recv_sem=frecv_sems.at[a, k - 1],
                                             device_id=(x, y, 1 - c), device_id_type=MESH).wait_recv()
        for w in waits:
            w()

    return pl.pallas_call(
        body, name=name, in_specs=[HBM_SPEC] * n, out_specs=[HBM_SPEC] * n,
        out_shape=[jax.ShapeDtypeStruct(b.shape, b.dtype) for b in bufs],
        input_output_aliases={a: a for a in range(n)},
        scratch_shapes=[pltpu.SemaphoreType.DMA((n, 3))] * 4,
    )(*bufs)


def _swap_with_sibling(gs, name):
    n = len(gs)

    def body(*refs):
        ins, outs = refs[:n], refs[n:2 * n]
        send_sems, recv_sems = refs[2 * n:]
        x, y, c, _ = _place()
        cps = []
        for a in range(n):
            hr = gs[a].shape[1] // 2
            cp = pltpu.make_async_remote_copy(src_ref=ins[a].at[:, pl.ds((1 - c) * hr, hr)], dst_ref=outs[a], send_sem=send_sems.at[a],
                                              recv_sem=recv_sems.at[a], device_id=(x, y, 1 - c), device_id_type=MESH)
            cp.start()
            cps.append(cp)
        for cp in cps:
            cp.wait()

    return pl.pallas_call(
        body, name=name, in_specs=[HBM_SPEC] * n, out_specs=[HBM_SPEC] * n,
        out_shape=[jax.ShapeDtypeStruct((g.shape[0], g.shape[1] // 2, g.shape[2]), g.dtype) for g in gs],
        scratch_shapes=[pltpu.SemaphoreType.DMA((n,))] * 2,
    )(*gs)


def _scatter_to_chips(hs, name):
    n = len(hs)

    def body(*refs):
        ins, outs = refs[:n], refs[n:2 * n]
        send_sems, recv_sems = refs[2 * n:]
        x, y, c, ci = _place()
        waits = []
        for a in range(n):
            for k in (1, 2, 3):
                px, py, pj = _flip(x, y, k)
                cp = pltpu.make_async_remote_copy(src_ref=ins[a].at[pj], dst_ref=outs[a].at[ci], send_sem=send_sems.at[a, k - 1],
                                                  recv_sem=recv_sems.at[a, k - 1], device_id=(px, py, c), device_id_type=MESH)
                cp.start()
                waits.append(cp.wait_send)
                got = outs[a].at[pj]
                waits.append(pltpu.make_async_remote_copy(src_ref=got, dst_ref=got, send_sem=send_sems.at[a, k - 1],
                                                          recv_sem=recv_sems.at[a, k - 1], device_id=(px, py, c),
                                                          device_id_type=MESH).wait_recv)
        for w in waits:
            w()

    return pl.pallas_call(
        body, name=name, in_specs=[HBM_SPEC] * n, out_specs=[HBM_SPEC] * n,
        out_shape=[jax.ShapeDtypeStruct(h.shape, h.dtype) for h in hs],
        scratch_shapes=[pltpu.SemaphoreType.DMA((n, 3))] * 2,
    )(*hs)


def _share_with_sibling(bufs, name):
    n = len(bufs)

    def body(*refs):
        outs = refs[n:2 * n]
        send_sems, recv_sems = refs[2 * n:]
        x, y, c, _ = _place()
        waits = []
        for a in range(n):
            mine = outs[a].at[c]
            cp = pltpu.make_async_remote_copy(src_ref=mine, dst_ref=mine, send_sem=send_sems.at[a], recv_sem=recv_sems.at[a],
                                              device_id=(x, y, 1 - c), device_id_type=MESH)
            cp.start()
            waits.append(cp.wait_send)
            got = outs[a].at[1 - c]
            waits.append(pltpu.make_async_remote_copy(src_ref=got, dst_ref=got, send_sem=send_sems.at[a], recv_sem=recv_sems.at[a],
                                                      device_id=(x, y, 1 - c), device_id_type=MESH).wait_recv)
        for w in waits:
            w()

    return pl.pallas_call(
        body, name=name, in_specs=[HBM_SPEC] * n, out_specs=[HBM_SPEC] * n,
        out_shape=[jax.ShapeDtypeStruct(b.shape, b.dtype) for b in bufs],
        input_output_aliases={a: a for a in range(n)},
        scratch_shapes=[pltpu.SemaphoreType.DMA((n,))] * 2,
    )(*bufs)


def _add_sibling_half(g, got, core, name):
    nc, r, cols = g.shape
    hr = r // 2
    tm = _pick(hr, 256, 16)

    def body(core_ref, g_ref, o_ref, s_ref):
        s_ref[...] = (g_ref[...].astype(F32) + o_ref[...].astype(F32)).astype(s_ref.dtype)

    return pl.pallas_call(
        body, name=name,
        grid_spec=pltpu.PrefetchScalarGridSpec(
            num_scalar_prefetch=1, grid=(nc, hr // tm),
            in_specs=[pl.BlockSpec((None, None, tm, cols), lambda j, i, cr: (j, cr[0], i, 0)),
                      pl.BlockSpec((None, tm, cols), lambda j, i, cr: (j, i, 0))],
            out_specs=pl.BlockSpec((None, tm, cols), lambda j, i, cr: (j, i, 0))),
        out_shape=jax.ShapeDtypeStruct((nc, hr, cols), g.dtype),
        compiler_params=_params(("parallel", "parallel")),
    )(core, g.reshape(nc, 2, hr, cols), got)


def _sum_chips(own, parts, pos, name):
    nc, r, cols = parts.shape
    tm = _pick(r, 256, 16)

    def body(x_ref, y_ref, c_ref, own_ref, p_ref, o_ref):
        chip = 2 * x_ref[0] + y_ref[0]
        acc = own_ref[...].astype(F32)
        for k in range(1, nc):
            acc = acc + p_ref[chip ^ k].astype(F32)
        o_ref[...] = acc

    return pl.pallas_call(
        body, name=name,
        grid_spec=pltpu.PrefetchScalarGridSpec(
            num_scalar_prefetch=3, grid=(r // tm,),
            in_specs=[pl.BlockSpec((None, tm, cols), lambda i, xr, yr, cr: (2 * xr[0] + yr[0], i, 0)),
                      pl.BlockSpec((nc, tm, cols), lambda i, xr, yr, cr: (0, i, 0))],
            out_specs=pl.BlockSpec((None, tm, cols), lambda i, xr, yr, cr: (cr[0], i, 0))),
        out_shape=jax.ShapeDtypeStruct((2, r, cols), F32),
        compiler_params=_params(("parallel",)),
    )(*pos, own, parts)


def _reduce_scatter(gs, pos, tag):
    got = _swap_with_sibling(gs, tag + "rs_swap")
    hs = [_add_sibling_half(g, o, pos[2], tag + "rs_add%d" % i) for i, (g, o) in enumerate(zip(gs, got))]
    parts = _scatter_to_chips(hs, tag + "rs_scatter")
    rs = [_sum_chips(h, p, pos, tag + "rs_sum%d" % i) for i, (h, p) in enumerate(zip(hs, parts))]
    both = _share_with_sibling(rs, tag + "rs_share")
    return [b.reshape(2 * b.shape[1], b.shape[2]) for b in both]


def _allreduce_small(v, name):
    rows = v.shape[0]

    def body(v_ref, o_ref, buf, send_sems, recv_sems, local_sem):
        x, y, c, _ = _place()
        me, sibling = (x, y, c), (x, y, 1 - c)
        chips = [_flip(x, y, k)[:2] for k in (1, 2, 3)]

        def slot(px, py, pc):
            return buf.at[4 * px + 2 * py + pc]

        def copy(k, block, to, src=None):
            return pltpu.make_async_remote_copy(src_ref=slot(*block) if src is None else src, dst_ref=slot(*block), send_sem=send_sems.at[k],
                                                recv_sem=recv_sems.at[k], device_id=to, device_id_type=MESH)

        mine = pltpu.make_async_copy(v_ref, slot(*me), local_sem)
        mine.start()
        first = [copy(0, me, sibling, src=v_ref)] + [copy(1 + j, me, (*chip, c), src=v_ref) for j, chip in enumerate(chips)]
        for cp in first:
            cp.start()
        passed = [copy(4 + j, (*chip, c), sibling) for j, chip in enumerate(chips)]
        for j, chip in enumerate(chips):
            copy(1 + j, (*chip, c), me).wait_recv()
            passed[j].start()
        copy(0, sibling, me).wait_recv()
        for j, chip in enumerate(chips):
            copy(4 + j, (*chip, 1 - c), me).wait_recv()
        for cp in first + passed:
            cp.wait_send()
        mine.wait()
        acc = buf[0]
        for i in range(1, 2 * N_CHIPS):
            acc = acc + buf[i]
        o_ref[...] = acc

    vm = pl.BlockSpec(memory_space=pltpu.VMEM)
    return pl.pallas_call(
        body, name=name, in_specs=[vm], out_specs=vm, out_shape=jax.ShapeDtypeStruct(v.shape, F32),
        scratch_shapes=[pltpu.VMEM((2 * N_CHIPS, rows, LANES), F32), pltpu.SemaphoreType.DMA((7,)), pltpu.SemaphoreType.DMA((7,)),
                        pltpu.SemaphoreType.DMA],
        compiler_params=pltpu.CompilerParams(vmem_limit_bytes=VMEM_LIMIT),
    )(v)


def _adamw(w, g, m, v, name):
    r, cols = w.shape
    tm = _pick(r, max(8, (1 << 18) // max(cols, 1) // 8 * 8), 8)

    def body(w_ref, g_ref, m_ref, v_ref, d_ref, nm_ref, nv_ref):
        gg = g_ref[...]
        nm = ADAM_B1 * m_ref[...] + (1.0 - ADAM_B1) * gg
        nv = ADAM_B2 * v_ref[...] + (1.0 - ADAM_B2) * jnp.square(gg)
        m_hat = nm / (1.0 - ADAM_B1 ** ADAM_STEP)
        v_hat = nv / (1.0 - ADAM_B2 ** ADAM_STEP)
        d_ref[...] = -ADAM_LR * (m_hat / (jnp.sqrt(v_hat) + ADAM_EPS) + ADAM_WD * w_ref[...])
        nm_ref[...] = nm
        nv_ref[...] = nv

    spec = pl.BlockSpec((tm, cols), lambda i: (i, 0))
    return pl.pallas_call(
        body, name=name, grid=(r // tm,), in_specs=[spec] * 4, out_specs=[spec] * 3,
        out_shape=[jax.ShapeDtypeStruct((r, cols), F32)] * 3, compiler_params=_params(("parallel",)),
    )(w, g, m, v)


def _adamw_nd(w, g, m, v, name):
    shape = w.shape
    two = (1, shape[0]) if len(shape) == 1 else (int(np.prod(shape[:-1])), shape[-1])
    outs = _adamw(w.reshape(two), g.reshape(two), m.reshape(two), v.reshape(two), name)
    return [o.reshape(shape) for o in outs]


WEIGHTS = ("norm1_g", "w_in", "conv_w", "a_log", "dt_bias", "gdn_norm_g", "attn_sinks", "w_branch_gdn", "w_branch_swa", "w_out",
           "norm2_g", "w_ff_up", "w_ff_down", "final_norm_g")
MATRICES = ("w_in", "w_branch_gdn", "w_branch_swa", "w_out", "w_ff_up", "w_ff_down")


def _to_rows(vec):
    n = vec.shape[0]
    rows = -(-n // (8 * LANES)) * 8
    return jnp.pad(vec, (0, rows * LANES - n)).reshape(rows, LANES)


def kernel(x, norm1_g, w_in, conv_w, a_log, dt_bias, gdn_norm_g, attn_sinks, w_branch_gdn, w_branch_swa, w_out, norm2_g, w_ff_up, w_ff_down, final_norm_g, loss_target, m_norm1_g, m_w_in, m_conv_w, m_a_log, m_dt_bias, m_gdn_norm_g, m_attn_sinks, m_w_branch_gdn, m_w_branch_swa, m_w_out, m_norm2_g, m_w_ff_up, m_w_ff_down, m_final_norm_g, v_norm1_g, v_w_in, v_conv_w, v_a_log, v_dt_bias, v_gdn_norm_g, v_attn_sinks, v_w_branch_gdn, v_w_branch_swa, v_w_out, v_norm2_g, v_w_ff_up, v_w_ff_down, v_final_norm_g):
    w = dict(norm1_g=norm1_g, w_in=w_in, conv_w=conv_w, a_log=a_log, dt_bias=dt_bias, gdn_norm_g=gdn_norm_g, attn_sinks=attn_sinks,
             w_branch_gdn=w_branch_gdn, w_branch_swa=w_branch_swa, w_out=w_out, norm2_g=norm2_g, w_ff_up=w_ff_up, w_ff_down=w_ff_down,
             final_norm_g=final_norm_g)
    mom = dict(norm1_g=m_norm1_g, w_in=m_w_in, conv_w=m_conv_w, a_log=m_a_log, dt_bias=m_dt_bias, gdn_norm_g=m_gdn_norm_g,
               attn_sinks=m_attn_sinks, w_branch_gdn=m_w_branch_gdn, w_branch_swa=m_w_branch_swa, w_out=m_w_out, norm2_g=m_norm2_g,
               w_ff_up=m_w_ff_up, w_ff_down=m_w_ff_down, final_norm_g=m_final_norm_g)
    var = dict(norm1_g=v_norm1_g, w_in=v_w_in, conv_w=v_conv_w, a_log=v_a_log, dt_bias=v_dt_bias, gdn_norm_g=v_gdn_norm_g,
               attn_sinks=v_attn_sinks, w_branch_gdn=v_w_branch_gdn, w_branch_swa=v_w_branch_swa, w_out=v_w_out, norm2_g=v_norm2_g,
               w_ff_up=v_w_ff_up, w_ff_down=v_w_ff_down, final_norm_g=v_final_norm_g)
    depth, d = norm1_g.shape
    xs, target = x[0], loss_target[0]
    core = lax.axis_index("c")
    chip = 2 * lax.axis_index("x") + lax.axis_index("y")
    pos = tuple(jnp.reshape(lax.axis_index(a), (1,)).astype(jnp.int32) for a in ("x", "y", "c"))

    cw = conv_w.shape[-1]
    placed = lax.dynamic_update_slice(jnp.zeros((depth, CONV_K, N_CHIPS * cw), F32), conv_w, (0, 0, chip * cw))
    placed = placed * (core == 0).astype(F32)
    conv_full = _allreduce_small(_to_rows(placed.reshape(-1)), "gather_conv_w")
    conv_full = conv_full.reshape(-1)[:depth * CONV_K * N_CHIPS * cw].reshape(depth, CONV_K, N_CHIPS * cw)

    alibi = _alibi_row(d)
    layers = []
    for l in range(depth):
        bufs = [_cast_into_slot(w[n], l, pos, "l%d_cast_%s" % (l, n)) for n in MATRICES]
        full = dict(zip(MATRICES, _gather_weights(bufs, "l%d_gather" % l)))
        w_in_full = jnp.transpose(full["w_in"], (1, 0, 2)).reshape(d, -1)
        layers.append(dict(
            norm1_g=norm1_g[l][None], norm2_g=norm2_g[l][None], conv_w=conv_full[l], a_log=_pad_row(a_log[l]), dt_bias=_pad_row(dt_bias[l]),
            gdn_norm_g=gdn_norm_g[l][None], attn_sinks=_pad_row(attn_sinks[l]), alibi=alibi, w_in=_pack_w_in(w_in_full, d),
            w_branch_gdn=full["w_branch_gdn"].reshape(-1, d), w_branch_swa=full["w_branch_swa"].reshape(-1, d),
            w_out=full["w_out"].reshape(-1, d), w_ff_up=full["w_ff_up"], w_ff_down=full["w_ff_down"].reshape(-1, d)))

    h = xs
    saved = []
    for l in range(depth):
        h, s = _layer_fwd(h, layers[l], "l%d_" % l)
        saved.append(s)
    dh, d_final, loss_row = _loss_head(h, final_norm_g[None], target, "loss_head")

    grads = {n: [None] * depth for n in WEIGHTS if n != "final_norm_g"}
    for l in reversed(range(depth)):
        dh, g = _layer_bwd(dh, layers[l], saved[l], "l%d_" % l)
        g_in = _unpack_w_in(g["w_in"], d)
        g_in = jnp.transpose(g_in.reshape(d, N_CHIPS, -1), (1, 0, 2))
        mats = [g_in] + [g[n] if n == "w_ff_up" else g[n].reshape(N_CHIPS, -1, g[n].shape[-1]) for n in MATRICES[1:]]
        for n, r in zip(MATRICES, _reduce_scatter(mats, pos, "l%d_" % l)):
            grads[n][l] = r
        for n in ("norm1_g", "norm2_g", "a_log", "dt_bias", "gdn_norm_g", "attn_sinks", "conv_w"):
            grads[n][l] = g[n].reshape(-1)

    small = ("norm1_g", "norm2_g", "a_log", "dt_bias", "gdn_norm_g", "attn_sinks", "conv_w")
    pieces = [jnp.stack(grads[n]).reshape(-1) for n in small] + [d_final.reshape(-1), loss_row[0, :1]]
    sizes = [p.shape[0] for p in pieces]
    packed = _allreduce_small(_to_rows(jnp.concatenate(pieces)), "reduce_small").reshape(-1)
    offs = np.concatenate([[0], np.cumsum(sizes)])
    red = {n: packed[offs[i]:offs[i + 1]] for i, n in enumerate(small + ("final_norm_g", "loss"))}
    loss = red["loss"][0]

    grad_out = {}
    for n in MATRICES:
        grad_out[n] = jnp.stack(grads[n]).reshape(w[n].shape)
    for n in ("norm1_g", "norm2_g", "a_log", "dt_bias", "gdn_norm_g", "attn_sinks"):
        grad_out[n] = red[n].reshape(w[n].shape)
    grad_out["final_norm_g"] = red["final_norm_g"]
    conv_g = red["conv_w"].reshape(depth, CONV_K, N_CHIPS * cw)
    grad_out["conv_w"] = lax.dynamic_slice(conv_g, (0, 0, chip * cw), (depth, CONV_K, cw))

    delta, new_m, new_v = {}, {}, {}
    for n in WEIGHTS:
        delta[n], new_m[n], new_v[n] = _adamw_nd(w[n], grad_out[n], mom[n], var[n], "adamw_" + n)
    return (loss, dh[None], *[grad_out[n] for n in WEIGHTS], *[delta[n] for n in WEIGHTS], *[new_m[n] for n in WEIGHTS],
            *[new_v[n] for n in WEIGHTS])
```

```python
import functools

import jax
import jax.numpy as jnp
import numpy as np
from jax import lax
from jax.experimental import pallas as pl
from jax.experimental.pallas import tpu as pltpu

F32 = jnp.float32
BF16 = jnp.bfloat16

GDN_HEAD_DIM = 128
CHUNK = 64
SWA_HEAD_DIM = 64
WINDOW = 128
CONV_K = 4
GQA_GROUP = 8
NORM_EPS = 1e-6
N_CHIPS = 4
LANES = 128
CONV_HALO = 8
VMEM_LIMIT = 56 * 1024 * 1024

ADAM_LR = 0.001
ADAM_B1 = 0.9
ADAM_B2 = 0.999
ADAM_EPS = 1e-08
ADAM_WD = 0.01
ADAM_STEP = 10

NN = (((1,), (0,)), ((), ()))
NT = (((1,), (1,)), ((), ()))
TN = (((0,), (0,)), ((), ()))


def _pick(dim, cap, mult=LANES):
    if dim <= cap:
        return dim
    t = (cap // mult) * mult
    while t >= mult:
        if dim % t == 0:
            return t
        t -= mult
    return dim


def _params(sem):
    return pltpu.CompilerParams(dimension_semantics=sem, vmem_limit_bytes=VMEM_LIMIT)


def _bdot(a, b, dn):
    return lax.dot_general(a.astype(BF16), b.astype(BF16), dn, preferred_element_type=F32)


@jax.custom_vjp
def mm_nn(a, b):
    return _bdot(a, b, NN)


@jax.custom_vjp
def mm_nt(a, b):
    return _bdot(a, b, NT)


@jax.custom_vjp
def mm_tn(a, b):
    return _bdot(a, b, TN)


mm_nn.defvjp(lambda a, b: (_bdot(a, b, NN), (a, b)), lambda r, g: (_bdot(g, r[1], NT), _bdot(r[0], g, TN)))
mm_nt.defvjp(lambda a, b: (_bdot(a, b, NT), (a, b)), lambda r, g: (_bdot(g, r[1], NN), _bdot(g, r[0], TN)))
mm_tn.defvjp(lambda a, b: (_bdot(a, b, TN), (a, b)), lambda r, g: (_bdot(r[1], g, NT), _bdot(r[0], g, NN)))


def _hdot(a, b, dn=NN):
    return lax.dot_general(a, b, dn, precision=lax.Precision.HIGHEST, preferred_element_type=F32)


def _sigmoid(x):
    return 1.0 / (1.0 + jnp.exp(-x))


def _silu(x):
    return x * _sigmoid(x)


def _softplus(x):
    return jnp.maximum(x, 0.0) + jnp.log(1.0 + jnp.exp(-jnp.abs(x)))


def _lane_pick(row, lane, idx):
    return jnp.sum(jnp.where(lane == idx, row, 0.0), axis=1, keepdims=True)


def _matmul(a, b, *, ta=False, tb=False, out_dtype=F32, add=None, name, tm_cap=1024, tn_cap=1024, tk_cap=2048, b_split=1,
            out_split=1):
    m, k = (a.shape[1], a.shape[0]) if ta else a.shape
    b_rows, b_cols = (b.shape[-2], b.shape[-1] * b_split)
    n = b_rows if tb else b_cols
    assert k == (b_cols if tb else b_rows), (a.shape, b.shape, ta, tb)
    tm = _pick(m, tm_cap)
    tn = _pick(n // max(1 if tb else b_split, out_split), tn_cap)
    tk = _pick(k // (b_split if tb else 1), tk_cap)
    nk = k // tk
    dn = (((0 if ta else 1,), (1 if tb else 0,)), ((), ()))

    def body(*refs):
        if add is None:
            a_ref, b_ref, o_ref, acc_ref = refs
            add_ref = None
        else:
            a_ref, b_ref, add_ref, o_ref, acc_ref = refs
        kk = pl.program_id(2)
        p = lax.dot_general(a_ref[...].astype(BF16), b_ref[...].astype(BF16), dn, preferred_element_type=F32)

        @pl.when(kk == 0)
        def _():
            acc_ref[...] = p

        @pl.when(kk > 0)
        def _():
            acc_ref[...] += p

        @pl.when(kk == nk - 1)
        def _():
            r = acc_ref[...]
            if add_ref is not None:
                r = r + add_ref[...].astype(F32)
            o_ref[...] = r.astype(o_ref.dtype)

    a_spec = pl.BlockSpec((tk, tm), lambda i, j, q: (q, i)) if ta else pl.BlockSpec((tm, tk), lambda i, j, q: (i, q))
    if b_split == 1:
        b_spec = pl.BlockSpec((tn, tk), lambda i, j, q: (j, q)) if tb else pl.BlockSpec((tk, tn), lambda i, j, q: (q, j))
    elif tb:
        per_b = k // b_split // tk
        b_spec = pl.BlockSpec((None, tn, tk), lambda i, j, q: (q // per_b, j, q % per_b))
    else:
        per_b = n // b_split // tn
        b_spec = pl.BlockSpec((None, tk, tn), lambda i, j, q: (j // per_b, q, j % per_b))
    add_spec = pl.BlockSpec((tm, tn), lambda i, j, q: (i, j))
    if out_split == 1:
        o_spec, o_shape = add_spec, (m, n)
    else:
        per_o = n // out_split // tn
        o_spec, o_shape = pl.BlockSpec((None, tm, tn), lambda i, j, q: (j // per_o, i, j % per_o)), (out_split, m, n // out_split)
    in_specs = [a_spec, b_spec] + ([add_spec] if add is not None else [])
    args = (a, b) + ((add,) if add is not None else ())
    return pl.pallas_call(
        body, name=name, grid=(m // tm, n // tn, nk), in_specs=in_specs, out_specs=o_spec,
        out_shape=jax.ShapeDtypeStruct(o_shape, out_dtype), scratch_shapes=[pltpu.VMEM((tm, tn), F32)],
        compiler_params=_params(("parallel", "parallel", "arbitrary")),
    )(*args)


def _rows(fn, row_args, full_args, row_outs, acc_outs, *, t, tm, name):
    n_row, n_full, n_ro = len(row_args), len(full_args), len(row_outs)

    def body(*refs):
        ins = [r[...] for r in refs[:n_row + n_full]]
        outs = fn(*ins)
        o_refs = refs[n_row + n_full:]
        for r, v in zip(o_refs[:n_ro], outs[:n_ro]):
            r[...] = v.astype(r.dtype)
        i = pl.program_id(0)
        for r, v in zip(o_refs[n_ro:], outs[n_ro:]):
            @pl.when(i == 0)
            def _(r=r, v=v):
                r[...] = v

            @pl.when(i > 0)
            def _(r=r, v=v):
                r[...] += v

    in_specs = [pl.BlockSpec((tm, w), functools.partial(lambda i, cb: (i, cb), cb=cb)) for (_, w, cb) in row_args]
    in_specs += [pl.BlockSpec(f.shape, lambda i: (0, 0)) for f in full_args]
    out_specs = [pl.BlockSpec((tm, w), lambda i: (i, 0)) for (w, _) in row_outs]
    out_specs += [pl.BlockSpec(s, lambda i: (0, 0)) for s in acc_outs]
    out_shape = [jax.ShapeDtypeStruct((t, w), d) for (w, d) in row_outs]
    out_shape += [jax.ShapeDtypeStruct(s, F32) for s in acc_outs]
    return pl.pallas_call(
        body, name=name, grid=(t // tm,), in_specs=in_specs, out_specs=out_specs, out_shape=out_shape,
        compiler_params=_params(("arbitrary",)),
    )(*[a for (a, _, _) in row_args], *full_args)


def _rms(x, g):
    return x * lax.rsqrt(jnp.mean(x * x, axis=-1, keepdims=True) + NORM_EPS) * g


def _rmsnorm_fwd(x, g, name):
    t, d = x.shape
    (h,) = _rows(lambda xb, gb: (_rms(xb, gb),), [(x, d, 0)], [g], [(d, BF16)], [], t=t, tm=_pick(t, 512, 8), name=name)
    return h


def _rmsnorm_bwd(x, g, dh, dx_in, name):
    t, d = x.shape

    def fn(xb, dhb, dxb, gb):
        _, vjp = jax.vjp(_rms, xb, gb)
        dx, dg = vjp(dhb)
        return dxb + dx, dg

    return _rows(fn, [(x, d, 0), (dh, d, 0), (dx_in, d, 0)], [g], [(d, F32)], [(1, d)], t=t, tm=_pick(t, 256, 8), name=name)


def _merge(yg, ys, lg, ls):
    return _sigmoid(lg) * yg + _sigmoid(ls) * ys


def _merge_fwd(y_gdn, y_swa, proj, gate_off, name):
    t, d = y_gdn.shape
    cb = gate_off // d
    (mix,) = _rows(lambda a, b, c, e: (_merge(a, b, c, e),), [(y_gdn, d, 0), (y_swa, d, 0), (proj, d, cb), (proj, d, cb + 1)], [],
                   [(d, BF16)], [], t=t, tm=_pick(t, 256, 8), name=name)
    return mix


def _merge_bwd(y_gdn, y_swa, proj, gate_off, dmix, name):
    t, d = y_gdn.shape
    cb = gate_off // d

    def fn(a, b, c, e, g):
        _, vjp = jax.vjp(_merge, a, b, c, e)
        da, db, dc, de = vjp(g)
        return da, db, jnp.concatenate([dc, de], axis=1)

    return _rows(fn, [(y_gdn, d, 0), (y_swa, d, 0), (proj, d, cb), (proj, d, cb + 1), (dmix, d, 0)], [],
                 [(d, BF16), (d, BF16), (2 * d, BF16)], [], t=t, tm=_pick(t, 128, 8), name=name)


def _relu2_fwd(up, name):
    t, f = up.shape
    (act,) = _rows(lambda u: (jnp.square(jnp.maximum(u, 0.0)),), [(up, f, 0)], [], [(f, BF16)], [], t=t, tm=_pick(t, 256, 8), name=name)
    return act


def _relu2_bwd(up, dact, name):
    t, f = up.shape
    (dup,) = _rows(lambda u, g: (g * 2.0 * jnp.maximum(u, 0.0),), [(up, f, 0), (dact, f, 0)], [], [(f, BF16)], [], t=t,
                   tm=_pick(t, 128, 8), name=name)
    return dup


def _loss_head(x, g, target, name):
    t, d = x.shape

    def loss_fn(xb, gb, tb):
        err = _rms(xb, gb) - tb
        return 0.5 * jnp.sum(jnp.mean(err * err, axis=-1, keepdims=True), axis=0, keepdims=True)

    def fn(xb, tb, gb):
        lv, vjp = jax.vjp(lambda a, b: loss_fn(a, b, tb), xb, gb)
        dx, dg = vjp(jnp.ones((1, 1), F32))
        return dx, dg, jnp.broadcast_to(lv, (1, LANES))

    return _rows(fn, [(x, d, 0), (target, d, 0)], [g], [(d, F32)], [(1, d), (1, LANES)], t=t, tm=_pick(t, 256, 8), name=name)


def _conv_silu(prev, cur, w, keep_prev):
    tm = cur.shape[0]
    xp = jnp.concatenate([prev * keep_prev, cur], axis=0)
    y = w[0:1, :] * xp[CONV_HALO - 3:CONV_HALO - 3 + tm]
    for j in range(1, CONV_K):
        y = y + w[j:j + 1, :] * xp[CONV_HALO - 3 + j:CONV_HALO - 3 + j + tm]
    return _silu(y)


def _conv_tiles(t, width):
    tm = _pick(t, 512, CONV_HALO)
    tc = _pick(width, 512)
    return tm, tc, t // tm, width // tc


def _conv_fwd(proj, conv_w, width, name):
    t = proj.shape[0]
    tm, tc, nt, ncw = _conv_tiles(t, width)
    hb = tm // CONV_HALO

    def body(prev_ref, cur_ref, w_ref, o_ref):
        keep = (pl.program_id(1) > 0).astype(F32)
        o_ref[0] = _conv_silu(prev_ref[...], cur_ref[...], w_ref[...], keep)

    return pl.pallas_call(
        body, name=name, grid=(3 * ncw, nt),
        in_specs=[pl.BlockSpec((CONV_HALO, tc), lambda j, i: (jnp.maximum(i * hb - 1, 0), j)),
                  pl.BlockSpec((tm, tc), lambda j, i: (i, j)),
                  pl.BlockSpec((CONV_K, tc), lambda j, i: (0, j))],
        out_specs=pl.BlockSpec((1, tm, tc), lambda j, i: (j // ncw, i, j % ncw)),
        out_shape=jax.ShapeDtypeStruct((3, t, width), F32),
        compiler_params=_params(("parallel", "arbitrary")),
    )(proj, proj, conv_w)


def _conv_bwd(proj, conv_w, dout, width, name):
    t = proj.shape[0]
    tm, tc, nt, ncw = _conv_tiles(t, width)
    hb = tm // CONV_HALO

    def body(prev_ref, cur_ref, w_ref, g_ref, dx_ref, dw_ref, carry_ref):
        s = pl.program_id(1)
        keep = (s < nt - 1).astype(F32)
        _, vjp = jax.vjp(lambda p, c, w: _conv_silu(p, c, w, keep), prev_ref[...], cur_ref[...], w_ref[...])
        dprev, dcur, dw = vjp(g_ref[0])

        @pl.when(s == 0)
        def _():
            carry_ref[...] = jnp.zeros_like(carry_ref)
            dw_ref[...] = dw

        @pl.when(s > 0)
        def _():
            dw_ref[...] += dw

        tail = jnp.concatenate([jnp.zeros((tm - CONV_HALO, tc), F32), carry_ref[...]], axis=0)
        dx_ref[...] = (dcur + tail).astype(dx_ref.dtype)
        carry_ref[...] = dprev

    def row(s):
        return nt - 1 - s

    return pl.pallas_call(
        body, name=name, grid=(3 * ncw, nt),
        in_specs=[pl.BlockSpec((CONV_HALO, tc), lambda j, s: (jnp.maximum(row(s) * hb - 1, 0), j)),
                  pl.BlockSpec((tm, tc), lambda j, s: (row(s), j)),
                  pl.BlockSpec((CONV_K, tc), lambda j, s: (0, j)),
                  pl.BlockSpec((1, tm, tc), lambda j, s: (j // ncw, row(s), j % ncw))],
        out_specs=[pl.BlockSpec((tm, tc), lambda j, s: (row(s), j)),
                   pl.BlockSpec((CONV_K, tc), lambda j, s: (0, j))],
        out_shape=[jax.ShapeDtypeStruct((t, 3 * width), BF16), jax.ShapeDtypeStruct((CONV_K, 3 * width), F32)],
        scratch_shapes=[pltpu.VMEM((CONV_HALO, tc), F32)],
        compiler_params=_params(("parallel", "arbitrary")),
    )(proj, proj, conv_w, dout)


def _inv_unit_lower_raw(mats):
    n = mats[0].shape[0]
    r = lax.broadcasted_iota(jnp.int32, (n, n), 0)
    c = lax.broadcasted_iota(jnp.int32, (n, n), 1)
    eye = (r == c).astype(F32)
    same = jnp.right_shift(r, 4) == jnp.right_shift(c, 4)
    dg = [jnp.where(same, a, 0.0) for a in mats]
    lo = [a - d for a, d in zip(mats, dg)]
    p = [eye - d for d in dg]
    q = dg
    for _ in range(3):
        q = [_hdot(x, x) for x in q]
        p = [_hdot(x, eye + y) for x, y in zip(p, q)]
    nm = [_hdot(x, y) for x, y in zip(p, lo)]
    n2 = [_hdot(x, x) for x in nm]
    left = [_hdot(eye - x, eye + y) for x, y in zip(nm, n2)]
    return [_hdot(x, y) for x, y in zip(left, p)]


@jax.custom_vjp
def _inv_unit_lower(mats):
    return _inv_unit_lower_raw(mats)


def _inv_fwd(mats):
    t = _inv_unit_lower_raw(mats)
    return t, t


def _inv_bwd(ts, gs):
    x = [_hdot(t, g, TN) for t, g in zip(ts, gs)]
    return ([-_hdot(a, t, NT) for a, t in zip(x, ts)],)


_inv_unit_lower.defvjp(_inv_fwd, _inv_bwd)


def _l2n(x):
    return x * lax.rsqrt(jnp.sum(x * x, axis=-1, keepdims=True) + NORM_EPS)


def _gdn_chunk(qcs, kcs, vcs, zs, bg, alog_row, dtb_row, gnorm, states, first_head, n_heads):
    nb = len(qcs)
    hs = range(nb)
    cs = qcs[0].shape[0]
    lane = lax.broadcasted_iota(jnp.int32, (1, LANES), 1)
    r = lax.broadcasted_iota(jnp.int32, (cs, cs), 0)
    c = lax.broadcasted_iota(jnp.int32, (cs, cs), 1)
    q = [_l2n(x) * (GDN_HEAD_DIM ** -0.5) for x in qcs]
    k = [_l2n(x) for x in kcs]
    beta = [_sigmoid(_lane_pick(bg, lane, first_head + i)) for i in hs]
    g = [-jnp.exp(_lane_pick(alog_row, lane, first_head + i)) *
         _softplus(_lane_pick(bg, lane, n_heads + first_head + i) + _lane_pick(dtb_row, lane, first_head + i)) for i in hs]
    g_row = [jnp.sum(jnp.where(r == c, x, 0.0), axis=0, keepdims=True) for x in g]
    dec_col = [jnp.sum(jnp.where(r >= c, x, 0.0), axis=1, keepdims=True) for x in g_row]
    dec_row = [jnp.sum(jnp.where(r <= c, x, 0.0), axis=0, keepdims=True) for x in g]
    gamma = [jnp.exp(jnp.where(r >= c, dc - dr, -1e30)) for dc, dr in zip(dec_col, dec_row)]
    kb = [x * b for x, b in zip(k, beta)]
    a = [jnp.where(r > c, mm_nt(x, y) * gm, 0.0) for x, y, gm in zip(kb, k, gamma)]
    tinv = _inv_unit_lower(a)
    e_col = [jnp.exp(x) for x in dec_col]
    u = [mm_nn(t, v * b) for t, v, b in zip(tinv, vcs, beta)]
    w = [mm_nn(t, x * e) for t, x, e in zip(tinv, kb, e_col)]
    qk = [mm_nt(x, y) * gm for x, y, gm in zip(q, k, gamma)]
    total = [jnp.sum(x, axis=0, keepdims=True) for x in g]
    v_new = [x - mm_nn(y, s) for x, y, s in zip(u, w, states)]
    o = [mm_nn(x * e, s) + mm_nn(y, v) for x, e, s, y, v in zip(q, e_col, states, qk, v_new)]
    new_states = [s * jnp.exp(tt) + mm_tn(x * jnp.exp(tt - dc), v) for s, tt, x, dc, v in zip(states, total, k, dec_col, v_new)]
    ys = [_rms(x, gnorm) * _silu(z) for x, z in zip(o, zs)]
    return ys, new_states


GDN_HEADS_FWD = 16
GDN_HEADS_BWD = 16


def _gdn_fwd(qkvc, proj, alog_row, dtb_row, gnorm, *, d, z_off, bg_off, name, hb=GDN_HEADS_FWD):
    t = qkvc.shape[1]
    nh = d // GDN_HEAD_DIM
    hb = min(hb, nh)
    wb = hb * GDN_HEAD_DIM
    nc = t // CHUNK
    ng = nh // hb

    def body(qkv_ref, z_ref, bg_ref, al_ref, dt_ref, gn_ref, y_ref, sin_ref, s_scr):
        n, hg = pl.program_id(0), pl.program_id(1)

        @pl.when(n == 0)
        def _():
            s_scr[hg] = jnp.zeros((hb, GDN_HEAD_DIM, GDN_HEAD_DIM), F32)

        states = [s_scr[hg, i] for i in range(hb)]
        bg, al, dt, gn = bg_ref[...], al_ref[...], dt_ref[...], gn_ref[...]
        sls = [slice(i * GDN_HEAD_DIM, (i + 1) * GDN_HEAD_DIM) for i in range(hb)]
        ys, new_states = _gdn_chunk([qkv_ref[0, :, sl] for sl in sls], [qkv_ref[1, :, sl] for sl in sls], [qkv_ref[2, :, sl] for sl in sls],
                                    [z_ref[:, sl] for sl in sls], bg, al, dt, gn, states, hg * hb, nh)
        for i in range(hb):
            sin_ref[0, i] = states[i]
            y_ref[:, sls[i]] = ys[i].astype(y_ref.dtype)
            s_scr[hg, i] = new_states[i]

    row = lambda n, hg: (0, 0)
    return pl.pallas_call(
        body, name=name, grid=(nc, ng),
        in_specs=[pl.BlockSpec((3, CHUNK, wb), lambda n, hg: (0, n, hg)),
                  pl.BlockSpec((CHUNK, wb), lambda n, hg: (n, z_off // wb + hg)),
                  pl.BlockSpec((CHUNK, LANES), lambda n, hg: (n, bg_off // LANES)),
                  pl.BlockSpec((1, LANES), row), pl.BlockSpec((1, LANES), row), pl.BlockSpec((1, LANES), row)],
        out_specs=[pl.BlockSpec((CHUNK, wb), lambda n, hg: (n, hg)),
                   pl.BlockSpec((1, hb, GDN_HEAD_DIM, GDN_HEAD_DIM), lambda n, hg: (n, hg, 0, 0))],
        out_shape=[jax.ShapeDtypeStruct((t, d), BF16), jax.ShapeDtypeStruct((nc, nh, GDN_HEAD_DIM, GDN_HEAD_DIM), F32)],
        scratch_shapes=[pltpu.VMEM((ng, hb, GDN_HEAD_DIM, GDN_HEAD_DIM), F32)],
        compiler_params=_params(("arbitrary", "arbitrary")),
    )(qkvc, proj, proj, alog_row, dtb_row, gnorm)


def _gdn_bwd(qkvc, proj, alog_row, dtb_row, gnorm, states, dy, *, d, z_off, bg_off, name, hb=GDN_HEADS_BWD):
    t = qkvc.shape[1]
    nh = d // GDN_HEAD_DIM
    hb = min(hb, nh)
    wb = hb * GDN_HEAD_DIM
    nc = t // CHUNK
    ng = nh // hb

    def body(qkv_ref, z_ref, bg_ref, al_ref, dt_ref, gn_ref, sin_ref, dy_ref, dqkv_ref, dz_ref, dbg_ref, dal_ref, ddt_ref, dgn_ref,
             ds_scr):
        s, hg = pl.program_id(0), pl.program_id(1)

        @pl.when(s == 0)
        def _():
            ds_scr[hg] = jnp.zeros((hb, GDN_HEAD_DIM, GDN_HEAD_DIM), F32)

        @pl.when(hg == 0)
        def _():
            dbg_ref[...] = jnp.zeros_like(dbg_ref)

        @pl.when((s == 0) & (hg == 0))
        def _():
            dal_ref[...] = jnp.zeros_like(dal_ref)
            ddt_ref[...] = jnp.zeros_like(ddt_ref)
            dgn_ref[...] = jnp.zeros_like(dgn_ref)

        dstates = [ds_scr[hg, i] for i in range(hb)]
        bg, al, dt, gn = bg_ref[...], al_ref[...], dt_ref[...], gn_ref[...]
        sls = [slice(i * GDN_HEAD_DIM, (i + 1) * GDN_HEAD_DIM) for i in range(hb)]
        fn = functools.partial(_gdn_chunk, first_head=hg * hb, n_heads=nh)
        _, vjp = jax.vjp(fn, [qkv_ref[0, :, sl] for sl in sls], [qkv_ref[1, :, sl] for sl in sls], [qkv_ref[2, :, sl] for sl in sls],
                         [z_ref[:, sl] for sl in sls], bg, al, dt, gn, [sin_ref[0, i] for i in range(hb)])
        dq, dk, dv, dz, dbg, dal, ddt, dgn, dst = vjp(([dy_ref[:, sl] for sl in sls], dstates))
        for i in range(hb):
            dqkv_ref[0, :, sls[i]] = dq[i]
            dqkv_ref[1, :, sls[i]] = dk[i]
            dqkv_ref[2, :, sls[i]] = dv[i]
            dz_ref[:, sls[i]] = dz[i].astype(dz_ref.dtype)
            ds_scr[hg, i] = dst[i]
        dbg_ref[...] += dbg
        dal_ref[...] += dal
        ddt_ref[...] += ddt
        dgn_ref[...] += dgn

    def ch(s):
        return nc - 1 - s

    row = lambda s, hg: (0, 0)
    return pl.pallas_call(
        body, name=name, grid=(nc, ng),
        in_specs=[pl.BlockSpec((3, CHUNK, wb), lambda s, hg: (0, ch(s), hg)),
                  pl.BlockSpec((CHUNK, wb), lambda s, hg: (ch(s), z_off // wb + hg)),
                  pl.BlockSpec((CHUNK, LANES), lambda s, hg: (ch(s), bg_off // LANES)),
                  pl.BlockSpec((1, LANES), row), pl.BlockSpec((1, LANES), row), pl.BlockSpec((1, LANES), row),
                  pl.BlockSpec((1, hb, GDN_HEAD_DIM, GDN_HEAD_DIM), lambda s, hg: (ch(s), hg, 0, 0)),
                  pl.BlockSpec((CHUNK, wb), lambda s, hg: (ch(s), hg))],
        out_specs=[pl.BlockSpec((3, CHUNK, wb), lambda s, hg: (0, ch(s), hg)),
                   pl.BlockSpec((CHUNK, wb), lambda s, hg: (ch(s), hg)),
                   pl.BlockSpec((CHUNK, LANES), lambda s, hg: (ch(s), 0)),
                   pl.BlockSpec((1, LANES), row), pl.BlockSpec((1, LANES), row), pl.BlockSpec((1, LANES), row)],
        out_shape=[jax.ShapeDtypeStruct((3, t, d), F32), jax.ShapeDtypeStruct((t, d), BF16), jax.ShapeDtypeStruct((t, LANES), F32),
                   jax.ShapeDtypeStruct((1, LANES), F32), jax.ShapeDtypeStruct((1, LANES), F32), jax.ShapeDtypeStruct((1, LANES), F32)],
        scratch_shapes=[pltpu.VMEM((ng, hb, GDN_HEAD_DIM, GDN_HEAD_DIM), F32)],
        compiler_params=_params(("arbitrary", "arbitrary")),
    )(qkvc, proj, proj, alog_row, dtb_row, gnorm, states, dy)


@jax.custom_vjp
def _swap_halves(x):
    return pltpu.roll(x, SWA_HEAD_DIM, 1)


_swap_halves.defvjp(lambda x: (pltpu.roll(x, SWA_HEAD_DIM, 1), None), lambda _, g: (pltpu.roll(g, SWA_HEAD_DIM, 1),))

SWA_PAIR_Q = 2 * GQA_GROUP * SWA_HEAD_DIM


def _swa_block(q, kp, kc, vp, vc, sink_row, slope_row, keep_prev, pair):
    kb = jnp.concatenate([kp, kc], axis=0)
    vb = jnp.concatenate([vp, vc], axis=0)
    lane = lax.broadcasted_iota(jnp.int32, (1, LANES), 1)
    low = lane < SWA_HEAD_DIM
    high = jnp.logical_not(low)
    qi = lax.broadcasted_iota(jnp.int32, (WINDOW, 2 * WINDOW), 0)
    sj = lax.broadcasted_iota(jnp.int32, (WINDOW, 2 * WINDOW), 1)
    dist = qi + WINDOW - sj
    valid = (dist >= 0) & (dist < WINDOW) & ((sj >= WINDOW) | (keep_prev > 0.5))
    distf = dist.astype(F32)
    kk, vv = [], []
    for mine in (low, high):
        x = jnp.where(mine, kb, 0.0)
        kk.append(x + _swap_halves(x))
        y = jnp.where(mine, vb, 0.0)
        vv.append(y + _swap_halves(y))
    hl = range(2 * GQA_GROUP)
    half = [low if h % 2 == 0 else high for h in hl]
    slope = [_lane_pick(slope_row, lane, pair * (2 * GQA_GROUP) + h) for h in hl]
    sink = [_lane_pick(sink_row, lane, pair * (2 * GQA_GROUP) + h) for h in hl]
    qm = [jnp.where(half[h], q[:, (h // 2) * LANES:(h // 2 + 1) * LANES], 0.0) for h in hl]
    sc = [mm_nt(qm[h], kk[h // GQA_GROUP]) * (SWA_HEAD_DIM ** -0.5) for h in hl]
    sc = [jnp.where(valid, sc[h] - slope[h] * distf, -1e30) for h in hl]
    m = [lax.stop_gradient(jnp.maximum(jnp.max(sc[h], axis=-1, keepdims=True), sink[h])) for h in hl]
    p = [jnp.exp(sc[h] - m[h]) for h in hl]
    probs = [p[h] / (jnp.sum(p[h], axis=-1, keepdims=True) + jnp.exp(sink[h] - m[h])) for h in hl]
    od = [jnp.where(half[h], mm_nn(probs[h], vv[h // GQA_GROUP]), 0.0) for h in hl]
    return jnp.concatenate([od[2 * i] + od[2 * i + 1] for i in range(GQA_GROUP)], axis=1)


def _swa_specs(t, q_off, k_off, v_off, order):
    nb = t // WINDOW

    def blk(s):
        return order(s, nb)

    return nb, [pl.BlockSpec((WINDOW, SWA_PAIR_Q), lambda p, s: (blk(s), q_off // SWA_PAIR_Q + p)),
                pl.BlockSpec((WINDOW, LANES), lambda p, s: (jnp.maximum(blk(s) - 1, 0), k_off // LANES + p)),
                pl.BlockSpec((WINDOW, LANES), lambda p, s: (blk(s), k_off // LANES + p)),
                pl.BlockSpec((WINDOW, LANES), lambda p, s: (jnp.maximum(blk(s) - 1, 0), v_off // LANES + p)),
                pl.BlockSpec((WINDOW, LANES), lambda p, s: (blk(s), v_off // LANES + p)),
                pl.BlockSpec((1, LANES), lambda p, s: (0, 0)), pl.BlockSpec((1, LANES), lambda p, s: (0, 0))]


def _swa_fwd(proj, sink_row, slope_row, *, d, q_off, k_off, v_off, name):
    t = proj.shape[0]
    n_pairs = d // SWA_PAIR_Q
    nb, in_specs = _swa_specs(t, q_off, k_off, v_off, lambda s, nb: s)

    def body(q_ref, kp_ref, kc_ref, vp_ref, vc_ref, sink_ref, slope_ref, o_ref):
        keep = (pl.program_id(1) > 0).astype(F32)
        o = _swa_block(q_ref[...], kp_ref[...], kc_ref[...], vp_ref[...], vc_ref[...], sink_ref[...], slope_ref[...], keep,
                       pl.program_id(0))
        o_ref[...] = o.astype(o_ref.dtype)

    return pl.pallas_call(
        body, name=name, grid=(n_pairs, nb), in_specs=in_specs,
        out_specs=pl.BlockSpec((WINDOW, SWA_PAIR_Q), lambda p, s: (s, p)),
        out_shape=jax.ShapeDtypeStruct((t, d), BF16),
        compiler_params=_params(("parallel", "arbitrary")),
    )(proj, proj, proj, proj, proj, sink_row, slope_row)


def _swa_bwd(proj, sink_row, slope_row, do, *, d, q_off, k_off, v_off, name):
    t = proj.shape[0]
    n_pairs = d // SWA_PAIR_Q
    nb, in_specs = _swa_specs(t, q_off, k_off, v_off, lambda s, nb: nb - 1 - s)

    def body(q_ref, kp_ref, kc_ref, vp_ref, vc_ref, sink_ref, slope_ref, do_ref, dq_ref, dk_ref, dv_ref, dsink_ref, ck_ref, cv_ref):
        p, s = pl.program_id(0), pl.program_id(1)
        keep = (s < nb - 1).astype(F32)
        fn = functools.partial(_swa_block, slope_row=slope_ref[...], keep_prev=keep, pair=p)
        _, vjp = jax.vjp(fn, q_ref[...], kp_ref[...], kc_ref[...], vp_ref[...], vc_ref[...], sink_ref[...])
        dq, dkp, dkc, dvp, dvc, dsink = vjp(do_ref[...])

        @pl.when(s == 0)
        def _():
            ck_ref[...] = jnp.zeros_like(ck_ref)
            cv_ref[...] = jnp.zeros_like(cv_ref)

        @pl.when((s == 0) & (p == 0))
        def _():
            dsink_ref[...] = jnp.zeros_like(dsink_ref)

        dq_ref[...] = dq.astype(dq_ref.dtype)
        dk_ref[...] = (dkc + ck_ref[...]).astype(dk_ref.dtype)
        dv_ref[...] = (dvc + cv_ref[...]).astype(dv_ref.dtype)
        ck_ref[...] = dkp
        cv_ref[...] = dvp
        dsink_ref[...] += dsink

    in_specs = in_specs + [pl.BlockSpec((WINDOW, SWA_PAIR_Q), lambda p, s: (nb - 1 - s, p))]
    kv_w = d // GQA_GROUP
    return pl.pallas_call(
        body, name=name, grid=(n_pairs, nb), in_specs=in_specs,
        out_specs=[pl.BlockSpec((WINDOW, SWA_PAIR_Q), lambda p, s: (nb - 1 - s, p)),
                   pl.BlockSpec((WINDOW, LANES), lambda p, s: (nb - 1 - s, p)),
                   pl.BlockSpec((WINDOW, LANES), lambda p, s: (nb - 1 - s, p)),
                   pl.BlockSpec((1, LANES), lambda p, s: (0, 0))],
        out_shape=[jax.ShapeDtypeStruct((t, d), BF16), jax.ShapeDtypeStruct((t, kv_w), BF16), jax.ShapeDtypeStruct((t, kv_w), BF16),
                   jax.ShapeDtypeStruct((1, LANES), F32)],
        scratch_shapes=[pltpu.VMEM((WINDOW, LANES), F32), pltpu.VMEM((WINDOW, LANES), F32)],
        compiler_params=_params(("arbitrary", "arbitrary")),
    )(proj, proj, proj, proj, proj, sink_row, slope_row, do)


def _layout(d):
    kv = d // GQA_GROUP
    return dict(z=3 * d, q=4 * d, gate=5 * d, k=7 * d, v=7 * d + kv, bg=7 * d + 2 * kv, width=7 * d + 2 * kv + LANES)


def _pack_w_in(w, d):
    nh = d // GDN_HEAD_DIM
    kv = d // GQA_GROUP
    o = 4 * d + 2 * nh
    parts = [w[..., :4 * d], w[..., o:o + d], w[..., o + d + 2 * kv:o + 3 * d + 2 * kv], w[..., o + d:o + d + 2 * kv],
             w[..., 4 * d:o], jnp.zeros(w.shape[:-1] + (LANES - 2 * nh,), w.dtype)]
    return jnp.concatenate(parts, axis=-1)


def _unpack_w_in(wp, d):
    nh = d // GDN_HEAD_DIM
    kv = d // GQA_GROUP
    lay = _layout(d)
    parts = [wp[..., :4 * d], wp[..., lay["bg"]:lay["bg"] + 2 * nh], wp[..., lay["q"]:lay["q"] + d],
             wp[..., lay["k"]:lay["k"] + 2 * kv], wp[..., lay["gate"]:lay["gate"] + 2 * d]]
    return jnp.concatenate(parts, axis=-1)


def _pad_row(v):
    return jnp.pad(v.astype(F32), (0, LANES - v.shape[0]))[None, :]


def _alibi_row(d):
    nq = d // SWA_HEAD_DIM
    return _pad_row(2.0 ** (-8.0 * jnp.arange(1, nq + 1, dtype=F32) / nq))


def _layer_fwd(x, p, tag):
    t, d = x.shape
    lay = _layout(d)
    tn = 1152 if lay["width"] % 1152 == 0 else 1024
    h1 = _rmsnorm_fwd(x, p["norm1_g"], tag + "rms1")
    proj = _matmul(h1, p["w_in"], name=tag + "mm_in", tn_cap=tn)
    qkvc = _conv_fwd(proj, p["conv_w"], d, tag + "conv")
    gdn_o, states = _gdn_fwd(qkvc, proj, p["a_log"], p["dt_bias"], p["gdn_norm_g"], d=d, z_off=lay["z"], bg_off=lay["bg"],
                             name=tag + "gdn")
    swa_o = _swa_fwd(proj, p["attn_sinks"], p["alibi"], d=d, q_off=lay["q"], k_off=lay["k"], v_off=lay["v"], name=tag + "swa")
    y_gdn = _matmul(gdn_o, p["w_branch_gdn"], name=tag + "mm_bg")
    y_swa = _matmul(swa_o, p["w_branch_swa"], name=tag + "mm_bs")
    mix = _merge_fwd(y_gdn, y_swa, proj, lay["gate"], tag + "merge")
    x1 = _matmul(mix, p["w_out"], add=x, name=tag + "mm_out")
    h2 = _rmsnorm_fwd(x1, p["norm2_g"], tag + "rms2")
    up = _matmul(h2, p["w_ff_up"], name=tag + "mm_up", b_split=N_CHIPS)
    act = _relu2_fwd(up, tag + "relu2")
    x2 = _matmul(act, p["w_ff_down"], add=x1, name=tag + "mm_down")
    return x2, dict(x=x, h1=h1, proj=proj, qkvc=qkvc, states=states, gdn_o=gdn_o, swa_o=swa_o, y_gdn=y_gdn, y_swa=y_swa, mix=mix,
                    x1=x1, h2=h2, up=up, act=act)


def _layer_bwd(dx2, p, s, tag):
    t, d = dx2.shape
    lay = _layout(d)
    tn = 1152 if lay["width"] % 1152 == 0 else 1024
    nh = d // GDN_HEAD_DIM
    g = {}
    dact = _matmul(dx2, p["w_ff_down"], tb=True, name=tag + "mm_dact", tm_cap=512)
    g["w_ff_down"] = _matmul(s["act"], dx2, ta=True, out_dtype=BF16, name=tag + "mm_dwdown", tk_cap=1024)
    dup = _relu2_bwd(s["up"], dact, tag + "relu2b")
    g["w_ff_up"] = _matmul(s["h2"], dup, ta=True, out_dtype=BF16, name=tag + "mm_dwup", out_split=N_CHIPS)
    dh2 = _matmul(dup, p["w_ff_up"], tb=True, name=tag + "mm_dh2", b_split=N_CHIPS)
    dx1, g["norm2_g"] = _rmsnorm_bwd(s["x1"], p["norm2_g"], dh2, dx2, tag + "rms2b")
    dmix = _matmul(dx1, p["w_out"], tb=True, name=tag + "mm_dmix", tm_cap=512)
    g["w_out"] = _matmul(s["mix"], dx1, ta=True, out_dtype=BF16, name=tag + "mm_dwout", tk_cap=1024)
    dyg, dys, dgl = _merge_bwd(s["y_gdn"], s["y_swa"], s["proj"], lay["gate"], dmix, tag + "mergeb")
    g["w_branch_gdn"] = _matmul(s["gdn_o"], dyg, ta=True, out_dtype=BF16, name=tag + "mm_dwbg")
    g["w_branch_swa"] = _matmul(s["swa_o"], dys, ta=True, out_dtype=BF16, name=tag + "mm_dwbs")
    dgdn_o = _matmul(dyg, p["w_branch_gdn"], tb=True, name=tag + "mm_dgdn")
    dswa_o = _matmul(dys, p["w_branch_swa"], tb=True, name=tag + "mm_dswa")
    dq_s, dk_s, dv_s, dsink = _swa_bwd(s["proj"], p["attn_sinks"], p["alibi"], dswa_o, d=d, q_off=lay["q"], k_off=lay["k"],
                                       v_off=lay["v"], name=tag + "swab")
    dqkvc, dz, dbg, dal, ddt, dgn = _gdn_bwd(s["qkvc"], s["proj"], p["a_log"], p["dt_bias"], p["gdn_norm_g"], s["states"], dgdn_o,
                                             d=d, z_off=lay["z"], bg_off=lay["bg"], name=tag + "gdnb")
    dqkv, g["conv_w"] = _conv_bwd(s["proj"], p["conv_w"], dqkvc, d, tag + "convb")
    dproj = jnp.concatenate([dqkv, dz, dq_s, dgl, dk_s, dv_s, lax.reduce_precision(dbg, 8, 7).astype(BF16)], axis=1)
    g["w_in"] = _matmul(s["h1"], dproj, ta=True, out_dtype=BF16, name=tag + "mm_dwin", tn_cap=tn)
    dh1 = _matmul(dproj, p["w_in"], tb=True, name=tag + "mm_dh1", tk_cap=tn)
    dx, g["norm1_g"] = _rmsnorm_bwd(s["x"], p["norm1_g"], dh1, dx1, tag + "rms1b")
    g["a_log"], g["dt_bias"], g["gdn_norm_g"], g["attn_sinks"] = dal[0, :nh], ddt[0, :nh], dgn[0], dsink[0, :d // SWA_HEAD_DIM]
    return dx, g


MESH = pl.DeviceIdType.MESH
HBM_SPEC = pl.BlockSpec(memory_space=pl.ANY)


def _place():
    x, y, c = lax.axis_index("x"), lax.axis_index("y"), lax.axis_index("c")
    return x, y, c, 2 * x + y


def _flip(x, y, k):
    px, py = x ^ (k >> 1), y ^ (k & 1)
    return px, py, 2 * px + py


def _cast_into_slot(w, layer, pos, name):
    _, r, cols = w.shape
    tm = _pick(r, max(16, (1 << 19) // cols // 16 * 16), 16)

    def body(x_ref, y_ref, w_ref, o_ref):
        o_ref[...] = w_ref[...].astype(o_ref.dtype)

    return pl.pallas_call(
        body, name=name,
        grid_spec=pltpu.PrefetchScalarGridSpec(
            num_scalar_prefetch=2, grid=(r // tm,),
            in_specs=[pl.BlockSpec((None, tm, cols), lambda i, xr, yr: (layer, i, 0))],
            out_specs=pl.BlockSpec((None, tm, cols), lambda i, xr, yr: (2 * xr[0] + yr[0], i, 0))),
        out_shape=jax.ShapeDtypeStruct((N_CHIPS, r, cols), BF16),
        compiler_params=_params(("parallel",)),
    )(pos[0], pos[1], w)


SEM_SPEC = pl.BlockSpec(memory_space=pltpu.SEMAPHORE)
SPLIT_COPY = pltpu.CompilerParams(has_side_effects=pltpu.SideEffectType.DATAFLOW_SIDE_EFFECTING)
TOKEN = jax.ShapeDtypeStruct((8, LANES), F32)
TOKEN_SPEC = pl.BlockSpec(memory_space=pltpu.VMEM)


def _gather_start(bufs, name):
    n = len(bufs)

    def body(*refs):
        outs, sems, token = refs[n:2 * n], refs[2 * n:8 * n], refs[8 * n]
        x, y, c, ci = _place()
        for a in range(n):
            hr = bufs[a].shape[1] // 2
            mine = outs[a].at[ci, pl.ds(c * hr, hr)]
            for k in (1, 2, 3):
                px, py, _ = _flip(x, y, k)
                pltpu.make_async_remote_copy(src_ref=mine, dst_ref=mine, send_sem=sems[3 * a + k - 1], recv_sem=sems[3 * n + 3 * a + k - 1],
                                             device_id=(px, py, c), device_id_type=MESH).start()
        token[...] = jnp.zeros_like(token)

    res = pl.pallas_call(
        body, name=name, in_specs=[HBM_SPEC] * n, out_specs=[HBM_SPEC] * n + [SEM_SPEC] * (6 * n) + [TOKEN_SPEC],
        out_shape=[pltpu.HBM(b.shape, b.dtype) for b in bufs] + [pltpu.SemaphoreType.DMA(())] * (6 * n) + [TOKEN],
        input_output_aliases={a: a for a in range(n)}, compiler_params=SPLIT_COPY,
    )(*[pltpu.with_memory_space_constraint(b, pltpu.HBM) for b in bufs])
    return res[:n], res[n:7 * n], res[7 * n]


def _gather_wait(bufs, sems, after, name):
    n = len(bufs)

    def body(*refs):
        sems, outs = refs[n:7 * n], refs[7 * n + 1:8 * n + 1]
        x, y, c, ci = _place()
        for a in range(n):
            hr = bufs[a].shape[1] // 2
            mine = outs[a].at[ci, pl.ds(c * hr, hr)]
            for k in (1, 2, 3):
                px, py, pj = _flip(x, y, k)
                landed = outs[a].at[pj, pl.ds(c * hr, hr)]
                cp = pltpu.make_async_remote_copy(src_ref=mine, dst_ref=landed, send_sem=sems[3 * a + k - 1], recv_sem=sems[3 * n + 3 * a + k - 1],
                                                  device_id=(px, py, c), device_id_type=MESH)
                cp.wait_send()
                cp.wait_recv()

    return pl.pallas_call(
        body, name=name, in_specs=[HBM_SPEC] * n + [SEM_SPEC] * (6 * n) + [HBM_SPEC], out_specs=[HBM_SPEC] * n,
        out_shape=[pltpu.HBM(b.shape, b.dtype) for b in bufs],
        input_output_aliases={a: a for a in range(n)}, compiler_params=SPLIT_COPY,
    )(*bufs, *sems, after)


def _gather_forward(bufs, name):
    n = len(bufs)

    def body(*refs):
        outs = refs[n:2 * n]
        send_sems, recv_sems = refs[2 * n:]
        x, y, c, _ = _place()
        waits = []
        for a in range(n):
            hr = bufs[a].shape[1] // 2
            for k in (1, 2, 3):
                _, _, pj = _flip(x, y, k)
                landed = outs[a].at[pj, pl.ds(c * hr, hr)]
                fw = pltpu.make_async_remote_copy(src_ref=landed, dst_ref=landed, send_sem=send_sems.at[a, k - 1], recv_sem=recv_sems.at[a, k - 1],
                                                  device_id=(x, y, 1 - c), device_id_type=MESH)
                fw.start()
                waits.append(fw.wait_send)
                passed = outs[a].at[pj, pl.ds((1 - c) * hr, hr)]
                waits.append(pltpu.make_async_remote_copy(src_ref=passed, dst_ref=passed, send_sem=send_sems.at[a, k - 1],
                                                          recv_sem=recv_sems.at[a, k - 1], device_id=(x, y, 1 - c),
                                                          device_id_type=MESH).wait_recv)
        for w in waits:
            w()

    return pl.pallas_call(
        body, name=name, in_specs=[HBM_SPEC] * n, out_specs=[HBM_SPEC] * n,
        out_shape=[jax.ShapeDtypeStruct(b.shape, b.dtype) for b in bufs],
        input_output_aliases={a: a for a in range(n)},
        scratch_shapes=[pltpu.SemaphoreType.DMA((n, 3))] * 2,
    )(*bufs)


def _swap_with_sibling(gs, name):
    n = len(gs)

    def body(*refs):
        ins, outs = refs[:n], refs[n:2 * n]
        send_sems, recv_sems = refs[2 * n:]
        x, y, c, _ = _place()
        cps = []
        for a in range(n):
            hr = gs[a].shape[1] // 2
            cp = pltpu.make_async_remote_copy(src_ref=ins[a].at[:, pl.ds((1 - c) * hr, hr)], dst_ref=outs[a], send_sem=send_sems.at[a],
                                              recv_sem=recv_sems.at[a], device_id=(x, y, 1 - c), device_id_type=MESH)
            cp.start()
            cps.append(cp)
        for cp in cps:
            cp.wait()

    return pl.pallas_call(
        body, name=name, in_specs=[HBM_SPEC] * n, out_specs=[HBM_SPEC] * n,
        out_shape=[jax.ShapeDtypeStruct((g.shape[0], g.shape[1] // 2, g.shape[2]), g.dtype) for g in gs],
        scratch_shapes=[pltpu.SemaphoreType.DMA((n,))] * 2,
    )(*gs)


def _scatter_start(hs, name):
    n = len(hs)

    def body(*refs):
        srcs, lands, sems, token = refs[n:2 * n], refs[2 * n:3 * n], refs[3 * n:9 * n], refs[9 * n]
        x, y, c, ci = _place()
        for a in range(n):
            for k in (1, 2, 3):
                px, py, pj = _flip(x, y, k)
                pltpu.make_async_remote_copy(src_ref=srcs[a].at[pj], dst_ref=lands[a].at[ci], send_sem=sems[3 * a + k - 1],
                                             recv_sem=sems[3 * n + 3 * a + k - 1], device_id=(px, py, c), device_id_type=MESH).start()
        token[...] = jnp.zeros_like(token)

    res = pl.pallas_call(
        body, name=name, in_specs=[HBM_SPEC] * n, out_specs=[HBM_SPEC] * (2 * n) + [SEM_SPEC] * (6 * n) + [TOKEN_SPEC],
        out_shape=[pltpu.HBM(h.shape, h.dtype) for h in hs] * 2 + [pltpu.SemaphoreType.DMA(())] * (6 * n) + [TOKEN],
        input_output_aliases={a: a for a in range(n)}, compiler_params=SPLIT_COPY,
    )(*[pltpu.with_memory_space_constraint(h, pltpu.HBM) for h in hs])
    return res[:n], res[n:2 * n], res[2 * n:8 * n], res[8 * n]


def _scatter_wait(hs, lands, sems, after, name):
    n = len(hs)

    def body(*refs):
        sems, srcs, lands_o = refs[2 * n:8 * n], refs[8 * n + 1:9 * n + 1], refs[9 * n + 1:10 * n + 1]
        x, y, c, ci = _place()
        for a in range(n):
            for k in (1, 2, 3):
                px, py, pj = _flip(x, y, k)
                cp = pltpu.make_async_remote_copy(src_ref=srcs[a].at[pj], dst_ref=lands_o[a].at[pj], send_sem=sems[3 * a + k - 1],
                                                  recv_sem=sems[3 * n + 3 * a + k - 1], device_id=(px, py, c), device_id_type=MESH)
                cp.wait_send()
                cp.wait_recv()

    res = pl.pallas_call(
        body, name=name, in_specs=[HBM_SPEC] * (2 * n) + [SEM_SPEC] * (6 * n) + [HBM_SPEC], out_specs=[HBM_SPEC] * (2 * n),
        out_shape=[pltpu.HBM(h.shape, h.dtype) for h in hs] * 2,
        input_output_aliases={a: a for a in range(2 * n)}, compiler_params=SPLIT_COPY,
    )(*hs, *lands, *sems, after)
    return res[:n], res[n:]


def _share_with_sibling(bufs, name):
    n = len(bufs)

    def body(*refs):
        outs = refs[n:2 * n]
        send_sems, recv_sems = refs[2 * n:]
        x, y, c, _ = _place()
        waits = []
        for a in range(n):
            mine = outs[a].at[c]
            cp = pltpu.make_async_remote_copy(src_ref=mine, dst_ref=mine, send_sem=send_sems.at[a], recv_sem=recv_sems.at[a],
                                              device_id=(x, y, 1 - c), device_id_type=MESH)
            cp.start()
            waits.append(cp.wait_send)
            got = outs[a].at[1 - c]
            waits.append(pltpu.make_async_remote_copy(src_ref=got, dst_ref=got, send_sem=send_sems.at[a], recv_sem=recv_sems.at[a],
                                                      device_id=(x, y, 1 - c), device_id_type=MESH).wait_recv)
        for w in waits:
            w()

    return pl.pallas_call(
        body, name=name, in_specs=[HBM_SPEC] * n, out_specs=[HBM_SPEC] * n,
        out_shape=[jax.ShapeDtypeStruct(b.shape, b.dtype) for b in bufs],
        input_output_aliases={a: a for a in range(n)},
        scratch_shapes=[pltpu.SemaphoreType.DMA((n,))] * 2,
    )(*bufs)


def _add_sibling_half(g, got, core, name):
    nc, r, cols = g.shape
    hr = r // 2
    tm = _pick(hr, 256, 16)

    def body(core_ref, g_ref, o_ref, s_ref):
        s_ref[...] = (g_ref[...].astype(F32) + o_ref[...].astype(F32)).astype(s_ref.dtype)

    return pl.pallas_call(
        body, name=name,
        grid_spec=pltpu.PrefetchScalarGridSpec(
            num_scalar_prefetch=1, grid=(nc, hr // tm),
            in_specs=[pl.BlockSpec((None, None, tm, cols), lambda j, i, cr: (j, cr[0], i, 0)),
                      pl.BlockSpec((None, tm, cols), lambda j, i, cr: (j, i, 0))],
            out_specs=pl.BlockSpec((None, tm, cols), lambda j, i, cr: (j, i, 0))),
        out_shape=jax.ShapeDtypeStruct((nc, hr, cols), g.dtype),
        compiler_params=_params(("parallel", "parallel")),
    )(core, g.reshape(nc, 2, hr, cols), got)


def _sum_chips(own, parts, pos, name):
    nc, r, cols = parts.shape
    tm = _pick(r, 256, 16)

    def body(x_ref, y_ref, c_ref, own_ref, p_ref, o_ref):
        chip = 2 * x_ref[0] + y_ref[0]
        acc = own_ref[...].astype(F32)
        for k in range(1, nc):
            acc = acc + p_ref[chip ^ k].astype(F32)
        o_ref[...] = acc

    return pl.pallas_call(
        body, name=name,
        grid_spec=pltpu.PrefetchScalarGridSpec(
            num_scalar_prefetch=3, grid=(r // tm,),
            in_specs=[pl.BlockSpec((None, tm, cols), lambda i, xr, yr, cr: (2 * xr[0] + yr[0], i, 0)),
                      pl.BlockSpec((nc, tm, cols), lambda i, xr, yr, cr: (0, i, 0))],
            out_specs=pl.BlockSpec((None, tm, cols), lambda i, xr, yr, cr: (cr[0], i, 0))),
        out_shape=jax.ShapeDtypeStruct((2, r, cols), F32),
        compiler_params=_params(("parallel",)),
    )(*pos, own, parts)


def _reduce_scatter_start(gs, pos, tag):
    got = _swap_with_sibling(gs, tag + "rs_swap")
    hs = [_add_sibling_half(g, o, pos[2], tag + "rs_add%d" % i) for i, (g, o) in enumerate(zip(gs, got))]
    hs, lands, sems, token = _scatter_start(hs, tag + "rs_scatter_start")
    return (hs, lands, sems), token


def _reduce_scatter_finish(pending, after, pos, tag):
    hs, lands, sems = pending
    hs, parts = _scatter_wait(hs, lands, sems, after, tag + "rs_scatter_wait")
    rs = [_sum_chips(h, p, pos, tag + "rs_sum%d" % i) for i, (h, p) in enumerate(zip(hs, parts))]
    both = _share_with_sibling(rs, tag + "rs_share")
    return [b.reshape(2 * b.shape[1], b.shape[2]) for b in both]


def _allreduce_small(v, name):
    rows = v.shape[0]

    def body(v_ref, o_ref, buf, send_sems, recv_sems, local_sem):
        x, y, c, _ = _place()
        me, sibling = (x, y, c), (x, y, 1 - c)
        chips = [_flip(x, y, k)[:2] for k in (1, 2, 3)]

        def slot(px, py, pc):
            return buf.at[4 * px + 2 * py + pc]

        def copy(k, block, to, src=None):
            return pltpu.make_async_remote_copy(src_ref=slot(*block) if src is None else src, dst_ref=slot(*block), send_sem=send_sems.at[k],
                                                recv_sem=recv_sems.at[k], device_id=to, device_id_type=MESH)

        mine = pltpu.make_async_copy(v_ref, slot(*me), local_sem)
        mine.start()
        first = [copy(0, me, sibling, src=v_ref)] + [copy(1 + j, me, (*chip, c), src=v_ref) for j, chip in enumerate(chips)]
        for cp in first:
            cp.start()
        passed = [copy(4 + j, (*chip, c), sibling) for j, chip in enumerate(chips)]
        for j, chip in enumerate(chips):
            copy(1 + j, (*chip, c), me).wait_recv()
            passed[j].start()
        copy(0, sibling, me).wait_recv()
        for j, chip in enumerate(chips):
            copy(4 + j, (*chip, 1 - c), me).wait_recv()
        for cp in first + passed:
            cp.wait_send()
        mine.wait()
        acc = buf[0]
        for i in range(1, 2 * N_CHIPS):
            acc = acc + buf[i]
        o_ref[...] = acc

    vm = pl.BlockSpec(memory_space=pltpu.VMEM)
    return pl.pallas_call(
        body, name=name, in_specs=[vm], out_specs=vm, out_shape=jax.ShapeDtypeStruct(v.shape, F32),
        scratch_shapes=[pltpu.VMEM((2 * N_CHIPS, rows, LANES), F32), pltpu.SemaphoreType.DMA((7,)), pltpu.SemaphoreType.DMA((7,)),
                        pltpu.SemaphoreType.DMA],
        compiler_params=pltpu.CompilerParams(vmem_limit_bytes=VMEM_LIMIT),
    )(v)


def _adamw(w, g, m, v, name):
    r, cols = w.shape
    tm = _pick(r, max(8, (1 << 18) // max(cols, 1) // 8 * 8), 8)

    def body(w_ref, g_ref, m_ref, v_ref, d_ref, nm_ref, nv_ref):
        gg = g_ref[...]
        nm = ADAM_B1 * m_ref[...] + (1.0 - ADAM_B1) * gg
        nv = ADAM_B2 * v_ref[...] + (1.0 - ADAM_B2) * jnp.square(gg)
        m_hat = nm / (1.0 - ADAM_B1 ** ADAM_STEP)
        v_hat = nv / (1.0 - ADAM_B2 ** ADAM_STEP)
        d_ref[...] = -ADAM_LR * (m_hat / (jnp.sqrt(v_hat) + ADAM_EPS) + ADAM_WD * w_ref[...])
        nm_ref[...] = nm
        nv_ref[...] = nv

    spec = pl.BlockSpec((tm, cols), lambda i: (i, 0))
    return pl.pallas_call(
        body, name=name, grid=(r // tm,), in_specs=[spec] * 4, out_specs=[spec] * 3,
        out_shape=[jax.ShapeDtypeStruct((r, cols), F32)] * 3, compiler_params=_params(("parallel",)),
    )(w, g, m, v)


def _adamw_layer(w, g, m, v, layer, prev, name):
    depth, r, cols = w.shape
    tm = _pick(r, max(8, (1 << 18) // max(cols, 1) // 8 * 8), 8)

    def body(*refs):
        w_ref, g_ref, m_ref, v_ref = refs[:4]
        go_ref, d_ref, nm_ref, nv_ref = refs[-4:]
        gg = g_ref[...]
        nm = ADAM_B1 * m_ref[...] + (1.0 - ADAM_B1) * gg
        nv = ADAM_B2 * v_ref[...] + (1.0 - ADAM_B2) * jnp.square(gg)
        m_hat = nm / (1.0 - ADAM_B1 ** ADAM_STEP)
        v_hat = nv / (1.0 - ADAM_B2 ** ADAM_STEP)
        go_ref[...] = gg
        d_ref[...] = -ADAM_LR * (m_hat / (jnp.sqrt(v_hat) + ADAM_EPS) + ADAM_WD * w_ref[...])
        nm_ref[...] = nm
        nv_ref[...] = nv

    lspec = pl.BlockSpec((None, tm, cols), lambda i: (layer, i, 0))
    gspec = pl.BlockSpec((tm, cols), lambda i: (i, 0))
    extra = [] if prev is None else list(prev)
    return pl.pallas_call(
        body, name=name, grid=(r // tm,), in_specs=[lspec, gspec, lspec, lspec] + [HBM_SPEC] * len(extra), out_specs=[lspec] * 4,
        out_shape=[jax.ShapeDtypeStruct((depth, r, cols), F32)] * 4,
        input_output_aliases={4 + j: j for j in range(len(extra))}, compiler_params=_params(("parallel",)),
    )(w, g, m, v, *extra)


def _adamw_nd(w, g, m, v, name):
    shape = w.shape
    two = (1, shape[0]) if len(shape) == 1 else (int(np.prod(shape[:-1])), shape[-1])
    outs = _adamw(w.reshape(two), g.reshape(two), m.reshape(two), v.reshape(two), name)
    return [o.reshape(shape) for o in outs]


WEIGHTS = ("norm1_g", "w_in", "conv_w", "a_log", "dt_bias", "gdn_norm_g", "attn_sinks", "w_branch_gdn", "w_branch_swa", "w_out",
           "norm2_g", "w_ff_up", "w_ff_down", "final_norm_g")
MATRICES = ("w_in", "w_branch_gdn", "w_branch_swa", "w_out", "w_ff_up", "w_ff_down")


def _to_rows(vec):
    n = vec.shape[0]
    rows = -(-n // (8 * LANES)) * 8
    return jnp.pad(vec, (0, rows * LANES - n)).reshape(rows, LANES)


def kernel(x, norm1_g, w_in, conv_w, a_log, dt_bias, gdn_norm_g, attn_sinks, w_branch_gdn, w_branch_swa, w_out, norm2_g, w_ff_up, w_ff_down, final_norm_g, loss_target, m_norm1_g, m_w_in, m_conv_w, m_a_log, m_dt_bias, m_gdn_norm_g, m_attn_sinks, m_w_branch_gdn, m_w_branch_swa, m_w_out, m_norm2_g, m_w_ff_up, m_w_ff_down, m_final_norm_g, v_norm1_g, v_w_in, v_conv_w, v_a_log, v_dt_bias, v_gdn_norm_g, v_attn_sinks, v_w_branch_gdn, v_w_branch_swa, v_w_out, v_norm2_g, v_w_ff_up, v_w_ff_down, v_final_norm_g):
    w = dict(norm1_g=norm1_g, w_in=w_in, conv_w=conv_w, a_log=a_log, dt_bias=dt_bias, gdn_norm_g=gdn_norm_g, attn_sinks=attn_sinks,
             w_branch_gdn=w_branch_gdn, w_branch_swa=w_branch_swa, w_out=w_out, norm2_g=norm2_g, w_ff_up=w_ff_up, w_ff_down=w_ff_down,
             final_norm_g=final_norm_g)
    mom = dict(norm1_g=m_norm1_g, w_in=m_w_in, conv_w=m_conv_w, a_log=m_a_log, dt_bias=m_dt_bias, gdn_norm_g=m_gdn_norm_g,
               attn_sinks=m_attn_sinks, w_branch_gdn=m_w_branch_gdn, w_branch_swa=m_w_branch_swa, w_out=m_w_out, norm2_g=m_norm2_g,
               w_ff_up=m_w_ff_up, w_ff_down=m_w_ff_down, final_norm_g=m_final_norm_g)
    var = dict(norm1_g=v_norm1_g, w_in=v_w_in, conv_w=v_conv_w, a_log=v_a_log, dt_bias=v_dt_bias, gdn_norm_g=v_gdn_norm_g,
               attn_sinks=v_attn_sinks, w_branch_gdn=v_w_branch_gdn, w_branch_swa=v_w_branch_swa, w_out=v_w_out, norm2_g=v_norm2_g,
               w_ff_up=v_w_ff_up, w_ff_down=v_w_ff_down, final_norm_g=v_final_norm_g)
    depth, d = norm1_g.shape
    xs, target = x[0], loss_target[0]
    core = lax.axis_index("c")
    chip = 2 * lax.axis_index("x") + lax.axis_index("y")
    pos = tuple(jnp.reshape(lax.axis_index(a), (1,)).astype(jnp.int32) for a in ("x", "y", "c"))

    cw = conv_w.shape[-1]
    placed = lax.dynamic_update_slice(jnp.zeros((depth, CONV_K, N_CHIPS * cw), F32), conv_w, (0, 0, chip * cw))
    placed = placed * (core == 0).astype(F32)
    conv_full = _allreduce_small(_to_rows(placed.reshape(-1)), "gather_conv_w")
    conv_full = conv_full.reshape(-1)[:depth * CONV_K * N_CHIPS * cw].reshape(depth, CONV_K, N_CHIPS * cw)

    alibi = _alibi_row(d)
    bufs = [[_cast_into_slot(w[n], l, pos, "l%d_cast_%s" % (l, n)) for n in MATRICES] for l in range(depth)]

    def finish_gather(started, after, l):
        got = _gather_wait(started[0], started[1], after, "l%d_gather_wait" % l)
        full = dict(zip(MATRICES, _gather_forward(got, "l%d_gather_forward" % l)))
        w_in_full = jnp.transpose(full["w_in"], (1, 0, 2)).reshape(d, -1)
        return dict(
            norm1_g=norm1_g[l][None], norm2_g=norm2_g[l][None], conv_w=conv_full[l], a_log=_pad_row(a_log[l]), dt_bias=_pad_row(dt_bias[l]),
            gdn_norm_g=gdn_norm_g[l][None], attn_sinks=_pad_row(attn_sinks[l]), alibi=alibi, w_in=_pack_w_in(w_in_full, d),
            w_branch_gdn=full["w_branch_gdn"].reshape(-1, d), w_branch_swa=full["w_branch_swa"].reshape(-1, d),
            w_out=full["w_out"].reshape(-1, d), w_ff_up=full["w_ff_up"], w_ff_down=full["w_ff_down"].reshape(-1, d))

    started = _gather_start(bufs[0], "l0_gather_start")
    layers = [finish_gather(started, started[2], 0)]
    h = xs
    saved = []
    for l in range(depth):
        p = layers[l]
        if l + 1 < depth:
            started = _gather_start(bufs[l + 1], "l%d_gather_start" % (l + 1))
            p = dict(p, norm1_g=p["norm1_g"] + started[2][:1, :1])
        h, s = _layer_fwd(h, p, "l%d_" % l)
        saved.append(s)
        if l + 1 < depth:
            layers.append(finish_gather(started, h, l + 1))
    dh, d_final, loss_row = _loss_head(h, final_norm_g[None], target, "loss_head")

    grads = {n: [None] * depth for n in ("norm1_g", "norm2_g", "a_log", "dt_bias", "gdn_norm_g", "attn_sinks", "conv_w")}
    updated = {n: None for n in MATRICES}

    def finish_scatter(pending, after, l):
        for n, r in zip(MATRICES, _reduce_scatter_finish(pending, after, pos, "l%d_" % l)):
            updated[n] = _adamw_layer(w[n], r, mom[n], var[n], l, updated[n], "l%d_adamw_%s" % (l, n))

    pending = None
    for l in reversed(range(depth)):
        if pending is not None:
            dh = dh + token[0, 0]
        dh, g = _layer_bwd(dh, layers[l], saved[l], "l%d_" % l)
        if pending is not None:
            finish_scatter(pending, dh, l + 1)
        g_in = _unpack_w_in(g["w_in"], d)
        g_in = jnp.transpose(g_in.reshape(d, N_CHIPS, -1), (1, 0, 2))
        mats = [g_in] + [g[n] if n == "w_ff_up" else g[n].reshape(N_CHIPS, -1, g[n].shape[-1]) for n in MATRICES[1:]]
        pending, token = _reduce_scatter_start(mats, pos, "l%d_" % l)
        for n in grads:
            grads[n][l] = g[n].reshape(-1)

    small = ("norm1_g", "norm2_g", "a_log", "dt_bias", "gdn_norm_g", "attn_sinks", "conv_w")
    pieces = [jnp.stack(grads[n]).reshape(-1) for n in small] + [d_final.reshape(-1), loss_row[0, :1] + token[0, 0]]
    sizes = [p.shape[0] for p in pieces]
    packed = _allreduce_small(_to_rows(jnp.concatenate(pieces)), "reduce_small").reshape(-1)
    offs = np.concatenate([[0], np.cumsum(sizes)])
    red = {n: packed[offs[i]:offs[i + 1]] for i, n in enumerate(small + ("final_norm_g", "loss"))}
    loss = red["loss"][0]

    grad_out = {}
    for n in ("norm1_g", "norm2_g", "a_log", "dt_bias", "gdn_norm_g", "attn_sinks"):
        grad_out[n] = red[n].reshape(w[n].shape)
    grad_out["final_norm_g"] = red["final_norm_g"]
    conv_g = red["conv_w"].reshape(depth, CONV_K, N_CHIPS * cw)
    grad_out["conv_w"] = lax.dynamic_slice(conv_g, (0, 0, chip * cw), (depth, CONV_K, cw))

    delta, new_m, new_v = {}, {}, {}
    for n in grad_out:
        delta[n], new_m[n], new_v[n] = _adamw_nd(w[n], grad_out[n], mom[n], var[n], "adamw_" + n)
    finish_scatter(pending, packed, 0)
    for n in MATRICES:
        grad_out[n], delta[n], new_m[n], new_v[n] = updated[n]
    return (loss, dh[None], *[grad_out[n] for n in WEIGHTS], *[delta[n] for n in WEIGHTS], *[new_m[n] for n in WEIGHTS],
            *[new_v[n] for n in WEIGHTS])
```

```python
import functools

import jax
import jax.numpy as jnp
import numpy as np
from jax import lax
from jax.experimental import pallas as pl
from jax.experimental.pallas import tpu as pltpu

F32 = jnp.float32
BF16 = jnp.bfloat16

GDN_HEAD_DIM = 128
CHUNK = 64
SWA_HEAD_DIM = 64
WINDOW = 128
CONV_K = 4
GQA_GROUP = 8
NORM_EPS = 1e-6
N_CHIPS = 4
LANES = 128
CONV_HALO = 8
VMEM_LIMIT = 56 * 1024 * 1024

ADAM_LR = 0.001
ADAM_B1 = 0.9
ADAM_B2 = 0.999
ADAM_EPS = 1e-08
ADAM_WD = 0.01
ADAM_STEP = 10

NN = (((1,), (0,)), ((), ()))
NT = (((1,), (1,)), ((), ()))
TN = (((0,), (0,)), ((), ()))


def _pick(dim, cap, mult=LANES):
    if dim <= cap:
        return dim
    t = (cap // mult) * mult
    while t >= mult:
        if dim % t == 0:
            return t
        t -= mult
    return dim


def _params(sem):
    return pltpu.CompilerParams(dimension_semantics=sem, vmem_limit_bytes=VMEM_LIMIT)


def _bdot(a, b, dn):
    return lax.dot_general(a.astype(BF16), b.astype(BF16), dn, preferred_element_type=F32)


@jax.custom_vjp
def mm_nn(a, b):
    return _bdot(a, b, NN)


@jax.custom_vjp
def mm_nt(a, b):
    return _bdot(a, b, NT)


@jax.custom_vjp
def mm_tn(a, b):
    return _bdot(a, b, TN)


mm_nn.defvjp(lambda a, b: (_bdot(a, b, NN), (a, b)), lambda r, g: (_bdot(g, r[1], NT), _bdot(r[0], g, TN)))
mm_nt.defvjp(lambda a, b: (_bdot(a, b, NT), (a, b)), lambda r, g: (_bdot(g, r[1], NN), _bdot(g, r[0], TN)))
mm_tn.defvjp(lambda a, b: (_bdot(a, b, TN), (a, b)), lambda r, g: (_bdot(r[1], g, NT), _bdot(r[0], g, NN)))


def _hdot(a, b, dn=NN):
    return lax.dot_general(a, b, dn, precision=lax.Precision.HIGHEST, preferred_element_type=F32)


def _sigmoid(x):
    return 1.0 / (1.0 + jnp.exp(-x))


def _silu(x):
    return x * _sigmoid(x)


def _softplus(x):
    return jnp.maximum(x, 0.0) + jnp.log(1.0 + jnp.exp(-jnp.abs(x)))


def _lane_pick(row, lane, idx):
    return jnp.sum(jnp.where(lane == idx, row, 0.0), axis=1, keepdims=True)


def _matmul(a, b, *, ta=False, tb=False, out_dtype=F32, add=None, name, tm_cap=1024, tn_cap=1024, tk_cap=2048, b_split=1,
            out_split=1):
    m, k = (a.shape[1], a.shape[0]) if ta else a.shape
    b_rows, b_cols = (b.shape[-2], b.shape[-1] * b_split)
    n = b_rows if tb else b_cols
    assert k == (b_cols if tb else b_rows), (a.shape, b.shape, ta, tb)
    tm = _pick(m, tm_cap)
    tn = _pick(n // max(1 if tb else b_split, out_split), tn_cap)
    tk = _pick(k // (b_split if tb else 1), tk_cap)
    nk = k // tk
    dn = (((0 if ta else 1,), (1 if tb else 0,)), ((), ()))

    def body(*refs):
        if add is None:
            a_ref, b_ref, o_ref, acc_ref = refs
            add_ref = None
        else:
            a_ref, b_ref, add_ref, o_ref, acc_ref = refs
        kk = pl.program_id(2)
        p = lax.dot_general(a_ref[...].astype(BF16), b_ref[...].astype(BF16), dn, preferred_element_type=F32)

        @pl.when(kk == 0)
        def _():
            acc_ref[...] = p

        @pl.when(kk > 0)
        def _():
            acc_ref[...] += p

        @pl.when(kk == nk - 1)
        def _():
            r = acc_ref[...]
            if add_ref is not None:
                r = r + add_ref[...].astype(F32)
            o_ref[...] = r.astype(o_ref.dtype)

    a_spec = pl.BlockSpec((tk, tm), lambda i, j, q: (q, i)) if ta else pl.BlockSpec((tm, tk), lambda i, j, q: (i, q))
    if b_split == 1:
        b_spec = pl.BlockSpec((tn, tk), lambda i, j, q: (j, q)) if tb else pl.BlockSpec((tk, tn), lambda i, j, q: (q, j))
    elif tb:
        per_b = k // b_split // tk
        b_spec = pl.BlockSpec((None, tn, tk), lambda i, j, q: (q // per_b, j, q % per_b))
    else:
        per_b = n // b_split // tn
        b_spec = pl.BlockSpec((None, tk, tn), lambda i, j, q: (j // per_b, q, j % per_b))
    add_spec = pl.BlockSpec((tm, tn), lambda i, j, q: (i, j))
    if out_split == 1:
        o_spec, o_shape = add_spec, (m, n)
    else:
        per_o = n // out_split // tn
        o_spec, o_shape = pl.BlockSpec((None, tm, tn), lambda i, j, q: (j // per_o, i, j % per_o)), (out_split, m, n // out_split)
    in_specs = [a_spec, b_spec] + ([add_spec] if add is not None else [])
    args = (a, b) + ((add,) if add is not None else ())
    return pl.pallas_call(
        body, name=name, grid=(m // tm, n // tn, nk), in_specs=in_specs, out_specs=o_spec,
        out_shape=jax.ShapeDtypeStruct(o_shape, out_dtype), scratch_shapes=[pltpu.VMEM((tm, tn), F32)],
        compiler_params=_params(("parallel", "parallel", "arbitrary")),
    )(*args)


def _rows(fn, row_args, full_args, row_outs, acc_outs, *, t, tm, name):
    n_row, n_full, n_ro = len(row_args), len(full_args), len(row_outs)

    def body(*refs):
        ins = [r[...] for r in refs[:n_row + n_full]]
        outs = fn(*ins)
        o_refs = refs[n_row + n_full:]
        for r, v in zip(o_refs[:n_ro], outs[:n_ro]):
            r[...] = v.astype(r.dtype)
        i = pl.program_id(0)
        for r, v in zip(o_refs[n_ro:], outs[n_ro:]):
            @pl.when(i == 0)
            def _(r=r, v=v):
                r[...] = v

            @pl.when(i > 0)
            def _(r=r, v=v):
                r[...] += v

    in_specs = [pl.BlockSpec((tm, w), functools.partial(lambda i, cb: (i, cb), cb=cb)) for (_, w, cb) in row_args]
    in_specs += [pl.BlockSpec(f.shape, lambda i: (0, 0)) for f in full_args]
    out_specs = [pl.BlockSpec((tm, w), lambda i: (i, 0)) for (w, _) in row_outs]
    out_specs += [pl.BlockSpec(s, lambda i: (0, 0)) for s in acc_outs]
    out_shape = [jax.ShapeDtypeStruct((t, w), d) for (w, d) in row_outs]
    out_shape += [jax.ShapeDtypeStruct(s, F32) for s in acc_outs]
    return pl.pallas_call(
        body, name=name, grid=(t // tm,), in_specs=in_specs, out_specs=out_specs, out_shape=out_shape,
        compiler_params=_params(("arbitrary",)),
    )(*[a for (a, _, _) in row_args], *full_args)


def _rms(x, g):
    return x * lax.rsqrt(jnp.mean(x * x, axis=-1, keepdims=True) + NORM_EPS) * g


def _rmsnorm_fwd(x, g, name):
    t, d = x.shape
    (h,) = _rows(lambda xb, gb: (_rms(xb, gb),), [(x, d, 0)], [g], [(d, BF16)], [], t=t, tm=_pick(t, 512, 8), name=name)
    return h


def _rmsnorm_bwd(x, g, dh, dx_in, name):
    t, d = x.shape

    def fn(xb, dhb, dxb, gb):
        _, vjp = jax.vjp(_rms, xb, gb)
        dx, dg = vjp(dhb)
        return dxb + dx, dg

    return _rows(fn, [(x, d, 0), (dh, d, 0), (dx_in, d, 0)], [g], [(d, F32)], [(1, d)], t=t, tm=_pick(t, 256, 8), name=name)


def _merge(yg, ys, lg, ls):
    return _sigmoid(lg) * yg + _sigmoid(ls) * ys


def _merge_fwd(y_gdn, y_swa, proj, gate_off, name):
    t, d = y_gdn.shape
    cb = gate_off // d
    (mix,) = _rows(lambda a, b, c, e: (_merge(a, b, c, e),), [(y_gdn, d, 0), (y_swa, d, 0), (proj, d, cb), (proj, d, cb + 1)], [],
                   [(d, BF16)], [], t=t, tm=_pick(t, 256, 8), name=name)
    return mix


def _merge_bwd(y_gdn, y_swa, proj, gate_off, dmix, name):
    t, d = y_gdn.shape
    cb = gate_off // d

    def fn(a, b, c, e, g):
        _, vjp = jax.vjp(_merge, a, b, c, e)
        da, db, dc, de = vjp(g)
        return da, db, jnp.concatenate([dc, de], axis=1)

    return _rows(fn, [(y_gdn, d, 0), (y_swa, d, 0), (proj, d, cb), (proj, d, cb + 1), (dmix, d, 0)], [],
                 [(d, BF16), (d, BF16), (2 * d, BF16)], [], t=t, tm=_pick(t, 128, 8), name=name)


def _relu2_fwd(up, name):
    t, f = up.shape
    (act,) = _rows(lambda u: (jnp.square(jnp.maximum(u, 0.0)),), [(up, f, 0)], [], [(f, BF16)], [], t=t, tm=_pick(t, 256, 8), name=name)
    return act


def _relu2_bwd(up, dact, name):
    t, f = up.shape
    (dup,) = _rows(lambda u, g: (g * 2.0 * jnp.maximum(u, 0.0),), [(up, f, 0), (dact, f, 0)], [], [(f, BF16)], [], t=t,
                   tm=_pick(t, 128, 8), name=name)
    return dup


def _loss_head(x, g, target, name):
    t, d = x.shape

    def loss_fn(xb, gb, tb):
        err = _rms(xb, gb) - tb
        return 0.5 * jnp.sum(jnp.mean(err * err, axis=-1, keepdims=True), axis=0, keepdims=True)

    def fn(xb, tb, gb):
        lv, vjp = jax.vjp(lambda a, b: loss_fn(a, b, tb), xb, gb)
        dx, dg = vjp(jnp.ones((1, 1), F32))
        return dx, dg, jnp.broadcast_to(lv, (1, LANES))

    return _rows(fn, [(x, d, 0), (target, d, 0)], [g], [(d, F32)], [(1, d), (1, LANES)], t=t, tm=_pick(t, 256, 8), name=name)


def _conv_silu(prev, cur, w, keep_prev):
    tm = cur.shape[0]
    xp = jnp.concatenate([prev * keep_prev, cur], axis=0)
    y = w[0:1, :] * xp[CONV_HALO - 3:CONV_HALO - 3 + tm]
    for j in range(1, CONV_K):
        y = y + w[j:j + 1, :] * xp[CONV_HALO - 3 + j:CONV_HALO - 3 + j + tm]
    return _silu(y)


def _conv_tiles(t, width):
    tm = _pick(t, 512, CONV_HALO)
    tc = _pick(width, 512)
    return tm, tc, t // tm, width // tc


def _conv_fwd(proj, conv_w, width, name):
    t = proj.shape[0]
    tm, tc, nt, ncw = _conv_tiles(t, width)
    hb = tm // CONV_HALO

    def body(prev_ref, cur_ref, w_ref, o_ref):
        keep = (pl.program_id(1) > 0).astype(F32)
        o_ref[0] = _conv_silu(prev_ref[...], cur_ref[...], w_ref[...], keep)

    return pl.pallas_call(
        body, name=name, grid=(3 * ncw, nt),
        in_specs=[pl.BlockSpec((CONV_HALO, tc), lambda j, i: (jnp.maximum(i * hb - 1, 0), j)),
                  pl.BlockSpec((tm, tc), lambda j, i: (i, j)),
                  pl.BlockSpec((CONV_K, tc), lambda j, i: (0, j))],
        out_specs=pl.BlockSpec((1, tm, tc), lambda j, i: (j // ncw, i, j % ncw)),
        out_shape=jax.ShapeDtypeStruct((3, t, width), F32),
        compiler_params=_params(("parallel", "arbitrary")),
    )(proj, proj, conv_w)


def _conv_bwd(proj, conv_w, dout, width, name):
    t = proj.shape[0]
    tm, tc, nt, ncw = _conv_tiles(t, width)
    hb = tm // CONV_HALO

    def body(prev_ref, cur_ref, w_ref, g_ref, dx_ref, dw_ref, carry_ref):
        s = pl.program_id(1)
        keep = (s < nt - 1).astype(F32)
        _, vjp = jax.vjp(lambda p, c, w: _conv_silu(p, c, w, keep), prev_ref[...], cur_ref[...], w_ref[...])
        dprev, dcur, dw = vjp(g_ref[0])

        @pl.when(s == 0)
        def _():
            carry_ref[...] = jnp.zeros_like(carry_ref)
            dw_ref[...] = dw

        @pl.when(s > 0)
        def _():
            dw_ref[...] += dw

        tail = jnp.concatenate([jnp.zeros((tm - CONV_HALO, tc), F32), carry_ref[...]], axis=0)
        dx_ref[...] = (dcur + tail).astype(dx_ref.dtype)
        carry_ref[...] = dprev

    def row(s):
        return nt - 1 - s

    return pl.pallas_call(
        body, name=name, grid=(3 * ncw, nt),
        in_specs=[pl.BlockSpec((CONV_HALO, tc), lambda j, s: (jnp.maximum(row(s) * hb - 1, 0), j)),
                  pl.BlockSpec((tm, tc), lambda j, s: (row(s), j)),
                  pl.BlockSpec((CONV_K, tc), lambda j, s: (0, j)),
                  pl.BlockSpec((1, tm, tc), lambda j, s: (j // ncw, row(s), j % ncw))],
        out_specs=[pl.BlockSpec((tm, tc), lambda j, s: (row(s), j)),
                   pl.BlockSpec((CONV_K, tc), lambda j, s: (0, j))],
        out_shape=[jax.ShapeDtypeStruct((t, 3 * width), BF16), jax.ShapeDtypeStruct((CONV_K, 3 * width), F32)],
        scratch_shapes=[pltpu.VMEM((CONV_HALO, tc), F32)],
        compiler_params=_params(("parallel", "arbitrary")),
    )(proj, proj, conv_w, dout)


def _inv_unit_lower_raw(mats):
    n = mats[0].shape[0]
    r = lax.broadcasted_iota(jnp.int32, (n, n), 0)
    c = lax.broadcasted_iota(jnp.int32, (n, n), 1)
    eye = (r == c).astype(F32)
    same = jnp.right_shift(r, 4) == jnp.right_shift(c, 4)
    dg = [jnp.where(same, a, 0.0) for a in mats]
    lo = [a - d for a, d in zip(mats, dg)]
    p = [eye - d for d in dg]
    q = dg
    for _ in range(3):
        q = [_hdot(x, x) for x in q]
        p = [_hdot(x, eye + y) for x, y in zip(p, q)]
    nm = [_hdot(x, y) for x, y in zip(p, lo)]
    n2 = [_hdot(x, x) for x in nm]
    left = [_hdot(eye - x, eye + y) for x, y in zip(nm, n2)]
    return [_hdot(x, y) for x, y in zip(left, p)]


@jax.custom_vjp
def _inv_unit_lower(mats):
    return _inv_unit_lower_raw(mats)


def _inv_fwd(mats):
    t = _inv_unit_lower_raw(mats)
    return t, t


def _inv_bwd(ts, gs):
    x = [_hdot(t, g, TN) for t, g in zip(ts, gs)]
    return ([-_hdot(a, t, NT) for a, t in zip(x, ts)],)


_inv_unit_lower.defvjp(_inv_fwd, _inv_bwd)


def _l2n(x):
    return x * lax.rsqrt(jnp.sum(x * x, axis=-1, keepdims=True) + NORM_EPS)


def _gdn_chunk(qcs, kcs, vcs, zs, bg, alog_row, dtb_row, gnorm, states, first_head, n_heads):
    nb = len(qcs)
    hs = range(nb)
    cs = qcs[0].shape[0]
    lane = lax.broadcasted_iota(jnp.int32, (1, LANES), 1)
    r = lax.broadcasted_iota(jnp.int32, (cs, cs), 0)
    c = lax.broadcasted_iota(jnp.int32, (cs, cs), 1)
    q = [_l2n(x) * (GDN_HEAD_DIM ** -0.5) for x in qcs]
    k = [_l2n(x) for x in kcs]
    beta = [_sigmoid(_lane_pick(bg, lane, first_head + i)) for i in hs]
    g = [-jnp.exp(_lane_pick(alog_row, lane, first_head + i)) *
         _softplus(_lane_pick(bg, lane, n_heads + first_head + i) + _lane_pick(dtb_row, lane, first_head + i)) for i in hs]
    g_row = [jnp.sum(jnp.where(r == c, x, 0.0), axis=0, keepdims=True) for x in g]
    dec_col = [jnp.sum(jnp.where(r >= c, x, 0.0), axis=1, keepdims=True) for x in g_row]
    dec_row = [jnp.sum(jnp.where(r <= c, x, 0.0), axis=0, keepdims=True) for x in g]
    gamma = [jnp.exp(jnp.where(r >= c, dc - dr, -1e30)) for dc, dr in zip(dec_col, dec_row)]
    kb = [x * b for x, b in zip(k, beta)]
    a = [jnp.where(r > c, mm_nt(x, y) * gm, 0.0) for x, y, gm in zip(kb, k, gamma)]
    tinv = _inv_unit_lower(a)
    e_col = [jnp.exp(x) for x in dec_col]
    u = [mm_nn(t, v * b) for t, v, b in zip(tinv, vcs, beta)]
    w = [mm_nn(t, x * e) for t, x, e in zip(tinv, kb, e_col)]
    qk = [mm_nt(x, y) * gm for x, y, gm in zip(q, k, gamma)]
    total = [jnp.sum(x, axis=0, keepdims=True) for x in g]
    v_new = [x - mm_nn(y, s) for x, y, s in zip(u, w, states)]
    o = [mm_nn(x * e, s) + mm_nn(y, v) for x, e, s, y, v in zip(q, e_col, states, qk, v_new)]
    new_states = [s * jnp.exp(tt) + mm_tn(x * jnp.exp(tt - dc), v) for s, tt, x, dc, v in zip(states, total, k, dec_col, v_new)]
    ys = [_rms(x, gnorm) * _silu(z) for x, z in zip(o, zs)]
    return ys, new_states


GDN_HEADS_FWD = 16
GDN_HEADS_BWD = 16


def _gdn_fwd(qkvc, proj, alog_row, dtb_row, gnorm, *, d, z_off, bg_off, name, hb=GDN_HEADS_FWD):
    t = qkvc.shape[1]
    nh = d // GDN_HEAD_DIM
    hb = min(hb, nh)
    wb = hb * GDN_HEAD_DIM
    nc = t // CHUNK
    ng = nh // hb

    def body(qkv_ref, z_ref, bg_ref, al_ref, dt_ref, gn_ref, y_ref, sin_ref, s_scr):
        n, hg = pl.program_id(0), pl.program_id(1)

        @pl.when(n == 0)
        def _():
            s_scr[hg] = jnp.zeros((hb, GDN_HEAD_DIM, GDN_HEAD_DIM), F32)

        states = [s_scr[hg, i] for i in range(hb)]
        bg, al, dt, gn = bg_ref[...], al_ref[...], dt_ref[...], gn_ref[...]
        sls = [slice(i * GDN_HEAD_DIM, (i + 1) * GDN_HEAD_DIM) for i in range(hb)]
        ys, new_states = _gdn_chunk([qkv_ref[0, :, sl] for sl in sls], [qkv_ref[1, :, sl] for sl in sls], [qkv_ref[2, :, sl] for sl in sls],
                                    [z_ref[:, sl] for sl in sls], bg, al, dt, gn, states, hg * hb, nh)
        for i in range(hb):
            sin_ref[0, i] = states[i]
            y_ref[:, sls[i]] = ys[i].astype(y_ref.dtype)
            s_scr[hg, i] = new_states[i]

    row = lambda n, hg: (0, 0)
    return pl.pallas_call(
        body, name=name, grid=(nc, ng),
        in_specs=[pl.BlockSpec((3, CHUNK, wb), lambda n, hg: (0, n, hg)),
                  pl.BlockSpec((CHUNK, wb), lambda n, hg: (n, z_off // wb + hg)),
                  pl.BlockSpec((CHUNK, LANES), lambda n, hg: (n, bg_off // LANES)),
                  pl.BlockSpec((1, LANES), row), pl.BlockSpec((1, LANES), row), pl.BlockSpec((1, LANES), row)],
        out_specs=[pl.BlockSpec((CHUNK, wb), lambda n, hg: (n, hg)),
                   pl.BlockSpec((1, hb, GDN_HEAD_DIM, GDN_HEAD_DIM), lambda n, hg: (n, hg, 0, 0))],
        out_shape=[jax.ShapeDtypeStruct((t, d), BF16), jax.ShapeDtypeStruct((nc, nh, GDN_HEAD_DIM, GDN_HEAD_DIM), F32)],
        scratch_shapes=[pltpu.VMEM((ng, hb, GDN_HEAD_DIM, GDN_HEAD_DIM), F32)],
        compiler_params=_params(("arbitrary", "arbitrary")),
    )(qkvc, proj, proj, alog_row, dtb_row, gnorm)


def _gdn_bwd(qkvc, proj, alog_row, dtb_row, gnorm, states, dy, *, d, z_off, bg_off, name, hb=GDN_HEADS_BWD):
    t = qkvc.shape[1]
    nh = d // GDN_HEAD_DIM
    hb = min(hb, nh)
    wb = hb * GDN_HEAD_DIM
    nc = t // CHUNK
    ng = nh // hb

    def body(qkv_ref, z_ref, bg_ref, al_ref, dt_ref, gn_ref, sin_ref, dy_ref, dqkv_ref, dz_ref, dbg_ref, dal_ref, ddt_ref, dgn_ref,
             ds_scr):
        s, hg = pl.program_id(0), pl.program_id(1)

        @pl.when(s == 0)
        def _():
            ds_scr[hg] = jnp.zeros((hb, GDN_HEAD_DIM, GDN_HEAD_DIM), F32)

        @pl.when(hg == 0)
        def _():
            dbg_ref[...] = jnp.zeros_like(dbg_ref)

        @pl.when((s == 0) & (hg == 0))
        def _():
            dal_ref[...] = jnp.zeros_like(dal_ref)
            ddt_ref[...] = jnp.zeros_like(ddt_ref)
            dgn_ref[...] = jnp.zeros_like(dgn_ref)

        dstates = [ds_scr[hg, i] for i in range(hb)]
        bg, al, dt, gn = bg_ref[...], al_ref[...], dt_ref[...], gn_ref[...]
        sls = [slice(i * GDN_HEAD_DIM, (i + 1) * GDN_HEAD_DIM) for i in range(hb)]
        fn = functools.partial(_gdn_chunk, first_head=hg * hb, n_heads=nh)
        _, vjp = jax.vjp(fn, [qkv_ref[0, :, sl] for sl in sls], [qkv_ref[1, :, sl] for sl in sls], [qkv_ref[2, :, sl] for sl in sls],
                         [z_ref[:, sl] for sl in sls], bg, al, dt, gn, [sin_ref[0, i] for i in range(hb)])
        dq, dk, dv, dz, dbg, dal, ddt, dgn, dst = vjp(([dy_ref[:, sl] for sl in sls], dstates))
        for i in range(hb):
            dqkv_ref[0, :, sls[i]] = dq[i]
            dqkv_ref[1, :, sls[i]] = dk[i]
            dqkv_ref[2, :, sls[i]] = dv[i]
            dz_ref[:, sls[i]] = dz[i].astype(dz_ref.dtype)
            ds_scr[hg, i] = dst[i]
        dbg_ref[...] += dbg
        dal_ref[...] += dal
        ddt_ref[...] += ddt
        dgn_ref[...] += dgn

    def ch(s):
        return nc - 1 - s

    row = lambda s, hg: (0, 0)
    return pl.pallas_call(
        body, name=name, grid=(nc, ng),
        in_specs=[pl.BlockSpec((3, CHUNK, wb), lambda s, hg: (0, ch(s), hg)),
                  pl.BlockSpec((CHUNK, wb), lambda s, hg: (ch(s), z_off // wb + hg)),
                  pl.BlockSpec((CHUNK, LANES), lambda s, hg: (ch(s), bg_off // LANES)),
                  pl.BlockSpec((1, LANES), row), pl.BlockSpec((1, LANES), row), pl.BlockSpec((1, LANES), row),
                  pl.BlockSpec((1, hb, GDN_HEAD_DIM, GDN_HEAD_DIM), lambda s, hg: (ch(s), hg, 0, 0)),
                  pl.BlockSpec((CHUNK, wb), lambda s, hg: (ch(s), hg))],
        out_specs=[pl.BlockSpec((3, CHUNK, wb), lambda s, hg: (0, ch(s), hg)),
                   pl.BlockSpec((CHUNK, wb), lambda s, hg: (ch(s), hg)),
                   pl.BlockSpec((CHUNK, LANES), lambda s, hg: (ch(s), 0)),
                   pl.BlockSpec((1, LANES), row), pl.BlockSpec((1, LANES), row), pl.BlockSpec((1, LANES), row)],
        out_shape=[jax.ShapeDtypeStruct((3, t, d), F32), jax.ShapeDtypeStruct((t, d), BF16), jax.ShapeDtypeStruct((t, LANES), F32),
                   jax.ShapeDtypeStruct((1, LANES), F32), jax.ShapeDtypeStruct((1, LANES), F32), jax.ShapeDtypeStruct((1, LANES), F32)],
        scratch_shapes=[pltpu.VMEM((ng, hb, GDN_HEAD_DIM, GDN_HEAD_DIM), F32)],
        compiler_params=_params(("arbitrary", "arbitrary")),
    )(qkvc, proj, proj, alog_row, dtb_row, gnorm, states, dy)


@jax.custom_vjp
def _swap_halves(x):
    return pltpu.roll(x, SWA_HEAD_DIM, 1)


_swap_halves.defvjp(lambda x: (pltpu.roll(x, SWA_HEAD_DIM, 1), None), lambda _, g: (pltpu.roll(g, SWA_HEAD_DIM, 1),))

SWA_PAIR_Q = 2 * GQA_GROUP * SWA_HEAD_DIM


def _swa_block(q, kp, kc, vp, vc, sink_row, slope_row, keep_prev, pair):
    kb = jnp.concatenate([kp, kc], axis=0)
    vb = jnp.concatenate([vp, vc], axis=0)
    lane = lax.broadcasted_iota(jnp.int32, (1, LANES), 1)
    low = lane < SWA_HEAD_DIM
    high = jnp.logical_not(low)
    qi = lax.broadcasted_iota(jnp.int32, (WINDOW, 2 * WINDOW), 0)
    sj = lax.broadcasted_iota(jnp.int32, (WINDOW, 2 * WINDOW), 1)
    dist = qi + WINDOW - sj
    valid = (dist >= 0) & (dist < WINDOW) & ((sj >= WINDOW) | (keep_prev > 0.5))
    distf = dist.astype(F32)
    kk, vv = [], []
    for mine in (low, high):
        x = jnp.where(mine, kb, 0.0)
        kk.append(x + _swap_halves(x))
        y = jnp.where(mine, vb, 0.0)
        vv.append(y + _swap_halves(y))
    hl = range(2 * GQA_GROUP)
    half = [low if h % 2 == 0 else high for h in hl]
    slope = [_lane_pick(slope_row, lane, pair * (2 * GQA_GROUP) + h) for h in hl]
    sink = [_lane_pick(sink_row, lane, pair * (2 * GQA_GROUP) + h) for h in hl]
    qm = [jnp.where(half[h], q[:, (h // 2) * LANES:(h // 2 + 1) * LANES], 0.0) for h in hl]
    sc = [mm_nt(qm[h], kk[h // GQA_GROUP]) * (SWA_HEAD_DIM ** -0.5) for h in hl]
    sc = [jnp.where(valid, sc[h] - slope[h] * distf, -1e30) for h in hl]
    m = [lax.stop_gradient(jnp.maximum(jnp.max(sc[h], axis=-1, keepdims=True), sink[h])) for h in hl]
    p = [jnp.exp(sc[h] - m[h]) for h in hl]
    probs = [p[h] / (jnp.sum(p[h], axis=-1, keepdims=True) + jnp.exp(sink[h] - m[h])) for h in hl]
    od = [jnp.where(half[h], mm_nn(probs[h], vv[h // GQA_GROUP]), 0.0) for h in hl]
    return jnp.concatenate([od[2 * i] + od[2 * i + 1] for i in range(GQA_GROUP)], axis=1)


def _swa_specs(t, q_off, k_off, v_off, order):
    nb = t // WINDOW

    def blk(s):
        return order(s, nb)

    return nb, [pl.BlockSpec((WINDOW, SWA_PAIR_Q), lambda p, s: (blk(s), q_off // SWA_PAIR_Q + p)),
                pl.BlockSpec((WINDOW, LANES), lambda p, s: (jnp.maximum(blk(s) - 1, 0), k_off // LANES + p)),
                pl.BlockSpec((WINDOW, LANES), lambda p, s: (blk(s), k_off // LANES + p)),
                pl.BlockSpec((WINDOW, LANES), lambda p, s: (jnp.maximum(blk(s) - 1, 0), v_off // LANES + p)),
                pl.BlockSpec((WINDOW, LANES), lambda p, s: (blk(s), v_off // LANES + p)),
                pl.BlockSpec((1, LANES), lambda p, s: (0, 0)), pl.BlockSpec((1, LANES), lambda p, s: (0, 0))]


def _swa_fwd(proj, sink_row, slope_row, *, d, q_off, k_off, v_off, name):
    t = proj.shape[0]
    n_pairs = d // SWA_PAIR_Q
    nb, in_specs = _swa_specs(t, q_off, k_off, v_off, lambda s, nb: s)

    def body(q_ref, kp_ref, kc_ref, vp_ref, vc_ref, sink_ref, slope_ref, o_ref):
        keep = (pl.program_id(1) > 0).astype(F32)
        o = _swa_block(q_ref[...], kp_ref[...], kc_ref[...], vp_ref[...], vc_ref[...], sink_ref[...], slope_ref[...], keep,
                       pl.program_id(0))
        o_ref[...] = o.astype(o_ref.dtype)

    return pl.pallas_call(
        body, name=name, grid=(n_pairs, nb), in_specs=in_specs,
        out_specs=pl.BlockSpec((WINDOW, SWA_PAIR_Q), lambda p, s: (s, p)),
        out_shape=jax.ShapeDtypeStruct((t, d), BF16),
        compiler_params=_params(("parallel", "arbitrary")),
    )(proj, proj, proj, proj, proj, sink_row, slope_row)


def _swa_bwd(proj, sink_row, slope_row, do, *, d, q_off, k_off, v_off, name):
    t = proj.shape[0]
    n_pairs = d // SWA_PAIR_Q
    nb, in_specs = _swa_specs(t, q_off, k_off, v_off, lambda s, nb: nb - 1 - s)

    def body(q_ref, kp_ref, kc_ref, vp_ref, vc_ref, sink_ref, slope_ref, do_ref, dq_ref, dk_ref, dv_ref, dsink_ref, ck_ref, cv_ref):
        p, s = pl.program_id(0), pl.program_id(1)
        keep = (s < nb - 1).astype(F32)
        fn = functools.partial(_swa_block, slope_row=slope_ref[...], keep_prev=keep, pair=p)
        _, vjp = jax.vjp(fn, q_ref[...], kp_ref[...], kc_ref[...], vp_ref[...], vc_ref[...], sink_ref[...])
        dq, dkp, dkc, dvp, dvc, dsink = vjp(do_ref[...])

        @pl.when(s == 0)
        def _():
            ck_ref[...] = jnp.zeros_like(ck_ref)
            cv_ref[...] = jnp.zeros_like(cv_ref)

        @pl.when((s == 0) & (p == 0))
        def _():
            dsink_ref[...] = jnp.zeros_like(dsink_ref)

        dq_ref[...] = dq.astype(dq_ref.dtype)
        dk_ref[...] = (dkc + ck_ref[...]).astype(dk_ref.dtype)
        dv_ref[...] = (dvc + cv_ref[...]).astype(dv_ref.dtype)
        ck_ref[...] = dkp
        cv_ref[...] = dvp
        dsink_ref[...] += dsink

    in_specs = in_specs + [pl.BlockSpec((WINDOW, SWA_PAIR_Q), lambda p, s: (nb - 1 - s, p))]
    kv_w = d // GQA_GROUP
    return pl.pallas_call(
        body, name=name, grid=(n_pairs, nb), in_specs=in_specs,
        out_specs=[pl.BlockSpec((WINDOW, SWA_PAIR_Q), lambda p, s: (nb - 1 - s, p)),
                   pl.BlockSpec((WINDOW, LANES), lambda p, s: (nb - 1 - s, p)),
                   pl.BlockSpec((WINDOW, LANES), lambda p, s: (nb - 1 - s, p)),
                   pl.BlockSpec((1, LANES), lambda p, s: (0, 0))],
        out_shape=[jax.ShapeDtypeStruct((t, d), BF16), jax.ShapeDtypeStruct((t, kv_w), BF16), jax.ShapeDtypeStruct((t, kv_w), BF16),
                   jax.ShapeDtypeStruct((1, LANES), F32)],
        scratch_shapes=[pltpu.VMEM((WINDOW, LANES), F32), pltpu.VMEM((WINDOW, LANES), F32)],
        compiler_params=_params(("arbitrary", "arbitrary")),
    )(proj, proj, proj, proj, proj, sink_row, slope_row, do)


def _layout(d):
    kv = d // GQA_GROUP
    return dict(z=3 * d, q=4 * d, gate=5 * d, k=7 * d, v=7 * d + kv, bg=7 * d + 2 * kv, width=7 * d + 2 * kv + LANES)


def _pack_w_in(w, d):
    nh = d // GDN_HEAD_DIM
    kv = d // GQA_GROUP
    o = 4 * d + 2 * nh
    parts = [w[..., :4 * d], w[..., o:o + d], w[..., o + d + 2 * kv:o + 3 * d + 2 * kv], w[..., o + d:o + d + 2 * kv],
             w[..., 4 * d:o], jnp.zeros(w.shape[:-1] + (LANES - 2 * nh,), w.dtype)]
    return jnp.concatenate(parts, axis=-1)


def _unpack_w_in(wp, d):
    nh = d // GDN_HEAD_DIM
    kv = d // GQA_GROUP
    lay = _layout(d)
    parts = [wp[..., :4 * d], wp[..., lay["bg"]:lay["bg"] + 2 * nh], wp[..., lay["q"]:lay["q"] + d],
             wp[..., lay["k"]:lay["k"] + 2 * kv], wp[..., lay["gate"]:lay["gate"] + 2 * d]]
    return jnp.concatenate(parts, axis=-1)


def _pad_row(v):
    return jnp.pad(v.astype(F32), (0, LANES - v.shape[0]))[None, :]


def _alibi_row(d):
    nq = d // SWA_HEAD_DIM
    return _pad_row(2.0 ** (-8.0 * jnp.arange(1, nq + 1, dtype=F32) / nq))


def _layer_fwd(x, p, tag):
    t, d = x.shape
    lay = _layout(d)
    tn = 1152 if lay["width"] % 1152 == 0 else 1024
    h1 = _rmsnorm_fwd(x, p["norm1_g"], tag + "rms1")
    proj = _matmul(h1, p["w_in"], name=tag + "mm_in", tn_cap=tn)
    qkvc = _conv_fwd(proj, p["conv_w"], d, tag + "conv")
    gdn_o, states = _gdn_fwd(qkvc, proj, p["a_log"], p["dt_bias"], p["gdn_norm_g"], d=d, z_off=lay["z"], bg_off=lay["bg"],
                             name=tag + "gdn")
    swa_o = _swa_fwd(proj, p["attn_sinks"], p["alibi"], d=d, q_off=lay["q"], k_off=lay["k"], v_off=lay["v"], name=tag + "swa")
    y_gdn = _matmul(gdn_o, p["w_branch_gdn"], name=tag + "mm_bg")
    y_swa = _matmul(swa_o, p["w_branch_swa"], name=tag + "mm_bs")
    mix = _merge_fwd(y_gdn, y_swa, proj, lay["gate"], tag + "merge")
    x1 = _matmul(mix, p["w_out"], add=x, name=tag + "mm_out")
    h2 = _rmsnorm_fwd(x1, p["norm2_g"], tag + "rms2")
    up = _matmul(h2, p["w_ff_up"], name=tag + "mm_up", b_split=N_CHIPS)
    act = _relu2_fwd(up, tag + "relu2")
    x2 = _matmul(act, p["w_ff_down"], add=x1, name=tag + "mm_down")
    return x2, dict(x=x, h1=h1, proj=proj, qkvc=qkvc, states=states, gdn_o=gdn_o, swa_o=swa_o, y_gdn=y_gdn, y_swa=y_swa, mix=mix,
                    x1=x1, h2=h2, up=up, act=act)


def _layer_bwd(dx2, p, s, tag):
    t, d = dx2.shape
    lay = _layout(d)
    tn = 1152 if lay["width"] % 1152 == 0 else 1024
    nh = d // GDN_HEAD_DIM
    g = {}
    dact = _matmul(dx2, p["w_ff_down"], tb=True, name=tag + "mm_dact", tm_cap=512)
    g["w_ff_down"] = _matmul(s["act"], dx2, ta=True, out_dtype=BF16, name=tag + "mm_dwdown", tk_cap=1024)
    dup = _relu2_bwd(s["up"], dact, tag + "relu2b")
    g["w_ff_up"] = _matmul(s["h2"], dup, ta=True, out_dtype=BF16, name=tag + "mm_dwup", out_split=N_CHIPS)
    dh2 = _matmul(dup, p["w_ff_up"], tb=True, name=tag + "mm_dh2", b_split=N_CHIPS)
    dx1, g["norm2_g"] = _rmsnorm_bwd(s["x1"], p["norm2_g"], dh2, dx2, tag + "rms2b")
    dmix = _matmul(dx1, p["w_out"], tb=True, name=tag + "mm_dmix", tm_cap=512)
    g["w_out"] = _matmul(s["mix"], dx1, ta=True, out_dtype=BF16, name=tag + "mm_dwout", tk_cap=1024)
    dyg, dys, dgl = _merge_bwd(s["y_gdn"], s["y_swa"], s["proj"], lay["gate"], dmix, tag + "mergeb")
    g["w_branch_gdn"] = _matmul(s["gdn_o"], dyg, ta=True, out_dtype=BF16, name=tag + "mm_dwbg")
    g["w_branch_swa"] = _matmul(s["swa_o"], dys, ta=True, out_dtype=BF16, name=tag + "mm_dwbs")
    dgdn_o = _matmul(dyg, p["w_branch_gdn"], tb=True, name=tag + "mm_dgdn")
    dswa_o = _matmul(dys, p["w_branch_swa"], tb=True, name=tag + "mm_dswa")
    dq_s, dk_s, dv_s, dsink = _swa_bwd(s["proj"], p["attn_sinks"], p["alibi"], dswa_o, d=d, q_off=lay["q"], k_off=lay["k"],
                                       v_off=lay["v"], name=tag + "swab")
    dqkvc, dz, dbg, dal, ddt, dgn = _gdn_bwd(s["qkvc"], s["proj"], p["a_log"], p["dt_bias"], p["gdn_norm_g"], s["states"], dgdn_o,
                                             d=d, z_off=lay["z"], bg_off=lay["bg"], name=tag + "gdnb")
    dqkv, g["conv_w"] = _conv_bwd(s["proj"], p["conv_w"], dqkvc, d, tag + "convb")
    dproj = jnp.concatenate([dqkv, dz, dq_s, dgl, dk_s, dv_s, lax.reduce_precision(dbg, 8, 7).astype(BF16)], axis=1)
    g["w_in"] = _matmul(s["h1"], dproj, ta=True, out_dtype=BF16, name=tag + "mm_dwin", tn_cap=tn)
    dh1 = _matmul(dproj, p["w_in"], tb=True, name=tag + "mm_dh1", tk_cap=tn)
    dx, g["norm1_g"] = _rmsnorm_bwd(s["x"], p["norm1_g"], dh1, dx1, tag + "rms1b")
    g["a_log"], g["dt_bias"], g["gdn_norm_g"], g["attn_sinks"] = dal[0, :nh], ddt[0, :nh], dgn[0], dsink[0, :d // SWA_HEAD_DIM]
    return dx, g


MESH = pl.DeviceIdType.MESH
HBM_SPEC = pl.BlockSpec(memory_space=pl.ANY)


def _place():
    x, y, c = lax.axis_index("x"), lax.axis_index("y"), lax.axis_index("c")
    return x, y, c, 2 * x + y


def _flip(x, y, k):
    px, py = x ^ (k >> 1), y ^ (k & 1)
    return px, py, 2 * px + py


def _cast_into_slot(w, layer, pos, name):
    _, r, cols = w.shape
    tm = _pick(r, max(16, (1 << 19) // cols // 16 * 16), 16)

    def body(x_ref, y_ref, w_ref, o_ref):
        o_ref[...] = w_ref[...].astype(o_ref.dtype)

    return pl.pallas_call(
        body, name=name,
        grid_spec=pltpu.PrefetchScalarGridSpec(
            num_scalar_prefetch=2, grid=(r // tm,),
            in_specs=[pl.BlockSpec((None, tm, cols), lambda i, xr, yr: (layer, i, 0))],
            out_specs=pl.BlockSpec((None, tm, cols), lambda i, xr, yr: (2 * xr[0] + yr[0], i, 0))),
        out_shape=jax.ShapeDtypeStruct((N_CHIPS, r, cols), BF16),
        compiler_params=_params(("parallel",)),
    )(pos[0], pos[1], w)


SEM_SPEC = pl.BlockSpec(memory_space=pltpu.SEMAPHORE)
SPLIT_COPY = pltpu.CompilerParams(has_side_effects=pltpu.SideEffectType.DATAFLOW_SIDE_EFFECTING)
TOKEN = jax.ShapeDtypeStruct((8, LANES), F32)
TOKEN_SPEC = pl.BlockSpec(memory_space=pltpu.VMEM)


def _gather_start(bufs, name):
    n = len(bufs)

    def body(*refs):
        outs, sems, token = refs[n:2 * n], refs[2 * n:8 * n], refs[8 * n]
        x, y, c, ci = _place()
        for a in range(n):
            hr = bufs[a].shape[1] // 2
            mine = outs[a].at[ci, pl.ds(c * hr, hr)]
            for k in (1, 2, 3):
                px, py, _ = _flip(x, y, k)
                pltpu.make_async_remote_copy(src_ref=mine, dst_ref=mine, send_sem=sems[3 * a + k - 1], recv_sem=sems[3 * n + 3 * a + k - 1],
                                             device_id=(px, py, c), device_id_type=MESH).start()
        token[...] = jnp.zeros_like(token)

    res = pl.pallas_call(
        body, name=name, in_specs=[HBM_SPEC] * n, out_specs=[HBM_SPEC] * n + [SEM_SPEC] * (6 * n) + [TOKEN_SPEC],
        out_shape=[pltpu.HBM(b.shape, b.dtype) for b in bufs] + [pltpu.SemaphoreType.DMA(())] * (6 * n) + [TOKEN],
        input_output_aliases={a: a for a in range(n)}, compiler_params=SPLIT_COPY,
    )(*[pltpu.with_memory_space_constraint(b, pltpu.HBM) for b in bufs])
    return res[:n], res[n:7 * n], res[7 * n]


def _gather_wait(bufs, sems, after, name):
    n = len(bufs)

    def body(*refs):
        sems, outs = refs[n:7 * n], refs[7 * n + 1:8 * n + 1]
        x, y, c, ci = _place()
        for a in range(n):
            hr = bufs[a].shape[1] // 2
            mine = outs[a].at[ci, pl.ds(c * hr, hr)]
            for k in (1, 2, 3):
                px, py, pj = _flip(x, y, k)
                landed = outs[a].at[pj, pl.ds(c * hr, hr)]
                cp = pltpu.make_async_remote_copy(src_ref=mine, dst_ref=landed, send_sem=sems[3 * a + k - 1], recv_sem=sems[3 * n + 3 * a + k - 1],
                                                  device_id=(px, py, c), device_id_type=MESH)
                cp.wait_send()
                cp.wait_recv()

    return pl.pallas_call(
        body, name=name, in_specs=[HBM_SPEC] * n + [SEM_SPEC] * (6 * n) + [HBM_SPEC], out_specs=[HBM_SPEC] * n,
        out_shape=[pltpu.HBM(b.shape, b.dtype) for b in bufs],
        input_output_aliases={a: a for a in range(n)}, compiler_params=SPLIT_COPY,
    )(*bufs, *sems, after)


def _gather_forward(bufs, name):
    n = len(bufs)

    def body(*refs):
        outs = refs[n:2 * n]
        send_sems, recv_sems = refs[2 * n:]
        x, y, c, _ = _place()
        waits = []
        for a in range(n):
            hr = bufs[a].shape[1] // 2
            for k in (1, 2, 3):
                _, _, pj = _flip(x, y, k)
                landed = outs[a].at[pj, pl.ds(c * hr, hr)]
                fw = pltpu.make_async_remote_copy(src_ref=landed, dst_ref=landed, send_sem=send_sems.at[a, k - 1], recv_sem=recv_sems.at[a, k - 1],
                                                  device_id=(x, y, 1 - c), device_id_type=MESH)
                fw.start()
                waits.append(fw.wait_send)
                passed = outs[a].at[pj, pl.ds((1 - c) * hr, hr)]
                waits.append(pltpu.make_async_remote_copy(src_ref=passed, dst_ref=passed, send_sem=send_sems.at[a, k - 1],
                                                          recv_sem=recv_sems.at[a, k - 1], device_id=(x, y, 1 - c),
                                                          device_id_type=MESH).wait_recv)
        for w in waits:
            w()

    return pl.pallas_call(
        body, name=name, in_specs=[HBM_SPEC] * n, out_specs=[HBM_SPEC] * n,
        out_shape=[jax.ShapeDtypeStruct(b.shape, b.dtype) for b in bufs],
        input_output_aliases={a: a for a in range(n)},
        scratch_shapes=[pltpu.SemaphoreType.DMA((n, 3))] * 2,
    )(*bufs)


def _swap_with_sibling(gs, name):
    n = len(gs)

    def body(*refs):
        ins, outs = refs[:n], refs[n:2 * n]
        send_sems, recv_sems = refs[2 * n:]
        x, y, c, _ = _place()
        cps = []
        for a in range(n):
            hr = gs[a].shape[1] // 2
            cp = pltpu.make_async_remote_copy(src_ref=ins[a].at[:, pl.ds((1 - c) * hr, hr)], dst_ref=outs[a], send_sem=send_sems.at[a],
                                              recv_sem=recv_sems.at[a], device_id=(x, y, 1 - c), device_id_type=MESH)
            cp.start()
            cps.append(cp)
        for cp in cps:
            cp.wait()

    return pl.pallas_call(
        body, name=name, in_specs=[HBM_SPEC] * n, out_specs=[HBM_SPEC] * n,
        out_shape=[jax.ShapeDtypeStruct((g.shape[0], g.shape[1] // 2, g.shape[2]), g.dtype) for g in gs],
        scratch_shapes=[pltpu.SemaphoreType.DMA((n,))] * 2,
    )(*gs)


def _scatter_start(hs, name):
    n = len(hs)

    def body(*refs):
        srcs, lands, sems, token = refs[n:2 * n], refs[2 * n:3 * n], refs[3 * n:9 * n], refs[9 * n]
        x, y, c, ci = _place()
        for a in range(n):
            for k in (1, 2, 3):
                px, py, pj = _flip(x, y, k)
                pltpu.make_async_remote_copy(src_ref=srcs[a].at[pj], dst_ref=lands[a].at[ci], send_sem=sems[3 * a + k - 1],
                                             recv_sem=sems[3 * n + 3 * a + k - 1], device_id=(px, py, c), device_id_type=MESH).start()
        token[...] = jnp.zeros_like(token)

    res = pl.pallas_call(
        body, name=name, in_specs=[HBM_SPEC] * n, out_specs=[HBM_SPEC] * (2 * n) + [SEM_SPEC] * (6 * n) + [TOKEN_SPEC],
        out_shape=[pltpu.HBM(h.shape, h.dtype) for h in hs] * 2 + [pltpu.SemaphoreType.DMA(())] * (6 * n) + [TOKEN],
        input_output_aliases={a: a for a in range(n)}, compiler_params=SPLIT_COPY,
    )(*[pltpu.with_memory_space_constraint(h, pltpu.HBM) for h in hs])
    return res[:n], res[n:2 * n], res[2 * n:8 * n], res[8 * n]


def _scatter_wait(hs, lands, sems, after, name):
    n = len(hs)

    def body(*refs):
        sems, srcs, lands_o = refs[2 * n:8 * n], refs[8 * n + 1:9 * n + 1], refs[9 * n + 1:10 * n + 1]
        x, y, c, ci = _place()
        for a in range(n):
            for k in (1, 2, 3):
                px, py, pj = _flip(x, y, k)
                cp = pltpu.make_async_remote_copy(src_ref=srcs[a].at[pj], dst_ref=lands_o[a].at[pj], send_sem=sems[3 * a + k - 1],
                                                  recv_sem=sems[3 * n + 3 * a + k - 1], device_id=(px, py, c), device_id_type=MESH)
                cp.wait_send()
                cp.wait_recv()

    res = pl.pallas_call(
        body, name=name, in_specs=[HBM_SPEC] * (2 * n) + [SEM_SPEC] * (6 * n) + [HBM_SPEC], out_specs=[HBM_SPEC] * (2 * n),
        out_shape=[pltpu.HBM(h.shape, h.dtype) for h in hs] * 2,
        input_output_aliases={a: a for a in range(2 * n)}, compiler_params=SPLIT_COPY,
    )(*hs, *lands, *sems, after)
    return res[:n], res[n:]


def _share_with_sibling(bufs, name):
    n = len(bufs)

    def body(*refs):
        outs = refs[n:2 * n]
        send_sems, recv_sems = refs[2 * n:]
        x, y, c, _ = _place()
        waits = []
        for a in range(n):
            mine = outs[a].at[c]
            cp = pltpu.make_async_remote_copy(src_ref=mine, dst_ref=mine, send_sem=send_sems.at[a], recv_sem=recv_sems.at[a],
                                              device_id=(x, y, 1 - c), device_id_type=MESH)
            cp.start()
            waits.append(cp.wait_send)
            got = outs[a].at[1 - c]
            waits.append(pltpu.make_async_remote_copy(src_ref=got, dst_ref=got, send_sem=send_sems.at[a], recv_sem=recv_sems.at[a],
                                                      device_id=(x, y, 1 - c), device_id_type=MESH).wait_recv)
        for w in waits:
            w()

    return pl.pallas_call(
        body, name=name, in_specs=[HBM_SPEC] * n, out_specs=[HBM_SPEC] * n,
        out_shape=[jax.ShapeDtypeStruct(b.shape, b.dtype) for b in bufs],
        input_output_aliases={a: a for a in range(n)},
        scratch_shapes=[pltpu.SemaphoreType.DMA((n,))] * 2,
    )(*bufs)


def _add_sibling_half(g, got, core, name):
    nc, r, cols = g.shape
    hr = r // 2
    tm = _pick(hr, 256, 16)

    def body(core_ref, g_ref, o_ref, s_ref):
        s_ref[...] = (g_ref[...].astype(F32) + o_ref[...].astype(F32)).astype(s_ref.dtype)

    return pl.pallas_call(
        body, name=name,
        grid_spec=pltpu.PrefetchScalarGridSpec(
            num_scalar_prefetch=1, grid=(nc, hr // tm),
            in_specs=[pl.BlockSpec((None, None, tm, cols), lambda j, i, cr: (j, cr[0], i, 0)),
                      pl.BlockSpec((None, tm, cols), lambda j, i, cr: (j, i, 0))],
            out_specs=pl.BlockSpec((None, tm, cols), lambda j, i, cr: (j, i, 0))),
        out_shape=jax.ShapeDtypeStruct((nc, hr, cols), g.dtype),
        compiler_params=_params(("parallel", "parallel")),
    )(core, g.reshape(nc, 2, hr, cols), got)


def _sum_chips(own, parts, pos, name):
    nc, r, cols = parts.shape
    tm = _pick(r, 256, 16)

    def body(x_ref, y_ref, c_ref, own_ref, p_ref, o_ref):
        chip = 2 * x_ref[0] + y_ref[0]
        acc = own_ref[...].astype(F32)
        for k in range(1, nc):
            acc = acc + p_ref[chip ^ k].astype(F32)
        o_ref[...] = acc

    return pl.pallas_call(
        body, name=name,
        grid_spec=pltpu.PrefetchScalarGridSpec(
            num_scalar_prefetch=3, grid=(r // tm,),
            in_specs=[pl.BlockSpec((None, tm, cols), lambda i, xr, yr, cr: (2 * xr[0] + yr[0], i, 0)),
                      pl.BlockSpec((nc, tm, cols), lambda i, xr, yr, cr: (0, i, 0))],
            out_specs=pl.BlockSpec((None, tm, cols), lambda i, xr, yr, cr: (cr[0], i, 0))),
        out_shape=jax.ShapeDtypeStruct((2, r, cols), F32),
        compiler_params=_params(("parallel",)),
    )(*pos, own, parts)


def _reduce_scatter_start(gs, pos, tag):
    got = _swap_with_sibling(gs, tag + "rs_swap")
    hs = [_add_sibling_half(g, o, pos[2], tag + "rs_add%d" % i) for i, (g, o) in enumerate(zip(gs, got))]
    hs, lands, sems, token = _scatter_start(hs, tag + "rs_scatter_start")
    return (hs, lands, sems), token


def _reduce_scatter_finish(pending, after, pos, tag):
    hs, lands, sems = pending
    hs, parts = _scatter_wait(hs, lands, sems, after, tag + "rs_scatter_wait")
    rs = [_sum_chips(h, p, pos, tag + "rs_sum%d" % i) for i, (h, p) in enumerate(zip(hs, parts))]
    both = _share_with_sibling(rs, tag + "rs_share")
    return [b.reshape(2 * b.shape[1], b.shape[2]) for b in both]


def _allreduce_small(v, name):
    rows = v.shape[0]

    def body(v_ref, o_ref, buf, send_sems, recv_sems, local_sem):
        x, y, c, _ = _place()
        me, sibling = (x, y, c), (x, y, 1 - c)
        chips = [_flip(x, y, k)[:2] for k in (1, 2, 3)]

        def slot(px, py, pc):
            return buf.at[4 * px + 2 * py + pc]

        def copy(k, block, to, src=None):
            return pltpu.make_async_remote_copy(src_ref=slot(*block) if src is None else src, dst_ref=slot(*block), send_sem=send_sems.at[k],
                                                recv_sem=recv_sems.at[k], device_id=to, device_id_type=MESH)

        mine = pltpu.make_async_copy(v_ref, slot(*me), local_sem)
        mine.start()
        first = [copy(0, me, sibling, src=v_ref)] + [copy(1 + j, me, (*chip, c), src=v_ref) for j, chip in enumerate(chips)]
        for cp in first:
            cp.start()
        passed = [copy(4 + j, (*chip, c), sibling) for j, chip in enumerate(chips)]
        for j, chip in enumerate(chips):
            copy(1 + j, (*chip, c), me).wait_recv()
            passed[j].start()
        copy(0, sibling, me).wait_recv()
        for j, chip in enumerate(chips):
            copy(4 + j, (*chip, 1 - c), me).wait_recv()
        for cp in first + passed:
            cp.wait_send()
        mine.wait()
        acc = buf[0]
        for i in range(1, 2 * N_CHIPS):
            acc = acc + buf[i]
        o_ref[...] = acc

    vm = pl.BlockSpec(memory_space=pltpu.VMEM)
    return pl.pallas_call(
        body, name=name, in_specs=[vm], out_specs=vm, out_shape=jax.ShapeDtypeStruct(v.shape, F32),
        scratch_shapes=[pltpu.VMEM((2 * N_CHIPS, rows, LANES), F32), pltpu.SemaphoreType.DMA((7,)), pltpu.SemaphoreType.DMA((7,)),
                        pltpu.SemaphoreType.DMA],
        compiler_params=pltpu.CompilerParams(vmem_limit_bytes=VMEM_LIMIT),
    )(v)


def _adamw(w, g, m, v, name):
    r, cols = w.shape
    tm = _pick(r, max(8, (1 << 18) // max(cols, 1) // 8 * 8), 8)

    def body(w_ref, g_ref, m_ref, v_ref, d_ref, nm_ref, nv_ref):
        gg = g_ref[...]
        nm = ADAM_B1 * m_ref[...] + (1.0 - ADAM_B1) * gg
        nv = ADAM_B2 * v_ref[...] + (1.0 - ADAM_B2) * jnp.square(gg)
        m_hat = nm / (1.0 - ADAM_B1 ** ADAM_STEP)
        v_hat = nv / (1.0 - ADAM_B2 ** ADAM_STEP)
        d_ref[...] = -ADAM_LR * (m_hat / (jnp.sqrt(v_hat) + ADAM_EPS) + ADAM_WD * w_ref[...])
        nm_ref[...] = nm
        nv_ref[...] = nv

    spec = pl.BlockSpec((tm, cols), lambda i: (i, 0))
    return pl.pallas_call(
        body, name=name, grid=(r // tm,), in_specs=[spec] * 4, out_specs=[spec] * 3,
        out_shape=[jax.ShapeDtypeStruct((r, cols), F32)] * 3, compiler_params=_params(("parallel",)),
    )(w, g, m, v)


def _adamw_layer(w, g, m, v, layer, prev, name):
    depth, r, cols = w.shape
    tm = _pick(r, max(8, (1 << 18) // max(cols, 1) // 8 * 8), 8)

    def body(*refs):
        w_ref, g_ref, m_ref, v_ref = refs[:4]
        go_ref, d_ref, nm_ref, nv_ref = refs[-4:]
        gg = g_ref[...]
        nm = ADAM_B1 * m_ref[...] + (1.0 - ADAM_B1) * gg
        nv = ADAM_B2 * v_ref[...] + (1.0 - ADAM_B2) * jnp.square(gg)
        m_hat = nm / (1.0 - ADAM_B1 ** ADAM_STEP)
        v_hat = nv / (1.0 - ADAM_B2 ** ADAM_STEP)
        go_ref[...] = gg
        d_ref[...] = -ADAM_LR * (m_hat / (jnp.sqrt(v_hat) + ADAM_EPS) + ADAM_WD * w_ref[...])
        nm_ref[...] = nm
        nv_ref[...] = nv

    lspec = pl.BlockSpec((None, tm, cols), lambda i: (layer, i, 0))
    gspec = pl.BlockSpec((tm, cols), lambda i: (i, 0))
    extra = [] if prev is None else list(prev)
    return pl.pallas_call(
        body, name=name, grid=(r // tm,), in_specs=[lspec, gspec, lspec, lspec] + [HBM_SPEC] * len(extra), out_specs=[lspec] * 4,
        out_shape=[jax.ShapeDtypeStruct((depth, r, cols), F32)] * 4,
        input_output_aliases={4 + j: j for j in range(len(extra))}, compiler_params=_params(("parallel",)),
    )(w, g, m, v, *extra)


def _adamw_nd(w, g, m, v, name):
    shape = w.shape
    two = (1, shape[0]) if len(shape) == 1 else (int(np.prod(shape[:-1])), shape[-1])
    outs = _adamw(w.reshape(two), g.reshape(two), m.reshape(two), v.reshape(two), name)
    return [o.reshape(shape) for o in outs]


WEIGHTS = ("norm1_g", "w_in", "conv_w", "a_log", "dt_bias", "gdn_norm_g", "attn_sinks", "w_branch_gdn", "w_branch_swa", "w_out",
           "norm2_g", "w_ff_up", "w_ff_down", "final_norm_g")
MATRICES = ("w_in", "w_branch_gdn", "w_branch_swa", "w_out", "w_ff_up", "w_ff_down")


def _to_rows(vec):
    n = vec.shape[0]
    rows = -(-n // (8 * LANES)) * 8
    return jnp.pad(vec, (0, rows * LANES - n)).reshape(rows, LANES)


def kernel(x, norm1_g, w_in, conv_w, a_log, dt_bias, gdn_norm_g, attn_sinks, w_branch_gdn, w_branch_swa, w_out, norm2_g, w_ff_up, w_ff_down, final_norm_g, loss_target, m_norm1_g, m_w_in, m_conv_w, m_a_log, m_dt_bias, m_gdn_norm_g, m_attn_sinks, m_w_branch_gdn, m_w_branch_swa, m_w_out, m_norm2_g, m_w_ff_up, m_w_ff_down, m_final_norm_g, v_norm1_g, v_w_in, v_conv_w, v_a_log, v_dt_bias, v_gdn_norm_g, v_attn_sinks, v_w_branch_gdn, v_w_branch_swa, v_w_out, v_norm2_g, v_w_ff_up, v_w_ff_down, v_final_norm_g):
    w = dict(norm1_g=norm1_g, w_in=w_in, conv_w=conv_w, a_log=a_log, dt_bias=dt_bias, gdn_norm_g=gdn_norm_g, attn_sinks=attn_sinks,
             w_branch_gdn=w_branch_gdn, w_branch_swa=w_branch_swa, w_out=w_out, norm2_g=norm2_g, w_ff_up=w_ff_up, w_ff_down=w_ff_down,
             final_norm_g=final_norm_g)
    mom = dict(norm1_g=m_norm1_g, w_in=m_w_in, conv_w=m_conv_w, a_log=m_a_log, dt_bias=m_dt_bias, gdn_norm_g=m_gdn_norm_g,
               attn_sinks=m_attn_sinks, w_branch_gdn=m_w_branch_gdn, w_branch_swa=m_w_branch_swa, w_out=m_w_out, norm2_g=m_norm2_g,
               w_ff_up=m_w_ff_up, w_ff_down=m_w_ff_down, final_norm_g=m_final_norm_g)
    var = dict(norm1_g=v_norm1_g, w_in=v_w_in, conv_w=v_conv_w, a_log=v_a_log, dt_bias=v_dt_bias, gdn_norm_g=v_gdn_norm_g,
               attn_sinks=v_attn_sinks, w_branch_gdn=v_w_branch_gdn, w_branch_swa=v_w_branch_swa, w_out=v_w_out, norm2_g=v_norm2_g,
               w_ff_up=v_w_ff_up, w_ff_down=v_w_ff_down, final_norm_g=v_final_norm_g)
    depth, d = norm1_g.shape
    xs, target = x[0], loss_target[0]
    core = lax.axis_index("c")
    chip = 2 * lax.axis_index("x") + lax.axis_index("y")
    pos = tuple(jnp.reshape(lax.axis_index(a), (1,)).astype(jnp.int32) for a in ("x", "y", "c"))

    cw = conv_w.shape[-1]
    placed = lax.dynamic_update_slice(jnp.zeros((depth, CONV_K, N_CHIPS * cw), F32), conv_w, (0, 0, chip * cw))
    placed = placed * (core == 0).astype(F32)
    conv_full = _allreduce_small(_to_rows(placed.reshape(-1)), "gather_conv_w")
    conv_full = conv_full.reshape(-1)[:depth * CONV_K * N_CHIPS * cw].reshape(depth, CONV_K, N_CHIPS * cw)

    alibi = _alibi_row(d)
    bufs = [[_cast_into_slot(w[n], l, pos, "l%d_cast_%s" % (l, n)) for n in MATRICES] for l in range(depth)]

    def finish_gather(started, after, l):
        got = _gather_wait(started[0], started[1], after, "l%d_gather_wait" % l)
        full = dict(zip(MATRICES, _gather_forward(got, "l%d_gather_forward" % l)))
        w_in_full = jnp.transpose(full["w_in"], (1, 0, 2)).reshape(d, -1)
        return dict(
            norm1_g=norm1_g[l][None], norm2_g=norm2_g[l][None], conv_w=conv_full[l], a_log=_pad_row(a_log[l]), dt_bias=_pad_row(dt_bias[l]),
            gdn_norm_g=gdn_norm_g[l][None], attn_sinks=_pad_row(attn_sinks[l]), alibi=alibi, w_in=_pack_w_in(w_in_full, d),
            w_branch_gdn=full["w_branch_gdn"].reshape(-1, d), w_branch_swa=full["w_branch_swa"].reshape(-1, d),
            w_out=full["w_out"].reshape(-1, d), w_ff_up=full["w_ff_up"], w_ff_down=full["w_ff_down"].reshape(-1, d))

    first, _ = lax.optimization_barrier((bufs[0], conv_full))
    started = _gather_start(first, "l0_gather_start")
    layers = [finish_gather(started, started[2], 0)]
    h = xs
    saved = []
    for l in range(depth):
        p = layers[l]
        if l + 1 < depth:
            nxt, _ = lax.optimization_barrier((bufs[l + 1], h))
            started = _gather_start(nxt, "l%d_gather_start" % (l + 1))
            p = dict(p, norm1_g=p["norm1_g"] + started[2][:1, :1])
        h, s = _layer_fwd(h, p, "l%d_" % l)
        saved.append(s)
        if l + 1 < depth:
            layers.append(finish_gather(started, h, l + 1))
    dh, d_final, loss_row = _loss_head(h, final_norm_g[None], target, "loss_head")

    grads = {n: [None] * depth for n in ("norm1_g", "norm2_g", "a_log", "dt_bias", "gdn_norm_g", "attn_sinks", "conv_w")}
    updated = {n: None for n in MATRICES}

    def finish_scatter(pending, after, l):
        for n, r in zip(MATRICES, _reduce_scatter_finish(pending, after, pos, "l%d_" % l)):
            updated[n] = _adamw_layer(w[n], r, mom[n], var[n], l, updated[n], "l%d_adamw_%s" % (l, n))

    small = ("norm1_g", "norm2_g", "a_log", "dt_bias", "gdn_norm_g", "attn_sinks", "conv_w")
    pending = None
    for l in reversed(range(depth)):
        if pending is not None:
            dh = dh + token[0, 0]
        dh, g = _layer_bwd(dh, layers[l], saved[l], "l%d_" % l)
        if pending is not None:
            finish_scatter(pending, dh, l + 1)
        for n in grads:
            grads[n][l] = g[n].reshape(-1)
        g_in = _unpack_w_in(g["w_in"], d)
        g_in = jnp.transpose(g_in.reshape(d, N_CHIPS, -1), (1, 0, 2))
        mats = [g_in] + [g[n] if n == "w_ff_up" else g[n].reshape(N_CHIPS, -1, g[n].shape[-1]) for n in MATRICES[1:]]
        if l == 0:
            pieces = [jnp.stack(grads[n]).reshape(-1) for n in small] + [d_final.reshape(-1), loss_row[0, :1]]
            sizes = [p.shape[0] for p in pieces]
            packed = _allreduce_small(_to_rows(jnp.concatenate(pieces)), "reduce_small").reshape(-1)
            mats, _ = lax.optimization_barrier((mats, packed))
        pending, token = _reduce_scatter_start(mats, pos, "l%d_" % l)

    offs = np.concatenate([[0], np.cumsum(sizes)])
    red = {n: packed[offs[i]:offs[i + 1]] for i, n in enumerate(small + ("final_norm_g", "loss"))}
    loss = red["loss"][0]

    grad_out = {}
    for n in ("norm1_g", "norm2_g", "a_log", "dt_bias", "gdn_norm_g", "attn_sinks"):
        grad_out[n] = red[n].reshape(w[n].shape)
    grad_out["final_norm_g"] = red["final_norm_g"]
    conv_g = red["conv_w"].reshape(depth, CONV_K, N_CHIPS * cw)
    grad_out["conv_w"] = lax.dynamic_slice(conv_g, (0, 0, chip * cw), (depth, CONV_K, cw))

    delta, new_m, new_v = {}, {}, {}
    for n in grad_out:
        delta[n], new_m[n], new_v[n] = _adamw_nd(w[n], grad_out[n], mom[n], var[n], "adamw_" + n)
    finish_scatter(pending, delta["norm1_g"], 0)
    for n in MATRICES:
        grad_out[n], delta[n], new_m[n], new_v[n] = updated[n]
    return (loss, dh[None], *[grad_out[n] for n in WEIGHTS], *[delta[n] for n in WEIGHTS], *[new_m[n] for n in WEIGHTS],
            *[new_v[n] for n in WEIGHTS])
```

```python
import functools

import jax
import jax.numpy as jnp
import numpy as np
from jax import lax
from jax.experimental import pallas as pl
from jax.experimental.pallas import tpu as pltpu

F32 = jnp.float32
BF16 = jnp.bfloat16

GDN_HEAD_DIM = 128
CHUNK = 64
SWA_HEAD_DIM = 64
WINDOW = 128
CONV_K = 4
GQA_GROUP = 8
NORM_EPS = 1e-6
N_CHIPS = 4
LANES = 128
CONV_HALO = 8
VMEM_LIMIT = 56 * 1024 * 1024

ADAM_LR = 0.001
ADAM_B1 = 0.9
ADAM_B2 = 0.999
ADAM_EPS = 1e-08
ADAM_WD = 0.01
ADAM_STEP = 10

NN = (((1,), (0,)), ((), ()))
NT = (((1,), (1,)), ((), ()))
TN = (((0,), (0,)), ((), ()))


def _pick(dim, cap, mult=LANES):
    if dim <= cap:
        return dim
    t = (cap // mult) * mult
    while t >= mult:
        if dim % t == 0:
            return t
        t -= mult
    return dim


def _params(sem):
    return pltpu.CompilerParams(dimension_semantics=sem, vmem_limit_bytes=VMEM_LIMIT)


def _bdot(a, b, dn):
    return lax.dot_general(a.astype(BF16), b.astype(BF16), dn, preferred_element_type=F32)


@jax.custom_vjp
def mm_nn(a, b):
    return _bdot(a, b, NN)


@jax.custom_vjp
def mm_nt(a, b):
    return _bdot(a, b, NT)


@jax.custom_vjp
def mm_tn(a, b):
    return _bdot(a, b, TN)


mm_nn.defvjp(lambda a, b: (_bdot(a, b, NN), (a, b)), lambda r, g: (_bdot(g, r[1], NT), _bdot(r[0], g, TN)))
mm_nt.defvjp(lambda a, b: (_bdot(a, b, NT), (a, b)), lambda r, g: (_bdot(g, r[1], NN), _bdot(g, r[0], TN)))
mm_tn.defvjp(lambda a, b: (_bdot(a, b, TN), (a, b)), lambda r, g: (_bdot(r[1], g, NT), _bdot(r[0], g, NN)))


def _hdot(a, b, dn=NN):
    return lax.dot_general(a, b, dn, precision=lax.Precision.HIGHEST, preferred_element_type=F32)


def _sigmoid(x):
    return 1.0 / (1.0 + jnp.exp(-x))


def _silu(x):
    return x * _sigmoid(x)


def _softplus(x):
    return jnp.maximum(x, 0.0) + jnp.log(1.0 + jnp.exp(-jnp.abs(x)))


def _lane_pick(row, lane, idx):
    return jnp.sum(jnp.where(lane == idx, row, 0.0), axis=1, keepdims=True)


def _matmul(a, b, *, ta=False, tb=False, out_dtype=F32, add=None, name, tm_cap=1024, tn_cap=1024, tk_cap=2048, b_split=1,
            out_split=1):
    m, k = (a.shape[1], a.shape[0]) if ta else a.shape
    b_rows, b_cols = (b.shape[-2], b.shape[-1] * b_split)
    n = b_rows if tb else b_cols
    assert k == (b_cols if tb else b_rows), (a.shape, b.shape, ta, tb)
    tm = _pick(m, tm_cap)
    tn = _pick(n // max(1 if tb else b_split, out_split), tn_cap)
    tk = _pick(k // (b_split if tb else 1), tk_cap)
    nk = k // tk
    dn = (((0 if ta else 1,), (1 if tb else 0,)), ((), ()))

    def body(*refs):
        if add is None:
            a_ref, b_ref, o_ref, acc_ref = refs
            add_ref = None
        else:
            a_ref, b_ref, add_ref, o_ref, acc_ref = refs
        kk = pl.program_id(2)
        p = lax.dot_general(a_ref[...].astype(BF16), b_ref[...].astype(BF16), dn, preferred_element_type=F32)

        @pl.when(kk == 0)
        def _():
            acc_ref[...] = p

        @pl.when(kk > 0)
        def _():
            acc_ref[...] += p

        @pl.when(kk == nk - 1)
        def _():
            r = acc_ref[...]
            if add_ref is not None:
                r = r + add_ref[...].astype(F32)
            o_ref[...] = r.astype(o_ref.dtype)

    a_spec = pl.BlockSpec((tk, tm), lambda i, j, q: (q, i)) if ta else pl.BlockSpec((tm, tk), lambda i, j, q: (i, q))
    if b_split == 1:
        b_spec = pl.BlockSpec((tn, tk), lambda i, j, q: (j, q)) if tb else pl.BlockSpec((tk, tn), lambda i, j, q: (q, j))
    elif tb:
        per_b = k // b_split // tk
        b_spec = pl.BlockSpec((None, tn, tk), lambda i, j, q: (q // per_b, j, q % per_b))
    else:
        per_b = n // b_split // tn
        b_spec = pl.BlockSpec((None, tk, tn), lambda i, j, q: (j // per_b, q, j % per_b))
    add_spec = pl.BlockSpec((tm, tn), lambda i, j, q: (i, j))
    if out_split == 1:
        o_spec, o_shape = add_spec, (m, n)
    else:
        per_o = n // out_split // tn
        o_spec, o_shape = pl.BlockSpec((None, tm, tn), lambda i, j, q: (j // per_o, i, j % per_o)), (out_split, m, n // out_split)
    in_specs = [a_spec, b_spec] + ([add_spec] if add is not None else [])
    args = (a, b) + ((add,) if add is not None else ())
    return pl.pallas_call(
        body, name=name, grid=(m // tm, n // tn, nk), in_specs=in_specs, out_specs=o_spec,
        out_shape=jax.ShapeDtypeStruct(o_shape, out_dtype), scratch_shapes=[pltpu.VMEM((tm, tn), F32)],
        compiler_params=_params(("parallel", "parallel", "arbitrary")),
    )(*args)


def _rows(fn, row_args, full_args, row_outs, acc_outs, *, t, tm, name):
    n_row, n_full, n_ro = len(row_args), len(full_args), len(row_outs)

    def body(*refs):
        ins = [r[...] for r in refs[:n_row + n_full]]
        outs = fn(*ins)
        o_refs = refs[n_row + n_full:]
        for r, v in zip(o_refs[:n_ro], outs[:n_ro]):
            r[...] = v.astype(r.dtype)
        i = pl.program_id(0)
        for r, v in zip(o_refs[n_ro:], outs[n_ro:]):
            @pl.when(i == 0)
            def _(r=r, v=v):
                r[...] = v

            @pl.when(i > 0)
            def _(r=r, v=v):
                r[...] += v

    in_specs = [pl.BlockSpec((tm, w), functools.partial(lambda i, cb: (i, cb), cb=cb)) for (_, w, cb) in row_args]
    in_specs += [pl.BlockSpec(f.shape, lambda i: (0, 0)) for f in full_args]
    out_specs = [pl.BlockSpec((tm, w), lambda i: (i, 0)) for (w, _) in row_outs]
    out_specs += [pl.BlockSpec(s, lambda i: (0, 0)) for s in acc_outs]
    out_shape = [jax.ShapeDtypeStruct((t, w), d) for (w, d) in row_outs]
    out_shape += [jax.ShapeDtypeStruct(s, F32) for s in acc_outs]
    return pl.pallas_call(
        body, name=name, grid=(t // tm,), in_specs=in_specs, out_specs=out_specs, out_shape=out_shape,
        compiler_params=_params(("arbitrary",)),
    )(*[a for (a, _, _) in row_args], *full_args)


def _rms(x, g):
    return x * lax.rsqrt(jnp.mean(x * x, axis=-1, keepdims=True) + NORM_EPS) * g


def _rmsnorm_fwd(x, g, name):
    t, d = x.shape
    (h,) = _rows(lambda xb, gb: (_rms(xb, gb),), [(x, d, 0)], [g], [(d, BF16)], [], t=t, tm=_pick(t, 512, 8), name=name)
    return h


def _rmsnorm_bwd(x, g, dh, dx_in, name):
    t, d = x.shape

    def fn(xb, dhb, dxb, gb):
        _, vjp = jax.vjp(_rms, xb, gb)
        dx, dg = vjp(dhb)
        return dxb + dx, dg

    return _rows(fn, [(x, d, 0), (dh, d, 0), (dx_in, d, 0)], [g], [(d, F32)], [(1, d)], t=t, tm=_pick(t, 256, 8), name=name)


def _merge(yg, ys, lg, ls):
    return _sigmoid(lg) * yg + _sigmoid(ls) * ys


def _merge_fwd(y_gdn, y_swa, proj, gate_off, name):
    t, d = y_gdn.shape
    cb = gate_off // d
    (mix,) = _rows(lambda a, b, c, e: (_merge(a, b, c, e),), [(y_gdn, d, 0), (y_swa, d, 0), (proj, d, cb), (proj, d, cb + 1)], [],
                   [(d, BF16)], [], t=t, tm=_pick(t, 256, 8), name=name)
    return mix


def _merge_bwd(y_gdn, y_swa, proj, gate_off, dmix, name):
    t, d = y_gdn.shape
    cb = gate_off // d

    def fn(a, b, c, e, g):
        _, vjp = jax.vjp(_merge, a, b, c, e)
        da, db, dc, de = vjp(g)
        return da, db, jnp.concatenate([dc, de], axis=1)

    return _rows(fn, [(y_gdn, d, 0), (y_swa, d, 0), (proj, d, cb), (proj, d, cb + 1), (dmix, d, 0)], [],
                 [(d, BF16), (d, BF16), (2 * d, BF16)], [], t=t, tm=_pick(t, 128, 8), name=name)


def _relu2_fwd(up, name):
    t, f = up.shape
    (act,) = _rows(lambda u: (jnp.square(jnp.maximum(u, 0.0)),), [(up, f, 0)], [], [(f, BF16)], [], t=t, tm=_pick(t, 256, 8), name=name)
    return act


def _relu2_bwd(up, dact, name):
    t, f = up.shape
    (dup,) = _rows(lambda u, g: (g * 2.0 * jnp.maximum(u, 0.0),), [(up, f, 0), (dact, f, 0)], [], [(f, BF16)], [], t=t,
                   tm=_pick(t, 128, 8), name=name)
    return dup


def _loss_head(x, g, target, name):
    t, d = x.shape

    def loss_fn(xb, gb, tb):
        err = _rms(xb, gb) - tb
        return 0.5 * jnp.sum(jnp.mean(err * err, axis=-1, keepdims=True), axis=0, keepdims=True)

    def fn(xb, tb, gb):
        lv, vjp = jax.vjp(lambda a, b: loss_fn(a, b, tb), xb, gb)
        dx, dg = vjp(jnp.ones((1, 1), F32))
        return dx, dg, jnp.broadcast_to(lv, (1, LANES))

    return _rows(fn, [(x, d, 0), (target, d, 0)], [g], [(d, F32)], [(1, d), (1, LANES)], t=t, tm=_pick(t, 256, 8), name=name)


def _conv_silu(prev, cur, w, keep_prev):
    tm = cur.shape[0]
    xp = jnp.concatenate([prev * keep_prev, cur], axis=0)
    y = w[0:1, :] * xp[CONV_HALO - 3:CONV_HALO - 3 + tm]
    for j in range(1, CONV_K):
        y = y + w[j:j + 1, :] * xp[CONV_HALO - 3 + j:CONV_HALO - 3 + j + tm]
    return _silu(y)


def _conv_tiles(t, width):
    tm = _pick(t, 512, CONV_HALO)
    tc = _pick(width, 512)
    return tm, tc, t // tm, width // tc


def _conv_fwd(proj, conv_w, width, name):
    t = proj.shape[0]
    tm, tc, nt, ncw = _conv_tiles(t, width)
    hb = tm // CONV_HALO

    def body(prev_ref, cur_ref, w_ref, o_ref):
        keep = (pl.program_id(1) > 0).astype(F32)
        o_ref[0] = _conv_silu(prev_ref[...], cur_ref[...], w_ref[...], keep)

    return pl.pallas_call(
        body, name=name, grid=(3 * ncw, nt),
        in_specs=[pl.BlockSpec((CONV_HALO, tc), lambda j, i: (jnp.maximum(i * hb - 1, 0), j)),
                  pl.BlockSpec((tm, tc), lambda j, i: (i, j)),
                  pl.BlockSpec((CONV_K, tc), lambda j, i: (0, j))],
        out_specs=pl.BlockSpec((1, tm, tc), lambda j, i: (j // ncw, i, j % ncw)),
        out_shape=jax.ShapeDtypeStruct((3, t, width), F32),
        compiler_params=_params(("parallel", "arbitrary")),
    )(proj, proj, conv_w)


def _conv_bwd(proj, conv_w, dout, width, name):
    t = proj.shape[0]
    tm, tc, nt, ncw = _conv_tiles(t, width)
    hb = tm // CONV_HALO

    def body(prev_ref, cur_ref, w_ref, g_ref, dx_ref, dw_ref, carry_ref):
        s = pl.program_id(1)
        keep = (s < nt - 1).astype(F32)
        _, vjp = jax.vjp(lambda p, c, w: _conv_silu(p, c, w, keep), prev_ref[...], cur_ref[...], w_ref[...])
        dprev, dcur, dw = vjp(g_ref[0])

        @pl.when(s == 0)
        def _():
            carry_ref[...] = jnp.zeros_like(carry_ref)
            dw_ref[...] = dw

        @pl.when(s > 0)
        def _():
            dw_ref[...] += dw

        tail = jnp.concatenate([jnp.zeros((tm - CONV_HALO, tc), F32), carry_ref[...]], axis=0)
        dx_ref[...] = (dcur + tail).astype(dx_ref.dtype)
        carry_ref[...] = dprev

    def row(s):
        return nt - 1 - s

    return pl.pallas_call(
        body, name=name, grid=(3 * ncw, nt),
        in_specs=[pl.BlockSpec((CONV_HALO, tc), lambda j, s: (jnp.maximum(row(s) * hb - 1, 0), j)),
                  pl.BlockSpec((tm, tc), lambda j, s: (row(s), j)),
                  pl.BlockSpec((CONV_K, tc), lambda j, s: (0, j)),
                  pl.BlockSpec((1, tm, tc), lambda j, s: (j // ncw, row(s), j % ncw))],
        out_specs=[pl.BlockSpec((tm, tc), lambda j, s: (row(s), j)),
                   pl.BlockSpec((CONV_K, tc), lambda j, s: (0, j))],
        out_shape=[jax.ShapeDtypeStruct((t, 3 * width), BF16), jax.ShapeDtypeStruct((CONV_K, 3 * width), F32)],
        scratch_shapes=[pltpu.VMEM((CONV_HALO, tc), F32)],
        compiler_params=_params(("parallel", "arbitrary")),
    )(proj, proj, conv_w, dout)


def _inv_unit_lower_raw(mats):
    n = mats[0].shape[0]
    r = lax.broadcasted_iota(jnp.int32, (n, n), 0)
    c = lax.broadcasted_iota(jnp.int32, (n, n), 1)
    eye = (r == c).astype(F32)
    same = jnp.right_shift(r, 4) == jnp.right_shift(c, 4)
    dg = [jnp.where(same, a, 0.0) for a in mats]
    lo = [a - d for a, d in zip(mats, dg)]
    p = [eye - d for d in dg]
    q = dg
    for _ in range(3):
        q = [_hdot(x, x) for x in q]
        p = [_hdot(x, eye + y) for x, y in zip(p, q)]
    nm = [_hdot(x, y) for x, y in zip(p, lo)]
    n2 = [_hdot(x, x) for x in nm]
    left = [_hdot(eye - x, eye + y) for x, y in zip(nm, n2)]
    return [_hdot(x, y) for x, y in zip(left, p)]


@jax.custom_vjp
def _inv_unit_lower(mats):
    return _inv_unit_lower_raw(mats)


def _inv_fwd(mats):
    t = _inv_unit_lower_raw(mats)
    return t, t


def _inv_bwd(ts, gs):
    x = [_hdot(t, g, TN) for t, g in zip(ts, gs)]
    return ([-_hdot(a, t, NT) for a, t in zip(x, ts)],)


_inv_unit_lower.defvjp(_inv_fwd, _inv_bwd)


def _l2n(x):
    return x * lax.rsqrt(jnp.sum(x * x, axis=-1, keepdims=True) + NORM_EPS)


def _gdn_chunk(qcs, kcs, vcs, zs, bg, alog_row, dtb_row, gnorm, states, first_head, n_heads):
    nb = len(qcs)
    hs = range(nb)
    cs = qcs[0].shape[0]
    lane = lax.broadcasted_iota(jnp.int32, (1, LANES), 1)
    r = lax.broadcasted_iota(jnp.int32, (cs, cs), 0)
    c = lax.broadcasted_iota(jnp.int32, (cs, cs), 1)
    q = [_l2n(x) * (GDN_HEAD_DIM ** -0.5) for x in qcs]
    k = [_l2n(x) for x in kcs]
    beta = [_sigmoid(_lane_pick(bg, lane, first_head + i)) for i in hs]
    g = [-jnp.exp(_lane_pick(alog_row, lane, first_head + i)) *
         _softplus(_lane_pick(bg, lane, n_heads + first_head + i) + _lane_pick(dtb_row, lane, first_head + i)) for i in hs]
    g_row = [jnp.sum(jnp.where(r == c, x, 0.0), axis=0, keepdims=True) for x in g]
    dec_col = [jnp.sum(jnp.where(r >= c, x, 0.0), axis=1, keepdims=True) for x in g_row]
    dec_row = [jnp.sum(jnp.where(r <= c, x, 0.0), axis=0, keepdims=True) for x in g]
    gamma = [jnp.exp(jnp.where(r >= c, dc - dr, -1e30)) for dc, dr in zip(dec_col, dec_row)]
    kb = [x * b for x, b in zip(k, beta)]
    a = [jnp.where(r > c, mm_nt(x, y) * gm, 0.0) for x, y, gm in zip(kb, k, gamma)]
    tinv = _inv_unit_lower(a)
    e_col = [jnp.exp(x) for x in dec_col]
    u = [mm_nn(t, v * b) for t, v, b in zip(tinv, vcs, beta)]
    w = [mm_nn(t, x * e) for t, x, e in zip(tinv, kb, e_col)]
    qk = [mm_nt(x, y) * gm for x, y, gm in zip(q, k, gamma)]
    total = [jnp.sum(x, axis=0, keepdims=True) for x in g]
    v_new = [x - mm_nn(y, s) for x, y, s in zip(u, w, states)]
    o = [mm_nn(x * e, s) + mm_nn(y, v) for x, e, s, y, v in zip(q, e_col, states, qk, v_new)]
    new_states = [s * jnp.exp(tt) + mm_tn(x * jnp.exp(tt - dc), v) for s, tt, x, dc, v in zip(states, total, k, dec_col, v_new)]
    ys = [_rms(x, gnorm) * _silu(z) for x, z in zip(o, zs)]
    return ys, new_states


GDN_HEADS_FWD = 16
GDN_HEADS_BWD = 16


def _gdn_fwd(qkvc, proj, alog_row, dtb_row, gnorm, *, d, z_off, bg_off, name, hb=GDN_HEADS_FWD):
    t = qkvc.shape[1]
    nh = d // GDN_HEAD_DIM
    hb = min(hb, nh)
    wb = hb * GDN_HEAD_DIM
    nc = t // CHUNK
    ng = nh // hb

    def body(qkv_ref, z_ref, bg_ref, al_ref, dt_ref, gn_ref, y_ref, sin_ref, s_scr):
        n, hg = pl.program_id(0), pl.program_id(1)

        @pl.when(n == 0)
        def _():
            s_scr[hg] = jnp.zeros((hb, GDN_HEAD_DIM, GDN_HEAD_DIM), F32)

        states = [s_scr[hg, i] for i in range(hb)]
        bg, al, dt, gn = bg_ref[...], al_ref[...], dt_ref[...], gn_ref[...]
        sls = [slice(i * GDN_HEAD_DIM, (i + 1) * GDN_HEAD_DIM) for i in range(hb)]
        ys, new_states = _gdn_chunk([qkv_ref[0, :, sl] for sl in sls], [qkv_ref[1, :, sl] for sl in sls], [qkv_ref[2, :, sl] for sl in sls],
                                    [z_ref[:, sl] for sl in sls], bg, al, dt, gn, states, hg * hb, nh)
        for i in range(hb):
            sin_ref[0, i] = states[i]
            y_ref[:, sls[i]] = ys[i].astype(y_ref.dtype)
            s_scr[hg, i] = new_states[i]

    row = lambda n, hg: (0, 0)
    return pl.pallas_call(
        body, name=name, grid=(nc, ng),
        in_specs=[pl.BlockSpec((3, CHUNK, wb), lambda n, hg: (0, n, hg)),
                  pl.BlockSpec((CHUNK, wb), lambda n, hg: (n, z_off // wb + hg)),
                  pl.BlockSpec((CHUNK, LANES), lambda n, hg: (n, bg_off // LANES)),
                  pl.BlockSpec((1, LANES), row), pl.BlockSpec((1, LANES), row), pl.BlockSpec((1, LANES), row)],
        out_specs=[pl.BlockSpec((CHUNK, wb), lambda n, hg: (n, hg)),
                   pl.BlockSpec((1, hb, GDN_HEAD_DIM, GDN_HEAD_DIM), lambda n, hg: (n, hg, 0, 0))],
        out_shape=[jax.ShapeDtypeStruct((t, d), BF16), jax.ShapeDtypeStruct((nc, nh, GDN_HEAD_DIM, GDN_HEAD_DIM), F32)],
        scratch_shapes=[pltpu.VMEM((ng, hb, GDN_HEAD_DIM, GDN_HEAD_DIM), F32)],
        compiler_params=_params(("arbitrary", "arbitrary")),
    )(qkvc, proj, proj, alog_row, dtb_row, gnorm)


def _gdn_bwd(qkvc, proj, alog_row, dtb_row, gnorm, states, dy, *, d, z_off, bg_off, name, hb=GDN_HEADS_BWD):
    t = qkvc.shape[1]
    nh = d // GDN_HEAD_DIM
    hb = min(hb, nh)
    wb = hb * GDN_HEAD_DIM
    nc = t // CHUNK
    ng = nh // hb

    def body(qkv_ref, z_ref, bg_ref, al_ref, dt_ref, gn_ref, sin_ref, dy_ref, dqkv_ref, dz_ref, dbg_ref, dal_ref, ddt_ref, dgn_ref,
             ds_scr):
        s, hg = pl.program_id(0), pl.program_id(1)

        @pl.when(s == 0)
        def _():
            ds_scr[hg] = jnp.zeros((hb, GDN_HEAD_DIM, GDN_HEAD_DIM), F32)

        @pl.when(hg == 0)
        def _():
            dbg_ref[...] = jnp.zeros_like(dbg_ref)

        @pl.when((s == 0) & (hg == 0))
        def _():
            dal_ref[...] = jnp.zeros_like(dal_ref)
            ddt_ref[...] = jnp.zeros_like(ddt_ref)
            dgn_ref[...] = jnp.zeros_like(dgn_ref)

        dstates = [ds_scr[hg, i] for i in range(hb)]
        bg, al, dt, gn = bg_ref[...], al_ref[...], dt_ref[...], gn_ref[...]
        sls = [slice(i * GDN_HEAD_DIM, (i + 1) * GDN_HEAD_DIM) for i in range(hb)]
        fn = functools.partial(_gdn_chunk, first_head=hg * hb, n_heads=nh)
        _, vjp = jax.vjp(fn, [qkv_ref[0, :, sl] for sl in sls], [qkv_ref[1, :, sl] for sl in sls], [qkv_ref[2, :, sl] for sl in sls],
                         [z_ref[:, sl] for sl in sls], bg, al, dt, gn, [sin_ref[0, i] for i in range(hb)])
        dq, dk, dv, dz, dbg, dal, ddt, dgn, dst = vjp(([dy_ref[:, sl] for sl in sls], dstates))
        for i in range(hb):
            dqkv_ref[0, :, sls[i]] = dq[i]
            dqkv_ref[1, :, sls[i]] = dk[i]
            dqkv_ref[2, :, sls[i]] = dv[i]
            dz_ref[:, sls[i]] = dz[i].astype(dz_ref.dtype)
            ds_scr[hg, i] = dst[i]
        dbg_ref[...] += dbg
        dal_ref[...] += dal
        ddt_ref[...] += ddt
        dgn_ref[...] += dgn

    def ch(s):
        return nc - 1 - s

    row = lambda s, hg: (0, 0)
    return pl.pallas_call(
        body, name=name, grid=(nc, ng),
        in_specs=[pl.BlockSpec((3, CHUNK, wb), lambda s, hg: (0, ch(s), hg)),
                  pl.BlockSpec((CHUNK, wb), lambda s, hg: (ch(s), z_off // wb + hg)),
                  pl.BlockSpec((CHUNK, LANES), lambda s, hg: (ch(s), bg_off // LANES)),
                  pl.BlockSpec((1, LANES), row), pl.BlockSpec((1, LANES), row), pl.BlockSpec((1, LANES), row),
                  pl.BlockSpec((1, hb, GDN_HEAD_DIM, GDN_HEAD_DIM), lambda s, hg: (ch(s), hg, 0, 0)),
                  pl.BlockSpec((CHUNK, wb), lambda s, hg: (ch(s), hg))],
        out_specs=[pl.BlockSpec((3, CHUNK, wb), lambda s, hg: (0, ch(s), hg)),
                   pl.BlockSpec((CHUNK, wb), lambda s, hg: (ch(s), hg)),
                   pl.BlockSpec((CHUNK, LANES), lambda s, hg: (ch(s), 0)),
                   pl.BlockSpec((1, LANES), row), pl.BlockSpec((1, LANES), row), pl.BlockSpec((1, LANES), row)],
        out_shape=[jax.ShapeDtypeStruct((3, t, d), F32), jax.ShapeDtypeStruct((t, d), BF16), jax.ShapeDtypeStruct((t, LANES), F32),
                   jax.ShapeDtypeStruct((1, LANES), F32), jax.ShapeDtypeStruct((1, LANES), F32), jax.ShapeDtypeStruct((1, LANES), F32)],
        scratch_shapes=[pltpu.VMEM((ng, hb, GDN_HEAD_DIM, GDN_HEAD_DIM), F32)],
        compiler_params=_params(("arbitrary", "arbitrary")),
    )(qkvc, proj, proj, alog_row, dtb_row, gnorm, states, dy)


@jax.custom_vjp
def _swap_halves(x):
    return pltpu.roll(x, SWA_HEAD_DIM, 1)


_swap_halves.defvjp(lambda x: (pltpu.roll(x, SWA_HEAD_DIM, 1), None), lambda _, g: (pltpu.roll(g, SWA_HEAD_DIM, 1),))

SWA_PAIR_Q = 2 * GQA_GROUP * SWA_HEAD_DIM


def _swa_block(q, kp, kc, vp, vc, sink_row, slope_row, keep_prev, pair):
    kb = jnp.concatenate([kp, kc], axis=0)
    vb = jnp.concatenate([vp, vc], axis=0)
    lane = lax.broadcasted_iota(jnp.int32, (1, LANES), 1)
    low = lane < SWA_HEAD_DIM
    high = jnp.logical_not(low)
    qi = lax.broadcasted_iota(jnp.int32, (WINDOW, 2 * WINDOW), 0)
    sj = lax.broadcasted_iota(jnp.int32, (WINDOW, 2 * WINDOW), 1)
    dist = qi + WINDOW - sj
    valid = (dist >= 0) & (dist < WINDOW) & ((sj >= WINDOW) | (keep_prev > 0.5))
    distf = dist.astype(F32)
    kk, vv = [], []
    for mine in (low, high):
        x = jnp.where(mine, kb, 0.0)
        kk.append(x + _swap_halves(x))
        y = jnp.where(mine, vb, 0.0)
        vv.append(y + _swap_halves(y))
    hl = range(2 * GQA_GROUP)
    half = [low if h % 2 == 0 else high for h in hl]
    slope = [_lane_pick(slope_row, lane, pair * (2 * GQA_GROUP) + h) for h in hl]
    sink = [_lane_pick(sink_row, lane, pair * (2 * GQA_GROUP) + h) for h in hl]
    qm = [jnp.where(half[h], q[:, (h // 2) * LANES:(h // 2 + 1) * LANES], 0.0) for h in hl]
    sc = [mm_nt(qm[h], kk[h // GQA_GROUP]) * (SWA_HEAD_DIM ** -0.5) for h in hl]
    sc = [jnp.where(valid, sc[h] - slope[h] * distf, -1e30) for h in hl]
    m = [lax.stop_gradient(jnp.maximum(jnp.max(sc[h], axis=-1, keepdims=True), sink[h])) for h in hl]
    p = [jnp.exp(sc[h] - m[h]) for h in hl]
    probs = [p[h] / (jnp.sum(p[h], axis=-1, keepdims=True) + jnp.exp(sink[h] - m[h])) for h in hl]
    od = [jnp.where(half[h], mm_nn(probs[h], vv[h // GQA_GROUP]), 0.0) for h in hl]
    return jnp.concatenate([od[2 * i] + od[2 * i + 1] for i in range(GQA_GROUP)], axis=1)


def _swa_specs(t, q_off, k_off, v_off, order):
    nb = t // WINDOW

    def blk(s):
        return order(s, nb)

    return nb, [pl.BlockSpec((WINDOW, SWA_PAIR_Q), lambda p, s: (blk(s), q_off // SWA_PAIR_Q + p)),
                pl.BlockSpec((WINDOW, LANES), lambda p, s: (jnp.maximum(blk(s) - 1, 0), k_off // LANES + p)),
                pl.BlockSpec((WINDOW, LANES), lambda p, s: (blk(s), k_off // LANES + p)),
                pl.BlockSpec((WINDOW, LANES), lambda p, s: (jnp.maximum(blk(s) - 1, 0), v_off // LANES + p)),
                pl.BlockSpec((WINDOW, LANES), lambda p, s: (blk(s), v_off // LANES + p)),
                pl.BlockSpec((1, LANES), lambda p, s: (0, 0)), pl.BlockSpec((1, LANES), lambda p, s: (0, 0))]


def _swa_fwd(proj, sink_row, slope_row, *, d, q_off, k_off, v_off, name):
    t = proj.shape[0]
    n_pairs = d // SWA_PAIR_Q
    nb, in_specs = _swa_specs(t, q_off, k_off, v_off, lambda s, nb: s)

    def body(q_ref, kp_ref, kc_ref, vp_ref, vc_ref, sink_ref, slope_ref, o_ref):
        keep = (pl.program_id(1) > 0).astype(F32)
        o = _swa_block(q_ref[...], kp_ref[...], kc_ref[...], vp_ref[...], vc_ref[...], sink_ref[...], slope_ref[...], keep,
                       pl.program_id(0))
        o_ref[...] = o.astype(o_ref.dtype)

    return pl.pallas_call(
        body, name=name, grid=(n_pairs, nb), in_specs=in_specs,
        out_specs=pl.BlockSpec((WINDOW, SWA_PAIR_Q), lambda p, s: (s, p)),
        out_shape=jax.ShapeDtypeStruct((t, d), BF16),
        compiler_params=_params(("parallel", "arbitrary")),
    )(proj, proj, proj, proj, proj, sink_row, slope_row)


def _swa_bwd(proj, sink_row, slope_row, do, *, d, q_off, k_off, v_off, name):
    t = proj.shape[0]
    n_pairs = d // SWA_PAIR_Q
    nb, in_specs = _swa_specs(t, q_off, k_off, v_off, lambda s, nb: nb - 1 - s)

    def body(q_ref, kp_ref, kc_ref, vp_ref, vc_ref, sink_ref, slope_ref, do_ref, dq_ref, dk_ref, dv_ref, dsink_ref, ck_ref, cv_ref):
        p, s = pl.program_id(0), pl.program_id(1)
        keep = (s < nb - 1).astype(F32)
        fn = functools.partial(_swa_block, slope_row=slope_ref[...], keep_prev=keep, pair=p)
        _, vjp = jax.vjp(fn, q_ref[...], kp_ref[...], kc_ref[...], vp_ref[...], vc_ref[...], sink_ref[...])
        dq, dkp, dkc, dvp, dvc, dsink = vjp(do_ref[...])

        @pl.when(s == 0)
        def _():
            ck_ref[...] = jnp.zeros_like(ck_ref)
            cv_ref[...] = jnp.zeros_like(cv_ref)

        @pl.when((s == 0) & (p == 0))
        def _():
            dsink_ref[...] = jnp.zeros_like(dsink_ref)

        dq_ref[...] = dq.astype(dq_ref.dtype)
        dk_ref[...] = (dkc + ck_ref[...]).astype(dk_ref.dtype)
        dv_ref[...] = (dvc + cv_ref[...]).astype(dv_ref.dtype)
        ck_ref[...] = dkp
        cv_ref[...] = dvp
        dsink_ref[...] += dsink

    in_specs = in_specs + [pl.BlockSpec((WINDOW, SWA_PAIR_Q), lambda p, s: (nb - 1 - s, p))]
    kv_w = d // GQA_GROUP
    return pl.pallas_call(
        body, name=name, grid=(n_pairs, nb), in_specs=in_specs,
        out_specs=[pl.BlockSpec((WINDOW, SWA_PAIR_Q), lambda p, s: (nb - 1 - s, p)),
                   pl.BlockSpec((WINDOW, LANES), lambda p, s: (nb - 1 - s, p)),
                   pl.BlockSpec((WINDOW, LANES), lambda p, s: (nb - 1 - s, p)),
                   pl.BlockSpec((1, LANES), lambda p, s: (0, 0))],
        out_shape=[jax.ShapeDtypeStruct((t, d), BF16), jax.ShapeDtypeStruct((t, kv_w), BF16), jax.ShapeDtypeStruct((t, kv_w), BF16),
                   jax.ShapeDtypeStruct((1, LANES), F32)],
        scratch_shapes=[pltpu.VMEM((WINDOW, LANES), F32), pltpu.VMEM((WINDOW, LANES), F32)],
        compiler_params=_params(("arbitrary", "arbitrary")),
    )(proj, proj, proj, proj, proj, sink_row, slope_row, do)


def _layout(d):
    kv = d // GQA_GROUP
    return dict(z=3 * d, q=4 * d, gate=5 * d, k=7 * d, v=7 * d + kv, bg=7 * d + 2 * kv, width=7 * d + 2 * kv + LANES)


def _pack_w_in(w, d):
    nh = d // GDN_HEAD_DIM
    kv = d // GQA_GROUP
    o = 4 * d + 2 * nh
    parts = [w[..., :4 * d], w[..., o:o + d], w[..., o + d + 2 * kv:o + 3 * d + 2 * kv], w[..., o + d:o + d + 2 * kv],
             w[..., 4 * d:o], jnp.zeros(w.shape[:-1] + (LANES - 2 * nh,), w.dtype)]
    return jnp.concatenate(parts, axis=-1)


def _unpack_w_in(wp, d):
    nh = d // GDN_HEAD_DIM
    kv = d // GQA_GROUP
    lay = _layout(d)
    parts = [wp[..., :4 * d], wp[..., lay["bg"]:lay["bg"] + 2 * nh], wp[..., lay["q"]:lay["q"] + d],
             wp[..., lay["k"]:lay["k"] + 2 * kv], wp[..., lay["gate"]:lay["gate"] + 2 * d]]
    return jnp.concatenate(parts, axis=-1)


def _pad_row(v):
    return jnp.pad(v.astype(F32), (0, LANES - v.shape[0]))[None, :]


def _alibi_row(d):
    nq = d // SWA_HEAD_DIM
    return _pad_row(2.0 ** (-8.0 * jnp.arange(1, nq + 1, dtype=F32) / nq))


def _layer_fwd(x, p, tag, late=None):
    t, d = x.shape
    lay = _layout(d)
    tn = 1152 if lay["width"] % 1152 == 0 else 1024
    h1 = _rmsnorm_fwd(x, p["norm1_g"], tag + "rms1")
    proj = _matmul(h1, p["w_in"], name=tag + "mm_in", tn_cap=tn)
    qkvc = _conv_fwd(proj, p["conv_w"], d, tag + "conv")
    gdn_o, states = _gdn_fwd(qkvc, proj, p["a_log"], p["dt_bias"], p["gdn_norm_g"], d=d, z_off=lay["z"], bg_off=lay["bg"],
                             name=tag + "gdn")
    swa_o = _swa_fwd(proj, p["attn_sinks"], p["alibi"], d=d, q_off=lay["q"], k_off=lay["k"], v_off=lay["v"], name=tag + "swa")
    if late is not None:
        p = dict(p, **late(swa_o))
    y_gdn = _matmul(gdn_o, p["w_branch_gdn"], name=tag + "mm_bg")
    y_swa = _matmul(swa_o, p["w_branch_swa"], name=tag + "mm_bs")
    mix = _merge_fwd(y_gdn, y_swa, proj, lay["gate"], tag + "merge")
    x1 = _matmul(mix, p["w_out"], add=x, name=tag + "mm_out")
    h2 = _rmsnorm_fwd(x1, p["norm2_g"], tag + "rms2")
    up = _matmul(h2, p["w_ff_up"], name=tag + "mm_up", b_split=N_CHIPS)
    act = _relu2_fwd(up, tag + "relu2")
    x2 = _matmul(act, p["w_ff_down"], add=x1, name=tag + "mm_down")
    return x2, dict(x=x, h1=h1, proj=proj, qkvc=qkvc, states=states, gdn_o=gdn_o, swa_o=swa_o, y_gdn=y_gdn, y_swa=y_swa, mix=mix,
                    x1=x1, h2=h2, up=up, act=act)


def _layer_bwd(dx2, p, s, tag, mid=None):
    t, d = dx2.shape
    lay = _layout(d)
    tn = 1152 if lay["width"] % 1152 == 0 else 1024
    nh = d // GDN_HEAD_DIM
    g = {}
    dact = _matmul(dx2, p["w_ff_down"], tb=True, name=tag + "mm_dact", tm_cap=512)
    g["w_ff_down"] = _matmul(s["act"], dx2, ta=True, out_dtype=BF16, name=tag + "mm_dwdown", tk_cap=1024)
    dup = _relu2_bwd(s["up"], dact, tag + "relu2b")
    g["w_ff_up"] = _matmul(s["h2"], dup, ta=True, out_dtype=BF16, name=tag + "mm_dwup", out_split=N_CHIPS)
    dh2 = _matmul(dup, p["w_ff_up"], tb=True, name=tag + "mm_dh2", b_split=N_CHIPS)
    dx1, g["norm2_g"] = _rmsnorm_bwd(s["x1"], p["norm2_g"], dh2, dx2, tag + "rms2b")
    dmix = _matmul(dx1, p["w_out"], tb=True, name=tag + "mm_dmix", tm_cap=512)
    g["w_out"] = _matmul(s["mix"], dx1, ta=True, out_dtype=BF16, name=tag + "mm_dwout", tk_cap=1024)
    dyg, dys, dgl = _merge_bwd(s["y_gdn"], s["y_swa"], s["proj"], lay["gate"], dmix, tag + "mergeb")
    g["w_branch_gdn"] = _matmul(s["gdn_o"], dyg, ta=True, out_dtype=BF16, name=tag + "mm_dwbg")
    g["w_branch_swa"] = _matmul(s["swa_o"], dys, ta=True, out_dtype=BF16, name=tag + "mm_dwbs")
    dgdn_o = _matmul(dyg, p["w_branch_gdn"], tb=True, name=tag + "mm_dgdn")
    dswa_o = _matmul(dys, p["w_branch_swa"], tb=True, name=tag + "mm_dswa")
    if mid is not None:
        mid(g, dswa_o)
    dq_s, dk_s, dv_s, dsink = _swa_bwd(s["proj"], p["attn_sinks"], p["alibi"], dswa_o, d=d, q_off=lay["q"], k_off=lay["k"],
                                       v_off=lay["v"], name=tag + "swab")
    dqkvc, dz, dbg, dal, ddt, dgn = _gdn_bwd(s["qkvc"], s["proj"], p["a_log"], p["dt_bias"], p["gdn_norm_g"], s["states"], dgdn_o,
                                             d=d, z_off=lay["z"], bg_off=lay["bg"], name=tag + "gdnb")
    dqkv, g["conv_w"] = _conv_bwd(s["proj"], p["conv_w"], dqkvc, d, tag + "convb")
    dproj = jnp.concatenate([dqkv, dz, dq_s, dgl, dk_s, dv_s, lax.reduce_precision(dbg, 8, 7).astype(BF16)], axis=1)
    g["w_in"] = _matmul(s["h1"], dproj, ta=True, out_dtype=BF16, name=tag + "mm_dwin", tn_cap=tn)
    dh1 = _matmul(dproj, p["w_in"], tb=True, name=tag + "mm_dh1", tk_cap=tn)
    dx, g["norm1_g"] = _rmsnorm_bwd(s["x"], p["norm1_g"], dh1, dx1, tag + "rms1b")
    g["a_log"], g["dt_bias"], g["gdn_norm_g"], g["attn_sinks"] = dal[0, :nh], ddt[0, :nh], dgn[0], dsink[0, :d // SWA_HEAD_DIM]
    return dx, g


MESH = pl.DeviceIdType.MESH
HBM_SPEC = pl.BlockSpec(memory_space=pl.ANY)


def _place():
    x, y, c = lax.axis_index("x"), lax.axis_index("y"), lax.axis_index("c")
    return x, y, c, 2 * x + y


def _flip(x, y, k):
    px, py = x ^ (k >> 1), y ^ (k & 1)
    return px, py, 2 * px + py


def _cast_into_slot(w, layer, pos, name):
    _, r, cols = w.shape
    tm = _pick(r, max(16, (1 << 19) // cols // 16 * 16), 16)

    def body(x_ref, y_ref, w_ref, o_ref):
        o_ref[...] = w_ref[...].astype(o_ref.dtype)

    return pl.pallas_call(
        body, name=name,
        grid_spec=pltpu.PrefetchScalarGridSpec(
            num_scalar_prefetch=2, grid=(r // tm,),
            in_specs=[pl.BlockSpec((None, tm, cols), lambda i, xr, yr: (layer, i, 0))],
            out_specs=pl.BlockSpec((None, tm, cols), lambda i, xr, yr: (2 * xr[0] + yr[0], i, 0))),
        out_shape=jax.ShapeDtypeStruct((N_CHIPS, r, cols), BF16),
        compiler_params=_params(("parallel",)),
    )(pos[0], pos[1], w)


SEM_SPEC = pl.BlockSpec(memory_space=pltpu.SEMAPHORE)
SPLIT_COPY = pltpu.CompilerParams(has_side_effects=pltpu.SideEffectType.DATAFLOW_SIDE_EFFECTING)
TOKEN = jax.ShapeDtypeStruct((8, LANES), F32)
TOKEN_SPEC = pl.BlockSpec(memory_space=pltpu.VMEM)


def _gather_start(bufs, name):
    n = len(bufs)

    def body(*refs):
        outs, sems, token = refs[n:2 * n], refs[2 * n:8 * n], refs[8 * n]
        x, y, c, ci = _place()
        for a in range(n):
            hr = bufs[a].shape[1] // 2
            mine = outs[a].at[ci, pl.ds(c * hr, hr)]
            for k in (1, 2, 3):
                px, py, _ = _flip(x, y, k)
                pltpu.make_async_remote_copy(src_ref=mine, dst_ref=mine, send_sem=sems[3 * a + k - 1], recv_sem=sems[3 * n + 3 * a + k - 1],
                                             device_id=(px, py, c), device_id_type=MESH).start()
        token[...] = jnp.zeros_like(token)

    res = pl.pallas_call(
        body, name=name, in_specs=[HBM_SPEC] * n, out_specs=[HBM_SPEC] * n + [SEM_SPEC] * (6 * n) + [TOKEN_SPEC],
        out_shape=[pltpu.HBM(b.shape, b.dtype) for b in bufs] + [pltpu.SemaphoreType.DMA(())] * (6 * n) + [TOKEN],
        input_output_aliases={a: a for a in range(n)}, compiler_params=SPLIT_COPY,
    )(*[pltpu.with_memory_space_constraint(b, pltpu.HBM) for b in bufs])
    return res[:n], res[n:7 * n], res[7 * n]


def _gather_wait(bufs, sems, after, name):
    n = len(bufs)

    def body(*refs):
        sems, outs = refs[n:7 * n], refs[7 * n + 1:8 * n + 1]
        x, y, c, ci = _place()
        for a in range(n):
            hr = bufs[a].shape[1] // 2
            mine = outs[a].at[ci, pl.ds(c * hr, hr)]
            for k in (1, 2, 3):
                px, py, pj = _flip(x, y, k)
                landed = outs[a].at[pj, pl.ds(c * hr, hr)]
                cp = pltpu.make_async_remote_copy(src_ref=mine, dst_ref=landed, send_sem=sems[3 * a + k - 1], recv_sem=sems[3 * n + 3 * a + k - 1],
                                                  device_id=(px, py, c), device_id_type=MESH)
                cp.wait_send()
                cp.wait_recv()

    return pl.pallas_call(
        body, name=name, in_specs=[HBM_SPEC] * n + [SEM_SPEC] * (6 * n) + [HBM_SPEC], out_specs=[HBM_SPEC] * n,
        out_shape=[pltpu.HBM(b.shape, b.dtype) for b in bufs],
        input_output_aliases={a: a for a in range(n)}, compiler_params=SPLIT_COPY,
    )(*bufs, *sems, after)


def _gather_forward(bufs, name):
    n = len(bufs)

    def body(*refs):
        outs = refs[n:2 * n]
        send_sems, recv_sems = refs[2 * n:]
        x, y, c, _ = _place()
        waits = []
        for a in range(n):
            hr = bufs[a].shape[1] // 2
            for k in (1, 2, 3):
                _, _, pj = _flip(x, y, k)
                landed = outs[a].at[pj, pl.ds(c * hr, hr)]
                fw = pltpu.make_async_remote_copy(src_ref=landed, dst_ref=landed, send_sem=send_sems.at[a, k - 1], recv_sem=recv_sems.at[a, k - 1],
                                                  device_id=(x, y, 1 - c), device_id_type=MESH)
                fw.start()
                waits.append(fw.wait_send)
                passed = outs[a].at[pj, pl.ds((1 - c) * hr, hr)]
                waits.append(pltpu.make_async_remote_copy(src_ref=passed, dst_ref=passed, send_sem=send_sems.at[a, k - 1],
                                                          recv_sem=recv_sems.at[a, k - 1], device_id=(x, y, 1 - c),
                                                          device_id_type=MESH).wait_recv)
        for w in waits:
            w()

    return pl.pallas_call(
        body, name=name, in_specs=[HBM_SPEC] * n, out_specs=[HBM_SPEC] * n,
        out_shape=[jax.ShapeDtypeStruct(b.shape, b.dtype) for b in bufs],
        input_output_aliases={a: a for a in range(n)},
        scratch_shapes=[pltpu.SemaphoreType.DMA((n, 3))] * 2,
    )(*bufs)


def _swap_with_sibling(gs, name):
    n = len(gs)

    def body(*refs):
        ins, outs = refs[:n], refs[n:2 * n]
        send_sems, recv_sems = refs[2 * n:]
        x, y, c, _ = _place()
        cps = []
        for a in range(n):
            hr = gs[a].shape[1] // 2
            cp = pltpu.make_async_remote_copy(src_ref=ins[a].at[:, pl.ds((1 - c) * hr, hr)], dst_ref=outs[a], send_sem=send_sems.at[a],
                                              recv_sem=recv_sems.at[a], device_id=(x, y, 1 - c), device_id_type=MESH)
            cp.start()
            cps.append(cp)
        for cp in cps:
            cp.wait()

    return pl.pallas_call(
        body, name=name, in_specs=[HBM_SPEC] * n, out_specs=[HBM_SPEC] * n,
        out_shape=[jax.ShapeDtypeStruct((g.shape[0], g.shape[1] // 2, g.shape[2]), g.dtype) for g in gs],
        scratch_shapes=[pltpu.SemaphoreType.DMA((n,))] * 2,
    )(*gs)


def _scatter_start(hs, name):
    n = len(hs)

    def body(*refs):
        srcs, lands, sems, token = refs[n:2 * n], refs[2 * n:3 * n], refs[3 * n:9 * n], refs[9 * n]
        x, y, c, ci = _place()
        for a in range(n):
            for k in (1, 2, 3):
                px, py, pj = _flip(x, y, k)
                pltpu.make_async_remote_copy(src_ref=srcs[a].at[pj], dst_ref=lands[a].at[ci], send_sem=sems[3 * a + k - 1],
                                             recv_sem=sems[3 * n + 3 * a + k - 1], device_id=(px, py, c), device_id_type=MESH).start()
        token[...] = jnp.zeros_like(token)

    res = pl.pallas_call(
        body, name=name, in_specs=[HBM_SPEC] * n, out_specs=[HBM_SPEC] * (2 * n) + [SEM_SPEC] * (6 * n) + [TOKEN_SPEC],
        out_shape=[pltpu.HBM(h.shape, h.dtype) for h in hs] * 2 + [pltpu.SemaphoreType.DMA(())] * (6 * n) + [TOKEN],
        input_output_aliases={a: a for a in range(n)}, compiler_params=SPLIT_COPY,
    )(*[pltpu.with_memory_space_constraint(h, pltpu.HBM) for h in hs])
    return res[:n], res[n:2 * n], res[2 * n:8 * n], res[8 * n]


def _scatter_wait(hs, lands, sems, after, name):
    n = len(hs)

    def body(*refs):
        sems, srcs, lands_o = refs[2 * n:8 * n], refs[8 * n + 1:9 * n + 1], refs[9 * n + 1:10 * n + 1]
        x, y, c, ci = _place()
        for a in range(n):
            for k in (1, 2, 3):
                px, py, pj = _flip(x, y, k)
                cp = pltpu.make_async_remote_copy(src_ref=srcs[a].at[pj], dst_ref=lands_o[a].at[pj], send_sem=sems[3 * a + k - 1],
                                                  recv_sem=sems[3 * n + 3 * a + k - 1], device_id=(px, py, c), device_id_type=MESH)
                cp.wait_send()
                cp.wait_recv()

    res = pl.pallas_call(
        body, name=name, in_specs=[HBM_SPEC] * (2 * n) + [SEM_SPEC] * (6 * n) + [HBM_SPEC], out_specs=[HBM_SPEC] * (2 * n),
        out_shape=[pltpu.HBM(h.shape, h.dtype) for h in hs] * 2,
        input_output_aliases={a: a for a in range(2 * n)}, compiler_params=SPLIT_COPY,
    )(*hs, *lands, *sems, after)
    return res[:n], res[n:]


def _share_with_sibling(bufs, name):
    n = len(bufs)

    def body(*refs):
        outs = refs[n:2 * n]
        send_sems, recv_sems = refs[2 * n:]
        x, y, c, _ = _place()
        waits = []
        for a in range(n):
            mine = outs[a].at[c]
            cp = pltpu.make_async_remote_copy(src_ref=mine, dst_ref=mine, send_sem=send_sems.at[a], recv_sem=recv_sems.at[a],
                                              device_id=(x, y, 1 - c), device_id_type=MESH)
            cp.start()
            waits.append(cp.wait_send)
            got = outs[a].at[1 - c]
            waits.append(pltpu.make_async_remote_copy(src_ref=got, dst_ref=got, send_sem=send_sems.at[a], recv_sem=recv_sems.at[a],
                                                      device_id=(x, y, 1 - c), device_id_type=MESH).wait_recv)
        for w in waits:
            w()

    return pl.pallas_call(
        body, name=name, in_specs=[HBM_SPEC] * n, out_specs=[HBM_SPEC] * n,
        out_shape=[jax.ShapeDtypeStruct(b.shape, b.dtype) for b in bufs],
        input_output_aliases={a: a for a in range(n)},
        scratch_shapes=[pltpu.SemaphoreType.DMA((n,))] * 2,
    )(*bufs)


def _add_sibling_half(g, got, core, name):
    nc, r, cols = g.shape
    hr = r // 2
    tm = _pick(hr, 256, 16)

    def body(core_ref, g_ref, o_ref, s_ref):
        s_ref[...] = (g_ref[...].astype(F32) + o_ref[...].astype(F32)).astype(s_ref.dtype)

    return pl.pallas_call(
        body, name=name,
        grid_spec=pltpu.PrefetchScalarGridSpec(
            num_scalar_prefetch=1, grid=(nc, hr // tm),
            in_specs=[pl.BlockSpec((None, None, tm, cols), lambda j, i, cr: (j, cr[0], i, 0)),
                      pl.BlockSpec((None, tm, cols), lambda j, i, cr: (j, i, 0))],
            out_specs=pl.BlockSpec((None, tm, cols), lambda j, i, cr: (j, i, 0))),
        out_shape=jax.ShapeDtypeStruct((nc, hr, cols), g.dtype),
        compiler_params=_params(("parallel", "parallel")),
    )(core, g.reshape(nc, 2, hr, cols), got)


def _sum_chips(own, parts, pos, name):
    nc, r, cols = parts.shape
    tm = _pick(r, 256, 16)

    def body(x_ref, y_ref, c_ref, own_ref, p_ref, o_ref):
        chip = 2 * x_ref[0] + y_ref[0]
        acc = own_ref[...].astype(F32)
        for k in range(1, nc):
            acc = acc + p_ref[chip ^ k].astype(F32)
        o_ref[...] = acc

    return pl.pallas_call(
        body, name=name,
        grid_spec=pltpu.PrefetchScalarGridSpec(
            num_scalar_prefetch=3, grid=(r // tm,),
            in_specs=[pl.BlockSpec((None, tm, cols), lambda i, xr, yr, cr: (2 * xr[0] + yr[0], i, 0)),
                      pl.BlockSpec((nc, tm, cols), lambda i, xr, yr, cr: (0, i, 0))],
            out_specs=pl.BlockSpec((None, tm, cols), lambda i, xr, yr, cr: (cr[0], i, 0))),
        out_shape=jax.ShapeDtypeStruct((2, r, cols), F32),
        compiler_params=_params(("parallel",)),
    )(*pos, own, parts)


def _reduce_scatter_start(gs, pos, tag):
    got = _swap_with_sibling(gs, tag + "rs_swap")
    hs = [_add_sibling_half(g, o, pos[2], tag + "rs_add%d" % i) for i, (g, o) in enumerate(zip(gs, got))]
    hs, lands, sems, token = _scatter_start(hs, tag + "rs_scatter_start")
    return (hs, lands, sems), token


def _reduce_scatter_finish(pending, after, pos, tag):
    hs, lands, sems = pending
    hs, parts = _scatter_wait(hs, lands, sems, after, tag + "rs_scatter_wait")
    rs = [_sum_chips(h, p, pos, tag + "rs_sum%d" % i) for i, (h, p) in enumerate(zip(hs, parts))]
    both = _share_with_sibling(rs, tag + "rs_share")
    return [b.reshape(2 * b.shape[1], b.shape[2]) for b in both]


def _allreduce_small(v, name):
    rows = v.shape[0]

    def body(v_ref, o_ref, buf, send_sems, recv_sems, local_sem):
        x, y, c, _ = _place()
        me, sibling = (x, y, c), (x, y, 1 - c)
        chips = [_flip(x, y, k)[:2] for k in (1, 2, 3)]

        def slot(px, py, pc):
            return buf.at[4 * px + 2 * py + pc]

        def copy(k, block, to, src=None):
            return pltpu.make_async_remote_copy(src_ref=slot(*block) if src is None else src, dst_ref=slot(*block), send_sem=send_sems.at[k],
                                                recv_sem=recv_sems.at[k], device_id=to, device_id_type=MESH)

        mine = pltpu.make_async_copy(v_ref, slot(*me), local_sem)
        mine.start()
        first = [copy(0, me, sibling, src=v_ref)] + [copy(1 + j, me, (*chip, c), src=v_ref) for j, chip in enumerate(chips)]
        for cp in first:
            cp.start()
        passed = [copy(4 + j, (*chip, c), sibling) for j, chip in enumerate(chips)]
        for j, chip in enumerate(chips):
            copy(1 + j, (*chip, c), me).wait_recv()
            passed[j].start()
        copy(0, sibling, me).wait_recv()
        for j, chip in enumerate(chips):
            copy(4 + j, (*chip, 1 - c), me).wait_recv()
        for cp in first + passed:
            cp.wait_send()
        mine.wait()
        acc = buf[0]
        for i in range(1, 2 * N_CHIPS):
            acc = acc + buf[i]
        o_ref[...] = acc

    vm = pl.BlockSpec(memory_space=pltpu.VMEM)
    return pl.pallas_call(
        body, name=name, in_specs=[vm], out_specs=vm, out_shape=jax.ShapeDtypeStruct(v.shape, F32),
        scratch_shapes=[pltpu.VMEM((2 * N_CHIPS, rows, LANES), F32), pltpu.SemaphoreType.DMA((7,)), pltpu.SemaphoreType.DMA((7,)),
                        pltpu.SemaphoreType.DMA],
        compiler_params=pltpu.CompilerParams(vmem_limit_bytes=VMEM_LIMIT),
    )(v)


def _adamw(w, g, m, v, name):
    r, cols = w.shape
    tm = _pick(r, max(8, (1 << 18) // max(cols, 1) // 8 * 8), 8)

    def body(w_ref, g_ref, m_ref, v_ref, d_ref, nm_ref, nv_ref):
        gg = g_ref[...]
        nm = ADAM_B1 * m_ref[...] + (1.0 - ADAM_B1) * gg
        nv = ADAM_B2 * v_ref[...] + (1.0 - ADAM_B2) * jnp.square(gg)
        m_hat = nm / (1.0 - ADAM_B1 ** ADAM_STEP)
        v_hat = nv / (1.0 - ADAM_B2 ** ADAM_STEP)
        d_ref[...] = -ADAM_LR * (m_hat / (jnp.sqrt(v_hat) + ADAM_EPS) + ADAM_WD * w_ref[...])
        nm_ref[...] = nm
        nv_ref[...] = nv

    spec = pl.BlockSpec((tm, cols), lambda i: (i, 0))
    return pl.pallas_call(
        body, name=name, grid=(r // tm,), in_specs=[spec] * 4, out_specs=[spec] * 3,
        out_shape=[jax.ShapeDtypeStruct((r, cols), F32)] * 3, compiler_params=_params(("parallel",)),
    )(w, g, m, v)


def _adamw_layer(w, g, m, v, layer, prev, name):
    depth, r, cols = w.shape
    tm = _pick(r, max(8, (1 << 18) // max(cols, 1) // 8 * 8), 8)

    def body(*refs):
        w_ref, g_ref, m_ref, v_ref = refs[:4]
        go_ref, d_ref, nm_ref, nv_ref = refs[-4:]
        gg = g_ref[...]
        nm = ADAM_B1 * m_ref[...] + (1.0 - ADAM_B1) * gg
        nv = ADAM_B2 * v_ref[...] + (1.0 - ADAM_B2) * jnp.square(gg)
        m_hat = nm / (1.0 - ADAM_B1 ** ADAM_STEP)
        v_hat = nv / (1.0 - ADAM_B2 ** ADAM_STEP)
        go_ref[...] = gg
        d_ref[...] = -ADAM_LR * (m_hat / (jnp.sqrt(v_hat) + ADAM_EPS) + ADAM_WD * w_ref[...])
        nm_ref[...] = nm
        nv_ref[...] = nv

    lspec = pl.BlockSpec((None, tm, cols), lambda i: (layer, i, 0))
    gspec = pl.BlockSpec((tm, cols), lambda i: (i, 0))
    extra = [] if prev is None else list(prev)
    return pl.pallas_call(
        body, name=name, grid=(r // tm,), in_specs=[lspec, gspec, lspec, lspec] + [HBM_SPEC] * len(extra), out_specs=[lspec] * 4,
        out_shape=[jax.ShapeDtypeStruct((depth, r, cols), F32)] * 4,
        input_output_aliases={4 + j: j for j in range(len(extra))}, compiler_params=_params(("parallel",)),
    )(w, g, m, v, *extra)


def _adamw_nd(w, g, m, v, name):
    shape = w.shape
    two = (1, shape[0]) if len(shape) == 1 else (int(np.prod(shape[:-1])), shape[-1])
    outs = _adamw(w.reshape(two), g.reshape(two), m.reshape(two), v.reshape(two), name)
    return [o.reshape(shape) for o in outs]


WEIGHTS = ("norm1_g", "w_in", "conv_w", "a_log", "dt_bias", "gdn_norm_g", "attn_sinks", "w_branch_gdn", "w_branch_swa", "w_out",
           "norm2_g", "w_ff_up", "w_ff_down", "final_norm_g")
MATRICES = ("w_in", "w_branch_gdn", "w_branch_swa", "w_out", "w_ff_up", "w_ff_down")


def _to_rows(vec):
    n = vec.shape[0]
    rows = -(-n // (8 * LANES)) * 8
    return jnp.pad(vec, (0, rows * LANES - n)).reshape(rows, LANES)


def kernel(x, norm1_g, w_in, conv_w, a_log, dt_bias, gdn_norm_g, attn_sinks, w_branch_gdn, w_branch_swa, w_out, norm2_g, w_ff_up, w_ff_down, final_norm_g, loss_target, m_norm1_g, m_w_in, m_conv_w, m_a_log, m_dt_bias, m_gdn_norm_g, m_attn_sinks, m_w_branch_gdn, m_w_branch_swa, m_w_out, m_norm2_g, m_w_ff_up, m_w_ff_down, m_final_norm_g, v_norm1_g, v_w_in, v_conv_w, v_a_log, v_dt_bias, v_gdn_norm_g, v_attn_sinks, v_w_branch_gdn, v_w_branch_swa, v_w_out, v_norm2_g, v_w_ff_up, v_w_ff_down, v_final_norm_g):
    w = dict(norm1_g=norm1_g, w_in=w_in, conv_w=conv_w, a_log=a_log, dt_bias=dt_bias, gdn_norm_g=gdn_norm_g, attn_sinks=attn_sinks,
             w_branch_gdn=w_branch_gdn, w_branch_swa=w_branch_swa, w_out=w_out, norm2_g=norm2_g, w_ff_up=w_ff_up, w_ff_down=w_ff_down,
             final_norm_g=final_norm_g)
    mom = dict(norm1_g=m_norm1_g, w_in=m_w_in, conv_w=m_conv_w, a_log=m_a_log, dt_bias=m_dt_bias, gdn_norm_g=m_gdn_norm_g,
               attn_sinks=m_attn_sinks, w_branch_gdn=m_w_branch_gdn, w_branch_swa=m_w_branch_swa, w_out=m_w_out, norm2_g=m_norm2_g,
               w_ff_up=m_w_ff_up, w_ff_down=m_w_ff_down, final_norm_g=m_final_norm_g)
    var = dict(norm1_g=v_norm1_g, w_in=v_w_in, conv_w=v_conv_w, a_log=v_a_log, dt_bias=v_dt_bias, gdn_norm_g=v_gdn_norm_g,
               attn_sinks=v_attn_sinks, w_branch_gdn=v_w_branch_gdn, w_branch_swa=v_w_branch_swa, w_out=v_w_out, norm2_g=v_norm2_g,
               w_ff_up=v_w_ff_up, w_ff_down=v_w_ff_down, final_norm_g=v_final_norm_g)
    depth, d = norm1_g.shape
    xs, target = x[0], loss_target[0]
    core = lax.axis_index("c")
    chip = 2 * lax.axis_index("x") + lax.axis_index("y")
    pos = tuple(jnp.reshape(lax.axis_index(a), (1,)).astype(jnp.int32) for a in ("x", "y", "c"))

    cw = conv_w.shape[-1]
    placed = lax.dynamic_update_slice(jnp.zeros((depth, CONV_K, N_CHIPS * cw), F32), conv_w, (0, 0, chip * cw))
    placed = placed * (core == 0).astype(F32)
    conv_full = _allreduce_small(_to_rows(placed.reshape(-1)), "gather_conv_w")
    conv_full = conv_full.reshape(-1)[:depth * CONV_K * N_CHIPS * cw].reshape(depth, CONV_K, N_CHIPS * cw)

    alibi = _alibi_row(d)
    first_group, late_group = MATRICES[:1], MATRICES[1:]
    bufs = [{n: _cast_into_slot(w[n], l, pos, "l%d_cast_%s" % (l, n)) for n in MATRICES} for l in range(depth)]

    def start_gather(l, names, after, tag):
        group, _ = lax.optimization_barrier(([bufs[l][n] for n in names], after))
        return _gather_start(group, "l%d_gather_start_%s" % (l, tag))

    def finish_gather(l, names, started, after, tag):
        got = _gather_wait(started[0], started[1], after, "l%d_gather_wait_%s" % (l, tag))
        full = dict(zip(names, _gather_forward(got, "l%d_gather_forward_%s" % (l, tag))))
        out = {}
        if "w_in" in full:
            out["w_in"] = _pack_w_in(jnp.transpose(full["w_in"], (1, 0, 2)).reshape(d, -1), d)
        for n in ("w_branch_gdn", "w_branch_swa", "w_out", "w_ff_down"):
            if n in full:
                out[n] = full[n].reshape(-1, d)
        if "w_ff_up" in full:
            out["w_ff_up"] = full["w_ff_up"]
        return out

    def small_params(l):
        return dict(norm1_g=norm1_g[l][None], norm2_g=norm2_g[l][None], conv_w=conv_full[l], a_log=_pad_row(a_log[l]),
                    dt_bias=_pad_row(dt_bias[l]), gdn_norm_g=gdn_norm_g[l][None], attn_sinks=_pad_row(attn_sinks[l]), alibi=alibi)

    st_a = start_gather(0, first_group, conv_full, "a")
    st_b = start_gather(0, late_group, st_a[2], "b")
    layers = [dict(small_params(0), **finish_gather(0, first_group, st_a, st_b[2], "a"))]
    h = xs
    saved = []
    for l in range(depth):
        p = layers[l]
        late = None
        if l == 0:
            def late(after, st_b=st_b):
                rest = finish_gather(0, late_group, st_b, after, "b")
                layers[0].update(rest)
                return rest
        if l + 1 < depth:
            nxt = start_gather(l + 1, MATRICES, p["w_in"], "all")
            p = dict(p, norm1_g=p["norm1_g"] + nxt[2][:1, :1])
        h, s = _layer_fwd(h, p, "l%d_" % l, late)
        saved.append(s)
        if l + 1 < depth:
            layers.append(dict(small_params(l + 1), **finish_gather(l + 1, MATRICES, nxt, h, "all")))
    dh, d_final, loss_row = _loss_head(h, final_norm_g[None], target, "loss_head")

    grads = {n: [None] * depth for n in ("norm1_g", "norm2_g", "a_log", "dt_bias", "gdn_norm_g", "attn_sinks", "conv_w")}
    updated = {n: None for n in MATRICES}
    small = ("norm1_g", "norm2_g", "a_log", "dt_bias", "gdn_norm_g", "attn_sinks", "conv_w")
    state = {"pending": None, "todo": None}

    def adamw_todo():
        if state["todo"] is not None:
            l, names, sums = state["todo"]
            for n, r in zip(names, sums):
                updated[n] = _adamw_layer(w[n], r, mom[n], var[n], l, updated[n], "l%d_adamw_%s" % (l, n))
            state["todo"] = None

    def finish_scatter(after):
        adamw_todo()
        if state["pending"] is not None:
            l, names, pending, tag = state["pending"]
            state["todo"] = (l, names, _reduce_scatter_finish(pending, after, pos, "l%d_%s_" % (l, tag)))
            state["pending"] = None

    def start_scatter(l, names, mats, tag):
        pending, token = _reduce_scatter_start(mats, pos, "l%d_%s_" % (l, tag))
        state["pending"] = (l, names, pending, tag)
        return token

    def stacked(g, n):
        return g[n] if n == "w_ff_up" else g[n].reshape(N_CHIPS, -1, g[n].shape[-1])

    for l in reversed(range(depth)):
        def mid(g, after, l=l):
            finish_scatter(after)
            start_scatter(l, late_group, [stacked(g, n) for n in late_group], "b")

        dh, g = _layer_bwd(dh, layers[l], saved[l], "l%d_" % l, mid)
        for n in grads:
            grads[n][l] = g[n].reshape(-1)
        finish_scatter(dh)
        g_in = _unpack_w_in(g["w_in"], d)
        mats = [jnp.transpose(g_in.reshape(d, N_CHIPS, -1), (1, 0, 2))]
        if l == 0:
            pieces = [jnp.stack(grads[n]).reshape(-1) for n in small] + [d_final.reshape(-1), loss_row[0, :1]]
            sizes = [p.shape[0] for p in pieces]
            packed = _allreduce_small(_to_rows(jnp.concatenate(pieces)), "reduce_small").reshape(-1)
            mats, _ = lax.optimization_barrier((mats, packed))
        token = start_scatter(l, first_group, mats, "a")
        if l > 0:
            dh = dh + token[0, 0]

    offs = np.concatenate([[0], np.cumsum(sizes)])
    red = {n: packed[offs[i]:offs[i + 1]] for i, n in enumerate(small + ("final_norm_g", "loss"))}
    loss = red["loss"][0]

    grad_out = {}
    for n in ("norm1_g", "norm2_g", "a_log", "dt_bias", "gdn_norm_g", "attn_sinks"):
        grad_out[n] = red[n].reshape(w[n].shape)
    grad_out["final_norm_g"] = red["final_norm_g"]
    conv_g = red["conv_w"].reshape(depth, CONV_K, N_CHIPS * cw)
    grad_out["conv_w"] = lax.dynamic_slice(conv_g, (0, 0, chip * cw), (depth, CONV_K, cw))

    delta, new_m, new_v = {}, {}, {}
    for n in grad_out:
        delta[n], new_m[n], new_v[n] = _adamw_nd(w[n], grad_out[n], mom[n], var[n], "adamw_" + n)
    adamw_todo()
    finish_scatter(updated["w_ff_down"][0])
    adamw_todo()
    for n in MATRICES:
        grad_out[n], delta[n], new_m[n], new_v[n] = updated[n]
    return (loss, dh[None], *[grad_out[n] for n in WEIGHTS], *[delta[n] for n in WEIGHTS], *[new_m[n] for n in WEIGHTS],
            *[new_v[n] for n in WEIGHTS])
```

```python
import functools

import jax
import jax.numpy as jnp
import numpy as np
from jax import lax
from jax.experimental import pallas as pl
from jax.experimental.pallas import tpu as pltpu

F32 = jnp.float32
BF16 = jnp.bfloat16

GDN_HEAD_DIM = 128
CHUNK = 64
SWA_HEAD_DIM = 64
WINDOW = 128
CONV_K = 4
GQA_GROUP = 8
NORM_EPS = 1e-6
N_CHIPS = 4
LANES = 128
CONV_HALO = 8
VMEM_LIMIT = 56 * 1024 * 1024

ADAM_LR = 0.001
ADAM_B1 = 0.9
ADAM_B2 = 0.999
ADAM_EPS = 1e-08
ADAM_WD = 0.01
ADAM_STEP = 10

NN = (((1,), (0,)), ((), ()))
NT = (((1,), (1,)), ((), ()))
TN = (((0,), (0,)), ((), ()))


def _pick(dim, cap, mult=LANES):
    if dim <= cap:
        return dim
    t = (cap // mult) * mult
    while t >= mult:
        if dim % t == 0:
            return t
        t -= mult
    return dim


def _params(sem):
    return pltpu.CompilerParams(dimension_semantics=sem, vmem_limit_bytes=VMEM_LIMIT)


def _bdot(a, b, dn):
    return lax.dot_general(a.astype(BF16), b.astype(BF16), dn, preferred_element_type=F32)


@jax.custom_vjp
def mm_nn(a, b):
    return _bdot(a, b, NN)


@jax.custom_vjp
def mm_nt(a, b):
    return _bdot(a, b, NT)


@jax.custom_vjp
def mm_tn(a, b):
    return _bdot(a, b, TN)


mm_nn.defvjp(lambda a, b: (_bdot(a, b, NN), (a, b)), lambda r, g: (_bdot(g, r[1], NT), _bdot(r[0], g, TN)))
mm_nt.defvjp(lambda a, b: (_bdot(a, b, NT), (a, b)), lambda r, g: (_bdot(g, r[1], NN), _bdot(g, r[0], TN)))
mm_tn.defvjp(lambda a, b: (_bdot(a, b, TN), (a, b)), lambda r, g: (_bdot(r[1], g, NT), _bdot(r[0], g, NN)))


def _hdot(a, b, dn=NN):
    return lax.dot_general(a, b, dn, precision=lax.Precision.HIGHEST, preferred_element_type=F32)


def _sigmoid(x):
    return 1.0 / (1.0 + jnp.exp(-x))


def _silu(x):
    return x * _sigmoid(x)


def _softplus(x):
    return jnp.maximum(x, 0.0) + jnp.log(1.0 + jnp.exp(-jnp.abs(x)))


def _lane_pick(row, lane, idx):
    return jnp.sum(jnp.where(lane == idx, row, 0.0), axis=1, keepdims=True)


def _matmul(a, b, *, ta=False, tb=False, out_dtype=F32, add=None, name, tm_cap=1024, tn_cap=1024, tk_cap=2048, b_split=1,
            out_split=1):
    m, k = (a.shape[1], a.shape[0]) if ta else a.shape
    b_rows, b_cols = (b.shape[-2], b.shape[-1] * b_split)
    n = b_rows if tb else b_cols
    assert k == (b_cols if tb else b_rows), (a.shape, b.shape, ta, tb)
    tm = _pick(m, tm_cap)
    tn = _pick(n // max(1 if tb else b_split, out_split), tn_cap)
    tk = _pick(k // (b_split if tb else 1), tk_cap)
    nk = k // tk
    dn = (((0 if ta else 1,), (1 if tb else 0,)), ((), ()))

    def body(*refs):
        if add is None:
            a_ref, b_ref, o_ref, acc_ref = refs
            add_ref = None
        else:
            a_ref, b_ref, add_ref, o_ref, acc_ref = refs
        kk = pl.program_id(2)
        p = lax.dot_general(a_ref[...].astype(BF16), b_ref[...].astype(BF16), dn, preferred_element_type=F32)

        @pl.when(kk == 0)
        def _():
            acc_ref[...] = p

        @pl.when(kk > 0)
        def _():
            acc_ref[...] += p

        @pl.when(kk == nk - 1)
        def _():
            r = acc_ref[...]
            if add_ref is not None:
                r = r + add_ref[...].astype(F32)
            o_ref[...] = r.astype(o_ref.dtype)

    a_spec = pl.BlockSpec((tk, tm), lambda i, j, q: (q, i)) if ta else pl.BlockSpec((tm, tk), lambda i, j, q: (i, q))
    if b_split == 1:
        b_spec = pl.BlockSpec((tn, tk), lambda i, j, q: (j, q)) if tb else pl.BlockSpec((tk, tn), lambda i, j, q: (q, j))
    elif tb:
        per_b = k // b_split // tk
        b_spec = pl.BlockSpec((None, tn, tk), lambda i, j, q: (q // per_b, j, q % per_b))
    else:
        per_b = n // b_split // tn
        b_spec = pl.BlockSpec((None, tk, tn), lambda i, j, q: (j // per_b, q, j % per_b))
    add_spec = pl.BlockSpec((tm, tn), lambda i, j, q: (i, j))
    if out_split == 1:
        o_spec, o_shape = add_spec, (m, n)
    else:
        per_o = n // out_split // tn
        o_spec, o_shape = pl.BlockSpec((None, tm, tn), lambda i, j, q: (j // per_o, i, j % per_o)), (out_split, m, n // out_split)
    in_specs = [a_spec, b_spec] + ([add_spec] if add is not None else [])
    args = (a, b) + ((add,) if add is not None else ())
    return pl.pallas_call(
        body, name=name, grid=(m // tm, n // tn, nk), in_specs=in_specs, out_specs=o_spec,
        out_shape=jax.ShapeDtypeStruct(o_shape, out_dtype), scratch_shapes=[pltpu.VMEM((tm, tn), F32)],
        compiler_params=_params(("parallel", "parallel", "arbitrary")),
    )(*args)


def _rows(fn, row_args, full_args, row_outs, acc_outs, *, t, tm, name):
    n_row, n_full, n_ro = len(row_args), len(full_args), len(row_outs)

    def body(*refs):
        ins = [r[...] for r in refs[:n_row + n_full]]
        outs = fn(*ins)
        o_refs = refs[n_row + n_full:]
        for r, v in zip(o_refs[:n_ro], outs[:n_ro]):
            r[...] = v.astype(r.dtype)
        i = pl.program_id(0)
        for r, v in zip(o_refs[n_ro:], outs[n_ro:]):
            @pl.when(i == 0)
            def _(r=r, v=v):
                r[...] = v

            @pl.when(i > 0)
            def _(r=r, v=v):
                r[...] += v

    in_specs = [pl.BlockSpec((tm, w), functools.partial(lambda i, cb: (i, cb), cb=cb)) for (_, w, cb) in row_args]
    in_specs += [pl.BlockSpec(f.shape, lambda i: (0, 0)) for f in full_args]
    out_specs = [pl.BlockSpec((tm, w), lambda i: (i, 0)) for (w, _) in row_outs]
    out_specs += [pl.BlockSpec(s, lambda i: (0, 0)) for s in acc_outs]
    out_shape = [jax.ShapeDtypeStruct((t, w), d) for (w, d) in row_outs]
    out_shape += [jax.ShapeDtypeStruct(s, F32) for s in acc_outs]
    return pl.pallas_call(
        body, name=name, grid=(t // tm,), in_specs=in_specs, out_specs=out_specs, out_shape=out_shape,
        compiler_params=_params(("arbitrary",)),
    )(*[a for (a, _, _) in row_args], *full_args)


def _rms(x, g):
    return x * lax.rsqrt(jnp.mean(x * x, axis=-1, keepdims=True) + NORM_EPS) * g


def _rmsnorm_fwd(x, g, name):
    t, d = x.shape
    (h,) = _rows(lambda xb, gb: (_rms(xb, gb),), [(x, d, 0)], [g], [(d, BF16)], [], t=t, tm=_pick(t, 512, 8), name=name)
    return h


def _rmsnorm_bwd(x, g, dh, dx_in, name):
    t, d = x.shape

    def fn(xb, dhb, dxb, gb):
        _, vjp = jax.vjp(_rms, xb, gb)
        dx, dg = vjp(dhb)
        return dxb + dx, dg

    return _rows(fn, [(x, d, 0), (dh, d, 0), (dx_in, d, 0)], [g], [(d, F32)], [(1, d)], t=t, tm=_pick(t, 256, 8), name=name)


def _merge(yg, ys, lg, ls):
    return _sigmoid(lg) * yg + _sigmoid(ls) * ys


def _merge_fwd(y_gdn, y_swa, proj, gate_off, name):
    t, d = y_gdn.shape
    cb = gate_off // d
    (mix,) = _rows(lambda a, b, c, e: (_merge(a, b, c, e),), [(y_gdn, d, 0), (y_swa, d, 0), (proj, d, cb), (proj, d, cb + 1)], [],
                   [(d, BF16)], [], t=t, tm=_pick(t, 256, 8), name=name)
    return mix


def _merge_bwd(y_gdn, y_swa, proj, gate_off, dmix, name):
    t, d = y_gdn.shape
    cb = gate_off // d

    def fn(a, b, c, e, g):
        _, vjp = jax.vjp(_merge, a, b, c, e)
        da, db, dc, de = vjp(g)
        return da, db, jnp.concatenate([dc, de], axis=1)

    return _rows(fn, [(y_gdn, d, 0), (y_swa, d, 0), (proj, d, cb), (proj, d, cb + 1), (dmix, d, 0)], [],
                 [(d, BF16), (d, BF16), (2 * d, BF16)], [], t=t, tm=_pick(t, 128, 8), name=name)


def _relu2_fwd(up, name):
    t, f = up.shape
    (act,) = _rows(lambda u: (jnp.square(jnp.maximum(u, 0.0)),), [(up, f, 0)], [], [(f, BF16)], [], t=t, tm=_pick(t, 256, 8), name=name)
    return act


def _relu2_bwd(up, dact, name):
    t, f = up.shape
    (dup,) = _rows(lambda u, g: (g * 2.0 * jnp.maximum(u, 0.0),), [(up, f, 0), (dact, f, 0)], [], [(f, BF16)], [], t=t,
                   tm=_pick(t, 128, 8), name=name)
    return dup


def _loss_head(x, g, target, name):
    t, d = x.shape

    def loss_fn(xb, gb, tb):
        err = _rms(xb, gb) - tb
        return 0.5 * jnp.sum(jnp.mean(err * err, axis=-1, keepdims=True), axis=0, keepdims=True)

    def fn(xb, tb, gb):
        lv, vjp = jax.vjp(lambda a, b: loss_fn(a, b, tb), xb, gb)
        dx, dg = vjp(jnp.ones((1, 1), F32))
        return dx, dg, jnp.broadcast_to(lv, (1, LANES))

    return _rows(fn, [(x, d, 0), (target, d, 0)], [g], [(d, F32)], [(1, d), (1, LANES)], t=t, tm=_pick(t, 256, 8), name=name)


def _conv_silu(prev, cur, w, keep_prev):
    tm = cur.shape[0]
    xp = jnp.concatenate([prev * keep_prev, cur], axis=0)
    y = w[0:1, :] * xp[CONV_HALO - 3:CONV_HALO - 3 + tm]
    for j in range(1, CONV_K):
        y = y + w[j:j + 1, :] * xp[CONV_HALO - 3 + j:CONV_HALO - 3 + j + tm]
    return _silu(y)


def _conv_tiles(t, width):
    tm = _pick(t, 512, CONV_HALO)
    tc = _pick(width, 512)
    return tm, tc, t // tm, width // tc


def _conv_fwd(proj, conv_w, width, name):
    t = proj.shape[0]
    tm, tc, nt, ncw = _conv_tiles(t, width)
    hb = tm // CONV_HALO

    def body(prev_ref, cur_ref, w_ref, o_ref):
        keep = (pl.program_id(1) > 0).astype(F32)
        o_ref[0] = _conv_silu(prev_ref[...], cur_ref[...], w_ref[...], keep)

    return pl.pallas_call(
        body, name=name, grid=(3 * ncw, nt),
        in_specs=[pl.BlockSpec((CONV_HALO, tc), lambda j, i: (jnp.maximum(i * hb - 1, 0), j)),
                  pl.BlockSpec((tm, tc), lambda j, i: (i, j)),
                  pl.BlockSpec((CONV_K, tc), lambda j, i: (0, j))],
        out_specs=pl.BlockSpec((1, tm, tc), lambda j, i: (j // ncw, i, j % ncw)),
        out_shape=jax.ShapeDtypeStruct((3, t, width), F32),
        compiler_params=_params(("parallel", "arbitrary")),
    )(proj, proj, conv_w)


def _conv_bwd(proj, conv_w, dout, width, name):
    t = proj.shape[0]
    tm, tc, nt, ncw = _conv_tiles(t, width)
    hb = tm // CONV_HALO

    def body(prev_ref, cur_ref, w_ref, g_ref, dx_ref, dw_ref, carry_ref):
        s = pl.program_id(1)
        keep = (s < nt - 1).astype(F32)
        _, vjp = jax.vjp(lambda p, c, w: _conv_silu(p, c, w, keep), prev_ref[...], cur_ref[...], w_ref[...])
        dprev, dcur, dw = vjp(g_ref[0])

        @pl.when(s == 0)
        def _():
            carry_ref[...] = jnp.zeros_like(carry_ref)
            dw_ref[...] = dw

        @pl.when(s > 0)
        def _():
            dw_ref[...] += dw

        tail = jnp.concatenate([jnp.zeros((tm - CONV_HALO, tc), F32), carry_ref[...]], axis=0)
        dx_ref[...] = (dcur + tail).astype(dx_ref.dtype)
        carry_ref[...] = dprev

    def row(s):
        return nt - 1 - s

    return pl.pallas_call(
        body, name=name, grid=(3 * ncw, nt),
        in_specs=[pl.BlockSpec((CONV_HALO, tc), lambda j, s: (jnp.maximum(row(s) * hb - 1, 0), j)),
                  pl.BlockSpec((tm, tc), lambda j, s: (row(s), j)),
                  pl.BlockSpec((CONV_K, tc), lambda j, s: (0, j)),
                  pl.BlockSpec((1, tm, tc), lambda j, s: (j // ncw, row(s), j % ncw))],
        out_specs=[pl.BlockSpec((tm, tc), lambda j, s: (row(s), j)),
                   pl.BlockSpec((CONV_K, tc), lambda j, s: (0, j))],
        out_shape=[jax.ShapeDtypeStruct((t, 3 * width), BF16), jax.ShapeDtypeStruct((CONV_K, 3 * width), F32)],
        scratch_shapes=[pltpu.VMEM((CONV_HALO, tc), F32)],
        compiler_params=_params(("parallel", "arbitrary")),
    )(proj, proj, conv_w, dout)


def _inv_unit_lower_raw(mats):
    n = mats[0].shape[0]
    r = lax.broadcasted_iota(jnp.int32, (n, n), 0)
    c = lax.broadcasted_iota(jnp.int32, (n, n), 1)
    eye = (r == c).astype(F32)
    same = jnp.right_shift(r, 4) == jnp.right_shift(c, 4)
    dg = [jnp.where(same, a, 0.0) for a in mats]
    lo = [a - d for a, d in zip(mats, dg)]
    p = [eye - d for d in dg]
    q = dg
    for _ in range(3):
        q = [_hdot(x, x) for x in q]
        p = [_hdot(x, eye + y) for x, y in zip(p, q)]
    nm = [_hdot(x, y) for x, y in zip(p, lo)]
    n2 = [_hdot(x, x) for x in nm]
    left = [_hdot(eye - x, eye + y) for x, y in zip(nm, n2)]
    return [_hdot(x, y) for x, y in zip(left, p)]


@jax.custom_vjp
def _inv_unit_lower(mats):
    return _inv_unit_lower_raw(mats)


def _inv_fwd(mats):
    t = _inv_unit_lower_raw(mats)
    return t, t


def _inv_bwd(ts, gs):
    x = [_hdot(t, g, TN) for t, g in zip(ts, gs)]
    return ([-_hdot(a, t, NT) for a, t in zip(x, ts)],)


_inv_unit_lower.defvjp(_inv_fwd, _inv_bwd)


def _l2n(x):
    return x * lax.rsqrt(jnp.sum(x * x, axis=-1, keepdims=True) + NORM_EPS)


def _gdn_chunk(qcs, kcs, vcs, zs, bg, alog_row, dtb_row, gnorm, states, first_head, n_heads):
    nb = len(qcs)
    hs = range(nb)
    cs = qcs[0].shape[0]
    lane = lax.broadcasted_iota(jnp.int32, (1, LANES), 1)
    r = lax.broadcasted_iota(jnp.int32, (cs, cs), 0)
    c = lax.broadcasted_iota(jnp.int32, (cs, cs), 1)
    q = [_l2n(x) * (GDN_HEAD_DIM ** -0.5) for x in qcs]
    k = [_l2n(x) for x in kcs]
    beta = [_sigmoid(_lane_pick(bg, lane, first_head + i)) for i in hs]
    g = [-jnp.exp(_lane_pick(alog_row, lane, first_head + i)) *
         _softplus(_lane_pick(bg, lane, n_heads + first_head + i) + _lane_pick(dtb_row, lane, first_head + i)) for i in hs]
    g_row = [jnp.sum(jnp.where(r == c, x, 0.0), axis=0, keepdims=True) for x in g]
    dec_col = [jnp.sum(jnp.where(r >= c, x, 0.0), axis=1, keepdims=True) for x in g_row]
    dec_row = [jnp.sum(jnp.where(r <= c, x, 0.0), axis=0, keepdims=True) for x in g]
    gamma = [jnp.exp(jnp.where(r >= c, dc - dr, -1e30)) for dc, dr in zip(dec_col, dec_row)]
    kb = [x * b for x, b in zip(k, beta)]
    a = [jnp.where(r > c, mm_nt(x, y) * gm, 0.0) for x, y, gm in zip(kb, k, gamma)]
    tinv = _inv_unit_lower(a)
    e_col = [jnp.exp(x) for x in dec_col]
    u = [mm_nn(t, v * b) for t, v, b in zip(tinv, vcs, beta)]
    w = [mm_nn(t, x * e) for t, x, e in zip(tinv, kb, e_col)]
    qk = [mm_nt(x, y) * gm for x, y, gm in zip(q, k, gamma)]
    total = [jnp.sum(x, axis=0, keepdims=True) for x in g]
    v_new = [x - mm_nn(y, s) for x, y, s in zip(u, w, states)]
    o = [mm_nn(x * e, s) + mm_nn(y, v) for x, e, s, y, v in zip(q, e_col, states, qk, v_new)]
    new_states = [s * jnp.exp(tt) + mm_tn(x * jnp.exp(tt - dc), v) for s, tt, x, dc, v in zip(states, total, k, dec_col, v_new)]
    ys = [_rms(x, gnorm) * _silu(z) for x, z in zip(o, zs)]
    return ys, new_states


GDN_HEADS_FWD = 16
GDN_HEADS_BWD = 16


def _gdn_fwd(qkvc, proj, alog_row, dtb_row, gnorm, *, d, z_off, bg_off, name, hb=GDN_HEADS_FWD):
    t = qkvc.shape[1]
    nh = d // GDN_HEAD_DIM
    hb = min(hb, nh)
    wb = hb * GDN_HEAD_DIM
    nc = t // CHUNK
    ng = nh // hb

    def body(qkv_ref, z_ref, bg_ref, al_ref, dt_ref, gn_ref, y_ref, sin_ref, s_scr):
        n, hg = pl.program_id(0), pl.program_id(1)

        @pl.when(n == 0)
        def _():
            s_scr[hg] = jnp.zeros((hb, GDN_HEAD_DIM, GDN_HEAD_DIM), F32)

        states = [s_scr[hg, i] for i in range(hb)]
        bg, al, dt, gn = bg_ref[...], al_ref[...], dt_ref[...], gn_ref[...]
        sls = [slice(i * GDN_HEAD_DIM, (i + 1) * GDN_HEAD_DIM) for i in range(hb)]
        ys, new_states = _gdn_chunk([qkv_ref[0, :, sl] for sl in sls], [qkv_ref[1, :, sl] for sl in sls], [qkv_ref[2, :, sl] for sl in sls],
                                    [z_ref[:, sl] for sl in sls], bg, al, dt, gn, states, hg * hb, nh)
        for i in range(hb):
            sin_ref[0, i] = states[i]
            y_ref[:, sls[i]] = ys[i].astype(y_ref.dtype)
            s_scr[hg, i] = new_states[i]

    row = lambda n, hg: (0, 0)
    return pl.pallas_call(
        body, name=name, grid=(nc, ng),
        in_specs=[pl.BlockSpec((3, CHUNK, wb), lambda n, hg: (0, n, hg)),
                  pl.BlockSpec((CHUNK, wb), lambda n, hg: (n, z_off // wb + hg)),
                  pl.BlockSpec((CHUNK, LANES), lambda n, hg: (n, bg_off // LANES)),
                  pl.BlockSpec((1, LANES), row), pl.BlockSpec((1, LANES), row), pl.BlockSpec((1, LANES), row)],
        out_specs=[pl.BlockSpec((CHUNK, wb), lambda n, hg: (n, hg)),
                   pl.BlockSpec((1, hb, GDN_HEAD_DIM, GDN_HEAD_DIM), lambda n, hg: (n, hg, 0, 0))],
        out_shape=[jax.ShapeDtypeStruct((t, d), BF16), jax.ShapeDtypeStruct((nc, nh, GDN_HEAD_DIM, GDN_HEAD_DIM), F32)],
        scratch_shapes=[pltpu.VMEM((ng, hb, GDN_HEAD_DIM, GDN_HEAD_DIM), F32)],
        compiler_params=_params(("arbitrary", "arbitrary")),
    )(qkvc, proj, proj, alog_row, dtb_row, gnorm)


def _gdn_bwd(qkvc, proj, alog_row, dtb_row, gnorm, states, dy, *, d, z_off, bg_off, name, hb=GDN_HEADS_BWD):
    t = qkvc.shape[1]
    nh = d // GDN_HEAD_DIM
    hb = min(hb, nh)
    wb = hb * GDN_HEAD_DIM
    nc = t // CHUNK
    ng = nh // hb

    def body(qkv_ref, z_ref, bg_ref, al_ref, dt_ref, gn_ref, sin_ref, dy_ref, dqkv_ref, dz_ref, dbg_ref, dal_ref, ddt_ref, dgn_ref,
             ds_scr):
        s, hg = pl.program_id(0), pl.program_id(1)

        @pl.when(s == 0)
        def _():
            ds_scr[hg] = jnp.zeros((hb, GDN_HEAD_DIM, GDN_HEAD_DIM), F32)

        @pl.when(hg == 0)
        def _():
            dbg_ref[...] = jnp.zeros_like(dbg_ref)

        @pl.when((s == 0) & (hg == 0))
        def _():
            dal_ref[...] = jnp.zeros_like(dal_ref)
            ddt_ref[...] = jnp.zeros_like(ddt_ref)
            dgn_ref[...] = jnp.zeros_like(dgn_ref)

        dstates = [ds_scr[hg, i] for i in range(hb)]
        bg, al, dt, gn = bg_ref[...], al_ref[...], dt_ref[...], gn_ref[...]
        sls = [slice(i * GDN_HEAD_DIM, (i + 1) * GDN_HEAD_DIM) for i in range(hb)]
        fn = functools.partial(_gdn_chunk, first_head=hg * hb, n_heads=nh)
        _, vjp = jax.vjp(fn, [qkv_ref[0, :, sl] for sl in sls], [qkv_ref[1, :, sl] for sl in sls], [qkv_ref[2, :, sl] for sl in sls],
                         [z_ref[:, sl] for sl in sls], bg, al, dt, gn, [sin_ref[0, i] for i in range(hb)])
        dq, dk, dv, dz, dbg, dal, ddt, dgn, dst = vjp(([dy_ref[:, sl] for sl in sls], dstates))
        for i in range(hb):
            dqkv_ref[0, :, sls[i]] = dq[i]
            dqkv_ref[1, :, sls[i]] = dk[i]
            dqkv_ref[2, :, sls[i]] = dv[i]
            dz_ref[:, sls[i]] = dz[i].astype(dz_ref.dtype)
            ds_scr[hg, i] = dst[i]
        dbg_ref[...] += dbg
        dal_ref[...] += dal
        ddt_ref[...] += ddt
        dgn_ref[...] += dgn

    def ch(s):
        return nc - 1 - s

    row = lambda s, hg: (0, 0)
    return pl.pallas_call(
        body, name=name, grid=(nc, ng),
        in_specs=[pl.BlockSpec((3, CHUNK, wb), lambda s, hg: (0, ch(s), hg)),
                  pl.BlockSpec((CHUNK, wb), lambda s, hg: (ch(s), z_off // wb + hg)),
                  pl.BlockSpec((CHUNK, LANES), lambda s, hg: (ch(s), bg_off // LANES)),
                  pl.BlockSpec((1, LANES), row), pl.BlockSpec((1, LANES), row), pl.BlockSpec((1, LANES), row),
                  pl.BlockSpec((1, hb, GDN_HEAD_DIM, GDN_HEAD_DIM), lambda s, hg: (ch(s), hg, 0, 0)),
                  pl.BlockSpec((CHUNK, wb), lambda s, hg: (ch(s), hg))],
        out_specs=[pl.BlockSpec((3, CHUNK, wb), lambda s, hg: (0, ch(s), hg)),
                   pl.BlockSpec((CHUNK, wb), lambda s, hg: (ch(s), hg)),
                   pl.BlockSpec((CHUNK, LANES), lambda s, hg: (ch(s), 0)),
                   pl.BlockSpec((1, LANES), row), pl.BlockSpec((1, LANES), row), pl.BlockSpec((1, LANES), row)],
        out_shape=[jax.ShapeDtypeStruct((3, t, d), F32), jax.ShapeDtypeStruct((t, d), BF16), jax.ShapeDtypeStruct((t, LANES), F32),
                   jax.ShapeDtypeStruct((1, LANES), F32), jax.ShapeDtypeStruct((1, LANES), F32), jax.ShapeDtypeStruct((1, LANES), F32)],
        scratch_shapes=[pltpu.VMEM((ng, hb, GDN_HEAD_DIM, GDN_HEAD_DIM), F32)],
        compiler_params=_params(("arbitrary", "arbitrary")),
    )(qkvc, proj, proj, alog_row, dtb_row, gnorm, states, dy)


@jax.custom_vjp
def _swap_halves(x):
    return pltpu.roll(x, SWA_HEAD_DIM, 1)


_swap_halves.defvjp(lambda x: (pltpu.roll(x, SWA_HEAD_DIM, 1), None), lambda _, g: (pltpu.roll(g, SWA_HEAD_DIM, 1),))

SWA_PAIR_Q = 2 * GQA_GROUP * SWA_HEAD_DIM


def _swa_block(q, kp, kc, vp, vc, sink_row, slope_row, keep_prev, pair):
    kb = jnp.concatenate([kp, kc], axis=0)
    vb = jnp.concatenate([vp, vc], axis=0)
    lane = lax.broadcasted_iota(jnp.int32, (1, LANES), 1)
    low = lane < SWA_HEAD_DIM
    high = jnp.logical_not(low)
    qi = lax.broadcasted_iota(jnp.int32, (WINDOW, 2 * WINDOW), 0)
    sj = lax.broadcasted_iota(jnp.int32, (WINDOW, 2 * WINDOW), 1)
    dist = qi + WINDOW - sj
    valid = (dist >= 0) & (dist < WINDOW) & ((sj >= WINDOW) | (keep_prev > 0.5))
    distf = dist.astype(F32)
    kk, vv = [], []
    for mine in (low, high):
        x = jnp.where(mine, kb, 0.0)
        kk.append(x + _swap_halves(x))
        y = jnp.where(mine, vb, 0.0)
        vv.append(y + _swap_halves(y))
    hl = range(2 * GQA_GROUP)
    half = [low if h % 2 == 0 else high for h in hl]
    slope = [_lane_pick(slope_row, lane, pair * (2 * GQA_GROUP) + h) for h in hl]
    sink = [_lane_pick(sink_row, lane, pair * (2 * GQA_GROUP) + h) for h in hl]
    qm = [jnp.where(half[h], q[:, (h // 2) * LANES:(h // 2 + 1) * LANES], 0.0) for h in hl]
    sc = [mm_nt(qm[h], kk[h // GQA_GROUP]) * (SWA_HEAD_DIM ** -0.5) for h in hl]
    sc = [jnp.where(valid, sc[h] - slope[h] * distf, -1e30) for h in hl]
    m = [lax.stop_gradient(jnp.maximum(jnp.max(sc[h], axis=-1, keepdims=True), sink[h])) for h in hl]
    p = [jnp.exp(sc[h] - m[h]) for h in hl]
    probs = [p[h] / (jnp.sum(p[h], axis=-1, keepdims=True) + jnp.exp(sink[h] - m[h])) for h in hl]
    od = [jnp.where(half[h], mm_nn(probs[h], vv[h // GQA_GROUP]), 0.0) for h in hl]
    return jnp.concatenate([od[2 * i] + od[2 * i + 1] for i in range(GQA_GROUP)], axis=1)


def _swa_specs(t, q_off, k_off, v_off, order):
    nb = t // WINDOW

    def blk(s):
        return order(s, nb)

    return nb, [pl.BlockSpec((WINDOW, SWA_PAIR_Q), lambda p, s: (blk(s), q_off // SWA_PAIR_Q + p)),
                pl.BlockSpec((WINDOW, LANES), lambda p, s: (jnp.maximum(blk(s) - 1, 0), k_off // LANES + p)),
                pl.BlockSpec((WINDOW, LANES), lambda p, s: (blk(s), k_off // LANES + p)),
                pl.BlockSpec((WINDOW, LANES), lambda p, s: (jnp.maximum(blk(s) - 1, 0), v_off // LANES + p)),
                pl.BlockSpec((WINDOW, LANES), lambda p, s: (blk(s), v_off // LANES + p)),
                pl.BlockSpec((1, LANES), lambda p, s: (0, 0)), pl.BlockSpec((1, LANES), lambda p, s: (0, 0))]


def _swa_fwd(proj, sink_row, slope_row, *, d, q_off, k_off, v_off, name):
    t = proj.shape[0]
    n_pairs = d // SWA_PAIR_Q
    nb, in_specs = _swa_specs(t, q_off, k_off, v_off, lambda s, nb: s)

    def body(q_ref, kp_ref, kc_ref, vp_ref, vc_ref, sink_ref, slope_ref, o_ref):
        keep = (pl.program_id(1) > 0).astype(F32)
        o = _swa_block(q_ref[...], kp_ref[...], kc_ref[...], vp_ref[...], vc_ref[...], sink_ref[...], slope_ref[...], keep,
                       pl.program_id(0))
        o_ref[...] = o.astype(o_ref.dtype)

    return pl.pallas_call(
        body, name=name, grid=(n_pairs, nb), in_specs=in_specs,
        out_specs=pl.BlockSpec((WINDOW, SWA_PAIR_Q), lambda p, s: (s, p)),
        out_shape=jax.ShapeDtypeStruct((t, d), BF16),
        compiler_params=_params(("parallel", "arbitrary")),
    )(proj, proj, proj, proj, proj, sink_row, slope_row)


def _swa_bwd(proj, sink_row, slope_row, do, *, d, q_off, k_off, v_off, name):
    t = proj.shape[0]
    n_pairs = d // SWA_PAIR_Q
    nb, in_specs = _swa_specs(t, q_off, k_off, v_off, lambda s, nb: nb - 1 - s)

    def body(q_ref, kp_ref, kc_ref, vp_ref, vc_ref, sink_ref, slope_ref, do_ref, dq_ref, dk_ref, dv_ref, dsink_ref, ck_ref, cv_ref):
        p, s = pl.program_id(0), pl.program_id(1)
        keep = (s < nb - 1).astype(F32)
        fn = functools.partial(_swa_block, slope_row=slope_ref[...], keep_prev=keep, pair=p)
        _, vjp = jax.vjp(fn, q_ref[...], kp_ref[...], kc_ref[...], vp_ref[...], vc_ref[...], sink_ref[...])
        dq, dkp, dkc, dvp, dvc, dsink = vjp(do_ref[...])

        @pl.when(s == 0)
        def _():
            ck_ref[...] = jnp.zeros_like(ck_ref)
            cv_ref[...] = jnp.zeros_like(cv_ref)

        @pl.when((s == 0) & (p == 0))
        def _():
            dsink_ref[...] = jnp.zeros_like(dsink_ref)

        dq_ref[...] = dq.astype(dq_ref.dtype)
        dk_ref[...] = (dkc + ck_ref[...]).astype(dk_ref.dtype)
        dv_ref[...] = (dvc + cv_ref[...]).astype(dv_ref.dtype)
        ck_ref[...] = dkp
        cv_ref[...] = dvp
        dsink_ref[...] += dsink

    in_specs = in_specs + [pl.BlockSpec((WINDOW, SWA_PAIR_Q), lambda p, s: (nb - 1 - s, p))]
    kv_w = d // GQA_GROUP
    return pl.pallas_call(
        body, name=name, grid=(n_pairs, nb), in_specs=in_specs,
        out_specs=[pl.BlockSpec((WINDOW, SWA_PAIR_Q), lambda p, s: (nb - 1 - s, p)),
                   pl.BlockSpec((WINDOW, LANES), lambda p, s: (nb - 1 - s, p)),
                   pl.BlockSpec((WINDOW, LANES), lambda p, s: (nb - 1 - s, p)),
                   pl.BlockSpec((1, LANES), lambda p, s: (0, 0))],
        out_shape=[jax.ShapeDtypeStruct((t, d), BF16), jax.ShapeDtypeStruct((t, kv_w), BF16), jax.ShapeDtypeStruct((t, kv_w), BF16),
                   jax.ShapeDtypeStruct((1, LANES), F32)],
        scratch_shapes=[pltpu.VMEM((WINDOW, LANES), F32), pltpu.VMEM((WINDOW, LANES), F32)],
        compiler_params=_params(("arbitrary", "arbitrary")),
    )(proj, proj, proj, proj, proj, sink_row, slope_row, do)


def _layout(d):
    kv = d // GQA_GROUP
    return dict(z=3 * d, q=4 * d, gate=5 * d, k=7 * d, v=7 * d + kv, bg=7 * d + 2 * kv, width=7 * d + 2 * kv + LANES)


def _pack_w_in(w, d):
    nh = d // GDN_HEAD_DIM
    kv = d // GQA_GROUP
    o = 4 * d + 2 * nh
    parts = [w[..., :4 * d], w[..., o:o + d], w[..., o + d + 2 * kv:o + 3 * d + 2 * kv], w[..., o + d:o + d + 2 * kv],
             w[..., 4 * d:o], jnp.zeros(w.shape[:-1] + (LANES - 2 * nh,), w.dtype)]
    return jnp.concatenate(parts, axis=-1)


def _unpack_w_in(wp, d):
    nh = d // GDN_HEAD_DIM
    kv = d // GQA_GROUP
    lay = _layout(d)
    parts = [wp[..., :4 * d], wp[..., lay["bg"]:lay["bg"] + 2 * nh], wp[..., lay["q"]:lay["q"] + d],
             wp[..., lay["k"]:lay["k"] + 2 * kv], wp[..., lay["gate"]:lay["gate"] + 2 * d]]
    return jnp.concatenate(parts, axis=-1)


def _pad_row(v):
    return jnp.pad(v.astype(F32), (0, LANES - v.shape[0]))[None, :]


def _alibi_row(d):
    nq = d // SWA_HEAD_DIM
    return _pad_row(2.0 ** (-8.0 * jnp.arange(1, nq + 1, dtype=F32) / nq))


def _layer_fwd(x, p, tag, late=None):
    t, d = x.shape
    lay = _layout(d)
    tn = 1152 if lay["width"] % 1152 == 0 else 1024
    h1 = _rmsnorm_fwd(x, p["norm1_g"], tag + "rms1")
    proj = _matmul(h1, p["w_in"], name=tag + "mm_in", tn_cap=tn)
    qkvc = _conv_fwd(proj, p["conv_w"], d, tag + "conv")
    gdn_o, states = _gdn_fwd(qkvc, proj, p["a_log"], p["dt_bias"], p["gdn_norm_g"], d=d, z_off=lay["z"], bg_off=lay["bg"],
                             name=tag + "gdn")
    swa_o = _swa_fwd(proj, p["attn_sinks"], p["alibi"], d=d, q_off=lay["q"], k_off=lay["k"], v_off=lay["v"], name=tag + "swa")
    if late is not None:
        p = dict(p, **late(swa_o))
    y_gdn = _matmul(gdn_o, p["w_branch_gdn"], name=tag + "mm_bg")
    y_swa = _matmul(swa_o, p["w_branch_swa"], name=tag + "mm_bs")
    mix = _merge_fwd(y_gdn, y_swa, proj, lay["gate"], tag + "merge")
    x1 = _matmul(mix, p["w_out"], add=x, name=tag + "mm_out")
    h2 = _rmsnorm_fwd(x1, p["norm2_g"], tag + "rms2")
    up = _matmul(h2, p["w_ff_up"], name=tag + "mm_up", b_split=N_CHIPS)
    act = _relu2_fwd(up, tag + "relu2")
    x2 = _matmul(act, p["w_ff_down"], add=x1, name=tag + "mm_down")
    return x2, dict(x=x, h1=h1, proj=proj, qkvc=qkvc, states=states, gdn_o=gdn_o, swa_o=swa_o, y_gdn=y_gdn, y_swa=y_swa, mix=mix,
                    x1=x1, h2=h2, up=up, act=act)


def _layer_bwd(dx2, p, s, tag, mid=None):
    t, d = dx2.shape
    lay = _layout(d)
    tn = 1152 if lay["width"] % 1152 == 0 else 1024
    nh = d // GDN_HEAD_DIM
    g = {}
    dact = _matmul(dx2, p["w_ff_down"], tb=True, name=tag + "mm_dact", tm_cap=512)
    g["w_ff_down"] = _matmul(s["act"], dx2, ta=True, out_dtype=BF16, name=tag + "mm_dwdown", tk_cap=1024)
    dup = _relu2_bwd(s["up"], dact, tag + "relu2b")
    g["w_ff_up"] = _matmul(s["h2"], dup, ta=True, out_dtype=BF16, name=tag + "mm_dwup", out_split=N_CHIPS)
    dh2 = _matmul(dup, p["w_ff_up"], tb=True, name=tag + "mm_dh2", b_split=N_CHIPS)
    dx1, g["norm2_g"] = _rmsnorm_bwd(s["x1"], p["norm2_g"], dh2, dx2, tag + "rms2b")
    dmix = _matmul(dx1, p["w_out"], tb=True, name=tag + "mm_dmix", tm_cap=512)
    g["w_out"] = _matmul(s["mix"], dx1, ta=True, out_dtype=BF16, name=tag + "mm_dwout", tk_cap=1024)
    dyg, dys, dgl = _merge_bwd(s["y_gdn"], s["y_swa"], s["proj"], lay["gate"], dmix, tag + "mergeb")
    g["w_branch_gdn"] = _matmul(s["gdn_o"], dyg, ta=True, out_dtype=BF16, name=tag + "mm_dwbg")
    g["w_branch_swa"] = _matmul(s["swa_o"], dys, ta=True, out_dtype=BF16, name=tag + "mm_dwbs")
    dgdn_o = _matmul(dyg, p["w_branch_gdn"], tb=True, name=tag + "mm_dgdn")
    dswa_o = _matmul(dys, p["w_branch_swa"], tb=True, name=tag + "mm_dswa")
    if mid is not None:
        token = mid(g, dswa_o)
        p = dict(p, attn_sinks=p["attn_sinks"] + token[:1], a_log=p["a_log"] + token[:1])
    dq_s, dk_s, dv_s, dsink = _swa_bwd(s["proj"], p["attn_sinks"], p["alibi"], dswa_o, d=d, q_off=lay["q"], k_off=lay["k"],
                                       v_off=lay["v"], name=tag + "swab")
    dqkvc, dz, dbg, dal, ddt, dgn = _gdn_bwd(s["qkvc"], s["proj"], p["a_log"], p["dt_bias"], p["gdn_norm_g"], s["states"], dgdn_o,
                                             d=d, z_off=lay["z"], bg_off=lay["bg"], name=tag + "gdnb")
    dqkv, g["conv_w"] = _conv_bwd(s["proj"], p["conv_w"], dqkvc, d, tag + "convb")
    dproj = jnp.concatenate([dqkv, dz, dq_s, dgl, dk_s, dv_s, lax.reduce_precision(dbg, 8, 7).astype(BF16)], axis=1)
    g["w_in"] = _matmul(s["h1"], dproj, ta=True, out_dtype=BF16, name=tag + "mm_dwin", tn_cap=tn)
    dh1 = _matmul(dproj, p["w_in"], tb=True, name=tag + "mm_dh1", tk_cap=tn)
    dx, g["norm1_g"] = _rmsnorm_bwd(s["x"], p["norm1_g"], dh1, dx1, tag + "rms1b")
    g["a_log"], g["dt_bias"], g["gdn_norm_g"], g["attn_sinks"] = dal[0, :nh], ddt[0, :nh], dgn[0], dsink[0, :d // SWA_HEAD_DIM]
    return dx, g


MESH = pl.DeviceIdType.MESH
HBM_SPEC = pl.BlockSpec(memory_space=pl.ANY)


def _place():
    x, y, c = lax.axis_index("x"), lax.axis_index("y"), lax.axis_index("c")
    return x, y, c, 2 * x + y


def _flip(x, y, k):
    px, py = x ^ (k >> 1), y ^ (k & 1)
    return px, py, 2 * px + py


def _cast_into_slot(w, layer, pos, name):
    _, r, cols = w.shape
    tm = _pick(r, max(16, (1 << 19) // cols // 16 * 16), 16)

    def body(x_ref, y_ref, w_ref, o_ref):
        o_ref[...] = w_ref[...].astype(o_ref.dtype)

    return pl.pallas_call(
        body, name=name,
        grid_spec=pltpu.PrefetchScalarGridSpec(
            num_scalar_prefetch=2, grid=(r // tm,),
            in_specs=[pl.BlockSpec((None, tm, cols), lambda i, xr, yr: (layer, i, 0))],
            out_specs=pl.BlockSpec((None, tm, cols), lambda i, xr, yr: (2 * xr[0] + yr[0], i, 0))),
        out_shape=jax.ShapeDtypeStruct((N_CHIPS, r, cols), BF16),
        compiler_params=_params(("parallel",)),
    )(pos[0], pos[1], w)


SEM_SPEC = pl.BlockSpec(memory_space=pltpu.SEMAPHORE)
SPLIT_COPY = pltpu.CompilerParams(has_side_effects=pltpu.SideEffectType.DATAFLOW_SIDE_EFFECTING)
TOKEN = jax.ShapeDtypeStruct((8, LANES), F32)
TOKEN_SPEC = pl.BlockSpec(memory_space=pltpu.VMEM)


def _gather_start(bufs, name):
    n = len(bufs)

    def body(*refs):
        outs, sems, token = refs[n:2 * n], refs[2 * n:8 * n], refs[8 * n]
        x, y, c, ci = _place()
        for a in range(n):
            hr = bufs[a].shape[1] // 2
            mine = outs[a].at[ci, pl.ds(c * hr, hr)]
            for k in (1, 2, 3):
                px, py, _ = _flip(x, y, k)
                pltpu.make_async_remote_copy(src_ref=mine, dst_ref=mine, send_sem=sems[3 * a + k - 1], recv_sem=sems[3 * n + 3 * a + k - 1],
                                             device_id=(px, py, c), device_id_type=MESH).start()
        token[...] = jnp.zeros_like(token)

    res = pl.pallas_call(
        body, name=name, in_specs=[HBM_SPEC] * n, out_specs=[HBM_SPEC] * n + [SEM_SPEC] * (6 * n) + [TOKEN_SPEC],
        out_shape=[pltpu.HBM(b.shape, b.dtype) for b in bufs] + [pltpu.SemaphoreType.DMA(())] * (6 * n) + [TOKEN],
        input_output_aliases={a: a for a in range(n)}, compiler_params=SPLIT_COPY,
    )(*[pltpu.with_memory_space_constraint(b, pltpu.HBM) for b in bufs])
    return res[:n], res[n:7 * n], res[7 * n]


def _gather_wait(bufs, sems, after, name):
    n = len(bufs)

    def body(*refs):
        sems, outs = refs[n:7 * n], refs[7 * n + 1:8 * n + 1]
        x, y, c, ci = _place()
        for a in range(n):
            hr = bufs[a].shape[1] // 2
            mine = outs[a].at[ci, pl.ds(c * hr, hr)]
            for k in (1, 2, 3):
                px, py, pj = _flip(x, y, k)
                landed = outs[a].at[pj, pl.ds(c * hr, hr)]
                cp = pltpu.make_async_remote_copy(src_ref=mine, dst_ref=landed, send_sem=sems[3 * a + k - 1], recv_sem=sems[3 * n + 3 * a + k - 1],
                                                  device_id=(px, py, c), device_id_type=MESH)
                cp.wait_send()
                cp.wait_recv()

    return pl.pallas_call(
        body, name=name, in_specs=[HBM_SPEC] * n + [SEM_SPEC] * (6 * n) + [HBM_SPEC], out_specs=[HBM_SPEC] * n,
        out_shape=[pltpu.HBM(b.shape, b.dtype) for b in bufs],
        input_output_aliases={a: a for a in range(n)}, compiler_params=SPLIT_COPY,
    )(*bufs, *sems, after)


def _gather_forward(bufs, name):
    n = len(bufs)

    def body(*refs):
        outs = refs[n:2 * n]
        send_sems, recv_sems = refs[2 * n:]
        x, y, c, _ = _place()
        waits = []
        for a in range(n):
            hr = bufs[a].shape[1] // 2
            for k in (1, 2, 3):
                _, _, pj = _flip(x, y, k)
                landed = outs[a].at[pj, pl.ds(c * hr, hr)]
                fw = pltpu.make_async_remote_copy(src_ref=landed, dst_ref=landed, send_sem=send_sems.at[a, k - 1], recv_sem=recv_sems.at[a, k - 1],
                                                  device_id=(x, y, 1 - c), device_id_type=MESH)
                fw.start()
                waits.append(fw.wait_send)
                passed = outs[a].at[pj, pl.ds((1 - c) * hr, hr)]
                waits.append(pltpu.make_async_remote_copy(src_ref=passed, dst_ref=passed, send_sem=send_sems.at[a, k - 1],
                                                          recv_sem=recv_sems.at[a, k - 1], device_id=(x, y, 1 - c),
                                                          device_id_type=MESH).wait_recv)
        for w in waits:
            w()

    return pl.pallas_call(
        body, name=name, in_specs=[HBM_SPEC] * n, out_specs=[HBM_SPEC] * n,
        out_shape=[jax.ShapeDtypeStruct(b.shape, b.dtype) for b in bufs],
        input_output_aliases={a: a for a in range(n)},
        scratch_shapes=[pltpu.SemaphoreType.DMA((n, 3))] * 2,
    )(*bufs)


def _swap_with_sibling(gs, name):
    n = len(gs)

    def body(*refs):
        ins, outs = refs[:n], refs[n:2 * n]
        send_sems, recv_sems = refs[2 * n:]
        x, y, c, _ = _place()
        cps = []
        for a in range(n):
            hr = gs[a].shape[1] // 2
            cp = pltpu.make_async_remote_copy(src_ref=ins[a].at[:, pl.ds((1 - c) * hr, hr)], dst_ref=outs[a], send_sem=send_sems.at[a],
                                              recv_sem=recv_sems.at[a], device_id=(x, y, 1 - c), device_id_type=MESH)
            cp.start()
            cps.append(cp)
        for cp in cps:
            cp.wait()

    return pl.pallas_call(
        body, name=name, in_specs=[HBM_SPEC] * n, out_specs=[HBM_SPEC] * n,
        out_shape=[jax.ShapeDtypeStruct((g.shape[0], g.shape[1] // 2, g.shape[2]), g.dtype) for g in gs],
        scratch_shapes=[pltpu.SemaphoreType.DMA((n,))] * 2,
    )(*gs)


def _scatter_start(hs, name):
    n = len(hs)

    def body(*refs):
        srcs, lands, sems, token = refs[n:2 * n], refs[2 * n:3 * n], refs[3 * n:9 * n], refs[9 * n]
        x, y, c, ci = _place()
        for a in range(n):
            for k in (1, 2, 3):
                px, py, pj = _flip(x, y, k)
                pltpu.make_async_remote_copy(src_ref=srcs[a].at[pj], dst_ref=lands[a].at[ci], send_sem=sems[3 * a + k - 1],
                                             recv_sem=sems[3 * n + 3 * a + k - 1], device_id=(px, py, c), device_id_type=MESH).start()
        token[...] = jnp.zeros_like(token)

    res = pl.pallas_call(
        body, name=name, in_specs=[HBM_SPEC] * n, out_specs=[HBM_SPEC] * (2 * n) + [SEM_SPEC] * (6 * n) + [TOKEN_SPEC],
        out_shape=[pltpu.HBM(h.shape, h.dtype) for h in hs] * 2 + [pltpu.SemaphoreType.DMA(())] * (6 * n) + [TOKEN],
        input_output_aliases={a: a for a in range(n)}, compiler_params=SPLIT_COPY,
    )(*[pltpu.with_memory_space_constraint(h, pltpu.HBM) for h in hs])
    return res[:n], res[n:2 * n], res[2 * n:8 * n], res[8 * n]


def _scatter_wait(hs, lands, sems, after, name):
    n = len(hs)

    def body(*refs):
        sems, srcs, lands_o = refs[2 * n:8 * n], refs[8 * n + 1:9 * n + 1], refs[9 * n + 1:10 * n + 1]
        x, y, c, ci = _place()
        for a in range(n):
            for k in (1, 2, 3):
                px, py, pj = _flip(x, y, k)
                cp = pltpu.make_async_remote_copy(src_ref=srcs[a].at[pj], dst_ref=lands_o[a].at[pj], send_sem=sems[3 * a + k - 1],
                                                  recv_sem=sems[3 * n + 3 * a + k - 1], device_id=(px, py, c), device_id_type=MESH)
                cp.wait_send()
                cp.wait_recv()

    res = pl.pallas_call(
        body, name=name, in_specs=[HBM_SPEC] * (2 * n) + [SEM_SPEC] * (6 * n) + [HBM_SPEC], out_specs=[HBM_SPEC] * (2 * n),
        out_shape=[pltpu.HBM(h.shape, h.dtype) for h in hs] * 2,
        input_output_aliases={a: a for a in range(2 * n)}, compiler_params=SPLIT_COPY,
    )(*hs, *lands, *sems, after)
    return res[:n], res[n:]


def _share_with_sibling(bufs, name):
    n = len(bufs)

    def body(*refs):
        outs = refs[n:2 * n]
        send_sems, recv_sems = refs[2 * n:]
        x, y, c, _ = _place()
        waits = []
        for a in range(n):
            mine = outs[a].at[c]
            cp = pltpu.make_async_remote_copy(src_ref=mine, dst_ref=mine, send_sem=send_sems.at[a], recv_sem=recv_sems.at[a],
                                              device_id=(x, y, 1 - c), device_id_type=MESH)
            cp.start()
            waits.append(cp.wait_send)
            got = outs[a].at[1 - c]
            waits.append(pltpu.make_async_remote_copy(src_ref=got, dst_ref=got, send_sem=send_sems.at[a], recv_sem=recv_sems.at[a],
                                                      device_id=(x, y, 1 - c), device_id_type=MESH).wait_recv)
        for w in waits:
            w()

    return pl.pallas_call(
        body, name=name, in_specs=[HBM_SPEC] * n, out_specs=[HBM_SPEC] * n,
        out_shape=[jax.ShapeDtypeStruct(b.shape, b.dtype) for b in bufs],
        input_output_aliases={a: a for a in range(n)},
        scratch_shapes=[pltpu.SemaphoreType.DMA((n,))] * 2,
    )(*bufs)


def _add_sibling_half(g, got, core, name):
    nc, r, cols = g.shape
    hr = r // 2
    tm = _pick(hr, 256, 16)

    def body(core_ref, g_ref, o_ref, s_ref):
        s_ref[...] = (g_ref[...].astype(F32) + o_ref[...].astype(F32)).astype(s_ref.dtype)

    return pl.pallas_call(
        body, name=name,
        grid_spec=pltpu.PrefetchScalarGridSpec(
            num_scalar_prefetch=1, grid=(nc, hr // tm),
            in_specs=[pl.BlockSpec((None, None, tm, cols), lambda j, i, cr: (j, cr[0], i, 0)),
                      pl.BlockSpec((None, tm, cols), lambda j, i, cr: (j, i, 0))],
            out_specs=pl.BlockSpec((None, tm, cols), lambda j, i, cr: (j, i, 0))),
        out_shape=jax.ShapeDtypeStruct((nc, hr, cols), g.dtype),
        compiler_params=_params(("parallel", "parallel")),
    )(core, g.reshape(nc, 2, hr, cols), got)


def _sum_chips(own, parts, pos, name):
    nc, r, cols = parts.shape
    tm = _pick(r, 256, 16)

    def body(x_ref, y_ref, c_ref, own_ref, p_ref, o_ref):
        chip = 2 * x_ref[0] + y_ref[0]
        acc = own_ref[...].astype(F32)
        for k in range(1, nc):
            acc = acc + p_ref[chip ^ k].astype(F32)
        o_ref[...] = acc

    return pl.pallas_call(
        body, name=name,
        grid_spec=pltpu.PrefetchScalarGridSpec(
            num_scalar_prefetch=3, grid=(r // tm,),
            in_specs=[pl.BlockSpec((None, tm, cols), lambda i, xr, yr, cr: (2 * xr[0] + yr[0], i, 0)),
                      pl.BlockSpec((nc, tm, cols), lambda i, xr, yr, cr: (0, i, 0))],
            out_specs=pl.BlockSpec((None, tm, cols), lambda i, xr, yr, cr: (cr[0], i, 0))),
        out_shape=jax.ShapeDtypeStruct((2, r, cols), F32),
        compiler_params=_params(("parallel",)),
    )(*pos, own, parts)


def _reduce_scatter_start(gs, pos, tag):
    got = _swap_with_sibling(gs, tag + "rs_swap")
    hs = [_add_sibling_half(g, o, pos[2], tag + "rs_add%d" % i) for i, (g, o) in enumerate(zip(gs, got))]
    hs, lands, sems, token = _scatter_start(hs, tag + "rs_scatter_start")
    return (hs, lands, sems), token


def _reduce_scatter_finish(pending, after, pos, tag):
    hs, lands, sems = pending
    hs, parts = _scatter_wait(hs, lands, sems, after, tag + "rs_scatter_wait")
    rs = [_sum_chips(h, p, pos, tag + "rs_sum%d" % i) for i, (h, p) in enumerate(zip(hs, parts))]
    both = _share_with_sibling(rs, tag + "rs_share")
    return [b.reshape(2 * b.shape[1], b.shape[2]) for b in both]


def _allreduce_small(v, name):
    rows = v.shape[0]

    def body(v_ref, o_ref, buf, send_sems, recv_sems, local_sem):
        x, y, c, _ = _place()
        me, sibling = (x, y, c), (x, y, 1 - c)
        chips = [_flip(x, y, k)[:2] for k in (1, 2, 3)]

        def slot(px, py, pc):
            return buf.at[4 * px + 2 * py + pc]

        def copy(k, block, to, src=None):
            return pltpu.make_async_remote_copy(src_ref=slot(*block) if src is None else src, dst_ref=slot(*block), send_sem=send_sems.at[k],
                                                recv_sem=recv_sems.at[k], device_id=to, device_id_type=MESH)

        mine = pltpu.make_async_copy(v_ref, slot(*me), local_sem)
        mine.start()
        first = [copy(0, me, sibling, src=v_ref)] + [copy(1 + j, me, (*chip, c), src=v_ref) for j, chip in enumerate(chips)]
        for cp in first:
            cp.start()
        passed = [copy(4 + j, (*chip, c), sibling) for j, chip in enumerate(chips)]
        for j, chip in enumerate(chips):
            copy(1 + j, (*chip, c), me).wait_recv()
            passed[j].start()
        copy(0, sibling, me).wait_recv()
        for j, chip in enumerate(chips):
            copy(4 + j, (*chip, 1 - c), me).wait_recv()
        for cp in first + passed:
            cp.wait_send()
        mine.wait()
        acc = buf[0]
        for i in range(1, 2 * N_CHIPS):
            acc = acc + buf[i]
        o_ref[...] = acc

    vm = pl.BlockSpec(memory_space=pltpu.VMEM)
    return pl.pallas_call(
        body, name=name, in_specs=[vm], out_specs=vm, out_shape=jax.ShapeDtypeStruct(v.shape, F32),
        scratch_shapes=[pltpu.VMEM((2 * N_CHIPS, rows, LANES), F32), pltpu.SemaphoreType.DMA((7,)), pltpu.SemaphoreType.DMA((7,)),
                        pltpu.SemaphoreType.DMA],
        compiler_params=pltpu.CompilerParams(vmem_limit_bytes=VMEM_LIMIT),
    )(v)


def _adamw(w, g, m, v, name):
    r, cols = w.shape
    tm = _pick(r, max(8, (1 << 18) // max(cols, 1) // 8 * 8), 8)

    def body(w_ref, g_ref, m_ref, v_ref, d_ref, nm_ref, nv_ref):
        gg = g_ref[...]
        nm = ADAM_B1 * m_ref[...] + (1.0 - ADAM_B1) * gg
        nv = ADAM_B2 * v_ref[...] + (1.0 - ADAM_B2) * jnp.square(gg)
        m_hat = nm / (1.0 - ADAM_B1 ** ADAM_STEP)
        v_hat = nv / (1.0 - ADAM_B2 ** ADAM_STEP)
        d_ref[...] = -ADAM_LR * (m_hat / (jnp.sqrt(v_hat) + ADAM_EPS) + ADAM_WD * w_ref[...])
        nm_ref[...] = nm
        nv_ref[...] = nv

    spec = pl.BlockSpec((tm, cols), lambda i: (i, 0))
    return pl.pallas_call(
        body, name=name, grid=(r // tm,), in_specs=[spec] * 4, out_specs=[spec] * 3,
        out_shape=[jax.ShapeDtypeStruct((r, cols), F32)] * 3, compiler_params=_params(("parallel",)),
    )(w, g, m, v)


def _adamw_layer(w, g, m, v, layer, prev, name):
    depth, r, cols = w.shape
    tm = _pick(r, max(8, (1 << 18) // max(cols, 1) // 8 * 8), 8)

    def body(*refs):
        w_ref, g_ref, m_ref, v_ref = refs[:4]
        go_ref, d_ref, nm_ref, nv_ref = refs[-4:]
        gg = g_ref[...]
        nm = ADAM_B1 * m_ref[...] + (1.0 - ADAM_B1) * gg
        nv = ADAM_B2 * v_ref[...] + (1.0 - ADAM_B2) * jnp.square(gg)
        m_hat = nm / (1.0 - ADAM_B1 ** ADAM_STEP)
        v_hat = nv / (1.0 - ADAM_B2 ** ADAM_STEP)
        go_ref[...] = gg
        d_ref[...] = -ADAM_LR * (m_hat / (jnp.sqrt(v_hat) + ADAM_EPS) + ADAM_WD * w_ref[...])
        nm_ref[...] = nm
        nv_ref[...] = nv

    lspec = pl.BlockSpec((None, tm, cols), lambda i: (layer, i, 0))
    gspec = pl.BlockSpec((tm, cols), lambda i: (i, 0))
    extra = [] if prev is None else list(prev)
    return pl.pallas_call(
        body, name=name, grid=(r // tm,), in_specs=[lspec, gspec, lspec, lspec] + [HBM_SPEC] * len(extra), out_specs=[lspec] * 4,
        out_shape=[jax.ShapeDtypeStruct((depth, r, cols), F32)] * 4,
        input_output_aliases={4 + j: j for j in range(len(extra))}, compiler_params=_params(("parallel",)),
    )(w, g, m, v, *extra)


def _adamw_nd(w, g, m, v, name):
    shape = w.shape
    two = (1, shape[0]) if len(shape) == 1 else (int(np.prod(shape[:-1])), shape[-1])
    outs = _adamw(w.reshape(two), g.reshape(two), m.reshape(two), v.reshape(two), name)
    return [o.reshape(shape) for o in outs]


WEIGHTS = ("norm1_g", "w_in", "conv_w", "a_log", "dt_bias", "gdn_norm_g", "attn_sinks", "w_branch_gdn", "w_branch_swa", "w_out",
           "norm2_g", "w_ff_up", "w_ff_down", "final_norm_g")
MATRICES = ("w_in", "w_branch_gdn", "w_branch_swa", "w_out", "w_ff_up", "w_ff_down")


def _to_rows(vec):
    n = vec.shape[0]
    rows = -(-n // (8 * LANES)) * 8
    return jnp.pad(vec, (0, rows * LANES - n)).reshape(rows, LANES)


def kernel(x, norm1_g, w_in, conv_w, a_log, dt_bias, gdn_norm_g, attn_sinks, w_branch_gdn, w_branch_swa, w_out, norm2_g, w_ff_up, w_ff_down, final_norm_g, loss_target, m_norm1_g, m_w_in, m_conv_w, m_a_log, m_dt_bias, m_gdn_norm_g, m_attn_sinks, m_w_branch_gdn, m_w_branch_swa, m_w_out, m_norm2_g, m_w_ff_up, m_w_ff_down, m_final_norm_g, v_norm1_g, v_w_in, v_conv_w, v_a_log, v_dt_bias, v_gdn_norm_g, v_attn_sinks, v_w_branch_gdn, v_w_branch_swa, v_w_out, v_norm2_g, v_w_ff_up, v_w_ff_down, v_final_norm_g):
    w = dict(norm1_g=norm1_g, w_in=w_in, conv_w=conv_w, a_log=a_log, dt_bias=dt_bias, gdn_norm_g=gdn_norm_g, attn_sinks=attn_sinks,
             w_branch_gdn=w_branch_gdn, w_branch_swa=w_branch_swa, w_out=w_out, norm2_g=norm2_g, w_ff_up=w_ff_up, w_ff_down=w_ff_down,
             final_norm_g=final_norm_g)
    mom = dict(norm1_g=m_norm1_g, w_in=m_w_in, conv_w=m_conv_w, a_log=m_a_log, dt_bias=m_dt_bias, gdn_norm_g=m_gdn_norm_g,
               attn_sinks=m_attn_sinks, w_branch_gdn=m_w_branch_gdn, w_branch_swa=m_w_branch_swa, w_out=m_w_out, norm2_g=m_norm2_g,
               w_ff_up=m_w_ff_up, w_ff_down=m_w_ff_down, final_norm_g=m_final_norm_g)
    var = dict(norm1_g=v_norm1_g, w_in=v_w_in, conv_w=v_conv_w, a_log=v_a_log, dt_bias=v_dt_bias, gdn_norm_g=v_gdn_norm_g,
               attn_sinks=v_attn_sinks, w_branch_gdn=v_w_branch_gdn, w_branch_swa=v_w_branch_swa, w_out=v_w_out, norm2_g=v_norm2_g,
               w_ff_up=v_w_ff_up, w_ff_down=v_w_ff_down, final_norm_g=v_final_norm_g)
    depth, d = norm1_g.shape
    xs, target = x[0], loss_target[0]
    core = lax.axis_index("c")
    chip = 2 * lax.axis_index("x") + lax.axis_index("y")
    pos = tuple(jnp.reshape(lax.axis_index(a), (1,)).astype(jnp.int32) for a in ("x", "y", "c"))

    cw = conv_w.shape[-1]
    placed = lax.dynamic_update_slice(jnp.zeros((depth, CONV_K, N_CHIPS * cw), F32), conv_w, (0, 0, chip * cw))
    placed = placed * (core == 0).astype(F32)
    conv_full = _allreduce_small(_to_rows(placed.reshape(-1)), "gather_conv_w")
    conv_full = conv_full.reshape(-1)[:depth * CONV_K * N_CHIPS * cw].reshape(depth, CONV_K, N_CHIPS * cw)

    alibi = _alibi_row(d)
    first_group, late_group = MATRICES[:1], MATRICES[1:]
    bufs = [{n: _cast_into_slot(w[n], l, pos, "l%d_cast_%s" % (l, n)) for n in MATRICES} for l in range(depth)]

    def start_gather(l, names, after, tag):
        group, _ = lax.optimization_barrier(([bufs[l][n] for n in names], after))
        return _gather_start(group, "l%d_gather_start_%s" % (l, tag))

    def finish_gather(l, names, started, after, tag):
        got = _gather_wait(started[0], started[1], after, "l%d_gather_wait_%s" % (l, tag))
        full = dict(zip(names, _gather_forward(got, "l%d_gather_forward_%s" % (l, tag))))
        out = {}
        if "w_in" in full:
            out["w_in"] = _pack_w_in(jnp.transpose(full["w_in"], (1, 0, 2)).reshape(d, -1), d)
        for n in ("w_branch_gdn", "w_branch_swa", "w_out", "w_ff_down"):
            if n in full:
                out[n] = full[n].reshape(-1, d)
        if "w_ff_up" in full:
            out["w_ff_up"] = full["w_ff_up"]
        return out

    def small_params(l):
        return dict(norm1_g=norm1_g[l][None], norm2_g=norm2_g[l][None], conv_w=conv_full[l], a_log=_pad_row(a_log[l]),
                    dt_bias=_pad_row(dt_bias[l]), gdn_norm_g=gdn_norm_g[l][None], attn_sinks=_pad_row(attn_sinks[l]), alibi=alibi)

    st_a = start_gather(0, first_group, conv_full, "a")
    st_b = start_gather(0, late_group, st_a[2], "b")
    layers = [dict(small_params(0), **finish_gather(0, first_group, st_a, st_b[2], "a"))]
    h = xs
    saved = []
    for l in range(depth):
        p = layers[l]
        late = None
        if l == 0:
            def late(after, st_b=st_b):
                rest = finish_gather(0, late_group, st_b, after, "b")
                layers[0].update(rest)
                return rest
        if l + 1 < depth:
            nxt = start_gather(l + 1, MATRICES, p["w_in"], "all")
            p = dict(p, norm1_g=p["norm1_g"] + nxt[2][:1, :1])
        h, s = _layer_fwd(h, p, "l%d_" % l, late)
        saved.append(s)
        if l + 1 < depth:
            layers.append(dict(small_params(l + 1), **finish_gather(l + 1, MATRICES, nxt, h, "all")))
    dh, d_final, loss_row = _loss_head(h, final_norm_g[None], target, "loss_head")

    grads = {n: [None] * depth for n in ("norm1_g", "norm2_g", "a_log", "dt_bias", "gdn_norm_g", "attn_sinks", "conv_w")}
    updated = {n: None for n in MATRICES}
    small = ("norm1_g", "norm2_g", "a_log", "dt_bias", "gdn_norm_g", "attn_sinks", "conv_w")
    state = {"pending": None, "todo": None}

    def adamw_todo():
        if state["todo"] is not None:
            l, names, sums = state["todo"]
            for n, r in zip(names, sums):
                updated[n] = _adamw_layer(w[n], r, mom[n], var[n], l, updated[n], "l%d_adamw_%s" % (l, n))
            state["todo"] = None

    def finish_scatter(after):
        adamw_todo()
        if state["pending"] is not None:
            l, names, pending, tag = state["pending"]
            state["todo"] = (l, names, _reduce_scatter_finish(pending, after, pos, "l%d_%s_" % (l, tag)))
            state["pending"] = None

    def start_scatter(l, names, mats, tag):
        pending, token = _reduce_scatter_start(mats, pos, "l%d_%s_" % (l, tag))
        state["pending"] = (l, names, pending, tag)
        return token

    def stacked(g, n):
        return g[n] if n == "w_ff_up" else g[n].reshape(N_CHIPS, -1, g[n].shape[-1])

    for l in reversed(range(depth)):
        def mid(g, after, l=l):
            finish_scatter(after)
            return start_scatter(l, late_group, [stacked(g, n) for n in late_group], "b")

        dh, g = _layer_bwd(dh, layers[l], saved[l], "l%d_" % l, mid)
        for n in grads:
            grads[n][l] = g[n].reshape(-1)
        finish_scatter(dh)
        g_in = _unpack_w_in(g["w_in"], d)
        mats = [jnp.transpose(g_in.reshape(d, N_CHIPS, -1), (1, 0, 2))]
        if l == 0:
            pieces = [jnp.stack(grads[n]).reshape(-1) for n in small] + [d_final.reshape(-1), loss_row[0, :1]]
            sizes = [p.shape[0] for p in pieces]
            packed = _allreduce_small(_to_rows(jnp.concatenate(pieces)), "reduce_small").reshape(-1)
            mats, _ = lax.optimization_barrier((mats, packed))
        token = start_scatter(l, first_group, mats, "a")
        if l > 0:
            dh = dh + token[0, 0]

    offs = np.concatenate([[0], np.cumsum(sizes)])
    red = {n: packed[offs[i]:offs[i + 1]] for i, n in enumerate(small + ("final_norm_g", "loss"))}
    loss = red["loss"][0]

    grad_out = {}
    for n in ("norm1_g", "norm2_g", "a_log", "dt_bias", "gdn_norm_g", "attn_sinks"):
        grad_out[n] = red[n].reshape(w[n].shape)
    grad_out["final_norm_g"] = red["final_norm_g"]
    conv_g = red["conv_w"].reshape(depth, CONV_K, N_CHIPS * cw)
    grad_out["conv_w"] = lax.dynamic_slice(conv_g, (0, 0, chip * cw), (depth, CONV_K, cw))

    delta, new_m, new_v = {}, {}, {}
    for n in grad_out:
        delta[n], new_m[n], new_v[n] = _adamw_nd(w[n], grad_out[n], mom[n], var[n], "adamw_" + n)
    adamw_todo()
    finish_scatter(updated["w_ff_down"][0])
    adamw_todo()
    for n in MATRICES:
        grad_out[n], delta[n], new_m[n], new_v[n] = updated[n]
    return (loss, dh[None], *[grad_out[n] for n in WEIGHTS], *[delta[n] for n in WEIGHTS], *[new_m[n] for n in WEIGHTS],
            *[new_v[n] for n in WEIGHTS])
```

```python
import functools

import jax
import jax.numpy as jnp
import numpy as np
from jax import lax
from jax.experimental import pallas as pl
from jax.experimental.pallas import tpu as pltpu

F32 = jnp.float32
BF16 = jnp.bfloat16

GDN_HEAD_DIM = 128
CHUNK = 64
SWA_HEAD_DIM = 64
WINDOW = 128
CONV_K = 4
GQA_GROUP = 8
NORM_EPS = 1e-6
N_CHIPS = 4
LANES = 128
CONV_HALO = 8
VMEM_LIMIT = 56 * 1024 * 1024

ADAM_LR = 0.001
ADAM_B1 = 0.9
ADAM_B2 = 0.999
ADAM_EPS = 1e-08
ADAM_WD = 0.01
ADAM_STEP = 10

NN = (((1,), (0,)), ((), ()))
NT = (((1,), (1,)), ((), ()))
TN = (((0,), (0,)), ((), ()))


def _pick(dim, cap, mult=LANES):
    if dim <= cap:
        return dim
    t = (cap // mult) * mult
    while t >= mult:
        if dim % t == 0:
            return t
        t -= mult
    return dim


def _params(sem):
    return pltpu.CompilerParams(dimension_semantics=sem, vmem_limit_bytes=VMEM_LIMIT)


def _bdot(a, b, dn):
    return lax.dot_general(a.astype(BF16), b.astype(BF16), dn, preferred_element_type=F32)


@jax.custom_vjp
def mm_nn(a, b):
    return _bdot(a, b, NN)


@jax.custom_vjp
def mm_nt(a, b):
    return _bdot(a, b, NT)


@jax.custom_vjp
def mm_tn(a, b):
    return _bdot(a, b, TN)


mm_nn.defvjp(lambda a, b: (_bdot(a, b, NN), (a, b)), lambda r, g: (_bdot(g, r[1], NT), _bdot(r[0], g, TN)))
mm_nt.defvjp(lambda a, b: (_bdot(a, b, NT), (a, b)), lambda r, g: (_bdot(g, r[1], NN), _bdot(g, r[0], TN)))
mm_tn.defvjp(lambda a, b: (_bdot(a, b, TN), (a, b)), lambda r, g: (_bdot(r[1], g, NT), _bdot(r[0], g, NN)))


def _hdot(a, b, dn=NN):
    return lax.dot_general(a, b, dn, precision=lax.Precision.HIGHEST, preferred_element_type=F32)


def _sigmoid(x):
    return 1.0 / (1.0 + jnp.exp(-x))


def _silu(x):
    return x * _sigmoid(x)


def _softplus(x):
    return jnp.maximum(x, 0.0) + jnp.log(1.0 + jnp.exp(-jnp.abs(x)))


def _lane_pick(row, lane, idx):
    return jnp.sum(jnp.where(lane == idx, row, 0.0), axis=1, keepdims=True)


def _matmul(a, b, *, ta=False, tb=False, out_dtype=F32, add=None, name, tm_cap=1024, tn_cap=1024, tk_cap=2048, b_split=1,
            out_split=1, epilogue=None, extras=(), also=None):
    m, k = (a.shape[1], a.shape[0]) if ta else a.shape
    b_rows, b_cols = (b.shape[-2], b.shape[-1] * b_split)
    n = b_rows if tb else b_cols
    assert k == (b_cols if tb else b_rows), (a.shape, b.shape, ta, tb)
    tm = _pick(m, tm_cap)
    tn = _pick(n // max(1 if tb else b_split, out_split), tn_cap)
    tk = _pick(k // (b_split if tb else 1), tk_cap)
    nk = k // tk
    dn = (((0 if ta else 1,), (1 if tb else 0,)), ((), ()))

    n_extra = len(extras)

    def body(*refs):
        a_ref, b_ref = refs[:2]
        add_ref = refs[2] if add is not None else None
        first = 2 + (add is not None)
        x_refs = refs[first:first + n_extra]
        o_ref = refs[first + n_extra]
        o2_ref = refs[first + n_extra + 1] if also is not None else None
        acc_ref = refs[-1]
        kk = pl.program_id(2)
        p = lax.dot_general(a_ref[...].astype(BF16), b_ref[...].astype(BF16), dn, preferred_element_type=F32)

        @pl.when(kk == 0)
        def _():
            acc_ref[...] = p

        @pl.when(kk > 0)
        def _():
            acc_ref[...] += p

        @pl.when(kk == nk - 1)
        def _():
            r = acc_ref[...]
            if add_ref is not None:
                r = r + add_ref[...].astype(F32)
            if also is not None:
                o2_ref[...] = also[0](r).astype(o2_ref.dtype)
            if epilogue is not None:
                r = epilogue(r, *[x[...] for x in x_refs])
            o_ref[...] = r.astype(o_ref.dtype)

    a_spec = pl.BlockSpec((tk, tm), lambda i, j, q: (q, i)) if ta else pl.BlockSpec((tm, tk), lambda i, j, q: (i, q))
    if b_split == 1:
        b_spec = pl.BlockSpec((tn, tk), lambda i, j, q: (j, q)) if tb else pl.BlockSpec((tk, tn), lambda i, j, q: (q, j))
    elif tb:
        per_b = k // b_split // tk
        b_spec = pl.BlockSpec((None, tn, tk), lambda i, j, q: (q // per_b, j, q % per_b))
    else:
        per_b = n // b_split // tn
        b_spec = pl.BlockSpec((None, tk, tn), lambda i, j, q: (j // per_b, q, j % per_b))
    add_spec = pl.BlockSpec((tm, tn), lambda i, j, q: (i, j))
    if out_split == 1:
        o_spec, o_shape = add_spec, (m, n)
    else:
        per_o = n // out_split // tn
        o_spec, o_shape = pl.BlockSpec((None, tm, tn), lambda i, j, q: (j // per_o, i, j % per_o)), (out_split, m, n // out_split)
    in_specs = [a_spec, b_spec] + ([add_spec] if add is not None else []) + [add_spec] * n_extra
    args = (a, b) + ((add,) if add is not None else ()) + tuple(extras)
    out_specs, out_shape = o_spec, jax.ShapeDtypeStruct(o_shape, out_dtype)
    if also is not None:
        out_specs, out_shape = [o_spec, o_spec], [out_shape, jax.ShapeDtypeStruct(o_shape, also[1])]
    return pl.pallas_call(
        body, name=name, grid=(m // tm, n // tn, nk), in_specs=in_specs, out_specs=out_specs,
        out_shape=out_shape, scratch_shapes=[pltpu.VMEM((tm, tn), F32)],
        compiler_params=_params(("parallel", "parallel", "arbitrary")),
    )(*args)


def _rows(fn, row_args, full_args, row_outs, acc_outs, *, t, tm, name):
    n_row, n_full, n_ro = len(row_args), len(full_args), len(row_outs)

    def body(*refs):
        ins = [r[...] for r in refs[:n_row + n_full]]
        outs = fn(*ins)
        o_refs = refs[n_row + n_full:]
        for r, v in zip(o_refs[:n_ro], outs[:n_ro]):
            r[...] = v.astype(r.dtype)
        i = pl.program_id(0)
        for r, v in zip(o_refs[n_ro:], outs[n_ro:]):
            @pl.when(i == 0)
            def _(r=r, v=v):
                r[...] = v

            @pl.when(i > 0)
            def _(r=r, v=v):
                r[...] += v

    in_specs = [pl.BlockSpec((tm, w), functools.partial(lambda i, cb: (i, cb), cb=cb)) for (_, w, cb) in row_args]
    in_specs += [pl.BlockSpec(f.shape, lambda i: (0, 0)) for f in full_args]
    out_specs = [pl.BlockSpec((tm, w), lambda i: (i, 0)) for (w, _) in row_outs]
    out_specs += [pl.BlockSpec(s, lambda i: (0, 0)) for s in acc_outs]
    out_shape = [jax.ShapeDtypeStruct((t, w), d) for (w, d) in row_outs]
    out_shape += [jax.ShapeDtypeStruct(s, F32) for s in acc_outs]
    return pl.pallas_call(
        body, name=name, grid=(t // tm,), in_specs=in_specs, out_specs=out_specs, out_shape=out_shape,
        compiler_params=_params(("arbitrary",)),
    )(*[a for (a, _, _) in row_args], *full_args)


def _rms(x, g):
    return x * lax.rsqrt(jnp.mean(x * x, axis=-1, keepdims=True) + NORM_EPS) * g


def _rmsnorm_fwd(x, g, name):
    t, d = x.shape
    (h,) = _rows(lambda xb, gb: (_rms(xb, gb),), [(x, d, 0)], [g], [(d, BF16)], [], t=t, tm=_pick(t, 512, 8), name=name)
    return h


def _rmsnorm_bwd(x, g, dh, dx_in, name):
    t, d = x.shape

    def fn(xb, dhb, dxb, gb):
        _, vjp = jax.vjp(_rms, xb, gb)
        dx, dg = vjp(dhb)
        return dxb + dx, dg

    return _rows(fn, [(x, d, 0), (dh, d, 0), (dx_in, d, 0)], [g], [(d, F32)], [(1, d)], t=t, tm=_pick(t, 256, 8), name=name)


def _merge(yg, ys, lg, ls):
    return _sigmoid(lg) * yg + _sigmoid(ls) * ys


def _merge_fwd(y_gdn, y_swa, proj, gate_off, name):
    t, d = y_gdn.shape
    cb = gate_off // d
    (mix,) = _rows(lambda a, b, c, e: (_merge(a, b, c, e),), [(y_gdn, d, 0), (y_swa, d, 0), (proj, d, cb), (proj, d, cb + 1)], [],
                   [(d, BF16)], [], t=t, tm=_pick(t, 256, 8), name=name)
    return mix


def _merge_bwd(y_gdn, y_swa, proj, gate_off, dmix, name):
    t, d = y_gdn.shape
    cb = gate_off // d

    def fn(a, b, c, e, g):
        _, vjp = jax.vjp(_merge, a, b, c, e)
        da, db, dc, de = vjp(g)
        return da, db, jnp.concatenate([dc, de], axis=1)

    return _rows(fn, [(y_gdn, d, 0), (y_swa, d, 0), (proj, d, cb), (proj, d, cb + 1), (dmix, d, 0)], [],
                 [(d, BF16), (d, BF16), (2 * d, BF16)], [], t=t, tm=_pick(t, 128, 8), name=name)


def _assemble(pieces, out_dtype, name):
    t = pieces[0].shape[0]
    widths = [p.shape[1] for p in pieces]
    offs = np.concatenate([[0], np.cumsum(widths)])

    def fn(*blocks):
        return (jnp.concatenate([b.astype(out_dtype) for b in blocks], axis=1),)

    (out,) = _rows(fn, [(p, w, 0) for p, w in zip(pieces, widths)], [], [(int(offs[-1]), out_dtype)], [], t=t, tm=_pick(t, 128, 16),
                   name=name)
    return out


def _loss_head(x, g, target, name):
    t, d = x.shape

    def loss_fn(xb, gb, tb):
        err = _rms(xb, gb) - tb
        return 0.5 * jnp.sum(jnp.mean(err * err, axis=-1, keepdims=True), axis=0, keepdims=True)

    def fn(xb, tb, gb):
        lv, vjp = jax.vjp(lambda a, b: loss_fn(a, b, tb), xb, gb)
        dx, dg = vjp(jnp.ones((1, 1), F32))
        return dx, dg, jnp.broadcast_to(lv, (1, LANES))

    return _rows(fn, [(x, d, 0), (target, d, 0)], [g], [(d, F32)], [(1, d), (1, LANES)], t=t, tm=_pick(t, 256, 8), name=name)


def _conv_silu(prev, cur, w, keep_prev):
    tm = cur.shape[0]
    xp = jnp.concatenate([prev * keep_prev, cur], axis=0)
    y = w[0:1, :] * xp[CONV_HALO - 3:CONV_HALO - 3 + tm]
    for j in range(1, CONV_K):
        y = y + w[j:j + 1, :] * xp[CONV_HALO - 3 + j:CONV_HALO - 3 + j + tm]
    return _silu(y)


def _conv_tiles(t, width):
    tm = _pick(t, 512, CONV_HALO)
    tc = _pick(width, 512)
    return tm, tc, t // tm, width // tc


def _conv_fwd(proj, conv_w, width, name):
    t = proj.shape[0]
    tm, tc, nt, ncw = _conv_tiles(t, width)
    hb = tm // CONV_HALO

    def body(prev_ref, cur_ref, w_ref, o_ref):
        keep = (pl.program_id(1) > 0).astype(F32)
        o_ref[0] = _conv_silu(prev_ref[...], cur_ref[...], w_ref[...], keep)

    return pl.pallas_call(
        body, name=name, grid=(3 * ncw, nt),
        in_specs=[pl.BlockSpec((CONV_HALO, tc), lambda j, i: (jnp.maximum(i * hb - 1, 0), j)),
                  pl.BlockSpec((tm, tc), lambda j, i: (i, j)),
                  pl.BlockSpec((CONV_K, tc), lambda j, i: (0, j))],
        out_specs=pl.BlockSpec((1, tm, tc), lambda j, i: (j // ncw, i, j % ncw)),
        out_shape=jax.ShapeDtypeStruct((3, t, width), F32),
        compiler_params=_params(("parallel", "arbitrary")),
    )(proj, proj, conv_w)


def _conv_bwd(proj, conv_w, dout, width, name):
    t = proj.shape[0]
    tm, tc, nt, ncw = _conv_tiles(t, width)
    hb = tm // CONV_HALO

    def body(prev_ref, cur_ref, w_ref, g_ref, dx_ref, dw_ref, carry_ref):
        s = pl.program_id(1)
        keep = (s < nt - 1).astype(F32)
        _, vjp = jax.vjp(lambda p, c, w: _conv_silu(p, c, w, keep), prev_ref[...], cur_ref[...], w_ref[...])
        dprev, dcur, dw = vjp(g_ref[0])

        @pl.when(s == 0)
        def _():
            carry_ref[...] = jnp.zeros_like(carry_ref)
            dw_ref[...] = dw

        @pl.when(s > 0)
        def _():
            dw_ref[...] += dw

        tail = jnp.concatenate([jnp.zeros((tm - CONV_HALO, tc), F32), carry_ref[...]], axis=0)
        dx_ref[...] = (dcur + tail).astype(dx_ref.dtype)
        carry_ref[...] = dprev

    def row(s):
        return nt - 1 - s

    return pl.pallas_call(
        body, name=name, grid=(3 * ncw, nt),
        in_specs=[pl.BlockSpec((CONV_HALO, tc), lambda j, s: (jnp.maximum(row(s) * hb - 1, 0), j)),
                  pl.BlockSpec((tm, tc), lambda j, s: (row(s), j)),
                  pl.BlockSpec((CONV_K, tc), lambda j, s: (0, j)),
                  pl.BlockSpec((1, tm, tc), lambda j, s: (j // ncw, row(s), j % ncw))],
        out_specs=[pl.BlockSpec((tm, tc), lambda j, s: (row(s), j)),
                   pl.BlockSpec((CONV_K, tc), lambda j, s: (0, j))],
        out_shape=[jax.ShapeDtypeStruct((t, 3 * width), BF16), jax.ShapeDtypeStruct((CONV_K, 3 * width), F32)],
        scratch_shapes=[pltpu.VMEM((CONV_HALO, tc), F32)],
        compiler_params=_params(("parallel", "arbitrary")),
    )(proj, proj, conv_w, dout)


def _inv_unit_lower_raw(mats):
    n = mats[0].shape[0]
    r = lax.broadcasted_iota(jnp.int32, (n, n), 0)
    c = lax.broadcasted_iota(jnp.int32, (n, n), 1)
    eye = (r == c).astype(F32)
    same = jnp.right_shift(r, 4) == jnp.right_shift(c, 4)
    dg = [jnp.where(same, a, 0.0) for a in mats]
    lo = [a - d for a, d in zip(mats, dg)]
    p = [eye - d for d in dg]
    q = dg
    for _ in range(3):
        q = [_hdot(x, x) for x in q]
        p = [_hdot(x, eye + y) for x, y in zip(p, q)]
    nm = [_hdot(x, y) for x, y in zip(p, lo)]
    n2 = [_hdot(x, x) for x in nm]
    left = [_hdot(eye - x, eye + y) for x, y in zip(nm, n2)]
    return [_hdot(x, y) for x, y in zip(left, p)]


@jax.custom_vjp
def _inv_unit_lower(mats):
    return _inv_unit_lower_raw(mats)


def _inv_fwd(mats):
    t = _inv_unit_lower_raw(mats)
    return t, t


def _inv_bwd(ts, gs):
    x = [_hdot(t, g, TN) for t, g in zip(ts, gs)]
    return ([-_hdot(a, t, NT) for a, t in zip(x, ts)],)


_inv_unit_lower.defvjp(_inv_fwd, _inv_bwd)


def _l2n(x):
    return x * lax.rsqrt(jnp.sum(x * x, axis=-1, keepdims=True) + NORM_EPS)


def _gdn_chunk(qcs, kcs, vcs, zs, bg, alog_row, dtb_row, gnorm, states, first_head, n_heads):
    nb = len(qcs)
    hs = range(nb)
    cs = qcs[0].shape[0]
    lane = lax.broadcasted_iota(jnp.int32, (1, LANES), 1)
    r = lax.broadcasted_iota(jnp.int32, (cs, cs), 0)
    c = lax.broadcasted_iota(jnp.int32, (cs, cs), 1)
    q = [_l2n(x) * (GDN_HEAD_DIM ** -0.5) for x in qcs]
    k = [_l2n(x) for x in kcs]
    beta = [_sigmoid(_lane_pick(bg, lane, first_head + i)) for i in hs]
    g = [-jnp.exp(_lane_pick(alog_row, lane, first_head + i)) *
         _softplus(_lane_pick(bg, lane, n_heads + first_head + i) + _lane_pick(dtb_row, lane, first_head + i)) for i in hs]
    g_row = [jnp.sum(jnp.where(r == c, x, 0.0), axis=0, keepdims=True) for x in g]
    dec_col = [jnp.sum(jnp.where(r >= c, x, 0.0), axis=1, keepdims=True) for x in g_row]
    dec_row = [jnp.sum(jnp.where(r <= c, x, 0.0), axis=0, keepdims=True) for x in g]
    gamma = [jnp.exp(jnp.where(r >= c, dc - dr, -1e30)) for dc, dr in zip(dec_col, dec_row)]
    kb = [x * b for x, b in zip(k, beta)]
    a = [jnp.where(r > c, mm_nt(x, y) * gm, 0.0) for x, y, gm in zip(kb, k, gamma)]
    tinv = _inv_unit_lower(a)
    e_col = [jnp.exp(x) for x in dec_col]
    u = [mm_nn(t, v * b) for t, v, b in zip(tinv, vcs, beta)]
    w = [mm_nn(t, x * e) for t, x, e in zip(tinv, kb, e_col)]
    qk = [mm_nt(x, y) * gm for x, y, gm in zip(q, k, gamma)]
    total = [jnp.sum(x, axis=0, keepdims=True) for x in g]
    v_new = [x - mm_nn(y, s) for x, y, s in zip(u, w, states)]
    o = [mm_nn(x * e, s) + mm_nn(y, v) for x, e, s, y, v in zip(q, e_col, states, qk, v_new)]
    new_states = [s * jnp.exp(tt) + mm_tn(x * jnp.exp(tt - dc), v) for s, tt, x, dc, v in zip(states, total, k, dec_col, v_new)]
    ys = [_rms(x, gnorm) * _silu(z) for x, z in zip(o, zs)]
    return ys, new_states


GDN_HEADS_FWD = 16
GDN_HEADS_BWD = 16


def _gdn_fwd(qkvc, proj, alog_row, dtb_row, gnorm, *, d, z_off, bg_off, name, hb=GDN_HEADS_FWD):
    t = qkvc.shape[1]
    nh = d // GDN_HEAD_DIM
    hb = min(hb, nh)
    wb = hb * GDN_HEAD_DIM
    nc = t // CHUNK
    ng = nh // hb

    def body(qkv_ref, z_ref, bg_ref, al_ref, dt_ref, gn_ref, y_ref, sin_ref, s_scr):
        n, hg = pl.program_id(0), pl.program_id(1)

        @pl.when(n == 0)
        def _():
            s_scr[hg] = jnp.zeros((hb, GDN_HEAD_DIM, GDN_HEAD_DIM), F32)

        states = [s_scr[hg, i] for i in range(hb)]
        bg, al, dt, gn = bg_ref[...], al_ref[...], dt_ref[...], gn_ref[...]
        sls = [slice(i * GDN_HEAD_DIM, (i + 1) * GDN_HEAD_DIM) for i in range(hb)]
        ys, new_states = _gdn_chunk([qkv_ref[0, :, sl] for sl in sls], [qkv_ref[1, :, sl] for sl in sls], [qkv_ref[2, :, sl] for sl in sls],
                                    [z_ref[:, sl] for sl in sls], bg, al, dt, gn, states, hg * hb, nh)
        for i in range(hb):
            sin_ref[0, i] = states[i]
            y_ref[:, sls[i]] = ys[i].astype(y_ref.dtype)
            s_scr[hg, i] = new_states[i]

    row = lambda n, hg: (0, 0)
    return pl.pallas_call(
        body, name=name, grid=(nc, ng),
        in_specs=[pl.BlockSpec((3, CHUNK, wb), lambda n, hg: (0, n, hg)),
                  pl.BlockSpec((CHUNK, wb), lambda n, hg: (n, z_off // wb + hg)),
                  pl.BlockSpec((CHUNK, LANES), lambda n, hg: (n, bg_off // LANES)),
                  pl.BlockSpec((1, LANES), row), pl.BlockSpec((1, LANES), row), pl.BlockSpec((1, LANES), row)],
        out_specs=[pl.BlockSpec((CHUNK, wb), lambda n, hg: (n, hg)),
                   pl.BlockSpec((1, hb, GDN_HEAD_DIM, GDN_HEAD_DIM), lambda n, hg: (n, hg, 0, 0))],
        out_shape=[jax.ShapeDtypeStruct((t, d), BF16), jax.ShapeDtypeStruct((nc, nh, GDN_HEAD_DIM, GDN_HEAD_DIM), F32)],
        scratch_shapes=[pltpu.VMEM((ng, hb, GDN_HEAD_DIM, GDN_HEAD_DIM), F32)],
        compiler_params=_params(("arbitrary", "arbitrary")),
    )(qkvc, proj, proj, alog_row, dtb_row, gnorm)


def _gdn_bwd(qkvc, proj, alog_row, dtb_row, gnorm, states, dy, *, d, z_off, bg_off, name, hb=GDN_HEADS_BWD):
    t = qkvc.shape[1]
    nh = d // GDN_HEAD_DIM
    hb = min(hb, nh)
    wb = hb * GDN_HEAD_DIM
    nc = t // CHUNK
    ng = nh // hb

    def body(qkv_ref, z_ref, bg_ref, al_ref, dt_ref, gn_ref, sin_ref, dy_ref, dqkv_ref, dz_ref, dbg_ref, dal_ref, ddt_ref, dgn_ref,
             ds_scr):
        s, hg = pl.program_id(0), pl.program_id(1)

        @pl.when(s == 0)
        def _():
            ds_scr[hg] = jnp.zeros((hb, GDN_HEAD_DIM, GDN_HEAD_DIM), F32)

        @pl.when(hg == 0)
        def _():
            dbg_ref[...] = jnp.zeros_like(dbg_ref)

        @pl.when((s == 0) & (hg == 0))
        def _():
            dal_ref[...] = jnp.zeros_like(dal_ref)
            ddt_ref[...] = jnp.zeros_like(ddt_ref)
            dgn_ref[...] = jnp.zeros_like(dgn_ref)

        dstates = [ds_scr[hg, i] for i in range(hb)]
        bg, al, dt, gn = bg_ref[...], al_ref[...], dt_ref[...], gn_ref[...]
        sls = [slice(i * GDN_HEAD_DIM, (i + 1) * GDN_HEAD_DIM) for i in range(hb)]
        fn = functools.partial(_gdn_chunk, first_head=hg * hb, n_heads=nh)
        _, vjp = jax.vjp(fn, [qkv_ref[0, :, sl] for sl in sls], [qkv_ref[1, :, sl] for sl in sls], [qkv_ref[2, :, sl] for sl in sls],
                         [z_ref[:, sl] for sl in sls], bg, al, dt, gn, [sin_ref[0, i] for i in range(hb)])
        dq, dk, dv, dz, dbg, dal, ddt, dgn, dst = vjp(([dy_ref[:, sl] for sl in sls], dstates))
        for i in range(hb):
            dqkv_ref[0, :, sls[i]] = dq[i]
            dqkv_ref[1, :, sls[i]] = dk[i]
            dqkv_ref[2, :, sls[i]] = dv[i]
            dz_ref[:, sls[i]] = dz[i].astype(dz_ref.dtype)
            ds_scr[hg, i] = dst[i]
        dbg_ref[...] += dbg
        dal_ref[...] += dal
        ddt_ref[...] += ddt
        dgn_ref[...] += dgn

    def ch(s):
        return nc - 1 - s

    row = lambda s, hg: (0, 0)
    return pl.pallas_call(
        body, name=name, grid=(nc, ng),
        in_specs=[pl.BlockSpec((3, CHUNK, wb), lambda s, hg: (0, ch(s), hg)),
                  pl.BlockSpec((CHUNK, wb), lambda s, hg: (ch(s), z_off // wb + hg)),
                  pl.BlockSpec((CHUNK, LANES), lambda s, hg: (ch(s), bg_off // LANES)),
                  pl.BlockSpec((1, LANES), row), pl.BlockSpec((1, LANES), row), pl.BlockSpec((1, LANES), row),
                  pl.BlockSpec((1, hb, GDN_HEAD_DIM, GDN_HEAD_DIM), lambda s, hg: (ch(s), hg, 0, 0)),
                  pl.BlockSpec((CHUNK, wb), lambda s, hg: (ch(s), hg))],
        out_specs=[pl.BlockSpec((3, CHUNK, wb), lambda s, hg: (0, ch(s), hg)),
                   pl.BlockSpec((CHUNK, wb), lambda s, hg: (ch(s), hg)),
                   pl.BlockSpec((CHUNK, LANES), lambda s, hg: (ch(s), 0)),
                   pl.BlockSpec((1, LANES), row), pl.BlockSpec((1, LANES), row), pl.BlockSpec((1, LANES), row)],
        out_shape=[jax.ShapeDtypeStruct((3, t, d), F32), jax.ShapeDtypeStruct((t, d), BF16), jax.ShapeDtypeStruct((t, LANES), F32),
                   jax.ShapeDtypeStruct((1, LANES), F32), jax.ShapeDtypeStruct((1, LANES), F32), jax.ShapeDtypeStruct((1, LANES), F32)],
        scratch_shapes=[pltpu.VMEM((ng, hb, GDN_HEAD_DIM, GDN_HEAD_DIM), F32)],
        compiler_params=_params(("arbitrary", "arbitrary")),
    )(qkvc, proj, proj, alog_row, dtb_row, gnorm, states, dy)


@jax.custom_vjp
def _swap_halves(x):
    return pltpu.roll(x, SWA_HEAD_DIM, 1)


_swap_halves.defvjp(lambda x: (pltpu.roll(x, SWA_HEAD_DIM, 1), None), lambda _, g: (pltpu.roll(g, SWA_HEAD_DIM, 1),))

SWA_PAIR_Q = 2 * GQA_GROUP * SWA_HEAD_DIM


def _swa_block(q, kp, kc, vp, vc, sink_row, slope_row, keep_prev, pair):
    kb = jnp.concatenate([kp, kc], axis=0)
    vb = jnp.concatenate([vp, vc], axis=0)
    lane = lax.broadcasted_iota(jnp.int32, (1, LANES), 1)
    low = lane < SWA_HEAD_DIM
    high = jnp.logical_not(low)
    qi = lax.broadcasted_iota(jnp.int32, (WINDOW, 2 * WINDOW), 0)
    sj = lax.broadcasted_iota(jnp.int32, (WINDOW, 2 * WINDOW), 1)
    dist = qi + WINDOW - sj
    valid = (dist >= 0) & (dist < WINDOW) & ((sj >= WINDOW) | (keep_prev > 0.5))
    distf = dist.astype(F32)
    kk, vv = [], []
    for mine in (low, high):
        x = jnp.where(mine, kb, 0.0)
        kk.append(x + _swap_halves(x))
        y = jnp.where(mine, vb, 0.0)
        vv.append(y + _swap_halves(y))
    hl = range(2 * GQA_GROUP)
    half = [low if h % 2 == 0 else high for h in hl]
    slope = [_lane_pick(slope_row, lane, pair * (2 * GQA_GROUP) + h) for h in hl]
    sink = [_lane_pick(sink_row, lane, pair * (2 * GQA_GROUP) + h) for h in hl]
    qm = [jnp.where(half[h], q[:, (h // 2) * LANES:(h // 2 + 1) * LANES], 0.0) for h in hl]
    sc = [mm_nt(qm[h], kk[h // GQA_GROUP]) * (SWA_HEAD_DIM ** -0.5) for h in hl]
    sc = [jnp.where(valid, sc[h] - slope[h] * distf, -1e30) for h in hl]
    m = [lax.stop_gradient(jnp.maximum(jnp.max(sc[h], axis=-1, keepdims=True), sink[h])) for h in hl]
    p = [jnp.exp(sc[h] - m[h]) for h in hl]
    probs = [p[h] / (jnp.sum(p[h], axis=-1, keepdims=True) + jnp.exp(sink[h] - m[h])) for h in hl]
    od = [jnp.where(half[h], mm_nn(probs[h], vv[h // GQA_GROUP]), 0.0) for h in hl]
    return jnp.concatenate([od[2 * i] + od[2 * i + 1] for i in range(GQA_GROUP)], axis=1)


def _swa_specs(t, q_off, k_off, v_off, order):
    nb = t // WINDOW

    def blk(s):
        return order(s, nb)

    return nb, [pl.BlockSpec((WINDOW, SWA_PAIR_Q), lambda p, s: (blk(s), q_off // SWA_PAIR_Q + p)),
                pl.BlockSpec((WINDOW, LANES), lambda p, s: (jnp.maximum(blk(s) - 1, 0), k_off // LANES + p)),
                pl.BlockSpec((WINDOW, LANES), lambda p, s: (blk(s), k_off // LANES + p)),
                pl.BlockSpec((WINDOW, LANES), lambda p, s: (jnp.maximum(blk(s) - 1, 0), v_off // LANES + p)),
                pl.BlockSpec((WINDOW, LANES), lambda p, s: (blk(s), v_off // LANES + p)),
                pl.BlockSpec((1, LANES), lambda p, s: (0, 0)), pl.BlockSpec((1, LANES), lambda p, s: (0, 0))]


def _swa_fwd(proj, sink_row, slope_row, *, d, q_off, k_off, v_off, name):
    t = proj.shape[0]
    n_pairs = d // SWA_PAIR_Q
    nb, in_specs = _swa_specs(t, q_off, k_off, v_off, lambda s, nb: s)

    def body(q_ref, kp_ref, kc_ref, vp_ref, vc_ref, sink_ref, slope_ref, o_ref):
        keep = (pl.program_id(1) > 0).astype(F32)
        o = _swa_block(q_ref[...], kp_ref[...], kc_ref[...], vp_ref[...], vc_ref[...], sink_ref[...], slope_ref[...], keep,
                       pl.program_id(0))
        o_ref[...] = o.astype(o_ref.dtype)

    return pl.pallas_call(
        body, name=name, grid=(n_pairs, nb), in_specs=in_specs,
        out_specs=pl.BlockSpec((WINDOW, SWA_PAIR_Q), lambda p, s: (s, p)),
        out_shape=jax.ShapeDtypeStruct((t, d), BF16),
        compiler_params=_params(("parallel", "arbitrary")),
    )(proj, proj, proj, proj, proj, sink_row, slope_row)


def _swa_bwd(proj, sink_row, slope_row, do, *, d, q_off, k_off, v_off, name):
    t = proj.shape[0]
    n_pairs = d // SWA_PAIR_Q
    nb, in_specs = _swa_specs(t, q_off, k_off, v_off, lambda s, nb: nb - 1 - s)

    def body(q_ref, kp_ref, kc_ref, vp_ref, vc_ref, sink_ref, slope_ref, do_ref, dq_ref, dk_ref, dv_ref, dsink_ref, ck_ref, cv_ref):
        p, s = pl.program_id(0), pl.program_id(1)
        keep = (s < nb - 1).astype(F32)
        fn = functools.partial(_swa_block, slope_row=slope_ref[...], keep_prev=keep, pair=p)
        _, vjp = jax.vjp(fn, q_ref[...], kp_ref[...], kc_ref[...], vp_ref[...], vc_ref[...], sink_ref[...])
        dq, dkp, dkc, dvp, dvc, dsink = vjp(do_ref[...])

        @pl.when(s == 0)
        def _():
            ck_ref[...] = jnp.zeros_like(ck_ref)
            cv_ref[...] = jnp.zeros_like(cv_ref)

        @pl.when((s == 0) & (p == 0))
        def _():
            dsink_ref[...] = jnp.zeros_like(dsink_ref)

        dq_ref[...] = dq.astype(dq_ref.dtype)
        dk_ref[...] = (dkc + ck_ref[...]).astype(dk_ref.dtype)
        dv_ref[...] = (dvc + cv_ref[...]).astype(dv_ref.dtype)
        ck_ref[...] = dkp
        cv_ref[...] = dvp
        dsink_ref[...] += dsink

    in_specs = in_specs + [pl.BlockSpec((WINDOW, SWA_PAIR_Q), lambda p, s: (nb - 1 - s, p))]
    kv_w = d // GQA_GROUP
    return pl.pallas_call(
        body, name=name, grid=(n_pairs, nb), in_specs=in_specs,
        out_specs=[pl.BlockSpec((WINDOW, SWA_PAIR_Q), lambda p, s: (nb - 1 - s, p)),
                   pl.BlockSpec((WINDOW, LANES), lambda p, s: (nb - 1 - s, p)),
                   pl.BlockSpec((WINDOW, LANES), lambda p, s: (nb - 1 - s, p)),
                   pl.BlockSpec((1, LANES), lambda p, s: (0, 0))],
        out_shape=[jax.ShapeDtypeStruct((t, d), BF16), jax.ShapeDtypeStruct((t, kv_w), BF16), jax.ShapeDtypeStruct((t, kv_w), BF16),
                   jax.ShapeDtypeStruct((1, LANES), F32)],
        scratch_shapes=[pltpu.VMEM((WINDOW, LANES), F32), pltpu.VMEM((WINDOW, LANES), F32)],
        compiler_params=_params(("arbitrary", "arbitrary")),
    )(proj, proj, proj, proj, proj, sink_row, slope_row, do)


def _layout(d):
    kv = d // GQA_GROUP
    return dict(z=3 * d, q=4 * d, gate=5 * d, k=7 * d, v=7 * d + kv, bg=7 * d + 2 * kv, width=7 * d + 2 * kv + LANES)


def _pack_w_in(w, d):
    nh = d // GDN_HEAD_DIM
    kv = d // GQA_GROUP
    o = 4 * d + 2 * nh
    parts = [w[..., :4 * d], w[..., o:o + d], w[..., o + d + 2 * kv:o + 3 * d + 2 * kv], w[..., o + d:o + d + 2 * kv],
             w[..., 4 * d:o], jnp.zeros(w.shape[:-1] + (LANES - 2 * nh,), w.dtype)]
    return jnp.concatenate(parts, axis=-1)


def _unpack_w_in(wp, d):
    nh = d // GDN_HEAD_DIM
    kv = d // GQA_GROUP
    lay = _layout(d)
    parts = [wp[..., :4 * d], wp[..., lay["bg"]:lay["bg"] + 2 * nh], wp[..., lay["q"]:lay["q"] + d],
             wp[..., lay["k"]:lay["k"] + 2 * kv], wp[..., lay["gate"]:lay["gate"] + 2 * d]]
    return jnp.concatenate(parts, axis=-1)


def _pad_row(v):
    return jnp.pad(v.astype(F32), (0, LANES - v.shape[0]))[None, :]


def _alibi_row(d):
    nq = d // SWA_HEAD_DIM
    return _pad_row(2.0 ** (-8.0 * jnp.arange(1, nq + 1, dtype=F32) / nq))


def _layer_fwd(x, p, tag, late=None):
    t, d = x.shape
    lay = _layout(d)
    tn = 1152 if lay["width"] % 1152 == 0 else 1024
    h1 = _rmsnorm_fwd(x, p["norm1_g"], tag + "rms1")
    proj = _matmul(h1, p["w_in"], name=tag + "mm_in", tn_cap=tn)
    qkvc = _conv_fwd(proj, p["conv_w"], d, tag + "conv")
    gdn_o, states = _gdn_fwd(qkvc, proj, p["a_log"], p["dt_bias"], p["gdn_norm_g"], d=d, z_off=lay["z"], bg_off=lay["bg"],
                             name=tag + "gdn")
    swa_o = _swa_fwd(proj, p["attn_sinks"], p["alibi"], d=d, q_off=lay["q"], k_off=lay["k"], v_off=lay["v"], name=tag + "swa")
    if late is not None:
        p = dict(p, **late(swa_o))
    y_gdn = _matmul(gdn_o, p["w_branch_gdn"], name=tag + "mm_bg")
    y_swa = _matmul(swa_o, p["w_branch_swa"], name=tag + "mm_bs")
    mix = _merge_fwd(y_gdn, y_swa, proj, lay["gate"], tag + "merge")
    x1 = _matmul(mix, p["w_out"], add=x, name=tag + "mm_out")
    h2 = _rmsnorm_fwd(x1, p["norm2_g"], tag + "rms2")
    up, act = _matmul(h2, p["w_ff_up"], name=tag + "mm_up", b_split=N_CHIPS, also=(lambda u: jnp.square(jnp.maximum(u, 0.0)), BF16))
    x2 = _matmul(act, p["w_ff_down"], add=x1, name=tag + "mm_down")
    return x2, dict(x=x, h1=h1, proj=proj, qkvc=qkvc, states=states, gdn_o=gdn_o, swa_o=swa_o, y_gdn=y_gdn, y_swa=y_swa, mix=mix,
                    x1=x1, h2=h2, up=up, act=act)


def _layer_bwd(dx2, p, s, tag, mid=None):
    t, d = dx2.shape
    lay = _layout(d)
    tn = 1152 if lay["width"] % 1152 == 0 else 1024
    nh = d // GDN_HEAD_DIM
    g = {}
    dup = _matmul(dx2, p["w_ff_down"], tb=True, out_dtype=BF16, name=tag + "mm_dup", tm_cap=512, extras=(s["up"],),
                  epilogue=lambda r, u: r * 2.0 * jnp.maximum(u, 0.0))
    g["w_ff_down"] = _matmul(s["act"], dx2, ta=True, out_dtype=BF16, name=tag + "mm_dwdown", tk_cap=1024)
    g["w_ff_up"] = _matmul(s["h2"], dup, ta=True, out_dtype=BF16, name=tag + "mm_dwup", out_split=N_CHIPS)
    dh2 = _matmul(dup, p["w_ff_up"], tb=True, name=tag + "mm_dh2", b_split=N_CHIPS)
    dx1, g["norm2_g"] = _rmsnorm_bwd(s["x1"], p["norm2_g"], dh2, dx2, tag + "rms2b")
    dmix = _matmul(dx1, p["w_out"], tb=True, name=tag + "mm_dmix", tm_cap=512)
    g["w_out"] = _matmul(s["mix"], dx1, ta=True, out_dtype=BF16, name=tag + "mm_dwout", tk_cap=1024)
    dyg, dys, dgl = _merge_bwd(s["y_gdn"], s["y_swa"], s["proj"], lay["gate"], dmix, tag + "mergeb")
    g["w_branch_gdn"] = _matmul(s["gdn_o"], dyg, ta=True, out_dtype=BF16, name=tag + "mm_dwbg")
    g["w_branch_swa"] = _matmul(s["swa_o"], dys, ta=True, out_dtype=BF16, name=tag + "mm_dwbs")
    dgdn_o = _matmul(dyg, p["w_branch_gdn"], tb=True, name=tag + "mm_dgdn")
    dswa_o = _matmul(dys, p["w_branch_swa"], tb=True, name=tag + "mm_dswa")
    if mid is not None:
        token = mid(g, dswa_o)
        p = dict(p, attn_sinks=p["attn_sinks"] + token[:1], a_log=p["a_log"] + token[:1])
    dq_s, dk_s, dv_s, dsink = _swa_bwd(s["proj"], p["attn_sinks"], p["alibi"], dswa_o, d=d, q_off=lay["q"], k_off=lay["k"],
                                       v_off=lay["v"], name=tag + "swab")
    dqkvc, dz, dbg, dal, ddt, dgn = _gdn_bwd(s["qkvc"], s["proj"], p["a_log"], p["dt_bias"], p["gdn_norm_g"], s["states"], dgdn_o,
                                             d=d, z_off=lay["z"], bg_off=lay["bg"], name=tag + "gdnb")
    dqkv, g["conv_w"] = _conv_bwd(s["proj"], p["conv_w"], dqkvc, d, tag + "convb")
    dproj = _assemble([dqkv, dz, dq_s, dgl, dk_s, dv_s, dbg], BF16, tag + "dproj")
    g["w_in"] = _matmul(s["h1"], dproj, ta=True, out_dtype=BF16, name=tag + "mm_dwin", tn_cap=tn)
    dh1 = _matmul(dproj, p["w_in"], tb=True, name=tag + "mm_dh1", tk_cap=tn)
    dx, g["norm1_g"] = _rmsnorm_bwd(s["x"], p["norm1_g"], dh1, dx1, tag + "rms1b")
    g["a_log"], g["dt_bias"], g["gdn_norm_g"], g["attn_sinks"] = dal[0, :nh], ddt[0, :nh], dgn[0], dsink[0, :d // SWA_HEAD_DIM]
    return dx, g


MESH = pl.DeviceIdType.MESH
HBM_SPEC = pl.BlockSpec(memory_space=pl.ANY)


def _place():
    x, y, c = lax.axis_index("x"), lax.axis_index("y"), lax.axis_index("c")
    return x, y, c, 2 * x + y


def _flip(x, y, k):
    px, py = x ^ (k >> 1), y ^ (k & 1)
    return px, py, 2 * px + py


def _cast_into_slot(w, layer, pos, name):
    _, r, cols = w.shape
    tm = _pick(r, max(16, (1 << 19) // cols // 16 * 16), 16)

    def body(x_ref, y_ref, w_ref, o_ref):
        o_ref[...] = w_ref[...].astype(o_ref.dtype)

    return pl.pallas_call(
        body, name=name,
        grid_spec=pltpu.PrefetchScalarGridSpec(
            num_scalar_prefetch=2, grid=(r // tm,),
            in_specs=[pl.BlockSpec((None, tm, cols), lambda i, xr, yr: (layer, i, 0))],
            out_specs=pl.BlockSpec((None, tm, cols), lambda i, xr, yr: (2 * xr[0] + yr[0], i, 0))),
        out_shape=jax.ShapeDtypeStruct((N_CHIPS, r, cols), BF16),
        compiler_params=_params(("parallel",)),
    )(pos[0], pos[1], w)


SEM_SPEC = pl.BlockSpec(memory_space=pltpu.SEMAPHORE)
SPLIT_COPY = pltpu.CompilerParams(has_side_effects=pltpu.SideEffectType.DATAFLOW_SIDE_EFFECTING)
TOKEN = jax.ShapeDtypeStruct((8, LANES), F32)
TOKEN_SPEC = pl.BlockSpec(memory_space=pltpu.VMEM)


def _gather_start(bufs, name):
    n = len(bufs)

    def body(*refs):
        outs, sems, token = refs[n:2 * n], refs[2 * n:8 * n], refs[8 * n]
        x, y, c, ci = _place()
        for a in range(n):
            hr = bufs[a].shape[1] // 2
            mine = outs[a].at[ci, pl.ds(c * hr, hr)]
            for k in (1, 2, 3):
                px, py, _ = _flip(x, y, k)
                pltpu.make_async_remote_copy(src_ref=mine, dst_ref=mine, send_sem=sems[3 * a + k - 1], recv_sem=sems[3 * n + 3 * a + k - 1],
                                             device_id=(px, py, c), device_id_type=MESH).start()
        token[...] = jnp.zeros_like(token)

    res = pl.pallas_call(
        body, name=name, in_specs=[HBM_SPEC] * n, out_specs=[HBM_SPEC] * n + [SEM_SPEC] * (6 * n) + [TOKEN_SPEC],
        out_shape=[pltpu.HBM(b.shape, b.dtype) for b in bufs] + [pltpu.SemaphoreType.DMA(())] * (6 * n) + [TOKEN],
        input_output_aliases={a: a for a in range(n)}, compiler_params=SPLIT_COPY,
    )(*[pltpu.with_memory_space_constraint(b, pltpu.HBM) for b in bufs])
    return res[:n], res[n:7 * n], res[7 * n]


def _gather_wait(bufs, sems, after, name):
    n = len(bufs)

    def body(*refs):
        sems, outs = refs[n:7 * n], refs[7 * n + 1:8 * n + 1]
        x, y, c, ci = _place()
        for a in range(n):
            hr = bufs[a].shape[1] // 2
            mine = outs[a].at[ci, pl.ds(c * hr, hr)]
            for k in (1, 2, 3):
                px, py, pj = _flip(x, y, k)
                landed = outs[a].at[pj, pl.ds(c * hr, hr)]
                cp = pltpu.make_async_remote_copy(src_ref=mine, dst_ref=landed, send_sem=sems[3 * a + k - 1], recv_sem=sems[3 * n + 3 * a + k - 1],
                                                  device_id=(px, py, c), device_id_type=MESH)
                cp.wait_send()
                cp.wait_recv()

    return pl.pallas_call(
        body, name=name, in_specs=[HBM_SPEC] * n + [SEM_SPEC] * (6 * n) + [HBM_SPEC], out_specs=[HBM_SPEC] * n,
        out_shape=[pltpu.HBM(b.shape, b.dtype) for b in bufs],
        input_output_aliases={a: a for a in range(n)}, compiler_params=SPLIT_COPY,
    )(*bufs, *sems, after)


def _gather_forward(bufs, name):
    n = len(bufs)

    def body(*refs):
        outs = refs[n:2 * n]
        send_sems, recv_sems = refs[2 * n:]
        x, y, c, _ = _place()
        waits = []
        for a in range(n):
            hr = bufs[a].shape[1] // 2
            for k in (1, 2, 3):
                _, _, pj = _flip(x, y, k)
                landed = outs[a].at[pj, pl.ds(c * hr, hr)]
                fw = pltpu.make_async_remote_copy(src_ref=landed, dst_ref=landed, send_sem=send_sems.at[a, k - 1], recv_sem=recv_sems.at[a, k - 1],
                                                  device_id=(x, y, 1 - c), device_id_type=MESH)
                fw.start()
                waits.append(fw.wait_send)
                passed = outs[a].at[pj, pl.ds((1 - c) * hr, hr)]
                waits.append(pltpu.make_async_remote_copy(src_ref=passed, dst_ref=passed, send_sem=send_sems.at[a, k - 1],
                                                          recv_sem=recv_sems.at[a, k - 1], device_id=(x, y, 1 - c),
                                                          device_id_type=MESH).wait_recv)
        for w in waits:
            w()

    return pl.pallas_call(
        body, name=name, in_specs=[HBM_SPEC] * n, out_specs=[HBM_SPEC] * n,
        out_shape=[jax.ShapeDtypeStruct(b.shape, b.dtype) for b in bufs],
        input_output_aliases={a: a for a in range(n)},
        scratch_shapes=[pltpu.SemaphoreType.DMA((n, 3))] * 2,
    )(*bufs)


def _swap_with_sibling(gs, name):
    n = len(gs)

    def body(*refs):
        ins, outs = refs[:n], refs[n:2 * n]
        send_sems, recv_sems = refs[2 * n:]
        x, y, c, _ = _place()
        cps = []
        for a in range(n):
            hr = gs[a].shape[1] // 2
            cp = pltpu.make_async_remote_copy(src_ref=ins[a].at[:, pl.ds((1 - c) * hr, hr)], dst_ref=outs[a], send_sem=send_sems.at[a],
                                              recv_sem=recv_sems.at[a], device_id=(x, y, 1 - c), device_id_type=MESH)
            cp.start()
            cps.append(cp)
        for cp in cps:
            cp.wait()

    return pl.pallas_call(
        body, name=name, in_specs=[HBM_SPEC] * n, out_specs=[HBM_SPEC] * n,
        out_shape=[jax.ShapeDtypeStruct((g.shape[0], g.shape[1] // 2, g.shape[2]), g.dtype) for g in gs],
        scratch_shapes=[pltpu.SemaphoreType.DMA((n,))] * 2,
    )(*gs)


def _scatter_start(hs, name):
    n = len(hs)

    def body(*refs):
        srcs, lands, sems, token = refs[n:2 * n], refs[2 * n:3 * n], refs[3 * n:9 * n], refs[9 * n]
        x, y, c, ci = _place()
        for a in range(n):
            for k in (1, 2, 3):
                px, py, pj = _flip(x, y, k)
                pltpu.make_async_remote_copy(src_ref=srcs[a].at[pj], dst_ref=lands[a].at[ci], send_sem=sems[3 * a + k - 1],
                                             recv_sem=sems[3 * n + 3 * a + k - 1], device_id=(px, py, c), device_id_type=MESH).start()
        token[...] = jnp.zeros_like(token)

    res = pl.pallas_call(
        body, name=name, in_specs=[HBM_SPEC] * n, out_specs=[HBM_SPEC] * (2 * n) + [SEM_SPEC] * (6 * n) + [TOKEN_SPEC],
        out_shape=[pltpu.HBM(h.shape, h.dtype) for h in hs] * 2 + [pltpu.SemaphoreType.DMA(())] * (6 * n) + [TOKEN],
        input_output_aliases={a: a for a in range(n)}, compiler_params=SPLIT_COPY,
    )(*[pltpu.with_memory_space_constraint(h, pltpu.HBM) for h in hs])
    return res[:n], res[n:2 * n], res[2 * n:8 * n], res[8 * n]


def _scatter_wait(hs, lands, sems, after, name):
    n = len(hs)

    def body(*refs):
        sems, srcs, lands_o = refs[2 * n:8 * n], refs[8 * n + 1:9 * n + 1], refs[9 * n + 1:10 * n + 1]
        x, y, c, ci = _place()
        for a in range(n):
            for k in (1, 2, 3):
                px, py, pj = _flip(x, y, k)
                cp = pltpu.make_async_remote_copy(src_ref=srcs[a].at[pj], dst_ref=lands_o[a].at[pj], send_sem=sems[3 * a + k - 1],
                                                  recv_sem=sems[3 * n + 3 * a + k - 1], device_id=(px, py, c), device_id_type=MESH)
                cp.wait_send()
                cp.wait_recv()

    res = pl.pallas_call(
        body, name=name, in_specs=[HBM_SPEC] * (2 * n) + [SEM_SPEC] * (6 * n) + [HBM_SPEC], out_specs=[HBM_SPEC] * (2 * n),
        out_shape=[pltpu.HBM(h.shape, h.dtype) for h in hs] * 2,
        input_output_aliases={a: a for a in range(2 * n)}, compiler_params=SPLIT_COPY,
    )(*hs, *lands, *sems, after)
    return res[:n], res[n:]


def _share_with_sibling(bufs, name):
    n = len(bufs)

    def body(*refs):
        outs = refs[n:2 * n]
        send_sems, recv_sems = refs[2 * n:]
        x, y, c, _ = _place()
        waits = []
        for a in range(n):
            mine = outs[a].at[c]
            cp = pltpu.make_async_remote_copy(src_ref=mine, dst_ref=mine, send_sem=send_sems.at[a], recv_sem=recv_sems.at[a],
                                              device_id=(x, y, 1 - c), device_id_type=MESH)
            cp.start()
            waits.append(cp.wait_send)
            got = outs[a].at[1 - c]
            waits.append(pltpu.make_async_remote_copy(src_ref=got, dst_ref=got, send_sem=send_sems.at[a], recv_sem=recv_sems.at[a],
                                                      device_id=(x, y, 1 - c), device_id_type=MESH).wait_recv)
        for w in waits:
            w()

    return pl.pallas_call(
        body, name=name, in_specs=[HBM_SPEC] * n, out_specs=[HBM_SPEC] * n,
        out_shape=[jax.ShapeDtypeStruct(b.shape, b.dtype) for b in bufs],
        input_output_aliases={a: a for a in range(n)},
        scratch_shapes=[pltpu.SemaphoreType.DMA((n,))] * 2,
    )(*bufs)


def _add_sibling_half(g, got, core, name):
    nc, r, cols = g.shape
    hr = r // 2
    tm = _pick(hr, 256, 16)

    def body(core_ref, g_ref, o_ref, s_ref):
        s_ref[...] = (g_ref[...].astype(F32) + o_ref[...].astype(F32)).astype(s_ref.dtype)

    return pl.pallas_call(
        body, name=name,
        grid_spec=pltpu.PrefetchScalarGridSpec(
            num_scalar_prefetch=1, grid=(nc, hr // tm),
            in_specs=[pl.BlockSpec((None, None, tm, cols), lambda j, i, cr: (j, cr[0], i, 0)),
                      pl.BlockSpec((None, tm, cols), lambda j, i, cr: (j, i, 0))],
            out_specs=pl.BlockSpec((None, tm, cols), lambda j, i, cr: (j, i, 0))),
        out_shape=jax.ShapeDtypeStruct((nc, hr, cols), g.dtype),
        compiler_params=_params(("parallel", "parallel")),
    )(core, g.reshape(nc, 2, hr, cols), got)


def _sum_chips(own, parts, pos, name):
    nc, r, cols = parts.shape
    tm = _pick(r, 256, 16)

    def body(x_ref, y_ref, c_ref, own_ref, p_ref, o_ref):
        chip = 2 * x_ref[0] + y_ref[0]
        acc = own_ref[...].astype(F32)
        for k in range(1, nc):
            acc = acc + p_ref[chip ^ k].astype(F32)
        o_ref[...] = acc

    return pl.pallas_call(
        body, name=name,
        grid_spec=pltpu.PrefetchScalarGridSpec(
            num_scalar_prefetch=3, grid=(r // tm,),
            in_specs=[pl.BlockSpec((None, tm, cols), lambda i, xr, yr, cr: (2 * xr[0] + yr[0], i, 0)),
                      pl.BlockSpec((nc, tm, cols), lambda i, xr, yr, cr: (0, i, 0))],
            out_specs=pl.BlockSpec((None, tm, cols), lambda i, xr, yr, cr: (cr[0], i, 0))),
        out_shape=jax.ShapeDtypeStruct((2, r, cols), F32),
        compiler_params=_params(("parallel",)),
    )(*pos, own, parts)


def _reduce_scatter_start(gs, pos, tag):
    got = _swap_with_sibling(gs, tag + "rs_swap")
    hs = [_add_sibling_half(g, o, pos[2], tag + "rs_add%d" % i) for i, (g, o) in enumerate(zip(gs, got))]
    hs, lands, sems, token = _scatter_start(hs, tag + "rs_scatter_start")
    return (hs, lands, sems), token


def _reduce_scatter_finish(pending, after, pos, tag):
    hs, lands, sems = pending
    hs, parts = _scatter_wait(hs, lands, sems, after, tag + "rs_scatter_wait")
    rs = [_sum_chips(h, p, pos, tag + "rs_sum%d" % i) for i, (h, p) in enumerate(zip(hs, parts))]
    both = _share_with_sibling(rs, tag + "rs_share")
    return [b.reshape(2 * b.shape[1], b.shape[2]) for b in both]


def _allreduce_small(v, name):
    rows = v.shape[0]

    def body(v_ref, o_ref, buf, send_sems, recv_sems, local_sem):
        x, y, c, _ = _place()
        me, sibling = (x, y, c), (x, y, 1 - c)
        chips = [_flip(x, y, k)[:2] for k in (1, 2, 3)]

        def slot(px, py, pc):
            return buf.at[4 * px + 2 * py + pc]

        def copy(k, block, to, src=None):
            return pltpu.make_async_remote_copy(src_ref=slot(*block) if src is None else src, dst_ref=slot(*block), send_sem=send_sems.at[k],
                                                recv_sem=recv_sems.at[k], device_id=to, device_id_type=MESH)

        mine = pltpu.make_async_copy(v_ref, slot(*me), local_sem)
        mine.start()
        first = [copy(0, me, sibling, src=v_ref)] + [copy(1 + j, me, (*chip, c), src=v_ref) for j, chip in enumerate(chips)]
        for cp in first:
            cp.start()
        passed = [copy(4 + j, (*chip, c), sibling) for j, chip in enumerate(chips)]
        for j, chip in enumerate(chips):
            copy(1 + j, (*chip, c), me).wait_recv()
            passed[j].start()
        copy(0, sibling, me).wait_recv()
        for j, chip in enumerate(chips):
            copy(4 + j, (*chip, 1 - c), me).wait_recv()
        for cp in first + passed:
            cp.wait_send()
        mine.wait()
        acc = buf[0]
        for i in range(1, 2 * N_CHIPS):
            acc = acc + buf[i]
        o_ref[...] = acc

    vm = pl.BlockSpec(memory_space=pltpu.VMEM)
    return pl.pallas_call(
        body, name=name, in_specs=[vm], out_specs=vm, out_shape=jax.ShapeDtypeStruct(v.shape, F32),
        scratch_shapes=[pltpu.VMEM((2 * N_CHIPS, rows, LANES), F32), pltpu.SemaphoreType.DMA((7,)), pltpu.SemaphoreType.DMA((7,)),
                        pltpu.SemaphoreType.DMA],
        compiler_params=pltpu.CompilerParams(vmem_limit_bytes=VMEM_LIMIT),
    )(v)


def _adamw(w, g, m, v, name):
    r, cols = w.shape
    tm = _pick(r, max(8, (1 << 18) // max(cols, 1) // 8 * 8), 8)

    def body(w_ref, g_ref, m_ref, v_ref, d_ref, nm_ref, nv_ref):
        gg = g_ref[...]
        nm = ADAM_B1 * m_ref[...] + (1.0 - ADAM_B1) * gg
        nv = ADAM_B2 * v_ref[...] + (1.0 - ADAM_B2) * jnp.square(gg)
        m_hat = nm / (1.0 - ADAM_B1 ** ADAM_STEP)
        v_hat = nv / (1.0 - ADAM_B2 ** ADAM_STEP)
        d_ref[...] = -ADAM_LR * (m_hat / (jnp.sqrt(v_hat) + ADAM_EPS) + ADAM_WD * w_ref[...])
        nm_ref[...] = nm
        nv_ref[...] = nv

    spec = pl.BlockSpec((tm, cols), lambda i: (i, 0))
    return pl.pallas_call(
        body, name=name, grid=(r // tm,), in_specs=[spec] * 4, out_specs=[spec] * 3,
        out_shape=[jax.ShapeDtypeStruct((r, cols), F32)] * 3, compiler_params=_params(("parallel",)),
    )(w, g, m, v)


def _adamw_layer(w, g, m, v, layer, prev, name):
    depth, r, cols = w.shape
    tm = _pick(r, max(8, (1 << 18) // max(cols, 1) // 8 * 8), 8)

    def body(*refs):
        w_ref, g_ref, m_ref, v_ref = refs[:4]
        go_ref, d_ref, nm_ref, nv_ref = refs[-4:]
        gg = g_ref[...]
        nm = ADAM_B1 * m_ref[...] + (1.0 - ADAM_B1) * gg
        nv = ADAM_B2 * v_ref[...] + (1.0 - ADAM_B2) * jnp.square(gg)
        m_hat = nm / (1.0 - ADAM_B1 ** ADAM_STEP)
        v_hat = nv / (1.0 - ADAM_B2 ** ADAM_STEP)
        go_ref[...] = gg
        d_ref[...] = -ADAM_LR * (m_hat / (jnp.sqrt(v_hat) + ADAM_EPS) + ADAM_WD * w_ref[...])
        nm_ref[...] = nm
        nv_ref[...] = nv

    lspec = pl.BlockSpec((None, tm, cols), lambda i: (layer, i, 0))
    gspec = pl.BlockSpec((tm, cols), lambda i: (i, 0))
    extra = [] if prev is None else list(prev)
    return pl.pallas_call(
        body, name=name, grid=(r // tm,), in_specs=[lspec, gspec, lspec, lspec] + [HBM_SPEC] * len(extra), out_specs=[lspec] * 4,
        out_shape=[jax.ShapeDtypeStruct((depth, r, cols), F32)] * 4,
        input_output_aliases={4 + j: j for j in range(len(extra))}, compiler_params=_params(("parallel",)),
    )(w, g, m, v, *extra)


def _adamw_nd(w, g, m, v, name):
    shape = w.shape
    two = (1, shape[0]) if len(shape) == 1 else (int(np.prod(shape[:-1])), shape[-1])
    outs = _adamw(w.reshape(two), g.reshape(two), m.reshape(two), v.reshape(two), name)
    return [o.reshape(shape) for o in outs]


WEIGHTS = ("norm1_g", "w_in", "conv_w", "a_log", "dt_bias", "gdn_norm_g", "attn_sinks", "w_branch_gdn", "w_branch_swa", "w_out",
           "norm2_g", "w_ff_up", "w_ff_down", "final_norm_g")
MATRICES = ("w_in", "w_branch_gdn", "w_branch_swa", "w_out", "w_ff_up", "w_ff_down")


def _to_rows(vec):
    n = vec.shape[0]
    rows = -(-n // (8 * LANES)) * 8
    return jnp.pad(vec, (0, rows * LANES - n)).reshape(rows, LANES)


def kernel(x, norm1_g, w_in, conv_w, a_log, dt_bias, gdn_norm_g, attn_sinks, w_branch_gdn, w_branch_swa, w_out, norm2_g, w_ff_up, w_ff_down, final_norm_g, loss_target, m_norm1_g, m_w_in, m_conv_w, m_a_log, m_dt_bias, m_gdn_norm_g, m_attn_sinks, m_w_branch_gdn, m_w_branch_swa, m_w_out, m_norm2_g, m_w_ff_up, m_w_ff_down, m_final_norm_g, v_norm1_g, v_w_in, v_conv_w, v_a_log, v_dt_bias, v_gdn_norm_g, v_attn_sinks, v_w_branch_gdn, v_w_branch_swa, v_w_out, v_norm2_g, v_w_ff_up, v_w_ff_down, v_final_norm_g):
    w = dict(norm1_g=norm1_g, w_in=w_in, conv_w=conv_w, a_log=a_log, dt_bias=dt_bias, gdn_norm_g=gdn_norm_g, attn_sinks=attn_sinks,
             w_branch_gdn=w_branch_gdn, w_branch_swa=w_branch_swa, w_out=w_out, norm2_g=norm2_g, w_ff_up=w_ff_up, w_ff_down=w_ff_down,
             final_norm_g=final_norm_g)
    mom = dict(norm1_g=m_norm1_g, w_in=m_w_in, conv_w=m_conv_w, a_log=m_a_log, dt_bias=m_dt_bias, gdn_norm_g=m_gdn_norm_g,
               attn_sinks=m_attn_sinks, w_branch_gdn=m_w_branch_gdn, w_branch_swa=m_w_branch_swa, w_out=m_w_out, norm2_g=m_norm2_g,
               w_ff_up=m_w_ff_up, w_ff_down=m_w_ff_down, final_norm_g=m_final_norm_g)
    var = dict(norm1_g=v_norm1_g, w_in=v_w_in, conv_w=v_conv_w, a_log=v_a_log, dt_bias=v_dt_bias, gdn_norm_g=v_gdn_norm_g,
               attn_sinks=v_attn_sinks, w_branch_gdn=v_w_branch_gdn, w_branch_swa=v_w_branch_swa, w_out=v_w_out, norm2_g=v_norm2_g,
               w_ff_up=v_w_ff_up, w_ff_down=v_w_ff_down, final_norm_g=v_final_norm_g)
    depth, d = norm1_g.shape
    xs, target = x[0], loss_target[0]
    core = lax.axis_index("c")
    chip = 2 * lax.axis_index("x") + lax.axis_index("y")
    pos = tuple(jnp.reshape(lax.axis_index(a), (1,)).astype(jnp.int32) for a in ("x", "y", "c"))

    cw = conv_w.shape[-1]
    placed = lax.dynamic_update_slice(jnp.zeros((depth, CONV_K, N_CHIPS * cw), F32), conv_w, (0, 0, chip * cw))
    placed = placed * (core == 0).astype(F32)
    conv_full = _allreduce_small(_to_rows(placed.reshape(-1)), "gather_conv_w")
    conv_full = conv_full.reshape(-1)[:depth * CONV_K * N_CHIPS * cw].reshape(depth, CONV_K, N_CHIPS * cw)

    alibi = _alibi_row(d)
    first_group, late_group = MATRICES[:1], MATRICES[1:]
    bufs = [{n: _cast_into_slot(w[n], l, pos, "l%d_cast_%s" % (l, n)) for n in MATRICES} for l in range(depth)]

    def start_gather(l, names, after, tag):
        group, _ = lax.optimization_barrier(([bufs[l][n] for n in names], after))
        return _gather_start(group, "l%d_gather_start_%s" % (l, tag))

    def finish_gather(l, names, started, after, tag):
        got = _gather_wait(started[0], started[1], after, "l%d_gather_wait_%s" % (l, tag))
        full = dict(zip(names, _gather_forward(got, "l%d_gather_forward_%s" % (l, tag))))
        out = {}
        if "w_in" in full:
            out["w_in"] = _pack_w_in(jnp.transpose(full["w_in"], (1, 0, 2)).reshape(d, -1), d)
        for n in ("w_branch_gdn", "w_branch_swa", "w_out", "w_ff_down"):
            if n in full:
                out[n] = full[n].reshape(-1, d)
        if "w_ff_up" in full:
            out["w_ff_up"] = full["w_ff_up"]
        return out

    def small_params(l):
        return dict(norm1_g=norm1_g[l][None], norm2_g=norm2_g[l][None], conv_w=conv_full[l], a_log=_pad_row(a_log[l]),
                    dt_bias=_pad_row(dt_bias[l]), gdn_norm_g=gdn_norm_g[l][None], attn_sinks=_pad_row(attn_sinks[l]), alibi=alibi)

    st_a = start_gather(0, first_group, conv_full, "a")
    st_b = start_gather(0, late_group, st_a[2], "b")
    layers = [dict(small_params(0), **finish_gather(0, first_group, st_a, st_b[2], "a"))]
    h = xs
    saved = []
    for l in range(depth):
        p = layers[l]
        late = None
        if l == 0:
            def late(after, st_b=st_b):
                rest = finish_gather(0, late_group, st_b, after, "b")
                layers[0].update(rest)
                return rest
        if l + 1 < depth:
            nxt = start_gather(l + 1, MATRICES, p["w_in"], "all")
            p = dict(p, norm1_g=p["norm1_g"] + nxt[2][:1, :1])
        h, s = _layer_fwd(h, p, "l%d_" % l, late)
        saved.append(s)
        if l + 1 < depth:
            layers.append(dict(small_params(l + 1), **finish_gather(l + 1, MATRICES, nxt, h, "all")))
    dh, d_final, loss_row = _loss_head(h, final_norm_g[None], target, "loss_head")

    grads = {n: [None] * depth for n in ("norm1_g", "norm2_g", "a_log", "dt_bias", "gdn_norm_g", "attn_sinks", "conv_w")}
    updated = {n: None for n in MATRICES}
    small = ("norm1_g", "norm2_g", "a_log", "dt_bias", "gdn_norm_g", "attn_sinks", "conv_w")
    state = {"pending": None, "todo": None}

    def adamw_todo():
        if state["todo"] is not None:
            l, names, sums = state["todo"]
            for n, r in zip(names, sums):
                updated[n] = _adamw_layer(w[n], r, mom[n], var[n], l, updated[n], "l%d_adamw_%s" % (l, n))
            state["todo"] = None

    def finish_scatter(after):
        adamw_todo()
        if state["pending"] is not None:
            l, names, pending, tag = state["pending"]
            state["todo"] = (l, names, _reduce_scatter_finish(pending, after, pos, "l%d_%s_" % (l, tag)))
            state["pending"] = None

    def start_scatter(l, names, mats, tag):
        pending, token = _reduce_scatter_start(mats, pos, "l%d_%s_" % (l, tag))
        state["pending"] = (l, names, pending, tag)
        return token

    def stacked(g, n):
        return g[n] if n == "w_ff_up" else g[n].reshape(N_CHIPS, -1, g[n].shape[-1])

    for l in reversed(range(depth)):
        def mid(g, after, l=l):
            finish_scatter(after)
            return start_scatter(l, late_group, [stacked(g, n) for n in late_group], "b")

        dh, g = _layer_bwd(dh, layers[l], saved[l], "l%d_" % l, mid)
        for n in grads:
            grads[n][l] = g[n].reshape(-1)
        finish_scatter(dh)
        g_in = _unpack_w_in(g["w_in"], d)
        mats = [jnp.transpose(g_in.reshape(d, N_CHIPS, -1), (1, 0, 2))]
        if l == 0:
            pieces = [jnp.stack(grads[n]).reshape(-1) for n in small] + [d_final.reshape(-1), loss_row[0, :1]]
            sizes = [p.shape[0] for p in pieces]
            packed = _allreduce_small(_to_rows(jnp.concatenate(pieces)), "reduce_small").reshape(-1)
            mats, _ = lax.optimization_barrier((mats, packed))
        token = start_scatter(l, first_group, mats, "a")
        if l > 0:
            dh = dh + token[0, 0]

    offs = np.concatenate([[0], np.cumsum(sizes)])
    red = {n: packed[offs[i]:offs[i + 1]] for i, n in enumerate(small + ("final_norm_g", "loss"))}
    loss = red["loss"][0]

    grad_out = {}
    for n in ("norm1_g", "norm2_g", "a_log", "dt_bias", "gdn_norm_g", "attn_sinks"):
        grad_out[n] = red[n].reshape(w[n].shape)
    grad_out["final_norm_g"] = red["final_norm_g"]
    conv_g = red["conv_w"].reshape(depth, CONV_K, N_CHIPS * cw)
    grad_out["conv_w"] = lax.dynamic_slice(conv_g, (0, 0, chip * cw), (depth, CONV_K, cw))

    delta, new_m, new_v = {}, {}, {}
    for n in grad_out:
        delta[n], new_m[n], new_v[n] = _adamw_nd(w[n], grad_out[n], mom[n], var[n], "adamw_" + n)
    adamw_todo()
    finish_scatter(updated["w_ff_down"][0])
    adamw_todo()
    for n in MATRICES:
        grad_out[n], delta[n], new_m[n], new_v[n] = updated[n]
    return (loss, dh[None], *[grad_out[n] for n in WEIGHTS], *[delta[n] for n in WEIGHTS], *[new_m[n] for n in WEIGHTS],
            *[new_v[n] for n in WEIGHTS])
```

```python
import functools

import jax
import jax.numpy as jnp
import numpy as np
from jax import lax
from jax.experimental import pallas as pl
from jax.experimental.pallas import tpu as pltpu

F32 = jnp.float32
BF16 = jnp.bfloat16

GDN_HEAD_DIM = 128
CHUNK = 64
SWA_HEAD_DIM = 64
WINDOW = 128
CONV_K = 4
GQA_GROUP = 8
NORM_EPS = 1e-6
N_CHIPS = 4
LANES = 128
CONV_HALO = 8
VMEM_LIMIT = 56 * 1024 * 1024

ADAM_LR = 0.001
ADAM_B1 = 0.9
ADAM_B2 = 0.999
ADAM_EPS = 1e-08
ADAM_WD = 0.01
ADAM_STEP = 10

NN = (((1,), (0,)), ((), ()))
NT = (((1,), (1,)), ((), ()))
TN = (((0,), (0,)), ((), ()))


def _pick(dim, cap, mult=LANES):
    if dim <= cap:
        return dim
    t = (cap // mult) * mult
    while t >= mult:
        if dim % t == 0:
            return t
        t -= mult
    return dim


def _params(sem):
    return pltpu.CompilerParams(dimension_semantics=sem, vmem_limit_bytes=VMEM_LIMIT)


def _bdot(a, b, dn):
    return lax.dot_general(a.astype(BF16), b.astype(BF16), dn, preferred_element_type=F32)


@jax.custom_vjp
def mm_nn(a, b):
    return _bdot(a, b, NN)


@jax.custom_vjp
def mm_nt(a, b):
    return _bdot(a, b, NT)


@jax.custom_vjp
def mm_tn(a, b):
    return _bdot(a, b, TN)


mm_nn.defvjp(lambda a, b: (_bdot(a, b, NN), (a, b)), lambda r, g: (_bdot(g, r[1], NT), _bdot(r[0], g, TN)))
mm_nt.defvjp(lambda a, b: (_bdot(a, b, NT), (a, b)), lambda r, g: (_bdot(g, r[1], NN), _bdot(g, r[0], TN)))
mm_tn.defvjp(lambda a, b: (_bdot(a, b, TN), (a, b)), lambda r, g: (_bdot(r[1], g, NT), _bdot(r[0], g, NN)))


def _hdot(a, b, dn=NN):
    return lax.dot_general(a, b, dn, precision=lax.Precision.HIGHEST, preferred_element_type=F32)


def _sigmoid(x):
    return 1.0 / (1.0 + jnp.exp(-x))


def _silu(x):
    return x * _sigmoid(x)


def _softplus(x):
    return jnp.maximum(x, 0.0) + jnp.log(1.0 + jnp.exp(-jnp.abs(x)))


def _lane_pick(row, lane, idx):
    return jnp.sum(jnp.where(lane == idx, row, 0.0), axis=1, keepdims=True)


def _matmul(a, b, *, ta=False, tb=False, out_dtype=F32, add=None, name, tm_cap=1024, tn_cap=1024, tk_cap=2048, b_split=1,
            out_split=1, epilogue=None, extras=(), also=None):
    m, k = (a.shape[1], a.shape[0]) if ta else a.shape
    b_rows, b_cols = (b.shape[-2], b.shape[-1] * b_split)
    n = b_rows if tb else b_cols
    assert k == (b_cols if tb else b_rows), (a.shape, b.shape, ta, tb)
    tm = _pick(m, tm_cap)
    tn = _pick(n // max(1 if tb else b_split, out_split), tn_cap)
    tk = _pick(k // (b_split if tb else 1), tk_cap)
    nk = k // tk
    dn = (((0 if ta else 1,), (1 if tb else 0,)), ((), ()))

    n_extra = len(extras)

    def body(*refs):
        a_ref, b_ref = refs[:2]
        add_ref = refs[2] if add is not None else None
        first = 2 + (add is not None)
        x_refs = refs[first:first + n_extra]
        o_ref = refs[first + n_extra]
        o2_ref = refs[first + n_extra + 1] if also is not None else None
        acc_ref = refs[-1]
        kk = pl.program_id(2)
        p = lax.dot_general(a_ref[...].astype(BF16), b_ref[...].astype(BF16), dn, preferred_element_type=F32)

        @pl.when(kk == 0)
        def _():
            acc_ref[...] = p

        @pl.when(kk > 0)
        def _():
            acc_ref[...] += p

        @pl.when(kk == nk - 1)
        def _():
            r = acc_ref[...]
            if add_ref is not None:
                r = r + add_ref[...].astype(F32)
            if also is not None:
                o2_ref[...] = also[0](r).astype(o2_ref.dtype)
            if epilogue is not None:
                r = epilogue(r, *[x[...] for x in x_refs])
            o_ref[...] = r.astype(o_ref.dtype)

    a_spec = pl.BlockSpec((tk, tm), lambda i, j, q: (q, i)) if ta else pl.BlockSpec((tm, tk), lambda i, j, q: (i, q))
    if b_split == 1:
        b_spec = pl.BlockSpec((tn, tk), lambda i, j, q: (j, q)) if tb else pl.BlockSpec((tk, tn), lambda i, j, q: (q, j))
    elif tb:
        per_b = k // b_split // tk
        b_spec = pl.BlockSpec((None, tn, tk), lambda i, j, q: (q // per_b, j, q % per_b))
    else:
        per_b = n // b_split // tn
        b_spec = pl.BlockSpec((None, tk, tn), lambda i, j, q: (j // per_b, q, j % per_b))
    add_spec = pl.BlockSpec((tm, tn), lambda i, j, q: (i, j))
    if out_split == 1:
        o_spec, o_shape = add_spec, (m, n)
    else:
        per_o = n // out_split // tn
        o_spec, o_shape = pl.BlockSpec((None, tm, tn), lambda i, j, q: (j // per_o, i, j % per_o)), (out_split, m, n // out_split)
    in_specs = [a_spec, b_spec] + ([add_spec] if add is not None else []) + [add_spec] * n_extra
    args = (a, b) + ((add,) if add is not None else ()) + tuple(extras)
    out_specs, out_shape = o_spec, jax.ShapeDtypeStruct(o_shape, out_dtype)
    if also is not None:
        out_specs, out_shape = [o_spec, o_spec], [out_shape, jax.ShapeDtypeStruct(o_shape, also[1])]
    return pl.pallas_call(
        body, name=name, grid=(m // tm, n // tn, nk), in_specs=in_specs, out_specs=out_specs,
        out_shape=out_shape, scratch_shapes=[pltpu.VMEM((tm, tn), F32)],
        compiler_params=_params(("parallel", "parallel", "arbitrary")),
    )(*args)


def _rows(fn, row_args, full_args, row_outs, acc_outs, *, t, tm, name):
    n_row, n_full, n_ro = len(row_args), len(full_args), len(row_outs)

    def body(*refs):
        ins = [r[...] for r in refs[:n_row + n_full]]
        outs = fn(*ins)
        o_refs = refs[n_row + n_full:]
        for r, v in zip(o_refs[:n_ro], outs[:n_ro]):
            r[...] = v.astype(r.dtype)
        i = pl.program_id(0)
        for r, v in zip(o_refs[n_ro:], outs[n_ro:]):
            @pl.when(i == 0)
            def _(r=r, v=v):
                r[...] = v

            @pl.when(i > 0)
            def _(r=r, v=v):
                r[...] += v

    in_specs = [pl.BlockSpec((tm, w), functools.partial(lambda i, cb: (i, cb), cb=cb)) for (_, w, cb) in row_args]
    in_specs += [pl.BlockSpec(f.shape, lambda i: (0, 0)) for f in full_args]
    out_specs = [pl.BlockSpec((tm, w), lambda i: (i, 0)) for (w, _) in row_outs]
    out_specs += [pl.BlockSpec(s, lambda i: (0, 0)) for s in acc_outs]
    out_shape = [jax.ShapeDtypeStruct((t, w), d) for (w, d) in row_outs]
    out_shape += [jax.ShapeDtypeStruct(s, F32) for s in acc_outs]
    return pl.pallas_call(
        body, name=name, grid=(t // tm,), in_specs=in_specs, out_specs=out_specs, out_shape=out_shape,
        compiler_params=_params(("arbitrary",)),
    )(*[a for (a, _, _) in row_args], *full_args)


def _rms(x, g):
    return x * lax.rsqrt(jnp.mean(x * x, axis=-1, keepdims=True) + NORM_EPS) * g


def _rmsnorm_fwd(x, g, name):
    t, d = x.shape
    (h,) = _rows(lambda xb, gb: (_rms(xb, gb),), [(x, d, 0)], [g], [(d, BF16)], [], t=t, tm=_pick(t, 512, 8), name=name)
    return h


def _rmsnorm_bwd(x, g, dh, dx_in, name):
    t, d = x.shape

    def fn(xb, dhb, dxb, gb):
        _, vjp = jax.vjp(_rms, xb, gb)
        dx, dg = vjp(dhb)
        return dxb + dx, dxb + dx, dg

    return _rows(fn, [(x, d, 0), (dh, d, 0), (dx_in, d, 0)], [g], [(d, F32), (d, BF16)], [(1, d)], t=t, tm=_pick(t, 256, 8), name=name)


def _merge(yg, ys, lg, ls):
    return _sigmoid(lg) * yg + _sigmoid(ls) * ys


def _merge_fwd(y_gdn, y_swa, proj, gate_off, name):
    t, d = y_gdn.shape
    cb = gate_off // d
    (mix,) = _rows(lambda a, b, c, e: (_merge(a, b, c, e),), [(y_gdn, d, 0), (y_swa, d, 0), (proj, d, cb), (proj, d, cb + 1)], [],
                   [(d, BF16)], [], t=t, tm=_pick(t, 256, 8), name=name)
    return mix


def _merge_bwd(y_gdn, y_swa, proj, gate_off, dmix, name):
    t, d = y_gdn.shape
    cb = gate_off // d

    def fn(a, b, c, e, g):
        _, vjp = jax.vjp(_merge, a, b, c, e)
        da, db, dc, de = vjp(g)
        return da, db, jnp.concatenate([dc, de], axis=1)

    return _rows(fn, [(y_gdn, d, 0), (y_swa, d, 0), (proj, d, cb), (proj, d, cb + 1), (dmix, d, 0)], [],
                 [(d, BF16), (d, BF16), (2 * d, BF16)], [], t=t, tm=_pick(t, 128, 8), name=name)


def _assemble(pieces, out_dtype, name):
    t = pieces[0].shape[0]
    widths = [p.shape[1] for p in pieces]
    offs = np.concatenate([[0], np.cumsum(widths)])

    def fn(*blocks):
        return (jnp.concatenate([b.astype(out_dtype) for b in blocks], axis=1),)

    (out,) = _rows(fn, [(p, w, 0) for p, w in zip(pieces, widths)], [], [(int(offs[-1]), out_dtype)], [], t=t, tm=_pick(t, 128, 16),
                   name=name)
    return out


def _loss_head(x, g, target, name):
    t, d = x.shape

    def loss_fn(xb, gb, tb):
        err = _rms(xb, gb) - tb
        return 0.5 * jnp.sum(jnp.mean(err * err, axis=-1, keepdims=True), axis=0, keepdims=True)

    def fn(xb, tb, gb):
        lv, vjp = jax.vjp(lambda a, b: loss_fn(a, b, tb), xb, gb)
        dx, dg = vjp(jnp.ones((1, 1), F32))
        return dx, dx, dg, jnp.broadcast_to(lv, (1, LANES))

    return _rows(fn, [(x, d, 0), (target, d, 0)], [g], [(d, F32), (d, BF16)], [(1, d), (1, LANES)], t=t, tm=_pick(t, 256, 8), name=name)


def _conv_silu(prev, cur, w, keep_prev):
    tm = cur.shape[0]
    xp = jnp.concatenate([prev * keep_prev, cur], axis=0)
    y = w[0:1, :] * xp[CONV_HALO - 3:CONV_HALO - 3 + tm]
    for j in range(1, CONV_K):
        y = y + w[j:j + 1, :] * xp[CONV_HALO - 3 + j:CONV_HALO - 3 + j + tm]
    return _silu(y)


def _conv_tiles(t, width):
    tm = _pick(t, 512, CONV_HALO)
    tc = _pick(width, 512)
    return tm, tc, t // tm, width // tc


def _conv_fwd(proj, conv_w, width, name):
    t = proj.shape[0]
    tm, tc, nt, ncw = _conv_tiles(t, width)
    hb = tm // CONV_HALO

    def body(prev_ref, cur_ref, w_ref, o_ref):
        keep = (pl.program_id(1) > 0).astype(F32)
        o_ref[0] = _conv_silu(prev_ref[...], cur_ref[...], w_ref[...], keep)

    return pl.pallas_call(
        body, name=name, grid=(3 * ncw, nt),
        in_specs=[pl.BlockSpec((CONV_HALO, tc), lambda j, i: (jnp.maximum(i * hb - 1, 0), j)),
                  pl.BlockSpec((tm, tc), lambda j, i: (i, j)),
                  pl.BlockSpec((CONV_K, tc), lambda j, i: (0, j))],
        out_specs=pl.BlockSpec((1, tm, tc), lambda j, i: (j // ncw, i, j % ncw)),
        out_shape=jax.ShapeDtypeStruct((3, t, width), F32),
        compiler_params=_params(("parallel", "arbitrary")),
    )(proj, proj, conv_w)


def _conv_bwd(proj, conv_w, dout, width, name):
    t = proj.shape[0]
    tm, tc, nt, ncw = _conv_tiles(t, width)
    hb = tm // CONV_HALO

    def body(prev_ref, cur_ref, w_ref, g_ref, dx_ref, dw_ref, carry_ref):
        s = pl.program_id(1)
        keep = (s < nt - 1).astype(F32)
        _, vjp = jax.vjp(lambda p, c, w: _conv_silu(p, c, w, keep), prev_ref[...], cur_ref[...], w_ref[...])
        dprev, dcur, dw = vjp(g_ref[0])

        @pl.when(s == 0)
        def _():
            carry_ref[...] = jnp.zeros_like(carry_ref)
            dw_ref[...] = dw

        @pl.when(s > 0)
        def _():
            dw_ref[...] += dw

        tail = jnp.concatenate([jnp.zeros((tm - CONV_HALO, tc), F32), carry_ref[...]], axis=0)
        dx_ref[...] = (dcur + tail).astype(dx_ref.dtype)
        carry_ref[...] = dprev

    def row(s):
        return nt - 1 - s

    return pl.pallas_call(
        body, name=name, grid=(3 * ncw, nt),
        in_specs=[pl.BlockSpec((CONV_HALO, tc), lambda j, s: (jnp.maximum(row(s) * hb - 1, 0), j)),
                  pl.BlockSpec((tm, tc), lambda j, s: (row(s), j)),
                  pl.BlockSpec((CONV_K, tc), lambda j, s: (0, j)),
                  pl.BlockSpec((1, tm, tc), lambda j, s: (j // ncw, row(s), j % ncw))],
        out_specs=[pl.BlockSpec((tm, tc), lambda j, s: (row(s), j)),
                   pl.BlockSpec((CONV_K, tc), lambda j, s: (0, j))],
        out_shape=[jax.ShapeDtypeStruct((t, 3 * width), BF16), jax.ShapeDtypeStruct((CONV_K, 3 * width), F32)],
        scratch_shapes=[pltpu.VMEM((CONV_HALO, tc), F32)],
        compiler_params=_params(("parallel", "arbitrary")),
    )(proj, proj, conv_w, dout)


def _inv_unit_lower_raw(mats):
    n = mats[0].shape[0]
    r = lax.broadcasted_iota(jnp.int32, (n, n), 0)
    c = lax.broadcasted_iota(jnp.int32, (n, n), 1)
    eye = (r == c).astype(F32)
    same = jnp.right_shift(r, 4) == jnp.right_shift(c, 4)
    dg = [jnp.where(same, a, 0.0) for a in mats]
    lo = [a - d for a, d in zip(mats, dg)]
    p = [eye - d for d in dg]
    q = dg
    for _ in range(3):
        q = [_hdot(x, x) for x in q]
        p = [_hdot(x, eye + y) for x, y in zip(p, q)]
    nm = [_hdot(x, y) for x, y in zip(p, lo)]
    n2 = [_hdot(x, x) for x in nm]
    left = [_hdot(eye - x, eye + y) for x, y in zip(nm, n2)]
    return [_hdot(x, y) for x, y in zip(left, p)]


@jax.custom_vjp
def _inv_unit_lower(mats):
    return _inv_unit_lower_raw(mats)


def _inv_fwd(mats):
    t = _inv_unit_lower_raw(mats)
    return t, t


def _inv_bwd(ts, gs):
    x = [_hdot(t, g, TN) for t, g in zip(ts, gs)]
    return ([-_hdot(a, t, NT) for a, t in zip(x, ts)],)


_inv_unit_lower.defvjp(_inv_fwd, _inv_bwd)


def _l2n(x):
    return x * lax.rsqrt(jnp.sum(x * x, axis=-1, keepdims=True) + NORM_EPS)


def _gdn_chunk(qcs, kcs, vcs, zs, bg, alog_row, dtb_row, gnorm, states, first_head, n_heads):
    nb = len(qcs)
    hs = range(nb)
    cs = qcs[0].shape[0]
    lane = lax.broadcasted_iota(jnp.int32, (1, LANES), 1)
    r = lax.broadcasted_iota(jnp.int32, (cs, cs), 0)
    c = lax.broadcasted_iota(jnp.int32, (cs, cs), 1)
    q = [_l2n(x) * (GDN_HEAD_DIM ** -0.5) for x in qcs]
    k = [_l2n(x) for x in kcs]
    beta = [_sigmoid(_lane_pick(bg, lane, first_head + i)) for i in hs]
    g = [-jnp.exp(_lane_pick(alog_row, lane, first_head + i)) *
         _softplus(_lane_pick(bg, lane, n_heads + first_head + i) + _lane_pick(dtb_row, lane, first_head + i)) for i in hs]
    g_row = [jnp.sum(jnp.where(r == c, x, 0.0), axis=0, keepdims=True) for x in g]
    dec_col = [jnp.sum(jnp.where(r >= c, x, 0.0), axis=1, keepdims=True) for x in g_row]
    dec_row = [jnp.sum(jnp.where(r <= c, x, 0.0), axis=0, keepdims=True) for x in g]
    gamma = [jnp.exp(jnp.where(r >= c, dc - dr, -1e30)) for dc, dr in zip(dec_col, dec_row)]
    kb = [x * b for x, b in zip(k, beta)]
    a = [jnp.where(r > c, mm_nt(x, y) * gm, 0.0) for x, y, gm in zip(kb, k, gamma)]
    tinv = _inv_unit_lower(a)
    e_col = [jnp.exp(x) for x in dec_col]
    u = [mm_nn(t, v * b) for t, v, b in zip(tinv, vcs, beta)]
    w = [mm_nn(t, x * e) for t, x, e in zip(tinv, kb, e_col)]
    qk = [mm_nt(x, y) * gm for x, y, gm in zip(q, k, gamma)]
    total = [jnp.sum(x, axis=0, keepdims=True) for x in g]
    v_new = [x - mm_nn(y, s) for x, y, s in zip(u, w, states)]
    o = [mm_nn(x * e, s) + mm_nn(y, v) for x, e, s, y, v in zip(q, e_col, states, qk, v_new)]
    new_states = [s * jnp.exp(tt) + mm_tn(x * jnp.exp(tt - dc), v) for s, tt, x, dc, v in zip(states, total, k, dec_col, v_new)]
    ys = [_rms(x, gnorm) * _silu(z) for x, z in zip(o, zs)]
    return ys, new_states


GDN_HEADS_FWD = 16
GDN_HEADS_BWD = 16


def _gdn_fwd(qkvc, proj, alog_row, dtb_row, gnorm, *, d, z_off, bg_off, name, hb=GDN_HEADS_FWD):
    t = qkvc.shape[1]
    nh = d // GDN_HEAD_DIM
    hb = min(hb, nh)
    wb = hb * GDN_HEAD_DIM
    nc = t // CHUNK
    ng = nh // hb

    def body(qkv_ref, z_ref, bg_ref, al_ref, dt_ref, gn_ref, y_ref, sin_ref, s_scr):
        n, hg = pl.program_id(0), pl.program_id(1)

        @pl.when(n == 0)
        def _():
            s_scr[hg] = jnp.zeros((hb, GDN_HEAD_DIM, GDN_HEAD_DIM), F32)

        states = [s_scr[hg, i] for i in range(hb)]
        bg, al, dt, gn = bg_ref[...], al_ref[...], dt_ref[...], gn_ref[...]
        sls = [slice(i * GDN_HEAD_DIM, (i + 1) * GDN_HEAD_DIM) for i in range(hb)]
        ys, new_states = _gdn_chunk([qkv_ref[0, :, sl] for sl in sls], [qkv_ref[1, :, sl] for sl in sls], [qkv_ref[2, :, sl] for sl in sls],
                                    [z_ref[:, sl] for sl in sls], bg, al, dt, gn, states, hg * hb, nh)
        for i in range(hb):
            sin_ref[0, i] = states[i]
            y_ref[:, sls[i]] = ys[i].astype(y_ref.dtype)
            s_scr[hg, i] = new_states[i]

    row = lambda n, hg: (0, 0)
    return pl.pallas_call(
        body, name=name, grid=(nc, ng),
        in_specs=[pl.BlockSpec((3, CHUNK, wb), lambda n, hg: (0, n, hg)),
                  pl.BlockSpec((CHUNK, wb), lambda n, hg: (n, z_off // wb + hg)),
                  pl.BlockSpec((CHUNK, LANES), lambda n, hg: (n, bg_off // LANES)),
                  pl.BlockSpec((1, LANES), row), pl.BlockSpec((1, LANES), row), pl.BlockSpec((1, LANES), row)],
        out_specs=[pl.BlockSpec((CHUNK, wb), lambda n, hg: (n, hg)),
                   pl.BlockSpec((1, hb, GDN_HEAD_DIM, GDN_HEAD_DIM), lambda n, hg: (n, hg, 0, 0))],
        out_shape=[jax.ShapeDtypeStruct((t, d), BF16), jax.ShapeDtypeStruct((nc, nh, GDN_HEAD_DIM, GDN_HEAD_DIM), F32)],
        scratch_shapes=[pltpu.VMEM((ng, hb, GDN_HEAD_DIM, GDN_HEAD_DIM), F32)],
        compiler_params=_params(("arbitrary", "arbitrary")),
    )(qkvc, proj, proj, alog_row, dtb_row, gnorm)


def _gdn_bwd(qkvc, proj, alog_row, dtb_row, gnorm, states, dy, *, d, z_off, bg_off, name, hb=GDN_HEADS_BWD):
    t = qkvc.shape[1]
    nh = d // GDN_HEAD_DIM
    hb = min(hb, nh)
    wb = hb * GDN_HEAD_DIM
    nc = t // CHUNK
    ng = nh // hb

    def body(qkv_ref, z_ref, bg_ref, al_ref, dt_ref, gn_ref, sin_ref, dy_ref, dqkv_ref, dz_ref, dbg_ref, dal_ref, ddt_ref, dgn_ref,
             ds_scr):
        s, hg = pl.program_id(0), pl.program_id(1)

        @pl.when(s == 0)
        def _():
            ds_scr[hg] = jnp.zeros((hb, GDN_HEAD_DIM, GDN_HEAD_DIM), F32)

        @pl.when(hg == 0)
        def _():
            dbg_ref[...] = jnp.zeros_like(dbg_ref)

        @pl.when((s == 0) & (hg == 0))
        def _():
            dal_ref[...] = jnp.zeros_like(dal_ref)
            ddt_ref[...] = jnp.zeros_like(ddt_ref)
            dgn_ref[...] = jnp.zeros_like(dgn_ref)

        dstates = [ds_scr[hg, i] for i in range(hb)]
        bg, al, dt, gn = bg_ref[...], al_ref[...], dt_ref[...], gn_ref[...]
        sls = [slice(i * GDN_HEAD_DIM, (i + 1) * GDN_HEAD_DIM) for i in range(hb)]
        fn = functools.partial(_gdn_chunk, first_head=hg * hb, n_heads=nh)
        _, vjp = jax.vjp(fn, [qkv_ref[0, :, sl] for sl in sls], [qkv_ref[1, :, sl] for sl in sls], [qkv_ref[2, :, sl] for sl in sls],
                         [z_ref[:, sl] for sl in sls], bg, al, dt, gn, [sin_ref[0, i] for i in range(hb)])
        dq, dk, dv, dz, dbg, dal, ddt, dgn, dst = vjp(([dy_ref[:, sl] for sl in sls], dstates))
        for i in range(hb):
            dqkv_ref[0, :, sls[i]] = dq[i]
            dqkv_ref[1, :, sls[i]] = dk[i]
            dqkv_ref[2, :, sls[i]] = dv[i]
            dz_ref[:, sls[i]] = dz[i].astype(dz_ref.dtype)
            ds_scr[hg, i] = dst[i]
        dbg_ref[...] += dbg
        dal_ref[...] += dal
        ddt_ref[...] += ddt
        dgn_ref[...] += dgn

    def ch(s):
        return nc - 1 - s

    row = lambda s, hg: (0, 0)
    return pl.pallas_call(
        body, name=name, grid=(nc, ng),
        in_specs=[pl.BlockSpec((3, CHUNK, wb), lambda s, hg: (0, ch(s), hg)),
                  pl.BlockSpec((CHUNK, wb), lambda s, hg: (ch(s), z_off // wb + hg)),
                  pl.BlockSpec((CHUNK, LANES), lambda s, hg: (ch(s), bg_off // LANES)),
                  pl.BlockSpec((1, LANES), row), pl.BlockSpec((1, LANES), row), pl.BlockSpec((1, LANES), row),
                  pl.BlockSpec((1, hb, GDN_HEAD_DIM, GDN_HEAD_DIM), lambda s, hg: (ch(s), hg, 0, 0)),
                  pl.BlockSpec((CHUNK, wb), lambda s, hg: (ch(s), hg))],
        out_specs=[pl.BlockSpec((3, CHUNK, wb), lambda s, hg: (0, ch(s), hg)),
                   pl.BlockSpec((CHUNK, wb), lambda s, hg: (ch(s), hg)),
                   pl.BlockSpec((CHUNK, LANES), lambda s, hg: (ch(s), 0)),
                   pl.BlockSpec((1, LANES), row), pl.BlockSpec((1, LANES), row), pl.BlockSpec((1, LANES), row)],
        out_shape=[jax.ShapeDtypeStruct((3, t, d), F32), jax.ShapeDtypeStruct((t, d), BF16), jax.ShapeDtypeStruct((t, LANES), F32),
                   jax.ShapeDtypeStruct((1, LANES), F32), jax.ShapeDtypeStruct((1, LANES), F32), jax.ShapeDtypeStruct((1, LANES), F32)],
        scratch_shapes=[pltpu.VMEM((ng, hb, GDN_HEAD_DIM, GDN_HEAD_DIM), F32)],
        compiler_params=_params(("arbitrary", "arbitrary")),
    )(qkvc, proj, proj, alog_row, dtb_row, gnorm, states, dy)


@jax.custom_vjp
def _swap_halves(x):
    return pltpu.roll(x, SWA_HEAD_DIM, 1)


_swap_halves.defvjp(lambda x: (pltpu.roll(x, SWA_HEAD_DIM, 1), None), lambda _, g: (pltpu.roll(g, SWA_HEAD_DIM, 1),))

SWA_PAIR_Q = 2 * GQA_GROUP * SWA_HEAD_DIM


def _swa_block(q, kp, kc, vp, vc, sink_row, slope_row, keep_prev, pair):
    kb = jnp.concatenate([kp, kc], axis=0)
    vb = jnp.concatenate([vp, vc], axis=0)
    lane = lax.broadcasted_iota(jnp.int32, (1, LANES), 1)
    low = lane < SWA_HEAD_DIM
    high = jnp.logical_not(low)
    qi = lax.broadcasted_iota(jnp.int32, (WINDOW, 2 * WINDOW), 0)
    sj = lax.broadcasted_iota(jnp.int32, (WINDOW, 2 * WINDOW), 1)
    dist = qi + WINDOW - sj
    valid = (dist >= 0) & (dist < WINDOW) & ((sj >= WINDOW) | (keep_prev > 0.5))
    distf = dist.astype(F32)
    kk, vv = [], []
    for mine in (low, high):
        x = jnp.where(mine, kb, 0.0)
        kk.append(x + _swap_halves(x))
        y = jnp.where(mine, vb, 0.0)
        vv.append(y + _swap_halves(y))
    hl = range(2 * GQA_GROUP)
    half = [low if h % 2 == 0 else high for h in hl]
    slope = [_lane_pick(slope_row, lane, pair * (2 * GQA_GROUP) + h) for h in hl]
    sink = [_lane_pick(sink_row, lane, pair * (2 * GQA_GROUP) + h) for h in hl]
    qm = [jnp.where(half[h], q[:, (h // 2) * LANES:(h // 2 + 1) * LANES], 0.0) for h in hl]
    sc = [mm_nt(qm[h], kk[h // GQA_GROUP]) * (SWA_HEAD_DIM ** -0.5) for h in hl]
    sc = [jnp.where(valid, sc[h] - slope[h] * distf, -1e30) for h in hl]
    m = [lax.stop_gradient(jnp.maximum(jnp.max(sc[h], axis=-1, keepdims=True), sink[h])) for h in hl]
    p = [jnp.exp(sc[h] - m[h]) for h in hl]
    probs = [p[h] / (jnp.sum(p[h], axis=-1, keepdims=True) + jnp.exp(sink[h] - m[h])) for h in hl]
    od = [jnp.where(half[h], mm_nn(probs[h], vv[h // GQA_GROUP]), 0.0) for h in hl]
    return jnp.concatenate([od[2 * i] + od[2 * i + 1] for i in range(GQA_GROUP)], axis=1)


def _swa_specs(t, q_off, k_off, v_off, order):
    nb = t // WINDOW

    def blk(s):
        return order(s, nb)

    return nb, [pl.BlockSpec((WINDOW, SWA_PAIR_Q), lambda p, s: (blk(s), q_off // SWA_PAIR_Q + p)),
                pl.BlockSpec((WINDOW, LANES), lambda p, s: (jnp.maximum(blk(s) - 1, 0), k_off // LANES + p)),
                pl.BlockSpec((WINDOW, LANES), lambda p, s: (blk(s), k_off // LANES + p)),
                pl.BlockSpec((WINDOW, LANES), lambda p, s: (jnp.maximum(blk(s) - 1, 0), v_off // LANES + p)),
                pl.BlockSpec((WINDOW, LANES), lambda p, s: (blk(s), v_off // LANES + p)),
                pl.BlockSpec((1, LANES), lambda p, s: (0, 0)), pl.BlockSpec((1, LANES), lambda p, s: (0, 0))]


def _swa_fwd(proj, sink_row, slope_row, *, d, q_off, k_off, v_off, name):
    t = proj.shape[0]
    n_pairs = d // SWA_PAIR_Q
    nb, in_specs = _swa_specs(t, q_off, k_off, v_off, lambda s, nb: s)

    def body(q_ref, kp_ref, kc_ref, vp_ref, vc_ref, sink_ref, slope_ref, o_ref):
        keep = (pl.program_id(1) > 0).astype(F32)
        o = _swa_block(q_ref[...], kp_ref[...], kc_ref[...], vp_ref[...], vc_ref[...], sink_ref[...], slope_ref[...], keep,
                       pl.program_id(0))
        o_ref[...] = o.astype(o_ref.dtype)

    return pl.pallas_call(
        body, name=name, grid=(n_pairs, nb), in_specs=in_specs,
        out_specs=pl.BlockSpec((WINDOW, SWA_PAIR_Q), lambda p, s: (s, p)),
        out_shape=jax.ShapeDtypeStruct((t, d), BF16),
        compiler_params=_params(("parallel", "arbitrary")),
    )(proj, proj, proj, proj, proj, sink_row, slope_row)


def _swa_bwd(proj, sink_row, slope_row, do, *, d, q_off, k_off, v_off, name):
    t = proj.shape[0]
    n_pairs = d // SWA_PAIR_Q
    nb, in_specs = _swa_specs(t, q_off, k_off, v_off, lambda s, nb: nb - 1 - s)

    def body(q_ref, kp_ref, kc_ref, vp_ref, vc_ref, sink_ref, slope_ref, do_ref, dq_ref, dk_ref, dv_ref, dsink_ref, ck_ref, cv_ref):
        p, s = pl.program_id(0), pl.program_id(1)
        keep = (s < nb - 1).astype(F32)
        fn = functools.partial(_swa_block, slope_row=slope_ref[...], keep_prev=keep, pair=p)
        _, vjp = jax.vjp(fn, q_ref[...], kp_ref[...], kc_ref[...], vp_ref[...], vc_ref[...], sink_ref[...])
        dq, dkp, dkc, dvp, dvc, dsink = vjp(do_ref[...])

        @pl.when(s == 0)
        def _():
            ck_ref[...] = jnp.zeros_like(ck_ref)
            cv_ref[...] = jnp.zeros_like(cv_ref)

        @pl.when((s == 0) & (p == 0))
        def _():
            dsink_ref[...] = jnp.zeros_like(dsink_ref)

        dq_ref[...] = dq.astype(dq_ref.dtype)
        dk_ref[...] = (dkc + ck_ref[...]).astype(dk_ref.dtype)
        dv_ref[...] = (dvc + cv_ref[...]).astype(dv_ref.dtype)
        ck_ref[...] = dkp
        cv_ref[...] = dvp
        dsink_ref[...] += dsink

    in_specs = in_specs + [pl.BlockSpec((WINDOW, SWA_PAIR_Q), lambda p, s: (nb - 1 - s, p))]
    kv_w = d // GQA_GROUP
    return pl.pallas_call(
        body, name=name, grid=(n_pairs, nb), in_specs=in_specs,
        out_specs=[pl.BlockSpec((WINDOW, SWA_PAIR_Q), lambda p, s: (nb - 1 - s, p)),
                   pl.BlockSpec((WINDOW, LANES), lambda p, s: (nb - 1 - s, p)),
                   pl.BlockSpec((WINDOW, LANES), lambda p, s: (nb - 1 - s, p)),
                   pl.BlockSpec((1, LANES), lambda p, s: (0, 0))],
        out_shape=[jax.ShapeDtypeStruct((t, d), BF16), jax.ShapeDtypeStruct((t, kv_w), BF16), jax.ShapeDtypeStruct((t, kv_w), BF16),
                   jax.ShapeDtypeStruct((1, LANES), F32)],
        scratch_shapes=[pltpu.VMEM((WINDOW, LANES), F32), pltpu.VMEM((WINDOW, LANES), F32)],
        compiler_params=_params(("arbitrary", "arbitrary")),
    )(proj, proj, proj, proj, proj, sink_row, slope_row, do)


def _layout(d):
    kv = d // GQA_GROUP
    return dict(z=3 * d, q=4 * d, gate=5 * d, k=7 * d, v=7 * d + kv, bg=7 * d + 2 * kv, width=7 * d + 2 * kv + LANES)


def _pack_w_in(w, d):
    nh = d // GDN_HEAD_DIM
    kv = d // GQA_GROUP
    o = 4 * d + 2 * nh
    parts = [w[..., :4 * d], w[..., o:o + d], w[..., o + d + 2 * kv:o + 3 * d + 2 * kv], w[..., o + d:o + d + 2 * kv],
             w[..., 4 * d:o], jnp.zeros(w.shape[:-1] + (LANES - 2 * nh,), w.dtype)]
    return jnp.concatenate(parts, axis=-1)


def _unpack_w_in(wp, d):
    nh = d // GDN_HEAD_DIM
    kv = d // GQA_GROUP
    lay = _layout(d)
    parts = [wp[..., :4 * d], wp[..., lay["bg"]:lay["bg"] + 2 * nh], wp[..., lay["q"]:lay["q"] + d],
             wp[..., lay["k"]:lay["k"] + 2 * kv], wp[..., lay["gate"]:lay["gate"] + 2 * d]]
    return jnp.concatenate(parts, axis=-1)


def _pad_row(v):
    return jnp.pad(v.astype(F32), (0, LANES - v.shape[0]))[None, :]


def _alibi_row(d):
    nq = d // SWA_HEAD_DIM
    return _pad_row(2.0 ** (-8.0 * jnp.arange(1, nq + 1, dtype=F32) / nq))


def _layer_fwd(x, p, tag, late=None):
    t, d = x.shape
    lay = _layout(d)
    tn = 1152 if lay["width"] % 1152 == 0 else 1024
    h1 = _rmsnorm_fwd(x, p["norm1_g"], tag + "rms1")
    proj = _matmul(h1, p["w_in"], name=tag + "mm_in", tn_cap=tn)
    qkvc = _conv_fwd(proj, p["conv_w"], d, tag + "conv")
    gdn_o, states = _gdn_fwd(qkvc, proj, p["a_log"], p["dt_bias"], p["gdn_norm_g"], d=d, z_off=lay["z"], bg_off=lay["bg"],
                             name=tag + "gdn")
    swa_o = _swa_fwd(proj, p["attn_sinks"], p["alibi"], d=d, q_off=lay["q"], k_off=lay["k"], v_off=lay["v"], name=tag + "swa")
    if late is not None:
        p = dict(p, **late(swa_o))
    y_gdn = _matmul(gdn_o, p["w_branch_gdn"], name=tag + "mm_bg")
    y_swa = _matmul(swa_o, p["w_branch_swa"], name=tag + "mm_bs")
    mix = _merge_fwd(y_gdn, y_swa, proj, lay["gate"], tag + "merge")
    x1 = _matmul(mix, p["w_out"], add=x, name=tag + "mm_out")
    h2 = _rmsnorm_fwd(x1, p["norm2_g"], tag + "rms2")
    up, act = _matmul(h2, p["w_ff_up"], name=tag + "mm_up", b_split=N_CHIPS, also=(lambda u: jnp.square(jnp.maximum(u, 0.0)), BF16))
    x2 = _matmul(act, p["w_ff_down"], add=x1, name=tag + "mm_down")
    return x2, dict(x=x, h1=h1, proj=proj, qkvc=qkvc, states=states, gdn_o=gdn_o, swa_o=swa_o, y_gdn=y_gdn, y_swa=y_swa, mix=mix,
                    x1=x1, h2=h2, up=up, act=act)


def _layer_bwd(dx2, dx2_b, p, s, tag, mid=None):
    t, d = dx2.shape
    lay = _layout(d)
    tn = 1152 if lay["width"] % 1152 == 0 else 1024
    nh = d // GDN_HEAD_DIM
    g = {}
    dup = _matmul(dx2_b, p["w_ff_down"], tb=True, out_dtype=BF16, name=tag + "mm_dup", tm_cap=512, extras=(s["up"],),
                  epilogue=lambda r, u: r * 2.0 * jnp.maximum(u, 0.0))
    g["w_ff_down"] = _matmul(s["act"], dx2_b, ta=True, out_dtype=BF16, name=tag + "mm_dwdown")
    g["w_ff_up"] = _matmul(s["h2"], dup, ta=True, out_dtype=BF16, name=tag + "mm_dwup", out_split=N_CHIPS)
    dh2 = _matmul(dup, p["w_ff_up"], tb=True, name=tag + "mm_dh2", b_split=N_CHIPS)
    dx1, dx1_b, g["norm2_g"] = _rmsnorm_bwd(s["x1"], p["norm2_g"], dh2, dx2, tag + "rms2b")
    dmix = _matmul(dx1_b, p["w_out"], tb=True, name=tag + "mm_dmix")
    g["w_out"] = _matmul(s["mix"], dx1_b, ta=True, out_dtype=BF16, name=tag + "mm_dwout")
    dyg, dys, dgl = _merge_bwd(s["y_gdn"], s["y_swa"], s["proj"], lay["gate"], dmix, tag + "mergeb")
    g["w_branch_gdn"] = _matmul(s["gdn_o"], dyg, ta=True, out_dtype=BF16, name=tag + "mm_dwbg")
    g["w_branch_swa"] = _matmul(s["swa_o"], dys, ta=True, out_dtype=BF16, name=tag + "mm_dwbs")
    dgdn_o = _matmul(dyg, p["w_branch_gdn"], tb=True, name=tag + "mm_dgdn")
    dswa_o = _matmul(dys, p["w_branch_swa"], tb=True, name=tag + "mm_dswa")
    if mid is not None:
        token = mid(g, dswa_o)
        p = dict(p, attn_sinks=p["attn_sinks"] + token[:1], a_log=p["a_log"] + token[:1])
    dq_s, dk_s, dv_s, dsink = _swa_bwd(s["proj"], p["attn_sinks"], p["alibi"], dswa_o, d=d, q_off=lay["q"], k_off=lay["k"],
                                       v_off=lay["v"], name=tag + "swab")
    dqkvc, dz, dbg, dal, ddt, dgn = _gdn_bwd(s["qkvc"], s["proj"], p["a_log"], p["dt_bias"], p["gdn_norm_g"], s["states"], dgdn_o,
                                             d=d, z_off=lay["z"], bg_off=lay["bg"], name=tag + "gdnb")
    dqkv, g["conv_w"] = _conv_bwd(s["proj"], p["conv_w"], dqkvc, d, tag + "convb")
    dproj = _assemble([dqkv, dz, dq_s, dgl, dk_s, dv_s, dbg], BF16, tag + "dproj")
    g["w_in"] = _matmul(s["h1"], dproj, ta=True, out_dtype=BF16, name=tag + "mm_dwin", tn_cap=tn)
    dh1 = _matmul(dproj, p["w_in"], tb=True, name=tag + "mm_dh1", tk_cap=tn)
    dx, dx_b, g["norm1_g"] = _rmsnorm_bwd(s["x"], p["norm1_g"], dh1, dx1, tag + "rms1b")
    g["a_log"], g["dt_bias"], g["gdn_norm_g"], g["attn_sinks"] = dal[0, :nh], ddt[0, :nh], dgn[0], dsink[0, :d // SWA_HEAD_DIM]
    return dx, dx_b, g


MESH = pl.DeviceIdType.MESH
HBM_SPEC = pl.BlockSpec(memory_space=pl.ANY)


def _place():
    x, y, c = lax.axis_index("x"), lax.axis_index("y"), lax.axis_index("c")
    return x, y, c, 2 * x + y


def _flip(x, y, k):
    px, py = x ^ (k >> 1), y ^ (k & 1)
    return px, py, 2 * px + py


def _cast_into_slot(w, layer, pos, name):
    _, r, cols = w.shape
    tm = _pick(r, max(16, (1 << 19) // cols // 16 * 16), 16)

    def body(x_ref, y_ref, w_ref, o_ref):
        o_ref[...] = w_ref[...].astype(o_ref.dtype)

    return pl.pallas_call(
        body, name=name,
        grid_spec=pltpu.PrefetchScalarGridSpec(
            num_scalar_prefetch=2, grid=(r // tm,),
            in_specs=[pl.BlockSpec((None, tm, cols), lambda i, xr, yr: (layer, i, 0))],
            out_specs=pl.BlockSpec((None, tm, cols), lambda i, xr, yr: (2 * xr[0] + yr[0], i, 0))),
        out_shape=jax.ShapeDtypeStruct((N_CHIPS, r, cols), BF16),
        compiler_params=_params(("parallel",)),
    )(pos[0], pos[1], w)


SEM_SPEC = pl.BlockSpec(memory_space=pltpu.SEMAPHORE)
SPLIT_COPY = pltpu.CompilerParams(has_side_effects=pltpu.SideEffectType.DATAFLOW_SIDE_EFFECTING)
TOKEN = jax.ShapeDtypeStruct((8, LANES), F32)
TOKEN_SPEC = pl.BlockSpec(memory_space=pltpu.VMEM)


def _gather_start(bufs, name):
    n = len(bufs)

    def body(*refs):
        outs, sems, token = refs[n:2 * n], refs[2 * n:8 * n], refs[8 * n]
        x, y, c, ci = _place()
        for a in range(n):
            hr = bufs[a].shape[1] // 2
            mine = outs[a].at[ci, pl.ds(c * hr, hr)]
            for k in (1, 2, 3):
                px, py, _ = _flip(x, y, k)
                pltpu.make_async_remote_copy(src_ref=mine, dst_ref=mine, send_sem=sems[3 * a + k - 1], recv_sem=sems[3 * n + 3 * a + k - 1],
                                             device_id=(px, py, c), device_id_type=MESH).start()
        token[...] = jnp.zeros_like(token)

    res = pl.pallas_call(
        body, name=name, in_specs=[HBM_SPEC] * n, out_specs=[HBM_SPEC] * n + [SEM_SPEC] * (6 * n) + [TOKEN_SPEC],
        out_shape=[pltpu.HBM(b.shape, b.dtype) for b in bufs] + [pltpu.SemaphoreType.DMA(())] * (6 * n) + [TOKEN],
        input_output_aliases={a: a for a in range(n)}, compiler_params=SPLIT_COPY,
    )(*[pltpu.with_memory_space_constraint(b, pltpu.HBM) for b in bufs])
    return res[:n], res[n:7 * n], res[7 * n]


def _gather_wait(bufs, sems, after, name):
    n = len(bufs)

    def body(*refs):
        sems, outs = refs[n:7 * n], refs[7 * n + 1:8 * n + 1]
        x, y, c, ci = _place()
        for a in range(n):
            hr = bufs[a].shape[1] // 2
            mine = outs[a].at[ci, pl.ds(c * hr, hr)]
            for k in (1, 2, 3):
                px, py, pj = _flip(x, y, k)
                landed = outs[a].at[pj, pl.ds(c * hr, hr)]
                cp = pltpu.make_async_remote_copy(src_ref=mine, dst_ref=landed, send_sem=sems[3 * a + k - 1], recv_sem=sems[3 * n + 3 * a + k - 1],
                                                  device_id=(px, py, c), device_id_type=MESH)
                cp.wait_send()
                cp.wait_recv()

    return pl.pallas_call(
        body, name=name, in_specs=[HBM_SPEC] * n + [SEM_SPEC] * (6 * n) + [HBM_SPEC], out_specs=[HBM_SPEC] * n,
        out_shape=[pltpu.HBM(b.shape, b.dtype) for b in bufs],
        input_output_aliases={a: a for a in range(n)}, compiler_params=SPLIT_COPY,
    )(*bufs, *sems, after)


def _gather_forward(bufs, name):
    n = len(bufs)

    def body(*refs):
        outs = refs[n:2 * n]
        send_sems, recv_sems = refs[2 * n:]
        x, y, c, _ = _place()
        waits = []
        for a in range(n):
            hr = bufs[a].shape[1] // 2
            for k in (1, 2, 3):
                _, _, pj = _flip(x, y, k)
                landed = outs[a].at[pj, pl.ds(c * hr, hr)]
                fw = pltpu.make_async_remote_copy(src_ref=landed, dst_ref=landed, send_sem=send_sems.at[a, k - 1], recv_sem=recv_sems.at[a, k - 1],
                                                  device_id=(x, y, 1 - c), device_id_type=MESH)
                fw.start()
                waits.append(fw.wait_send)
                passed = outs[a].at[pj, pl.ds((1 - c) * hr, hr)]
                waits.append(pltpu.make_async_remote_copy(src_ref=passed, dst_ref=passed, send_sem=send_sems.at[a, k - 1],
                                                          recv_sem=recv_sems.at[a, k - 1], device_id=(x, y, 1 - c),
                                                          device_id_type=MESH).wait_recv)
        for w in waits:
            w()

    return pl.pallas_call(
        body, name=name, in_specs=[HBM_SPEC] * n, out_specs=[HBM_SPEC] * n,
        out_shape=[jax.ShapeDtypeStruct(b.shape, b.dtype) for b in bufs],
        input_output_aliases={a: a for a in range(n)},
        scratch_shapes=[pltpu.SemaphoreType.DMA((n, 3))] * 2,
    )(*bufs)


def _swap_with_sibling(gs, name):
    n = len(gs)

    def body(*refs):
        ins, outs = refs[:n], refs[n:2 * n]
        send_sems, recv_sems = refs[2 * n:]
        x, y, c, _ = _place()
        cps = []
        for a in range(n):
            hr = gs[a].shape[1] // 2
            cp = pltpu.make_async_remote_copy(src_ref=ins[a].at[:, pl.ds((1 - c) * hr, hr)], dst_ref=outs[a], send_sem=send_sems.at[a],
                                              recv_sem=recv_sems.at[a], device_id=(x, y, 1 - c), device_id_type=MESH)
            cp.start()
            cps.append(cp)
        for cp in cps:
            cp.wait()

    return pl.pallas_call(
        body, name=name, in_specs=[HBM_SPEC] * n, out_specs=[HBM_SPEC] * n,
        out_shape=[jax.ShapeDtypeStruct((g.shape[0], g.shape[1] // 2, g.shape[2]), g.dtype) for g in gs],
        scratch_shapes=[pltpu.SemaphoreType.DMA((n,))] * 2,
    )(*gs)


def _scatter_start(hs, name):
    n = len(hs)

    def body(*refs):
        srcs, lands, sems, token = refs[n:2 * n], refs[2 * n:3 * n], refs[3 * n:9 * n], refs[9 * n]
        x, y, c, ci = _place()
        for a in range(n):
            for k in (1, 2, 3):
                px, py, pj = _flip(x, y, k)
                pltpu.make_async_remote_copy(src_ref=srcs[a].at[pj], dst_ref=lands[a].at[ci], send_sem=sems[3 * a + k - 1],
                                             recv_sem=sems[3 * n + 3 * a + k - 1], device_id=(px, py, c), device_id_type=MESH).start()
        token[...] = jnp.zeros_like(token)

    res = pl.pallas_call(
        body, name=name, in_specs=[HBM_SPEC] * n, out_specs=[HBM_SPEC] * (2 * n) + [SEM_SPEC] * (6 * n) + [TOKEN_SPEC],
        out_shape=[pltpu.HBM(h.shape, h.dtype) for h in hs] * 2 + [pltpu.SemaphoreType.DMA(())] * (6 * n) + [TOKEN],
        input_output_aliases={a: a for a in range(n)}, compiler_params=SPLIT_COPY,
    )(*[pltpu.with_memory_space_constraint(h, pltpu.HBM) for h in hs])
    return res[:n], res[n:2 * n], res[2 * n:8 * n], res[8 * n]


def _scatter_wait(hs, lands, sems, after, name):
    n = len(hs)

    def body(*refs):
        sems, srcs, lands_o = refs[2 * n:8 * n], refs[8 * n + 1:9 * n + 1], refs[9 * n + 1:10 * n + 1]
        x, y, c, ci = _place()
        for a in range(n):
            for k in (1, 2, 3):
                px, py, pj = _flip(x, y, k)
                cp = pltpu.make_async_remote_copy(src_ref=srcs[a].at[pj], dst_ref=lands_o[a].at[pj], send_sem=sems[3 * a + k - 1],
                                                  recv_sem=sems[3 * n + 3 * a + k - 1], device_id=(px, py, c), device_id_type=MESH)
                cp.wait_send()
                cp.wait_recv()

    res = pl.pallas_call(
        body, name=name, in_specs=[HBM_SPEC] * (2 * n) + [SEM_SPEC] * (6 * n) + [HBM_SPEC], out_specs=[HBM_SPEC] * (2 * n),
        out_shape=[pltpu.HBM(h.shape, h.dtype) for h in hs] * 2,
        input_output_aliases={a: a for a in range(2 * n)}, compiler_params=SPLIT_COPY,
    )(*hs, *lands, *sems, after)
    return res[:n], res[n:]


def _share_with_sibling(bufs, name):
    n = len(bufs)

    def body(*refs):
        outs = refs[n:2 * n]
        send_sems, recv_sems = refs[2 * n:]
        x, y, c, _ = _place()
        waits = []
        for a in range(n):
            mine = outs[a].at[c]
            cp = pltpu.make_async_remote_copy(src_ref=mine, dst_ref=mine, send_sem=send_sems.at[a], recv_sem=recv_sems.at[a],
                                              device_id=(x, y, 1 - c), device_id_type=MESH)
            cp.start()
            waits.append(cp.wait_send)
            got = outs[a].at[1 - c]
            waits.append(pltpu.make_async_remote_copy(src_ref=got, dst_ref=got, send_sem=send_sems.at[a], recv_sem=recv_sems.at[a],
                                                      device_id=(x, y, 1 - c), device_id_type=MESH).wait_recv)
        for w in waits:
            w()

    return pl.pallas_call(
        body, name=name, in_specs=[HBM_SPEC] * n, out_specs=[HBM_SPEC] * n,
        out_shape=[jax.ShapeDtypeStruct(b.shape, b.dtype) for b in bufs],
        input_output_aliases={a: a for a in range(n)},
        scratch_shapes=[pltpu.SemaphoreType.DMA((n,))] * 2,
    )(*bufs)


def _add_sibling_half(g, got, core, name):
    nc, r, cols = g.shape
    hr = r // 2
    tm = _pick(hr, 256, 16)

    def body(core_ref, g_ref, o_ref, s_ref):
        s_ref[...] = (g_ref[...].astype(F32) + o_ref[...].astype(F32)).astype(s_ref.dtype)

    return pl.pallas_call(
        body, name=name,
        grid_spec=pltpu.PrefetchScalarGridSpec(
            num_scalar_prefetch=1, grid=(nc, hr // tm),
            in_specs=[pl.BlockSpec((None, None, tm, cols), lambda j, i, cr: (j, cr[0], i, 0)),
                      pl.BlockSpec((None, tm, cols), lambda j, i, cr: (j, i, 0))],
            out_specs=pl.BlockSpec((None, tm, cols), lambda j, i, cr: (j, i, 0))),
        out_shape=jax.ShapeDtypeStruct((nc, hr, cols), g.dtype),
        compiler_params=_params(("parallel", "parallel")),
    )(core, g.reshape(nc, 2, hr, cols), got)


def _sum_chips(own, parts, pos, name):
    nc, r, cols = parts.shape
    tm = _pick(r, 256, 16)

    def body(x_ref, y_ref, c_ref, own_ref, p_ref, o_ref):
        chip = 2 * x_ref[0] + y_ref[0]
        acc = own_ref[...].astype(F32)
        for k in range(1, nc):
            acc = acc + p_ref[chip ^ k].astype(F32)
        o_ref[...] = acc

    return pl.pallas_call(
        body, name=name,
        grid_spec=pltpu.PrefetchScalarGridSpec(
            num_scalar_prefetch=3, grid=(r // tm,),
            in_specs=[pl.BlockSpec((None, tm, cols), lambda i, xr, yr, cr: (2 * xr[0] + yr[0], i, 0)),
                      pl.BlockSpec((nc, tm, cols), lambda i, xr, yr, cr: (0, i, 0))],
            out_specs=pl.BlockSpec((None, tm, cols), lambda i, xr, yr, cr: (cr[0], i, 0))),
        out_shape=jax.ShapeDtypeStruct((2, r, cols), F32),
        compiler_params=_params(("parallel",)),
    )(*pos, own, parts)


def _reduce_scatter_start(gs, pos, tag):
    got = _swap_with_sibling(gs, tag + "rs_swap")
    hs = [_add_sibling_half(g, o, pos[2], tag + "rs_add%d" % i) for i, (g, o) in enumerate(zip(gs, got))]
    hs, lands, sems, token = _scatter_start(hs, tag + "rs_scatter_start")
    return (hs, lands, sems), token


def _reduce_scatter_finish(pending, after, pos, tag):
    hs, lands, sems = pending
    hs, parts = _scatter_wait(hs, lands, sems, after, tag + "rs_scatter_wait")
    rs = [_sum_chips(h, p, pos, tag + "rs_sum%d" % i) for i, (h, p) in enumerate(zip(hs, parts))]
    both = _share_with_sibling(rs, tag + "rs_share")
    return [b.reshape(2 * b.shape[1], b.shape[2]) for b in both]


def _allreduce_small(v, name):
    rows = v.shape[0]

    def body(v_ref, o_ref, buf, send_sems, recv_sems, local_sem):
        x, y, c, _ = _place()
        me, sibling = (x, y, c), (x, y, 1 - c)
        chips = [_flip(x, y, k)[:2] for k in (1, 2, 3)]

        def slot(px, py, pc):
            return buf.at[4 * px + 2 * py + pc]

        def copy(k, block, to, src=None):
            return pltpu.make_async_remote_copy(src_ref=slot(*block) if src is None else src, dst_ref=slot(*block), send_sem=send_sems.at[k],
                                                recv_sem=recv_sems.at[k], device_id=to, device_id_type=MESH)

        mine = pltpu.make_async_copy(v_ref, slot(*me), local_sem)
        mine.start()
        first = [copy(0, me, sibling, src=v_ref)] + [copy(1 + j, me, (*chip, c), src=v_ref) for j, chip in enumerate(chips)]
        for cp in first:
            cp.start()
        passed = [copy(4 + j, (*chip, c), sibling) for j, chip in enumerate(chips)]
        for j, chip in enumerate(chips):
            copy(1 + j, (*chip, c), me).wait_recv()
            passed[j].start()
        copy(0, sibling, me).wait_recv()
        for j, chip in enumerate(chips):
            copy(4 + j, (*chip, 1 - c), me).wait_recv()
        for cp in first + passed:
            cp.wait_send()
        mine.wait()
        acc = buf[0]
        for i in range(1, 2 * N_CHIPS):
            acc = acc + buf[i]
        o_ref[...] = acc

    vm = pl.BlockSpec(memory_space=pltpu.VMEM)
    return pl.pallas_call(
        body, name=name, in_specs=[vm], out_specs=vm, out_shape=jax.ShapeDtypeStruct(v.shape, F32),
        scratch_shapes=[pltpu.VMEM((2 * N_CHIPS, rows, LANES), F32), pltpu.SemaphoreType.DMA((7,)), pltpu.SemaphoreType.DMA((7,)),
                        pltpu.SemaphoreType.DMA],
        compiler_params=pltpu.CompilerParams(vmem_limit_bytes=VMEM_LIMIT),
    )(v)


def _adamw(w, g, m, v, name):
    r, cols = w.shape
    tm = _pick(r, max(8, (1 << 18) // max(cols, 1) // 8 * 8), 8)

    def body(w_ref, g_ref, m_ref, v_ref, d_ref, nm_ref, nv_ref):
        gg = g_ref[...]
        nm = ADAM_B1 * m_ref[...] + (1.0 - ADAM_B1) * gg
        nv = ADAM_B2 * v_ref[...] + (1.0 - ADAM_B2) * jnp.square(gg)
        m_hat = nm / (1.0 - ADAM_B1 ** ADAM_STEP)
        v_hat = nv / (1.0 - ADAM_B2 ** ADAM_STEP)
        d_ref[...] = -ADAM_LR * (m_hat / (jnp.sqrt(v_hat) + ADAM_EPS) + ADAM_WD * w_ref[...])
        nm_ref[...] = nm
        nv_ref[...] = nv

    spec = pl.BlockSpec((tm, cols), lambda i: (i, 0))
    return pl.pallas_call(
        body, name=name, grid=(r // tm,), in_specs=[spec] * 4, out_specs=[spec] * 3,
        out_shape=[jax.ShapeDtypeStruct((r, cols), F32)] * 3, compiler_params=_params(("parallel",)),
    )(w, g, m, v)


def _adamw_layer(w, g, m, v, layer, prev, name):
    depth, r, cols = w.shape
    tm = _pick(r, max(8, (1 << 18) // max(cols, 1) // 8 * 8), 8)

    def body(*refs):
        w_ref, g_ref, m_ref, v_ref = refs[:4]
        go_ref, d_ref, nm_ref, nv_ref = refs[-4:]
        gg = g_ref[...]
        nm = ADAM_B1 * m_ref[...] + (1.0 - ADAM_B1) * gg
        nv = ADAM_B2 * v_ref[...] + (1.0 - ADAM_B2) * jnp.square(gg)
        m_hat = nm / (1.0 - ADAM_B1 ** ADAM_STEP)
        v_hat = nv / (1.0 - ADAM_B2 ** ADAM_STEP)
        go_ref[...] = gg
        d_ref[...] = -ADAM_LR * (m_hat / (jnp.sqrt(v_hat) + ADAM_EPS) + ADAM_WD * w_ref[...])
        nm_ref[...] = nm
        nv_ref[...] = nv

    lspec = pl.BlockSpec((None, tm, cols), lambda i: (layer, i, 0))
    gspec = pl.BlockSpec((tm, cols), lambda i: (i, 0))
    extra = [] if prev is None else list(prev)
    return pl.pallas_call(
        body, name=name, grid=(r // tm,), in_specs=[lspec, gspec, lspec, lspec] + [HBM_SPEC] * len(extra), out_specs=[lspec] * 4,
        out_shape=[jax.ShapeDtypeStruct((depth, r, cols), F32)] * 4,
        input_output_aliases={4 + j: j for j in range(len(extra))}, compiler_params=_params(("parallel",)),
    )(w, g, m, v, *extra)


def _adamw_nd(w, g, m, v, name):
    shape = w.shape
    two = (1, shape[0]) if len(shape) == 1 else (int(np.prod(shape[:-1])), shape[-1])
    outs = _adamw(w.reshape(two), g.reshape(two), m.reshape(two), v.reshape(two), name)
    return [o.reshape(shape) for o in outs]


WEIGHTS = ("norm1_g", "w_in", "conv_w", "a_log", "dt_bias", "gdn_norm_g", "attn_sinks", "w_branch_gdn", "w_branch_swa", "w_out",
           "norm2_g", "w_ff_up", "w_ff_down", "final_norm_g")
MATRICES = ("w_in", "w_branch_gdn", "w_branch_swa", "w_out", "w_ff_up", "w_ff_down")


def _to_rows(vec):
    n = vec.shape[0]
    rows = -(-n // (8 * LANES)) * 8
    return jnp.pad(vec, (0, rows * LANES - n)).reshape(rows, LANES)


def kernel(x, norm1_g, w_in, conv_w, a_log, dt_bias, gdn_norm_g, attn_sinks, w_branch_gdn, w_branch_swa, w_out, norm2_g, w_ff_up, w_ff_down, final_norm_g, loss_target, m_norm1_g, m_w_in, m_conv_w, m_a_log, m_dt_bias, m_gdn_norm_g, m_attn_sinks, m_w_branch_gdn, m_w_branch_swa, m_w_out, m_norm2_g, m_w_ff_up, m_w_ff_down, m_final_norm_g, v_norm1_g, v_w_in, v_conv_w, v_a_log, v_dt_bias, v_gdn_norm_g, v_attn_sinks, v_w_branch_gdn, v_w_branch_swa, v_w_out, v_norm2_g, v_w_ff_up, v_w_ff_down, v_final_norm_g):
    w = dict(norm1_g=norm1_g, w_in=w_in, conv_w=conv_w, a_log=a_log, dt_bias=dt_bias, gdn_norm_g=gdn_norm_g, attn_sinks=attn_sinks,
             w_branch_gdn=w_branch_gdn, w_branch_swa=w_branch_swa, w_out=w_out, norm2_g=norm2_g, w_ff_up=w_ff_up, w_ff_down=w_ff_down,
             final_norm_g=final_norm_g)
    mom = dict(norm1_g=m_norm1_g, w_in=m_w_in, conv_w=m_conv_w, a_log=m_a_log, dt_bias=m_dt_bias, gdn_norm_g=m_gdn_norm_g,
               attn_sinks=m_attn_sinks, w_branch_gdn=m_w_branch_gdn, w_branch_swa=m_w_branch_swa, w_out=m_w_out, norm2_g=m_norm2_g,
               w_ff_up=m_w_ff_up, w_ff_down=m_w_ff_down, final_norm_g=m_final_norm_g)
    var = dict(norm1_g=v_norm1_g, w_in=v_w_in, conv_w=v_conv_w, a_log=v_a_log, dt_bias=v_dt_bias, gdn_norm_g=v_gdn_norm_g,
               attn_sinks=v_attn_sinks, w_branch_gdn=v_w_branch_gdn, w_branch_swa=v_w_branch_swa, w_out=v_w_out, norm2_g=v_norm2_g,
               w_ff_up=v_w_ff_up, w_ff_down=v_w_ff_down, final_norm_g=v_final_norm_g)
    depth, d = norm1_g.shape
    xs, target = x[0], loss_target[0]
    core = lax.axis_index("c")
    chip = 2 * lax.axis_index("x") + lax.axis_index("y")
    pos = tuple(jnp.reshape(lax.axis_index(a), (1,)).astype(jnp.int32) for a in ("x", "y", "c"))

    cw = conv_w.shape[-1]
    placed = lax.dynamic_update_slice(jnp.zeros((depth, CONV_K, N_CHIPS * cw), F32), conv_w, (0, 0, chip * cw))
    placed = placed * (core == 0).astype(F32)
    conv_full = _allreduce_small(_to_rows(placed.reshape(-1)), "gather_conv_w")
    conv_full = conv_full.reshape(-1)[:depth * CONV_K * N_CHIPS * cw].reshape(depth, CONV_K, N_CHIPS * cw)

    alibi = _alibi_row(d)
    first_group, late_group = MATRICES[:1], MATRICES[1:]
    bufs = [{n: _cast_into_slot(w[n], l, pos, "l%d_cast_%s" % (l, n)) for n in MATRICES} for l in range(depth)]

    def start_gather(l, names, after, tag):
        group, _ = lax.optimization_barrier(([bufs[l][n] for n in names], after))
        return _gather_start(group, "l%d_gather_start_%s" % (l, tag))

    def finish_gather(l, names, started, after, tag):
        got = _gather_wait(started[0], started[1], after, "l%d_gather_wait_%s" % (l, tag))
        full = dict(zip(names, _gather_forward(got, "l%d_gather_forward_%s" % (l, tag))))
        out = {}
        if "w_in" in full:
            out["w_in"] = _pack_w_in(jnp.transpose(full["w_in"], (1, 0, 2)).reshape(d, -1), d)
        for n in ("w_branch_gdn", "w_branch_swa", "w_out", "w_ff_down"):
            if n in full:
                out[n] = full[n].reshape(-1, d)
        if "w_ff_up" in full:
            out["w_ff_up"] = full["w_ff_up"]
        return out

    def small_params(l):
        return dict(norm1_g=norm1_g[l][None], norm2_g=norm2_g[l][None], conv_w=conv_full[l], a_log=_pad_row(a_log[l]),
                    dt_bias=_pad_row(dt_bias[l]), gdn_norm_g=gdn_norm_g[l][None], attn_sinks=_pad_row(attn_sinks[l]), alibi=alibi)

    st_a = start_gather(0, first_group, conv_full, "a")
    st_b = start_gather(0, late_group, st_a[2], "b")
    layers = [dict(small_params(0), **finish_gather(0, first_group, st_a, st_b[2], "a"))]
    h = xs
    saved = []
    for l in range(depth):
        p = layers[l]
        late = None
        if l == 0:
            def late(after, st_b=st_b):
                rest = finish_gather(0, late_group, st_b, after, "b")
                layers[0].update(rest)
                return rest
        if l + 1 < depth:
            nxt = start_gather(l + 1, MATRICES, p["w_in"], "all")
            p = dict(p, norm1_g=p["norm1_g"] + nxt[2][:1, :1])
        h, s = _layer_fwd(h, p, "l%d_" % l, late)
        saved.append(s)
        if l + 1 < depth:
            layers.append(dict(small_params(l + 1), **finish_gather(l + 1, MATRICES, nxt, h, "all")))
    dh, dh_b, d_final, loss_row = _loss_head(h, final_norm_g[None], target, "loss_head")

    grads = {n: [None] * depth for n in ("norm1_g", "norm2_g", "a_log", "dt_bias", "gdn_norm_g", "attn_sinks", "conv_w")}
    updated = {n: None for n in MATRICES}
    small = ("norm1_g", "norm2_g", "a_log", "dt_bias", "gdn_norm_g", "attn_sinks", "conv_w")
    state = {"pending": None, "todo": None}

    def adamw_todo():
        if state["todo"] is not None:
            l, names, sums = state["todo"]
            for n, r in zip(names, sums):
                updated[n] = _adamw_layer(w[n], r, mom[n], var[n], l, updated[n], "l%d_adamw_%s" % (l, n))
            state["todo"] = None

    def finish_scatter(after):
        adamw_todo()
        if state["pending"] is not None:
            l, names, pending, tag = state["pending"]
            state["todo"] = (l, names, _reduce_scatter_finish(pending, after, pos, "l%d_%s_" % (l, tag)))
            state["pending"] = None

    def start_scatter(l, names, mats, tag):
        pending, token = _reduce_scatter_start(mats, pos, "l%d_%s_" % (l, tag))
        state["pending"] = (l, names, pending, tag)
        return token

    def stacked(g, n):
        return g[n] if n == "w_ff_up" else g[n].reshape(N_CHIPS, -1, g[n].shape[-1])

    for l in reversed(range(depth)):
        def mid(g, after, l=l):
            finish_scatter(after)
            return start_scatter(l, late_group, [stacked(g, n) for n in late_group], "b")

        dh, dh_b, g = _layer_bwd(dh, dh_b, layers[l], saved[l], "l%d_" % l, mid)
        for n in grads:
            grads[n][l] = g[n].reshape(-1)
        finish_scatter(dh)
        g_in = _unpack_w_in(g["w_in"], d)
        mats = [jnp.transpose(g_in.reshape(d, N_CHIPS, -1), (1, 0, 2))]
        if l == 0:
            pieces = [jnp.stack(grads[n]).reshape(-1) for n in small] + [d_final.reshape(-1), loss_row[0, :1]]
            sizes = [p.shape[0] for p in pieces]
            packed = _allreduce_small(_to_rows(jnp.concatenate(pieces)), "reduce_small").reshape(-1)
            mats, _ = lax.optimization_barrier((mats, packed))
        token = start_scatter(l, first_group, mats, "a")
        if l > 0:
            dh_b = dh_b + token[0, 0].astype(BF16)

    offs = np.concatenate([[0], np.cumsum(sizes)])
    red = {n: packed[offs[i]:offs[i + 1]] for i, n in enumerate(small + ("final_norm_g", "loss"))}
    loss = red["loss"][0]

    grad_out = {}
    for n in ("norm1_g", "norm2_g", "a_log", "dt_bias", "gdn_norm_g", "attn_sinks"):
        grad_out[n] = red[n].reshape(w[n].shape)
    grad_out["final_norm_g"] = red["final_norm_g"]
    conv_g = red["conv_w"].reshape(depth, CONV_K, N_CHIPS * cw)
    grad_out["conv_w"] = lax.dynamic_slice(conv_g, (0, 0, chip * cw), (depth, CONV_K, cw))

    delta, new_m, new_v = {}, {}, {}
    for n in grad_out:
        delta[n], new_m[n], new_v[n] = _adamw_nd(w[n], grad_out[n], mom[n], var[n], "adamw_" + n)
    adamw_todo()
    finish_scatter(updated["w_ff_down"][0])
    adamw_todo()
    for n in MATRICES:
        grad_out[n], delta[n], new_m[n], new_v[n] = updated[n]
    return (loss, dh[None], *[grad_out[n] for n in WEIGHTS], *[delta[n] for n in WEIGHTS], *[new_m[n] for n in WEIGHTS],
            *[new_v[n] for n in WEIGHTS])
```

```python
import functools

import jax
import jax.numpy as jnp
import numpy as np
from jax import lax
from jax.experimental import pallas as pl
from jax.experimental.pallas import tpu as pltpu

F32 = jnp.float32
BF16 = jnp.bfloat16

GDN_HEAD_DIM = 128
CHUNK = 64
SWA_HEAD_DIM = 64
WINDOW = 128
CONV_K = 4
GQA_GROUP = 8
NORM_EPS = 1e-6
N_CHIPS = 4
LANES = 128
CONV_HALO = 8
VMEM_LIMIT = 56 * 1024 * 1024

ADAM_LR = 0.001
ADAM_B1 = 0.9
ADAM_B2 = 0.999
ADAM_EPS = 1e-08
ADAM_WD = 0.01
ADAM_STEP = 10

NN = (((1,), (0,)), ((), ()))
NT = (((1,), (1,)), ((), ()))
TN = (((0,), (0,)), ((), ()))


def _pick(dim, cap, mult=LANES):
    if dim <= cap:
        return dim
    t = (cap // mult) * mult
    while t >= mult:
        if dim % t == 0:
            return t
        t -= mult
    return dim


def _params(sem):
    return pltpu.CompilerParams(dimension_semantics=sem, vmem_limit_bytes=VMEM_LIMIT)


def _bdot(a, b, dn):
    return lax.dot_general(a.astype(BF16), b.astype(BF16), dn, preferred_element_type=F32)


@jax.custom_vjp
def mm_nn(a, b):
    return _bdot(a, b, NN)


@jax.custom_vjp
def mm_nt(a, b):
    return _bdot(a, b, NT)


@jax.custom_vjp
def mm_tn(a, b):
    return _bdot(a, b, TN)


mm_nn.defvjp(lambda a, b: (_bdot(a, b, NN), (a, b)), lambda r, g: (_bdot(g, r[1], NT), _bdot(r[0], g, TN)))
mm_nt.defvjp(lambda a, b: (_bdot(a, b, NT), (a, b)), lambda r, g: (_bdot(g, r[1], NN), _bdot(g, r[0], TN)))
mm_tn.defvjp(lambda a, b: (_bdot(a, b, TN), (a, b)), lambda r, g: (_bdot(r[1], g, NT), _bdot(r[0], g, NN)))


def _hdot(a, b, dn=NN):
    return lax.dot_general(a, b, dn, precision=lax.Precision.HIGHEST, preferred_element_type=F32)


def _sigmoid(x):
    return 1.0 / (1.0 + jnp.exp(-x))


def _silu(x):
    return x * _sigmoid(x)


def _softplus(x):
    return jnp.maximum(x, 0.0) + jnp.log(1.0 + jnp.exp(-jnp.abs(x)))


def _lane_pick(row, lane, idx):
    return jnp.sum(jnp.where(lane == idx, row, 0.0), axis=1, keepdims=True)


def _matmul(a, b, *, ta=False, tb=False, out_dtype=F32, add=None, name, tm_cap=1024, tn_cap=1024, tk_cap=2048, b_split=1,
            out_split=1, epilogue=None, extras=(), also=None):
    m, k = (a.shape[1], a.shape[0]) if ta else a.shape
    b_rows, b_cols = (b.shape[-2], b.shape[-1] * b_split)
    n = b_rows if tb else b_cols
    assert k == (b_cols if tb else b_rows), (a.shape, b.shape, ta, tb)
    tm = _pick(m, tm_cap)
    tn = _pick(n // max(1 if tb else b_split, out_split), tn_cap)
    tk = _pick(k // (b_split if tb else 1), tk_cap)
    nk = k // tk
    dn = (((0 if ta else 1,), (1 if tb else 0,)), ((), ()))

    n_extra = len(extras)

    def body(*refs):
        a_ref, b_ref = refs[:2]
        add_ref = refs[2] if add is not None else None
        first = 2 + (add is not None)
        x_refs = refs[first:first + n_extra]
        o_ref = refs[first + n_extra]
        o2_ref = refs[first + n_extra + 1] if also is not None else None
        acc_ref = refs[-1]
        kk = pl.program_id(2)
        p = lax.dot_general(a_ref[...].astype(BF16), b_ref[...].astype(BF16), dn, preferred_element_type=F32)

        @pl.when(kk == 0)
        def _():
            acc_ref[...] = p

        @pl.when(kk > 0)
        def _():
            acc_ref[...] += p

        @pl.when(kk == nk - 1)
        def _():
            r = acc_ref[...]
            if add_ref is not None:
                r = r + add_ref[...].astype(F32)
            if also is not None:
                o2_ref[...] = also[0](r).astype(o2_ref.dtype)
            if epilogue is not None:
                r = epilogue(r, *[x[...] for x in x_refs])
            o_ref[...] = r.astype(o_ref.dtype)

    a_spec = pl.BlockSpec((tk, tm), lambda i, j, q: (q, i)) if ta else pl.BlockSpec((tm, tk), lambda i, j, q: (i, q))
    if b_split == 1:
        b_spec = pl.BlockSpec((tn, tk), lambda i, j, q: (j, q)) if tb else pl.BlockSpec((tk, tn), lambda i, j, q: (q, j))
    elif tb:
        per_b = k // b_split // tk
        b_spec = pl.BlockSpec((None, tn, tk), lambda i, j, q: (q // per_b, j, q % per_b))
    else:
        per_b = n // b_split // tn
        b_spec = pl.BlockSpec((None, tk, tn), lambda i, j, q: (j // per_b, q, j % per_b))
    add_spec = pl.BlockSpec((tm, tn), lambda i, j, q: (i, j))
    if out_split == 1:
        o_spec, o_shape = add_spec, (m, n)
    else:
        per_o = n // out_split // tn
        o_spec, o_shape = pl.BlockSpec((None, tm, tn), lambda i, j, q: (j // per_o, i, j % per_o)), (out_split, m, n // out_split)
    in_specs = [a_spec, b_spec] + ([add_spec] if add is not None else []) + [add_spec] * n_extra
    args = (a, b) + ((add,) if add is not None else ()) + tuple(extras)
    out_specs, out_shape = o_spec, jax.ShapeDtypeStruct(o_shape, out_dtype)
    if also is not None:
        out_specs, out_shape = [o_spec, o_spec], [out_shape, jax.ShapeDtypeStruct(o_shape, also[1])]
    return pl.pallas_call(
        body, name=name, grid=(m // tm, n // tn, nk), in_specs=in_specs, out_specs=out_specs,
        out_shape=out_shape, scratch_shapes=[pltpu.VMEM((tm, tn), F32)],
        compiler_params=_params(("parallel", "parallel", "arbitrary")),
    )(*args)


def _rows(fn, row_args, full_args, row_outs, acc_outs, *, t, tm, name):
    n_row, n_full, n_ro = len(row_args), len(full_args), len(row_outs)

    def body(*refs):
        ins = [r[...] for r in refs[:n_row + n_full]]
        outs = fn(*ins)
        o_refs = refs[n_row + n_full:]
        for r, v in zip(o_refs[:n_ro], outs[:n_ro]):
            r[...] = v.astype(r.dtype)
        i = pl.program_id(0)
        for r, v in zip(o_refs[n_ro:], outs[n_ro:]):
            @pl.when(i == 0)
            def _(r=r, v=v):
                r[...] = v

            @pl.when(i > 0)
            def _(r=r, v=v):
                r[...] += v

    in_specs = [pl.BlockSpec((tm, w), functools.partial(lambda i, cb: (i, cb), cb=cb)) for (_, w, cb) in row_args]
    in_specs += [pl.BlockSpec(f.shape, lambda i: (0, 0)) for f in full_args]
    out_specs = [pl.BlockSpec((tm, w), lambda i: (i, 0)) for (w, _) in row_outs]
    out_specs += [pl.BlockSpec(s, lambda i: (0, 0)) for s in acc_outs]
    out_shape = [jax.ShapeDtypeStruct((t, w), d) for (w, d) in row_outs]
    out_shape += [jax.ShapeDtypeStruct(s, F32) for s in acc_outs]
    return pl.pallas_call(
        body, name=name, grid=(t // tm,), in_specs=in_specs, out_specs=out_specs, out_shape=out_shape,
        compiler_params=_params(("arbitrary",)),
    )(*[a for (a, _, _) in row_args], *full_args)


def _rms(x, g):
    return x * lax.rsqrt(jnp.mean(x * x, axis=-1, keepdims=True) + NORM_EPS) * g


def _rmsnorm_fwd(x, g, name):
    t, d = x.shape
    (h,) = _rows(lambda xb, gb: (_rms(xb, gb),), [(x, d, 0)], [g], [(d, BF16)], [], t=t, tm=_pick(t, 512, 8), name=name)
    return h


def _rmsnorm_bwd(x, g, dh, dx_in, name):
    t, d = x.shape

    def fn(xb, dhb, dxb, gb):
        _, vjp = jax.vjp(_rms, xb, gb)
        dx, dg = vjp(dhb)
        return dxb + dx, dxb + dx, dg

    return _rows(fn, [(x, d, 0), (dh, d, 0), (dx_in, d, 0)], [g], [(d, F32), (d, BF16)], [(1, d)], t=t, tm=_pick(t, 256, 8), name=name)


def _merge(yg, ys, lg, ls):
    return _sigmoid(lg) * yg + _sigmoid(ls) * ys


def _merge_fwd(y_gdn, y_swa, proj, gate_off, name):
    t, d = y_gdn.shape
    cb = gate_off // d
    (mix,) = _rows(lambda a, b, c, e: (_merge(a, b, c, e),), [(y_gdn, d, 0), (y_swa, d, 0), (proj, d, cb), (proj, d, cb + 1)], [],
                   [(d, BF16)], [], t=t, tm=_pick(t, 256, 8), name=name)
    return mix


def _merge_bwd(y_gdn, y_swa, proj, gate_off, dmix, name):
    t, d = y_gdn.shape
    cb = gate_off // d

    def fn(a, b, c, e, g):
        _, vjp = jax.vjp(_merge, a, b, c, e)
        da, db, dc, de = vjp(g)
        return da, db, jnp.concatenate([dc, de], axis=1)

    return _rows(fn, [(y_gdn, d, 0), (y_swa, d, 0), (proj, d, cb), (proj, d, cb + 1), (dmix, d, 0)], [],
                 [(d, BF16), (d, BF16), (2 * d, BF16)], [], t=t, tm=_pick(t, 128, 8), name=name)


def _assemble(pieces, out_dtype, name):
    t = pieces[0].shape[0]
    widths = [p.shape[1] for p in pieces]
    offs = np.concatenate([[0], np.cumsum(widths)])

    def fn(*blocks):
        return (jnp.concatenate([b.astype(out_dtype) for b in blocks], axis=1),)

    (out,) = _rows(fn, [(p, w, 0) for p, w in zip(pieces, widths)], [], [(int(offs[-1]), out_dtype)], [], t=t, tm=_pick(t, 128, 16),
                   name=name)
    return out


def _loss_head(x, g, target, name):
    t, d = x.shape

    def loss_fn(xb, gb, tb):
        err = _rms(xb, gb) - tb
        return 0.5 * jnp.sum(jnp.mean(err * err, axis=-1, keepdims=True), axis=0, keepdims=True)

    def fn(xb, tb, gb):
        lv, vjp = jax.vjp(lambda a, b: loss_fn(a, b, tb), xb, gb)
        dx, dg = vjp(jnp.ones((1, 1), F32))
        return dx, dx, dg, jnp.broadcast_to(lv, (1, LANES))

    return _rows(fn, [(x, d, 0), (target, d, 0)], [g], [(d, F32), (d, BF16)], [(1, d), (1, LANES)], t=t, tm=_pick(t, 256, 8), name=name)


def _conv_silu(prev, cur, w, keep_prev):
    tm = cur.shape[0]
    xp = jnp.concatenate([prev * keep_prev, cur], axis=0)
    y = w[0:1, :] * xp[CONV_HALO - 3:CONV_HALO - 3 + tm]
    for j in range(1, CONV_K):
        y = y + w[j:j + 1, :] * xp[CONV_HALO - 3 + j:CONV_HALO - 3 + j + tm]
    return _silu(y)


def _conv_tiles(t, width):
    tm = _pick(t, 512, CONV_HALO)
    tc = _pick(width, 512)
    return tm, tc, t // tm, width // tc


def _conv_fwd(proj, conv_w, width, name):
    t = proj.shape[0]
    tm, tc, nt, ncw = _conv_tiles(t, width)
    hb = tm // CONV_HALO

    def body(prev_ref, cur_ref, w_ref, o_ref):
        keep = (pl.program_id(1) > 0).astype(F32)
        o_ref[0] = _conv_silu(prev_ref[...], cur_ref[...], w_ref[...], keep)

    return pl.pallas_call(
        body, name=name, grid=(3 * ncw, nt),
        in_specs=[pl.BlockSpec((CONV_HALO, tc), lambda j, i: (jnp.maximum(i * hb - 1, 0), j)),
                  pl.BlockSpec((tm, tc), lambda j, i: (i, j)),
                  pl.BlockSpec((CONV_K, tc), lambda j, i: (0, j))],
        out_specs=pl.BlockSpec((1, tm, tc), lambda j, i: (j // ncw, i, j % ncw)),
        out_shape=jax.ShapeDtypeStruct((3, t, width), F32),
        compiler_params=_params(("parallel", "arbitrary")),
    )(proj, proj, conv_w)


def _conv_bwd(proj, conv_w, dout, width, name):
    t = proj.shape[0]
    tm, tc, nt, ncw = _conv_tiles(t, width)
    hb = tm // CONV_HALO

    def body(prev_ref, cur_ref, w_ref, g_ref, dx_ref, dw_ref, carry_ref):
        s = pl.program_id(1)
        keep = (s < nt - 1).astype(F32)
        _, vjp = jax.vjp(lambda p, c, w: _conv_silu(p, c, w, keep), prev_ref[...], cur_ref[...], w_ref[...])
        dprev, dcur, dw = vjp(g_ref[0])

        @pl.when(s == 0)
        def _():
            carry_ref[...] = jnp.zeros_like(carry_ref)
            dw_ref[...] = dw

        @pl.when(s > 0)
        def _():
            dw_ref[...] += dw

        tail = jnp.concatenate([jnp.zeros((tm - CONV_HALO, tc), F32), carry_ref[...]], axis=0)
        dx_ref[...] = (dcur + tail).astype(dx_ref.dtype)
        carry_ref[...] = dprev

    def row(s):
        return nt - 1 - s

    return pl.pallas_call(
        body, name=name, grid=(3 * ncw, nt),
        in_specs=[pl.BlockSpec((CONV_HALO, tc), lambda j, s: (jnp.maximum(row(s) * hb - 1, 0), j)),
                  pl.BlockSpec((tm, tc), lambda j, s: (row(s), j)),
                  pl.BlockSpec((CONV_K, tc), lambda j, s: (0, j)),
                  pl.BlockSpec((1, tm, tc), lambda j, s: (j // ncw, row(s), j % ncw))],
        out_specs=[pl.BlockSpec((tm, tc), lambda j, s: (row(s), j)),
                   pl.BlockSpec((CONV_K, tc), lambda j, s: (0, j))],
        out_shape=[jax.ShapeDtypeStruct((t, 3 * width), BF16), jax.ShapeDtypeStruct((CONV_K, 3 * width), F32)],
        scratch_shapes=[pltpu.VMEM((CONV_HALO, tc), F32)],
        compiler_params=_params(("parallel", "arbitrary")),
    )(proj, proj, conv_w, dout)


def _inv_unit_lower_raw(mats):
    n = mats[0].shape[0]
    r = lax.broadcasted_iota(jnp.int32, (n, n), 0)
    c = lax.broadcasted_iota(jnp.int32, (n, n), 1)
    eye = (r == c).astype(F32)
    same = jnp.right_shift(r, 4) == jnp.right_shift(c, 4)
    dg = [jnp.where(same, a, 0.0) for a in mats]
    lo = [a - d for a, d in zip(mats, dg)]
    p = [eye - d for d in dg]
    q = dg
    for _ in range(3):
        q = [_hdot(x, x) for x in q]
        p = [_hdot(x, eye + y) for x, y in zip(p, q)]
    nm = [_hdot(x, y) for x, y in zip(p, lo)]
    n2 = [_hdot(x, x) for x in nm]
    left = [_hdot(eye - x, eye + y) for x, y in zip(nm, n2)]
    return [_hdot(x, y) for x, y in zip(left, p)]


@jax.custom_vjp
def _inv_unit_lower(mats):
    return _inv_unit_lower_raw(mats)


def _inv_fwd(mats):
    t = _inv_unit_lower_raw(mats)
    return t, t


def _inv_bwd(ts, gs):
    x = [_hdot(t, g, TN) for t, g in zip(ts, gs)]
    return ([-_hdot(a, t, NT) for a, t in zip(x, ts)],)


_inv_unit_lower.defvjp(_inv_fwd, _inv_bwd)


def _l2n(x):
    return x * lax.rsqrt(jnp.sum(x * x, axis=-1, keepdims=True) + NORM_EPS)


def _gdn_chunk(qcs, kcs, vcs, zs, bg, alog_row, dtb_row, gnorm, states, first_head, n_heads):
    nb = len(qcs)
    hs = range(nb)
    cs = qcs[0].shape[0]
    lane = lax.broadcasted_iota(jnp.int32, (1, LANES), 1)
    r = lax.broadcasted_iota(jnp.int32, (cs, cs), 0)
    c = lax.broadcasted_iota(jnp.int32, (cs, cs), 1)
    q = [_l2n(x) * (GDN_HEAD_DIM ** -0.5) for x in qcs]
    k = [_l2n(x) for x in kcs]
    beta = [_sigmoid(_lane_pick(bg, lane, first_head + i)) for i in hs]
    g = [-jnp.exp(_lane_pick(alog_row, lane, first_head + i)) *
         _softplus(_lane_pick(bg, lane, n_heads + first_head + i) + _lane_pick(dtb_row, lane, first_head + i)) for i in hs]
    g_row = [jnp.sum(jnp.where(r == c, x, 0.0), axis=0, keepdims=True) for x in g]
    dec_col = [jnp.sum(jnp.where(r >= c, x, 0.0), axis=1, keepdims=True) for x in g_row]
    dec_row = [jnp.sum(jnp.where(r <= c, x, 0.0), axis=0, keepdims=True) for x in g]
    gamma = [jnp.exp(jnp.where(r >= c, dc - dr, -1e30)) for dc, dr in zip(dec_col, dec_row)]
    kb = [x * b for x, b in zip(k, beta)]
    a = [jnp.where(r > c, mm_nt(x, y) * gm, 0.0) for x, y, gm in zip(kb, k, gamma)]
    tinv = _inv_unit_lower(a)
    e_col = [jnp.exp(x) for x in dec_col]
    u = [mm_nn(t, v * b) for t, v, b in zip(tinv, vcs, beta)]
    w = [mm_nn(t, x * e) for t, x, e in zip(tinv, kb, e_col)]
    qk = [mm_nt(x, y) * gm for x, y, gm in zip(q, k, gamma)]
    total = [jnp.sum(x, axis=0, keepdims=True) for x in g]
    v_new = [x - mm_nn(y, s) for x, y, s in zip(u, w, states)]
    o = [mm_nn(x * e, s) + mm_nn(y, v) for x, e, s, y, v in zip(q, e_col, states, qk, v_new)]
    new_states = [s * jnp.exp(tt) + mm_tn(x * jnp.exp(tt - dc), v) for s, tt, x, dc, v in zip(states, total, k, dec_col, v_new)]
    ys = [_rms(x, gnorm) * _silu(z) for x, z in zip(o, zs)]
    return ys, new_states


GDN_HEADS_FWD = 16
GDN_HEADS_BWD = 16


def _gdn_fwd(qkvc, proj, alog_row, dtb_row, gnorm, *, d, z_off, bg_off, name, hb=GDN_HEADS_FWD):
    t = qkvc.shape[1]
    nh = d // GDN_HEAD_DIM
    hb = min(hb, nh)
    wb = hb * GDN_HEAD_DIM
    nc = t // CHUNK
    ng = nh // hb

    def body(qkv_ref, z_ref, bg_ref, al_ref, dt_ref, gn_ref, y_ref, sin_ref, s_scr):
        n, hg = pl.program_id(0), pl.program_id(1)

        @pl.when(n == 0)
        def _():
            s_scr[hg] = jnp.zeros((hb, GDN_HEAD_DIM, GDN_HEAD_DIM), F32)

        states = [s_scr[hg, i] for i in range(hb)]
        bg, al, dt, gn = bg_ref[...], al_ref[...], dt_ref[...], gn_ref[...]
        sls = [slice(i * GDN_HEAD_DIM, (i + 1) * GDN_HEAD_DIM) for i in range(hb)]
        ys, new_states = _gdn_chunk([qkv_ref[0, :, sl] for sl in sls], [qkv_ref[1, :, sl] for sl in sls], [qkv_ref[2, :, sl] for sl in sls],
                                    [z_ref[:, sl] for sl in sls], bg, al, dt, gn, states, hg * hb, nh)
        for i in range(hb):
            sin_ref[0, i] = states[i]
            y_ref[:, sls[i]] = ys[i].astype(y_ref.dtype)
            s_scr[hg, i] = new_states[i]

    row = lambda n, hg: (0, 0)
    return pl.pallas_call(
        body, name=name, grid=(nc, ng),
        in_specs=[pl.BlockSpec((3, CHUNK, wb), lambda n, hg: (0, n, hg)),
                  pl.BlockSpec((CHUNK, wb), lambda n, hg: (n, z_off // wb + hg)),
                  pl.BlockSpec((CHUNK, LANES), lambda n, hg: (n, bg_off // LANES)),
                  pl.BlockSpec((1, LANES), row), pl.BlockSpec((1, LANES), row), pl.BlockSpec((1, LANES), row)],
        out_specs=[pl.BlockSpec((CHUNK, wb), lambda n, hg: (n, hg)),
                   pl.BlockSpec((1, hb, GDN_HEAD_DIM, GDN_HEAD_DIM), lambda n, hg: (n, hg, 0, 0))],
        out_shape=[jax.ShapeDtypeStruct((t, d), BF16), jax.ShapeDtypeStruct((nc, nh, GDN_HEAD_DIM, GDN_HEAD_DIM), F32)],
        scratch_shapes=[pltpu.VMEM((ng, hb, GDN_HEAD_DIM, GDN_HEAD_DIM), F32)],
        compiler_params=_params(("arbitrary", "arbitrary")),
    )(qkvc, proj, proj, alog_row, dtb_row, gnorm)


def _gdn_bwd(qkvc, proj, alog_row, dtb_row, gnorm, states, dy, *, d, z_off, bg_off, name, hb=GDN_HEADS_BWD):
    t = qkvc.shape[1]
    nh = d // GDN_HEAD_DIM
    hb = min(hb, nh)
    wb = hb * GDN_HEAD_DIM
    nc = t // CHUNK
    ng = nh // hb

    def body(qkv_ref, z_ref, bg_ref, al_ref, dt_ref, gn_ref, sin_ref, dy_ref, dqkv_ref, dz_ref, dbg_ref, dal_ref, ddt_ref, dgn_ref,
             ds_scr):
        s, hg = pl.program_id(0), pl.program_id(1)

        @pl.when(s == 0)
        def _():
            ds_scr[hg] = jnp.zeros((hb, GDN_HEAD_DIM, GDN_HEAD_DIM), F32)

        @pl.when(hg == 0)
        def _():
            dbg_ref[...] = jnp.zeros_like(dbg_ref)

        @pl.when((s == 0) & (hg == 0))
        def _():
            dal_ref[...] = jnp.zeros_like(dal_ref)
            ddt_ref[...] = jnp.zeros_like(ddt_ref)
            dgn_ref[...] = jnp.zeros_like(dgn_ref)

        dstates = [ds_scr[hg, i] for i in range(hb)]
        bg, al, dt, gn = bg_ref[...], al_ref[...], dt_ref[...], gn_ref[...]
        sls = [slice(i * GDN_HEAD_DIM, (i + 1) * GDN_HEAD_DIM) for i in range(hb)]
        fn = functools.partial(_gdn_chunk, first_head=hg * hb, n_heads=nh)
        _, vjp = jax.vjp(fn, [qkv_ref[0, :, sl] for sl in sls], [qkv_ref[1, :, sl] for sl in sls], [qkv_ref[2, :, sl] for sl in sls],
                         [z_ref[:, sl] for sl in sls], bg, al, dt, gn, [sin_ref[0, i] for i in range(hb)])
        dq, dk, dv, dz, dbg, dal, ddt, dgn, dst = vjp(([dy_ref[:, sl] for sl in sls], dstates))
        for i in range(hb):
            dqkv_ref[0, :, sls[i]] = dq[i]
            dqkv_ref[1, :, sls[i]] = dk[i]
            dqkv_ref[2, :, sls[i]] = dv[i]
            dz_ref[:, sls[i]] = dz[i].astype(dz_ref.dtype)
            ds_scr[hg, i] = dst[i]
        dbg_ref[...] += dbg
        dal_ref[...] += dal
        ddt_ref[...] += ddt
        dgn_ref[...] += dgn

    def ch(s):
        return nc - 1 - s

    row = lambda s, hg: (0, 0)
    return pl.pallas_call(
        body, name=name, grid=(nc, ng),
        in_specs=[pl.BlockSpec((3, CHUNK, wb), lambda s, hg: (0, ch(s), hg)),
                  pl.BlockSpec((CHUNK, wb), lambda s, hg: (ch(s), z_off // wb + hg)),
                  pl.BlockSpec((CHUNK, LANES), lambda s, hg: (ch(s), bg_off // LANES)),
                  pl.BlockSpec((1, LANES), row), pl.BlockSpec((1, LANES), row), pl.BlockSpec((1, LANES), row),
                  pl.BlockSpec((1, hb, GDN_HEAD_DIM, GDN_HEAD_DIM), lambda s, hg: (ch(s), hg, 0, 0)),
                  pl.BlockSpec((CHUNK, wb), lambda s, hg: (ch(s), hg))],
        out_specs=[pl.BlockSpec((3, CHUNK, wb), lambda s, hg: (0, ch(s), hg)),
                   pl.BlockSpec((CHUNK, wb), lambda s, hg: (ch(s), hg)),
                   pl.BlockSpec((CHUNK, LANES), lambda s, hg: (ch(s), 0)),
                   pl.BlockSpec((1, LANES), row), pl.BlockSpec((1, LANES), row), pl.BlockSpec((1, LANES), row)],
        out_shape=[jax.ShapeDtypeStruct((3, t, d), F32), jax.ShapeDtypeStruct((t, d), BF16), jax.ShapeDtypeStruct((t, LANES), F32),
                   jax.ShapeDtypeStruct((1, LANES), F32), jax.ShapeDtypeStruct((1, LANES), F32), jax.ShapeDtypeStruct((1, LANES), F32)],
        scratch_shapes=[pltpu.VMEM((ng, hb, GDN_HEAD_DIM, GDN_HEAD_DIM), F32)],
        compiler_params=_params(("arbitrary", "arbitrary")),
    )(qkvc, proj, proj, alog_row, dtb_row, gnorm, states, dy)


@jax.custom_vjp
def _swap_halves(x):
    return pltpu.roll(x, SWA_HEAD_DIM, 1)


_swap_halves.defvjp(lambda x: (pltpu.roll(x, SWA_HEAD_DIM, 1), None), lambda _, g: (pltpu.roll(g, SWA_HEAD_DIM, 1),))

SWA_PAIR_Q = 2 * GQA_GROUP * SWA_HEAD_DIM


def _swa_block(q, kp, kc, vp, vc, sink_row, slope_row, keep_prev, pair):
    kb = jnp.concatenate([kp, kc], axis=0)
    vb = jnp.concatenate([vp, vc], axis=0)
    lane = lax.broadcasted_iota(jnp.int32, (1, LANES), 1)
    low = lane < SWA_HEAD_DIM
    high = jnp.logical_not(low)
    qi = lax.broadcasted_iota(jnp.int32, (WINDOW, 2 * WINDOW), 0)
    sj = lax.broadcasted_iota(jnp.int32, (WINDOW, 2 * WINDOW), 1)
    dist = qi + WINDOW - sj
    valid = (dist >= 0) & (dist < WINDOW) & ((sj >= WINDOW) | (keep_prev > 0.5))
    distf = dist.astype(F32)
    kk, vv = [], []
    for mine in (low, high):
        x = jnp.where(mine, kb, 0.0)
        kk.append(x + _swap_halves(x))
        y = jnp.where(mine, vb, 0.0)
        vv.append(y + _swap_halves(y))
    hl = range(2 * GQA_GROUP)
    half = [low if h % 2 == 0 else high for h in hl]
    slope = [_lane_pick(slope_row, lane, pair * (2 * GQA_GROUP) + h) for h in hl]
    sink = [_lane_pick(sink_row, lane, pair * (2 * GQA_GROUP) + h) for h in hl]
    qm = [jnp.where(half[h], q[:, (h // 2) * LANES:(h // 2 + 1) * LANES], 0.0) for h in hl]
    sc = [mm_nt(qm[h], kk[h // GQA_GROUP]) * (SWA_HEAD_DIM ** -0.5) for h in hl]
    sc = [jnp.where(valid, sc[h] - slope[h] * distf, -1e30) for h in hl]
    m = [lax.stop_gradient(jnp.maximum(jnp.max(sc[h], axis=-1, keepdims=True), sink[h])) for h in hl]
    p = [jnp.exp(sc[h] - m[h]) for h in hl]
    probs = [p[h] / (jnp.sum(p[h], axis=-1, keepdims=True) + jnp.exp(sink[h] - m[h])) for h in hl]
    od = [jnp.where(half[h], mm_nn(probs[h], vv[h // GQA_GROUP]), 0.0) for h in hl]
    return jnp.concatenate([od[2 * i] + od[2 * i + 1] for i in range(GQA_GROUP)], axis=1)


def _swa_specs(t, q_off, k_off, v_off, order):
    nb = t // WINDOW

    def blk(s):
        return order(s, nb)

    return nb, [pl.BlockSpec((WINDOW, SWA_PAIR_Q), lambda p, s: (blk(s), q_off // SWA_PAIR_Q + p)),
                pl.BlockSpec((WINDOW, LANES), lambda p, s: (jnp.maximum(blk(s) - 1, 0), k_off // LANES + p)),
                pl.BlockSpec((WINDOW, LANES), lambda p, s: (blk(s), k_off // LANES + p)),
                pl.BlockSpec((WINDOW, LANES), lambda p, s: (jnp.maximum(blk(s) - 1, 0), v_off // LANES + p)),
                pl.BlockSpec((WINDOW, LANES), lambda p, s: (blk(s), v_off // LANES + p)),
                pl.BlockSpec((1, LANES), lambda p, s: (0, 0)), pl.BlockSpec((1, LANES), lambda p, s: (0, 0))]


def _swa_fwd(proj, sink_row, slope_row, *, d, q_off, k_off, v_off, name):
    t = proj.shape[0]
    n_pairs = d // SWA_PAIR_Q
    nb, in_specs = _swa_specs(t, q_off, k_off, v_off, lambda s, nb: s)

    def body(q_ref, kp_ref, kc_ref, vp_ref, vc_ref, sink_ref, slope_ref, o_ref):
        keep = (pl.program_id(1) > 0).astype(F32)
        o = _swa_block(q_ref[...], kp_ref[...], kc_ref[...], vp_ref[...], vc_ref[...], sink_ref[...], slope_ref[...], keep,
                       pl.program_id(0))
        o_ref[...] = o.astype(o_ref.dtype)

    return pl.pallas_call(
        body, name=name, grid=(n_pairs, nb), in_specs=in_specs,
        out_specs=pl.BlockSpec((WINDOW, SWA_PAIR_Q), lambda p, s: (s, p)),
        out_shape=jax.ShapeDtypeStruct((t, d), BF16),
        compiler_params=_params(("parallel", "arbitrary")),
    )(proj, proj, proj, proj, proj, sink_row, slope_row)


def _swa_bwd(proj, sink_row, slope_row, do, *, d, q_off, k_off, v_off, name):
    t = proj.shape[0]
    n_pairs = d // SWA_PAIR_Q
    nb, in_specs = _swa_specs(t, q_off, k_off, v_off, lambda s, nb: nb - 1 - s)

    def body(q_ref, kp_ref, kc_ref, vp_ref, vc_ref, sink_ref, slope_ref, do_ref, dq_ref, dk_ref, dv_ref, dsink_ref, ck_ref, cv_ref):
        p, s = pl.program_id(0), pl.program_id(1)
        keep = (s < nb - 1).astype(F32)
        fn = functools.partial(_swa_block, slope_row=slope_ref[...], keep_prev=keep, pair=p)
        _, vjp = jax.vjp(fn, q_ref[...], kp_ref[...], kc_ref[...], vp_ref[...], vc_ref[...], sink_ref[...])
        dq, dkp, dkc, dvp, dvc, dsink = vjp(do_ref[...])

        @pl.when(s == 0)
        def _():
            ck_ref[...] = jnp.zeros_like(ck_ref)
            cv_ref[...] = jnp.zeros_like(cv_ref)

        @pl.when((s == 0) & (p == 0))
        def _():
            dsink_ref[...] = jnp.zeros_like(dsink_ref)

        dq_ref[...] = dq.astype(dq_ref.dtype)
        dk_ref[...] = (dkc + ck_ref[...]).astype(dk_ref.dtype)
        dv_ref[...] = (dvc + cv_ref[...]).astype(dv_ref.dtype)
        ck_ref[...] = dkp
        cv_ref[...] = dvp
        dsink_ref[...] += dsink

    in_specs = in_specs + [pl.BlockSpec((WINDOW, SWA_PAIR_Q), lambda p, s: (nb - 1 - s, p))]
    kv_w = d // GQA_GROUP
    return pl.pallas_call(
        body, name=name, grid=(n_pairs, nb), in_specs=in_specs,
        out_specs=[pl.BlockSpec((WINDOW, SWA_PAIR_Q), lambda p, s: (nb - 1 - s, p)),
                   pl.BlockSpec((WINDOW, LANES), lambda p, s: (nb - 1 - s, p)),
                   pl.BlockSpec((WINDOW, LANES), lambda p, s: (nb - 1 - s, p)),
                   pl.BlockSpec((1, LANES), lambda p, s: (0, 0))],
        out_shape=[jax.ShapeDtypeStruct((t, d), BF16), jax.ShapeDtypeStruct((t, kv_w), BF16), jax.ShapeDtypeStruct((t, kv_w), BF16),
                   jax.ShapeDtypeStruct((1, LANES), F32)],
        scratch_shapes=[pltpu.VMEM((WINDOW, LANES), F32), pltpu.VMEM((WINDOW, LANES), F32)],
        compiler_params=_params(("arbitrary", "arbitrary")),
    )(proj, proj, proj, proj, proj, sink_row, slope_row, do)


def _layout(d):
    kv = d // GQA_GROUP
    return dict(z=3 * d, q=4 * d, gate=5 * d, k=7 * d, v=7 * d + kv, bg=7 * d + 2 * kv, width=7 * d + 2 * kv + LANES)


def _pack_w_in(w, d):
    nh = d // GDN_HEAD_DIM
    kv = d // GQA_GROUP
    o = 4 * d + 2 * nh
    parts = [w[..., :4 * d], w[..., o:o + d], w[..., o + d + 2 * kv:o + 3 * d + 2 * kv], w[..., o + d:o + d + 2 * kv],
             w[..., 4 * d:o], jnp.zeros(w.shape[:-1] + (LANES - 2 * nh,), w.dtype)]
    return jnp.concatenate(parts, axis=-1)


def _unpack_w_in(wp, d):
    nh = d // GDN_HEAD_DIM
    kv = d // GQA_GROUP
    lay = _layout(d)
    parts = [wp[..., :4 * d], wp[..., lay["bg"]:lay["bg"] + 2 * nh], wp[..., lay["q"]:lay["q"] + d],
             wp[..., lay["k"]:lay["k"] + 2 * kv], wp[..., lay["gate"]:lay["gate"] + 2 * d]]
    return jnp.concatenate(parts, axis=-1)


def _pad_row(v):
    return jnp.pad(v.astype(F32), (0, LANES - v.shape[0]))[None, :]


def _alibi_row(d):
    nq = d // SWA_HEAD_DIM
    return _pad_row(2.0 ** (-8.0 * jnp.arange(1, nq + 1, dtype=F32) / nq))


def _layer_fwd(x, p, tag, late=None):
    t, d = x.shape
    lay = _layout(d)
    tn = 1152 if lay["width"] % 1152 == 0 else 1024
    h1 = _rmsnorm_fwd(x, p["norm1_g"], tag + "rms1")
    proj = _matmul(h1, p["w_in"], name=tag + "mm_in", tn_cap=tn)
    qkvc = _conv_fwd(proj, p["conv_w"], d, tag + "conv")
    gdn_o, states = _gdn_fwd(qkvc, proj, p["a_log"], p["dt_bias"], p["gdn_norm_g"], d=d, z_off=lay["z"], bg_off=lay["bg"],
                             name=tag + "gdn")
    swa_o = _swa_fwd(proj, p["attn_sinks"], p["alibi"], d=d, q_off=lay["q"], k_off=lay["k"], v_off=lay["v"], name=tag + "swa")
    if late is not None:
        p = dict(p, **late(swa_o))
    y_gdn = _matmul(gdn_o, p["w_branch_gdn"], name=tag + "mm_bg")
    y_swa = _matmul(swa_o, p["w_branch_swa"], name=tag + "mm_bs")
    mix = _merge_fwd(y_gdn, y_swa, proj, lay["gate"], tag + "merge")
    x1 = _matmul(mix, p["w_out"], add=x, name=tag + "mm_out")
    h2 = _rmsnorm_fwd(x1, p["norm2_g"], tag + "rms2")
    up, act = _matmul(h2, p["w_ff_up"], name=tag + "mm_up", b_split=N_CHIPS, also=(lambda u: jnp.square(jnp.maximum(u, 0.0)), BF16))
    x2 = _matmul(act, p["w_ff_down"], add=x1, name=tag + "mm_down")
    return x2, dict(x=x, h1=h1, proj=proj, qkvc=qkvc, states=states, gdn_o=gdn_o, swa_o=swa_o, y_gdn=y_gdn, y_swa=y_swa, mix=mix,
                    x1=x1, h2=h2, up=up, act=act)


def _layer_bwd(dx2, dx2_b, p, s, tag, mid=None):
    t, d = dx2.shape
    lay = _layout(d)
    tn = 1152 if lay["width"] % 1152 == 0 else 1024
    nh = d // GDN_HEAD_DIM
    g = {}
    dup = _matmul(dx2_b, p["w_ff_down"], tb=True, out_dtype=BF16, name=tag + "mm_dup", tm_cap=512, extras=(s["up"],),
                  epilogue=lambda r, u: r * 2.0 * jnp.maximum(u, 0.0))
    g["w_ff_down"] = _matmul(s["act"], dx2_b, ta=True, out_dtype=BF16, name=tag + "mm_dwdown")
    g["w_ff_up"] = _matmul(s["h2"], dup, ta=True, out_dtype=BF16, name=tag + "mm_dwup", out_split=N_CHIPS)
    dh2 = _matmul(dup, p["w_ff_up"], tb=True, name=tag + "mm_dh2", b_split=N_CHIPS)
    dx1, dx1_b, g["norm2_g"] = _rmsnorm_bwd(s["x1"], p["norm2_g"], dh2, dx2, tag + "rms2b")
    dmix = _matmul(dx1_b, p["w_out"], tb=True, name=tag + "mm_dmix")
    g["w_out"] = _matmul(s["mix"], dx1_b, ta=True, out_dtype=BF16, name=tag + "mm_dwout")
    dyg, dys, dgl = _merge_bwd(s["y_gdn"], s["y_swa"], s["proj"], lay["gate"], dmix, tag + "mergeb")
    g["w_branch_gdn"] = _matmul(s["gdn_o"], dyg, ta=True, out_dtype=BF16, name=tag + "mm_dwbg")
    g["w_branch_swa"] = _matmul(s["swa_o"], dys, ta=True, out_dtype=BF16, name=tag + "mm_dwbs")
    dgdn_o = _matmul(dyg, p["w_branch_gdn"], tb=True, name=tag + "mm_dgdn")
    dswa_o = _matmul(dys, p["w_branch_swa"], tb=True, name=tag + "mm_dswa")
    if mid is not None:
        token = mid(g, dswa_o)
        p = dict(p, attn_sinks=p["attn_sinks"] + token[:1], a_log=p["a_log"] + token[:1])
    dq_s, dk_s, dv_s, dsink = _swa_bwd(s["proj"], p["attn_sinks"], p["alibi"], dswa_o, d=d, q_off=lay["q"], k_off=lay["k"],
                                       v_off=lay["v"], name=tag + "swab")
    dqkvc, dz, dbg, dal, ddt, dgn = _gdn_bwd(s["qkvc"], s["proj"], p["a_log"], p["dt_bias"], p["gdn_norm_g"], s["states"], dgdn_o,
                                             d=d, z_off=lay["z"], bg_off=lay["bg"], name=tag + "gdnb")
    dqkv, g["conv_w"] = _conv_bwd(s["proj"], p["conv_w"], dqkvc, d, tag + "convb")
    dproj = _assemble([dqkv, dz, dq_s, dgl, dk_s, dv_s, dbg], BF16, tag + "dproj")
    g["w_in"] = _matmul(s["h1"], dproj, ta=True, out_dtype=BF16, name=tag + "mm_dwin", tn_cap=tn)
    dh1 = _matmul(dproj, p["w_in"], tb=True, name=tag + "mm_dh1", tk_cap=tn)
    dx, dx_b, g["norm1_g"] = _rmsnorm_bwd(s["x"], p["norm1_g"], dh1, dx1, tag + "rms1b")
    g["a_log"], g["dt_bias"], g["gdn_norm_g"], g["attn_sinks"] = dal[0, :nh], ddt[0, :nh], dgn[0], dsink[0, :d // SWA_HEAD_DIM]
    return dx, dx_b, g


MESH = pl.DeviceIdType.MESH
HBM_SPEC = pl.BlockSpec(memory_space=pl.ANY)


def _place():
    x, y, c = lax.axis_index("x"), lax.axis_index("y"), lax.axis_index("c")
    return x, y, c, 2 * x + y


def _flip(x, y, k):
    px, py = x ^ (k >> 1), y ^ (k & 1)
    return px, py, 2 * px + py


def _cast_into_slot(w, layer, pos, name):
    _, r, cols = w.shape
    tm = _pick(r, max(16, (1 << 19) // cols // 16 * 16), 16)

    def body(x_ref, y_ref, w_ref, o_ref):
        o_ref[...] = w_ref[...].astype(o_ref.dtype)

    return pl.pallas_call(
        body, name=name,
        grid_spec=pltpu.PrefetchScalarGridSpec(
            num_scalar_prefetch=2, grid=(r // tm,),
            in_specs=[pl.BlockSpec((None, tm, cols), lambda i, xr, yr: (layer, i, 0))],
            out_specs=pl.BlockSpec((None, tm, cols), lambda i, xr, yr: (2 * xr[0] + yr[0], i, 0))),
        out_shape=jax.ShapeDtypeStruct((N_CHIPS, r, cols), BF16),
        compiler_params=_params(("parallel",)),
    )(pos[0], pos[1], w)


SEM_SPEC = pl.BlockSpec(memory_space=pltpu.SEMAPHORE)
SPLIT_COPY = pltpu.CompilerParams(has_side_effects=pltpu.SideEffectType.DATAFLOW_SIDE_EFFECTING)
TOKEN = jax.ShapeDtypeStruct((8, LANES), F32)
TOKEN_SPEC = pl.BlockSpec(memory_space=pltpu.VMEM)


def _gather_start(bufs, name):
    n = len(bufs)

    def body(*refs):
        outs, sems, token = refs[n:2 * n], refs[2 * n:8 * n], refs[8 * n]
        x, y, c, ci = _place()
        for a in range(n):
            hr = bufs[a].shape[1] // 2
            mine = outs[a].at[ci, pl.ds(c * hr, hr)]
            for k in (1, 2, 3):
                px, py, _ = _flip(x, y, k)
                pltpu.make_async_remote_copy(src_ref=mine, dst_ref=mine, send_sem=sems[3 * a + k - 1], recv_sem=sems[3 * n + 3 * a + k - 1],
                                             device_id=(px, py, c), device_id_type=MESH).start()
        token[...] = jnp.zeros_like(token)

    res = pl.pallas_call(
        body, name=name, in_specs=[HBM_SPEC] * n, out_specs=[HBM_SPEC] * n + [SEM_SPEC] * (6 * n) + [TOKEN_SPEC],
        out_shape=[pltpu.HBM(b.shape, b.dtype) for b in bufs] + [pltpu.SemaphoreType.DMA(())] * (6 * n) + [TOKEN],
        input_output_aliases={a: a for a in range(n)}, compiler_params=SPLIT_COPY,
    )(*[pltpu.with_memory_space_constraint(b, pltpu.HBM) for b in bufs])
    return res[:n], res[n:7 * n], res[7 * n]


def _gather_wait(bufs, sems, after, name):
    n = len(bufs)

    def body(*refs):
        sems, outs = refs[n:7 * n], refs[7 * n + 1:8 * n + 1]
        x, y, c, ci = _place()
        for a in range(n):
            hr = bufs[a].shape[1] // 2
            mine = outs[a].at[ci, pl.ds(c * hr, hr)]
            for k in (1, 2, 3):
                px, py, pj = _flip(x, y, k)
                landed = outs[a].at[pj, pl.ds(c * hr, hr)]
                cp = pltpu.make_async_remote_copy(src_ref=mine, dst_ref=landed, send_sem=sems[3 * a + k - 1], recv_sem=sems[3 * n + 3 * a + k - 1],
                                                  device_id=(px, py, c), device_id_type=MESH)
                cp.wait_send()
                cp.wait_recv()

    return pl.pallas_call(
        body, name=name, in_specs=[HBM_SPEC] * n + [SEM_SPEC] * (6 * n) + [HBM_SPEC], out_specs=[HBM_SPEC] * n,
        out_shape=[pltpu.HBM(b.shape, b.dtype) for b in bufs],
        input_output_aliases={a: a for a in range(n)}, compiler_params=SPLIT_COPY,
    )(*bufs, *sems, after)


def _gather_forward(bufs, name):
    n = len(bufs)

    def body(*refs):
        outs = refs[n:2 * n]
        send_sems, recv_sems = refs[2 * n:]
        x, y, c, _ = _place()
        waits = []
        for a in range(n):
            hr = bufs[a].shape[1] // 2
            for k in (1, 2, 3):
                _, _, pj = _flip(x, y, k)
                landed = outs[a].at[pj, pl.ds(c * hr, hr)]
                fw = pltpu.make_async_remote_copy(src_ref=landed, dst_ref=landed, send_sem=send_sems.at[a, k - 1], recv_sem=recv_sems.at[a, k - 1],
                                                  device_id=(x, y, 1 - c), device_id_type=MESH)
                fw.start()
                waits.append(fw.wait_send)
                passed = outs[a].at[pj, pl.ds((1 - c) * hr, hr)]
                waits.append(pltpu.make_async_remote_copy(src_ref=passed, dst_ref=passed, send_sem=send_sems.at[a, k - 1],
                                                          recv_sem=recv_sems.at[a, k - 1], device_id=(x, y, 1 - c),
                                                          device_id_type=MESH).wait_recv)
        for w in waits:
            w()

    return pl.pallas_call(
        body, name=name, in_specs=[HBM_SPEC] * n, out_specs=[HBM_SPEC] * n,
        out_shape=[jax.ShapeDtypeStruct(b.shape, b.dtype) for b in bufs],
        input_output_aliases={a: a for a in range(n)},
        scratch_shapes=[pltpu.SemaphoreType.DMA((n, 3))] * 2,
    )(*bufs)


def _swap_with_sibling(gs, name):
    n = len(gs)

    def body(*refs):
        ins, outs = refs[:n], refs[n:2 * n]
        send_sems, recv_sems = refs[2 * n:]
        x, y, c, _ = _place()
        cps = []
        for a in range(n):
            hr = gs[a].shape[1] // 2
            cp = pltpu.make_async_remote_copy(src_ref=ins[a].at[:, pl.ds((1 - c) * hr, hr)], dst_ref=outs[a], send_sem=send_sems.at[a],
                                              recv_sem=recv_sems.at[a], device_id=(x, y, 1 - c), device_id_type=MESH)
            cp.start()
            cps.append(cp)
        for cp in cps:
            cp.wait()

    return pl.pallas_call(
        body, name=name, in_specs=[HBM_SPEC] * n, out_specs=[HBM_SPEC] * n,
        out_shape=[jax.ShapeDtypeStruct((g.shape[0], g.shape[1] // 2, g.shape[2]), g.dtype) for g in gs],
        scratch_shapes=[pltpu.SemaphoreType.DMA((n,))] * 2,
    )(*gs)


def _scatter_start(hs, name):
    n = len(hs)

    def body(*refs):
        srcs, lands, sems, token = refs[n:2 * n], refs[2 * n:3 * n], refs[3 * n:9 * n], refs[9 * n]
        x, y, c, ci = _place()
        for a in range(n):
            for k in (1, 2, 3):
                px, py, pj = _flip(x, y, k)
                pltpu.make_async_remote_copy(src_ref=srcs[a].at[pj], dst_ref=lands[a].at[ci], send_sem=sems[3 * a + k - 1],
                                             recv_sem=sems[3 * n + 3 * a + k - 1], device_id=(px, py, c), device_id_type=MESH).start()
        token[...] = jnp.zeros_like(token)

    res = pl.pallas_call(
        body, name=name, in_specs=[HBM_SPEC] * n, out_specs=[HBM_SPEC] * (2 * n) + [SEM_SPEC] * (6 * n) + [TOKEN_SPEC],
        out_shape=[pltpu.HBM(h.shape, h.dtype) for h in hs] * 2 + [pltpu.SemaphoreType.DMA(())] * (6 * n) + [TOKEN],
        input_output_aliases={a: a for a in range(n)}, compiler_params=SPLIT_COPY,
    )(*[pltpu.with_memory_space_constraint(h, pltpu.HBM) for h in hs])
    return res[:n], res[n:2 * n], res[2 * n:8 * n], res[8 * n]


def _scatter_wait(hs, lands, sems, after, name):
    n = len(hs)

    def body(*refs):
        sems, srcs, lands_o = refs[2 * n:8 * n], refs[8 * n + 1:9 * n + 1], refs[9 * n + 1:10 * n + 1]
        x, y, c, ci = _place()
        for a in range(n):
            for k in (1, 2, 3):
                px, py, pj = _flip(x, y, k)
                cp = pltpu.make_async_remote_copy(src_ref=srcs[a].at[pj], dst_ref=lands_o[a].at[pj], send_sem=sems[3 * a + k - 1],
                                                  recv_sem=sems[3 * n + 3 * a + k - 1], device_id=(px, py, c), device_id_type=MESH)
                cp.wait_send()
                cp.wait_recv()

    res = pl.pallas_call(
        body, name=name, in_specs=[HBM_SPEC] * (2 * n) + [SEM_SPEC] * (6 * n) + [HBM_SPEC], out_specs=[HBM_SPEC] * (2 * n),
        out_shape=[pltpu.HBM(h.shape, h.dtype) for h in hs] * 2,
        input_output_aliases={a: a for a in range(2 * n)}, compiler_params=SPLIT_COPY,
    )(*hs, *lands, *sems, after)
    return res[:n], res[n:]


def _share_with_sibling(bufs, name):
    n = len(bufs)

    def body(*refs):
        outs = refs[n:2 * n]
        send_sems, recv_sems = refs[2 * n:]
        x, y, c, _ = _place()
        waits = []
        for a in range(n):
            mine = outs[a].at[c]
            cp = pltpu.make_async_remote_copy(src_ref=mine, dst_ref=mine, send_sem=send_sems.at[a], recv_sem=recv_sems.at[a],
                                              device_id=(x, y, 1 - c), device_id_type=MESH)
            cp.start()
            waits.append(cp.wait_send)
            got = outs[a].at[1 - c]
            waits.append(pltpu.make_async_remote_copy(src_ref=got, dst_ref=got, send_sem=send_sems.at[a], recv_sem=recv_sems.at[a],
                                                      device_id=(x, y, 1 - c), device_id_type=MESH).wait_recv)
        for w in waits:
            w()

    return pl.pallas_call(
        body, name=name, in_specs=[HBM_SPEC] * n, out_specs=[HBM_SPEC] * n,
        out_shape=[jax.ShapeDtypeStruct(b.shape, b.dtype) for b in bufs],
        input_output_aliases={a: a for a in range(n)},
        scratch_shapes=[pltpu.SemaphoreType.DMA((n,))] * 2,
    )(*bufs)


def _add_sibling_half(g, got, core, name):
    nc, r, cols = g.shape
    hr = r // 2
    tm = _pick(hr, 256, 16)

    def body(core_ref, g_ref, o_ref, s_ref):
        s_ref[...] = (g_ref[...].astype(F32) + o_ref[...].astype(F32)).astype(s_ref.dtype)

    return pl.pallas_call(
        body, name=name,
        grid_spec=pltpu.PrefetchScalarGridSpec(
            num_scalar_prefetch=1, grid=(nc, hr // tm),
            in_specs=[pl.BlockSpec((None, None, tm, cols), lambda j, i, cr: (j, cr[0], i, 0)),
                      pl.BlockSpec((None, tm, cols), lambda j, i, cr: (j, i, 0))],
            out_specs=pl.BlockSpec((None, tm, cols), lambda j, i, cr: (j, i, 0))),
        out_shape=jax.ShapeDtypeStruct((nc, hr, cols), g.dtype),
        compiler_params=_params(("parallel", "parallel")),
    )(core, g.reshape(nc, 2, hr, cols), got)


def _sum_chips(own, parts, pos, name):
    nc, r, cols = parts.shape
    tm = _pick(r, 256, 16)

    def body(x_ref, y_ref, c_ref, own_ref, p_ref, o_ref):
        chip = 2 * x_ref[0] + y_ref[0]
        acc = own_ref[...].astype(F32)
        for k in range(1, nc):
            acc = acc + p_ref[chip ^ k].astype(F32)
        o_ref[...] = acc

    return pl.pallas_call(
        body, name=name,
        grid_spec=pltpu.PrefetchScalarGridSpec(
            num_scalar_prefetch=3, grid=(r // tm,),
            in_specs=[pl.BlockSpec((None, tm, cols), lambda i, xr, yr, cr: (2 * xr[0] + yr[0], i, 0)),
                      pl.BlockSpec((nc, tm, cols), lambda i, xr, yr, cr: (0, i, 0))],
            out_specs=pl.BlockSpec((None, tm, cols), lambda i, xr, yr, cr: (cr[0], i, 0))),
        out_shape=jax.ShapeDtypeStruct((2, r, cols), F32),
        compiler_params=_params(("parallel",)),
    )(*pos, own, parts)


def _reduce_scatter_start(gs, pos, tag):
    got = _swap_with_sibling(gs, tag + "rs_swap")
    hs = [_add_sibling_half(g, o, pos[2], tag + "rs_add%d" % i) for i, (g, o) in enumerate(zip(gs, got))]
    hs, lands, sems, token = _scatter_start(hs, tag + "rs_scatter_start")
    return (hs, lands, sems), token


def _reduce_scatter_finish(pending, after, pos, tag):
    hs, lands, sems = pending
    hs, parts = _scatter_wait(hs, lands, sems, after, tag + "rs_scatter_wait")
    rs = [_sum_chips(h, p, pos, tag + "rs_sum%d" % i) for i, (h, p) in enumerate(zip(hs, parts))]
    both = _share_with_sibling(rs, tag + "rs_share")
    return [b.reshape(2 * b.shape[1], b.shape[2]) for b in both]


def _allreduce_small(v, name):
    rows = v.shape[0]

    def body(v_ref, o_ref, buf, send_sems, recv_sems, local_sem):
        x, y, c, _ = _place()
        me, sibling = (x, y, c), (x, y, 1 - c)
        chips = [_flip(x, y, k)[:2] for k in (1, 2, 3)]

        def slot(px, py, pc):
            return buf.at[4 * px + 2 * py + pc]

        def copy(k, block, to, src=None):
            return pltpu.make_async_remote_copy(src_ref=slot(*block) if src is None else src, dst_ref=slot(*block), send_sem=send_sems.at[k],
                                                recv_sem=recv_sems.at[k], device_id=to, device_id_type=MESH)

        mine = pltpu.make_async_copy(v_ref, slot(*me), local_sem)
        mine.start()
        first = [copy(0, me, sibling, src=v_ref)] + [copy(1 + j, me, (*chip, c), src=v_ref) for j, chip in enumerate(chips)]
        for cp in first:
            cp.start()
        passed = [copy(4 + j, (*chip, c), sibling) for j, chip in enumerate(chips)]
        for j, chip in enumerate(chips):
            copy(1 + j, (*chip, c), me).wait_recv()
            passed[j].start()
        copy(0, sibling, me).wait_recv()
        for j, chip in enumerate(chips):
            copy(4 + j, (*chip, 1 - c), me).wait_recv()
        for cp in first + passed:
            cp.wait_send()
        mine.wait()
        acc = buf[0]
        for i in range(1, 2 * N_CHIPS):
            acc = acc + buf[i]
        o_ref[...] = acc

    vm = pl.BlockSpec(memory_space=pltpu.VMEM)
    return pl.pallas_call(
        body, name=name, in_specs=[vm], out_specs=vm, out_shape=jax.ShapeDtypeStruct(v.shape, F32),
        scratch_shapes=[pltpu.VMEM((2 * N_CHIPS, rows, LANES), F32), pltpu.SemaphoreType.DMA((7,)), pltpu.SemaphoreType.DMA((7,)),
                        pltpu.SemaphoreType.DMA],
        compiler_params=pltpu.CompilerParams(vmem_limit_bytes=VMEM_LIMIT),
    )(v)


def _adamw(w, g, m, v, name):
    r, cols = w.shape
    tm = _pick(r, max(8, (1 << 18) // max(cols, 1) // 8 * 8), 8)

    def body(w_ref, g_ref, m_ref, v_ref, d_ref, nm_ref, nv_ref):
        gg = g_ref[...]
        nm = ADAM_B1 * m_ref[...] + (1.0 - ADAM_B1) * gg
        nv = ADAM_B2 * v_ref[...] + (1.0 - ADAM_B2) * jnp.square(gg)
        m_hat = nm / (1.0 - ADAM_B1 ** ADAM_STEP)
        v_hat = nv / (1.0 - ADAM_B2 ** ADAM_STEP)
        d_ref[...] = -ADAM_LR * (m_hat / (jnp.sqrt(v_hat) + ADAM_EPS) + ADAM_WD * w_ref[...])
        nm_ref[...] = nm
        nv_ref[...] = nv

    spec = pl.BlockSpec((tm, cols), lambda i: (i, 0))
    return pl.pallas_call(
        body, name=name, grid=(r // tm,), in_specs=[spec] * 4, out_specs=[spec] * 3,
        out_shape=[jax.ShapeDtypeStruct((r, cols), F32)] * 3, compiler_params=_params(("parallel",)),
    )(w, g, m, v)


def _adamw_layer(w, g, m, v, layer, prev, name):
    depth, r, cols = w.shape
    tm = _pick(r, max(8, (1 << 18) // max(cols, 1) // 8 * 8), 8)

    def body(*refs):
        w_ref, g_ref, m_ref, v_ref = refs[:4]
        go_ref, d_ref, nm_ref, nv_ref = refs[-4:]
        gg = g_ref[...]
        nm = ADAM_B1 * m_ref[...] + (1.0 - ADAM_B1) * gg
        nv = ADAM_B2 * v_ref[...] + (1.0 - ADAM_B2) * jnp.square(gg)
        m_hat = nm / (1.0 - ADAM_B1 ** ADAM_STEP)
        v_hat = nv / (1.0 - ADAM_B2 ** ADAM_STEP)
        go_ref[...] = gg
        d_ref[...] = -ADAM_LR * (m_hat / (jnp.sqrt(v_hat) + ADAM_EPS) + ADAM_WD * w_ref[...])
        nm_ref[...] = nm
        nv_ref[...] = nv

    lspec = pl.BlockSpec((None, tm, cols), lambda i: (layer, i, 0))
    gspec = pl.BlockSpec((tm, cols), lambda i: (i, 0))
    extra = [] if prev is None else list(prev)
    return pl.pallas_call(
        body, name=name, grid=(r // tm,), in_specs=[lspec, gspec, lspec, lspec] + [HBM_SPEC] * len(extra), out_specs=[lspec] * 4,
        out_shape=[jax.ShapeDtypeStruct((depth, r, cols), F32)] * 4,
        input_output_aliases={4 + j: j for j in range(len(extra))}, compiler_params=_params(("parallel",)),
    )(w, g, m, v, *extra)


def _adamw_nd(w, g, m, v, name):
    shape = w.shape
    two = (1, shape[0]) if len(shape) == 1 else (int(np.prod(shape[:-1])), shape[-1])
    outs = _adamw(w.reshape(two), g.reshape(two), m.reshape(two), v.reshape(two), name)
    return [o.reshape(shape) for o in outs]


WEIGHTS = ("norm1_g", "w_in", "conv_w", "a_log", "dt_bias", "gdn_norm_g", "attn_sinks", "w_branch_gdn", "w_branch_swa", "w_out",
           "norm2_g", "w_ff_up", "w_ff_down", "final_norm_g")
MATRICES = ("w_in", "w_branch_gdn", "w_branch_swa", "w_out", "w_ff_up", "w_ff_down")


def _to_rows(vec):
    n = vec.shape[0]
    rows = -(-n // (8 * LANES)) * 8
    return jnp.pad(vec, (0, rows * LANES - n)).reshape(rows, LANES)


def kernel(x, norm1_g, w_in, conv_w, a_log, dt_bias, gdn_norm_g, attn_sinks, w_branch_gdn, w_branch_swa, w_out, norm2_g, w_ff_up, w_ff_down, final_norm_g, loss_target, m_norm1_g, m_w_in, m_conv_w, m_a_log, m_dt_bias, m_gdn_norm_g, m_attn_sinks, m_w_branch_gdn, m_w_branch_swa, m_w_out, m_norm2_g, m_w_ff_up, m_w_ff_down, m_final_norm_g, v_norm1_g, v_w_in, v_conv_w, v_a_log, v_dt_bias, v_gdn_norm_g, v_attn_sinks, v_w_branch_gdn, v_w_branch_swa, v_w_out, v_norm2_g, v_w_ff_up, v_w_ff_down, v_final_norm_g):
    w = dict(norm1_g=norm1_g, w_in=w_in, conv_w=conv_w, a_log=a_log, dt_bias=dt_bias, gdn_norm_g=gdn_norm_g, attn_sinks=attn_sinks,
             w_branch_gdn=w_branch_gdn, w_branch_swa=w_branch_swa, w_out=w_out, norm2_g=norm2_g, w_ff_up=w_ff_up, w_ff_down=w_ff_down,
             final_norm_g=final_norm_g)
    mom = dict(norm1_g=m_norm1_g, w_in=m_w_in, conv_w=m_conv_w, a_log=m_a_log, dt_bias=m_dt_bias, gdn_norm_g=m_gdn_norm_g,
               attn_sinks=m_attn_sinks, w_branch_gdn=m_w_branch_gdn, w_branch_swa=m_w_branch_swa, w_out=m_w_out, norm2_g=m_norm2_g,
               w_ff_up=m_w_ff_up, w_ff_down=m_w_ff_down, final_norm_g=m_final_norm_g)
    var = dict(norm1_g=v_norm1_g, w_in=v_w_in, conv_w=v_conv_w, a_log=v_a_log, dt_bias=v_dt_bias, gdn_norm_g=v_gdn_norm_g,
               attn_sinks=v_attn_sinks, w_branch_gdn=v_w_branch_gdn, w_branch_swa=v_w_branch_swa, w_out=v_w_out, norm2_g=v_norm2_g,
               w_ff_up=v_w_ff_up, w_ff_down=v_w_ff_down, final_norm_g=v_final_norm_g)
    depth, d = norm1_g.shape
    xs, target = x[0], loss_target[0]
    core = lax.axis_index("c")
    chip = 2 * lax.axis_index("x") + lax.axis_index("y")
    pos = tuple(jnp.reshape(lax.axis_index(a), (1,)).astype(jnp.int32) for a in ("x", "y", "c"))

    cw = conv_w.shape[-1]
    placed = lax.dynamic_update_slice(jnp.zeros((depth, CONV_K, N_CHIPS * cw), F32), conv_w, (0, 0, chip * cw))
    placed = placed * (core == 0).astype(F32)

    alibi = _alibi_row(d)
    first_group, late_group = MATRICES[:1], MATRICES[1:]
    bufs = [{n: _cast_into_slot(w[n], l, pos, "l%d_cast_%s" % (l, n)) for n in MATRICES} for l in range(depth)]

    def start_gather(l, names, after, tag):
        group, _ = lax.optimization_barrier(([bufs[l][n] for n in names], after))
        return _gather_start(group, "l%d_gather_start_%s" % (l, tag))

    def finish_gather(l, names, started, after, tag):
        got = _gather_wait(started[0], started[1], after, "l%d_gather_wait_%s" % (l, tag))
        full = dict(zip(names, _gather_forward(got, "l%d_gather_forward_%s" % (l, tag))))
        out = {}
        if "w_in" in full:
            out["w_in"] = _pack_w_in(jnp.transpose(full["w_in"], (1, 0, 2)).reshape(d, -1), d)
        for n in ("w_branch_gdn", "w_branch_swa", "w_out", "w_ff_down"):
            if n in full:
                out[n] = full[n].reshape(-1, d)
        if "w_ff_up" in full:
            out["w_ff_up"] = full["w_ff_up"]
        return out

    def small_params(l):
        return dict(norm1_g=norm1_g[l][None], norm2_g=norm2_g[l][None], conv_w=conv_full[l], a_log=_pad_row(a_log[l]),
                    dt_bias=_pad_row(dt_bias[l]), gdn_norm_g=gdn_norm_g[l][None], attn_sinks=_pad_row(attn_sinks[l]), alibi=alibi)

    st_a = start_gather(0, first_group, placed, "a")
    conv_full = _allreduce_small(_to_rows((placed + st_a[2][0, 0]).reshape(-1)), "gather_conv_w")
    conv_full = conv_full.reshape(-1)[:depth * CONV_K * N_CHIPS * cw].reshape(depth, CONV_K, N_CHIPS * cw)
    st_b = start_gather(0, late_group, conv_full, "b")
    layers = [dict(small_params(0), **finish_gather(0, first_group, st_a, st_b[2], "a"))]
    h = xs
    saved = []
    for l in range(depth):
        p = layers[l]
        late = None
        if l == 0:
            def late(after, st_b=st_b):
                rest = finish_gather(0, late_group, st_b, after, "b")
                layers[0].update(rest)
                return rest
        if l + 1 < depth:
            nxt = start_gather(l + 1, MATRICES, p["w_in"], "all")
            p = dict(p, norm1_g=p["norm1_g"] + nxt[2][:1, :1])
        h, s = _layer_fwd(h, p, "l%d_" % l, late)
        saved.append(s)
        if l + 1 < depth:
            layers.append(dict(small_params(l + 1), **finish_gather(l + 1, MATRICES, nxt, h, "all")))
    dh, dh_b, d_final, loss_row = _loss_head(h, final_norm_g[None], target, "loss_head")

    grads = {n: [None] * depth for n in ("norm1_g", "norm2_g", "a_log", "dt_bias", "gdn_norm_g", "attn_sinks", "conv_w")}
    updated = {n: None for n in MATRICES}
    small = ("norm1_g", "norm2_g", "a_log", "dt_bias", "gdn_norm_g", "attn_sinks", "conv_w")
    state = {"pending": None, "todo": None}

    def adamw_todo():
        if state["todo"] is not None:
            l, names, sums = state["todo"]
            for n, r in zip(names, sums):
                updated[n] = _adamw_layer(w[n], r, mom[n], var[n], l, updated[n], "l%d_adamw_%s" % (l, n))
            state["todo"] = None

    def finish_scatter(after):
        adamw_todo()
        if state["pending"] is not None:
            l, names, pending, tag = state["pending"]
            state["todo"] = (l, names, _reduce_scatter_finish(pending, after, pos, "l%d_%s_" % (l, tag)))
            state["pending"] = None

    def start_scatter(l, names, mats, tag):
        pending, token = _reduce_scatter_start(mats, pos, "l%d_%s_" % (l, tag))
        state["pending"] = (l, names, pending, tag)
        return token

    def stacked(g, n):
        return g[n] if n == "w_ff_up" else g[n].reshape(N_CHIPS, -1, g[n].shape[-1])

    for l in reversed(range(depth)):
        def mid(g, after, l=l):
            finish_scatter(after)
            return start_scatter(l, late_group, [stacked(g, n) for n in late_group], "b")

        dh, dh_b, g = _layer_bwd(dh, dh_b, layers[l], saved[l], "l%d_" % l, mid)
        for n in grads:
            grads[n][l] = g[n].reshape(-1)
        finish_scatter(dh)
        g_in = _unpack_w_in(g["w_in"], d)
        mats = [jnp.transpose(g_in.reshape(d, N_CHIPS, -1), (1, 0, 2))]
        if l == 0:
            pieces = [jnp.stack(grads[n]).reshape(-1) for n in small] + [d_final.reshape(-1), loss_row[0, :1]]
            sizes = [p.shape[0] for p in pieces]
            packed = _allreduce_small(_to_rows(jnp.concatenate(pieces)), "reduce_small").reshape(-1)
            mats, _ = lax.optimization_barrier((mats, packed))
        token = start_scatter(l, first_group, mats, "a")
        if l > 0:
            dh_b = dh_b + token[0, 0].astype(BF16)

    offs = np.concatenate([[0], np.cumsum(sizes)])
    red = {n: packed[offs[i]:offs[i + 1]] for i, n in enumerate(small + ("final_norm_g", "loss"))}
    loss = red["loss"][0]

    grad_out = {}
    for n in ("norm1_g", "norm2_g", "a_log", "dt_bias", "gdn_norm_g", "attn_sinks"):
        grad_out[n] = red[n].reshape(w[n].shape)
    grad_out["final_norm_g"] = red["final_norm_g"]
    conv_g = red["conv_w"].reshape(depth, CONV_K, N_CHIPS * cw)
    grad_out["conv_w"] = lax.dynamic_slice(conv_g, (0, 0, chip * cw), (depth, CONV_K, cw))

    delta, new_m, new_v = {}, {}, {}
    for n in grad_out:
        delta[n], new_m[n], new_v[n] = _adamw_nd(w[n], grad_out[n], mom[n], var[n], "adamw_" + n)
    adamw_todo()
    finish_scatter(updated["w_ff_down"][0])
    adamw_todo()
    for n in MATRICES:
        grad_out[n], delta[n], new_m[n], new_v[n] = updated[n]
    return (loss, dh[None], *[grad_out[n] for n in WEIGHTS], *[delta[n] for n in WEIGHTS], *[new_m[n] for n in WEIGHTS],
            *[new_v[n] for n in WEIGHTS])
```
